```python
import jax, jax.numpy as jnp
from jax import lax
import numpy as np

D_MODEL = 1024
BATCH = 16
SEQ = 256
DEPTH = 4
DEC_BATCH = 4
DEC_SEQ = 2048
PAST_LEN = 512

GRID_W = 64
HEAD_DIM = 128
A_HEADS = 4
A_WIDTH = A_HEADS * HEAD_DIM
CHUNK = 64
CONV_K = 3
B_HEADS = 4
B_KV_HEADS = 2
B_WIDTH = B_HEADS * HEAD_DIM
WINDOW = 128
C_HEADS = 8
C_KV_HEADS = 2
C_WIDTH = C_HEADS * HEAD_DIM
BLOCK = 128
D_FF = 4 * D_MODEL
ROT_HALF = HEAD_DIM // 2
ROT_FREQS = ROT_HALF // 2
ROPE_THETA = 10000.0
EPS = 1e-6
NEG = -1e30
N_EVEN = (DEPTH + 1) // 2
N_ODD = DEPTH // 2
EVEN_IN = 4 * A_WIDTH + 4 * A_HEADS + B_WIDTH + 2 * B_KV_HEADS * HEAD_DIM
EVEN_SPLITS = (3 * A_WIDTH, 4 * A_WIDTH, 4 * A_WIDTH + 4 * A_HEADS, 4 * A_WIDTH + 4 * A_HEADS + B_WIDTH)
ODD_IN = C_WIDTH + 2 * C_KV_HEADS * HEAD_DIM
ODD_SPLITS = (C_WIDTH, C_WIDTH + C_KV_HEADS * HEAD_DIM)
F32 = jnp.float32

kernel_name = 'hybrid_deltanet_swa_qknorm_dit_step'


def _rms(x, g):
    xf = x.astype(F32)
    y = xf * lax.rsqrt(jnp.mean(xf * xf, axis=-1, keepdims=True) + EPS)
    return (y * g.astype(F32)).astype(x.dtype)


def _l2norm(x):
    return x * lax.rsqrt(jnp.sum(x * x, axis=-1, keepdims=True) + EPS)


def _adaln(cond, w, b):
    m = jax.nn.silu(cond) @ w + b
    return jnp.split(m[:, None, :], 6, axis=-1)


def _modulate(x, g, shift, scale):
    return _rms(x, g) * (1 + scale) + shift


def _mlp(h, w1, w2):
    return jnp.square(jax.nn.relu(h @ w1)) @ w2


def _short_conv(x, w):
    return lax.conv_general_dilated(x, w[:, None, :].astype(x.dtype), (1,), [(CONV_K // 2, CONV_K // 2)],
                                    dimension_numbers=('NWC', 'WIO', 'NWC'), feature_group_count=x.shape[-1])


def _axial_rope_tables(L):
    rows = L // GRID_W
    row = jnp.repeat(jnp.arange(rows, dtype=F32), GRID_W)
    col = jnp.tile(jnp.arange(GRID_W, dtype=F32), rows)
    inv = ROPE_THETA ** (-jnp.arange(ROT_FREQS, dtype=F32) / ROT_FREQS)
    ang = jnp.stack([row, col], axis=-1)[:, :, None] * inv
    return jnp.cos(ang)[:, None], jnp.sin(ang)[:, None]


def _rope(x, cos, sin):
    Bn, L, H, D = x.shape
    xr = x.reshape(Bn, L, H, 2, 2, ROT_FREQS).astype(F32)
    x1, x2 = xr[..., 0, :], xr[..., 1, :]
    out = jnp.stack([x1 * cos - x2 * sin, x2 * cos + x1 * sin], axis=-2)
    return out.reshape(Bn, L, H, D).astype(x.dtype)


def _group(q, n_kv):
    Bn, L, H, D = q.shape
    return q.reshape(Bn, L, n_kv, H // n_kv, D)


def _attn_blocked(q, k, v, sink):
    Bn, L, KV, G, D = q.shape
    nb = L // BLOCK
    qb = jnp.moveaxis(q.reshape(Bn, nb, BLOCK, KV, G, D), 1, 0)

    def one_block(qi):
        s = jnp.einsum('bqkgd,bmkd->bkgqm', qi, k, preferred_element_type=F32)
        if sink is not None:
            col = jnp.broadcast_to(sink.astype(F32)[None, :, :, None, None], s.shape[:-1] + (1,))
            p = jax.nn.softmax(jnp.concatenate([col, s], axis=-1), axis=-1)[..., 1:]
        else:
            p = jax.nn.softmax(s, axis=-1)
        return jnp.einsum('bkgqm,bmkd->bqkgd', p.astype(v.dtype), v)

    o = lax.map(one_block, qb)
    return jnp.moveaxis(o, 0, 1).reshape(Bn, L, KV * G * D)


def _banded_attn(q, k, v, ck, cv, sink):
    Bn, L, KV, G, D = q.shape
    nb = L // BLOCK
    P = ck.shape[1]
    qb = q.reshape(Bn, nb, BLOCK, KV, G, D)
    pad = lambda t: jnp.pad(t.reshape(Bn, nb, BLOCK, KV, D), ((0, 0), (1, 1), (0, 0), (0, 0), (0, 0)))
    band = lambda t: jnp.concatenate([t[:, :-2], t[:, 1:-1], t[:, 2:]], axis=2)
    kband, vband = band(pad(k)), band(pad(v))
    qpos = jnp.arange(nb)[:, None] * BLOCK + jnp.arange(BLOCK)[None, :]
    kpos = jnp.arange(nb)[:, None] * BLOCK + jnp.arange(-BLOCK, 2 * BLOCK)[None, :]
    valid = ((jnp.abs(qpos[:, :, None] - kpos[:, None, :]) <= WINDOW)
             & (kpos >= 0)[:, None, :] & (kpos < L)[:, None, :])
    s_loc = jnp.einsum('bnqkgd,bnmkd->bnkgqm', qb, kband, preferred_element_type=F32)
    s_loc = jnp.where(valid[None, :, None, None], s_loc, NEG)
    s_ctx = jnp.einsum('bnqkgd,bpkd->bnkgqp', qb, ck, preferred_element_type=F32)
    s_sink = jnp.broadcast_to(sink.astype(F32)[None, None, :, :, None, None], s_loc.shape[:-1] + (1,))
    p = jax.nn.softmax(jnp.concatenate([s_sink, s_ctx, s_loc], axis=-1), axis=-1).astype(v.dtype)
    o = (jnp.einsum('bnkgqp,bpkd->bnqkgd', p[..., 1:1 + P], cv)
         + jnp.einsum('bnkgqm,bnmkd->bnqkgd', p[..., 1 + P:], vband))
    return o.reshape(Bn, L, KV * G * D)


def _gdn_chunked(q, k, v, g, beta, s0):
    Bn, H, L, dk = q.shape
    dv = v.shape[-1]
    N = L // CHUNK
    q, k, v = [t.reshape(Bn, H, N, CHUNK, -1) for t in (q, k, v)]
    g = g.reshape(Bn, H, N, CHUNK)
    beta = beta.reshape(Bn, H, N, CHUNK)
    gc = jnp.cumsum(g, axis=-1)
    tril = jnp.tril(jnp.ones((CHUNK, CHUNK), bool))
    strict = jnp.tril(jnp.ones((CHUNK, CHUNK), bool), -1)
    diff = gc[..., :, None] - gc[..., None, :]
    decay = jnp.where(tril, jnp.exp(jnp.where(tril, diff, 0.0)), 0.0)
    kb = k * beta[..., None]
    lower = jnp.where(strict, jnp.einsum('bhncd,bhnsd->bhncs', kb, k) * decay, 0.0)
    a = lower + jnp.eye(CHUNK, dtype=F32)
    rhs = jnp.concatenate([v * beta[..., None], kb * jnp.exp(gc)[..., None]], axis=-1)
    sol = lax.linalg.triangular_solve(a, rhs, left_side=True, lower=True)
    u, w = sol[..., :dv], sol[..., dv:]
    qk = jnp.where(tril, jnp.einsum('bhncd,bhnsd->bhncs', q, k) * decay, 0.0)

    def step(S, xs):
        qi, ki, ui, wi, gi, qki = xs
        v_new = ui - jnp.einsum('bhcd,bhde->bhce', wi, S)
        o = (jnp.einsum('bhcd,bhde->bhce', qi * jnp.exp(gi)[..., None], S)
             + jnp.einsum('bhcs,bhse->bhce', qki, v_new))
        glast = gi[..., -1]
        S = (S * jnp.exp(glast)[..., None, None]
             + jnp.einsum('bhcd,bhce->bhde', ki * jnp.exp(glast[..., None] - gi)[..., None], v_new))
        return S, o

    xs = tuple(jnp.moveaxis(t, 2, 0) for t in (q, k, u, w, gc, qk))
    S, o = lax.scan(step, s0, xs)
    return jnp.moveaxis(o, 0, 2).reshape(Bn, H, L, dv), S


def _even_mixer(h, w_in, conv_w, a_log, dt_bias, norm_g, sink, w_out, ctx_state, ctx_kv):
    Bn, L, _ = h.shape
    qkv, gate, bg, qb, kvb = jnp.split(h @ w_in, EVEN_SPLITS, axis=-1)
    qkv = jax.nn.silu(_short_conv(qkv, conv_w)).astype(F32)
    qa, ka, va = [jnp.moveaxis(t.reshape(Bn, L, A_HEADS, HEAD_DIM), 2, 1) for t in jnp.split(qkv, 3, axis=-1)]
    qa = _l2norm(qa) * HEAD_DIM ** -0.5
    ka = _l2norm(ka)
    bg = bg.astype(F32).reshape(Bn, L, 2, 2, A_HEADS)
    beta = jnp.transpose(jax.nn.sigmoid(bg[:, :, 0]), (0, 2, 3, 1))
    g = -jnp.exp(a_log.astype(F32)) * jax.nn.softplus(bg[:, :, 1] + dt_bias.astype(F32))
    g = jnp.transpose(g, (0, 2, 3, 1))
    if ctx_state is None:
        s0 = jnp.zeros((Bn, 2, A_HEADS, HEAD_DIM, HEAD_DIM), F32)
    else:
        s0 = ctx_state.astype(F32)
    rev = lambda t: jnp.flip(t, axis=2)
    o_fwd, s_fwd = _gdn_chunked(qa, ka, va, g[:, 0], beta[:, 0], s0[:, 0])
    o_bwd, s_bwd = _gdn_chunked(rev(qa), rev(ka), rev(va), rev(g[:, 1]), rev(beta[:, 1]), s0[:, 1])
    o = jnp.moveaxis(o_fwd + rev(o_bwd), 1, 2)
    o = _rms(o, norm_g) * jax.nn.silu(gate.astype(F32).reshape(Bn, L, A_HEADS, HEAD_DIM))
    out_a = o.reshape(Bn, L, A_WIDTH).astype(h.dtype)
    state = jnp.stack([s_fwd, s_bwd], axis=1).astype(h.dtype)
    qb = qb.reshape(Bn, L, B_HEADS, HEAD_DIM)
    kb, vb = [t.reshape(Bn, L, B_KV_HEADS, HEAD_DIM) for t in jnp.split(kvb, 2, axis=-1)]
    kv_out = jnp.stack([kb, vb], axis=1)
    sink2 = sink.reshape(B_KV_HEADS, B_HEADS // B_KV_HEADS)
    if ctx_kv is None:
        out_b = _attn_blocked(_group(qb * HEAD_DIM ** -0.5, B_KV_HEADS), kb, vb, sink2)
    else:
        cos, sin = _axial_rope_tables(L)
        qr, kr = _rope(qb, cos, sin), _rope(kb, cos, sin)
        out_b = _banded_attn(_group(qr * HEAD_DIM ** -0.5, B_KV_HEADS), kr, vb, ctx_kv[:, 0], ctx_kv[:, 1], sink2)
    out = jnp.concatenate([out_a, out_b], axis=-1) @ w_out
    return out, state, kv_out


def _odd_mixer(h, w_in, q_g, k_g, w_out, ctx_kv):
    Bn, L, _ = h.shape
    q, k, v = jnp.split(h @ w_in, ODD_SPLITS, axis=-1)
    q = _rms(q.reshape(Bn, L, C_HEADS, HEAD_DIM), q_g)
    k = _rms(k.reshape(Bn, L, C_KV_HEADS, HEAD_DIM), k_g)
    v = v.reshape(Bn, L, C_KV_HEADS, HEAD_DIM)
    kv_out = jnp.stack([k, v], axis=1)
    if ctx_kv is not None:
        cos, sin = _axial_rope_tables(L)
        q, k = _rope(q, cos, sin), _rope(k, cos, sin)
        k = jnp.concatenate([ctx_kv[:, 0], k], axis=1)
        v = jnp.concatenate([ctx_kv[:, 1], v], axis=1)
    o = _attn_blocked(_group(q * HEAD_DIM ** -0.5, C_KV_HEADS), k, v, None)
    return o @ w_out, kv_out


def setup_inputs(seed: int = 0) -> dict:
    key = jax.random.key(seed)
    ks = jax.random.split(key, 32)
    nrm = lambda k, shape, s: jax.random.normal(k, shape, F32) * s
    dt = jnp.exp(jax.random.uniform(ks[14], (N_EVEN, 2, A_HEADS), F32, np.log(1e-3), np.log(1e-1)))
    return {
        'x_prompt': nrm(ks[0], (BATCH, SEQ, D_MODEL), 1.0),
        'x_sample': nrm(ks[1], (DEC_BATCH, DEC_SEQ, D_MODEL), 1.0),
        'state_a': nrm(ks[2], (DEC_BATCH, N_EVEN, 2, A_HEADS, HEAD_DIM, HEAD_DIM), 0.1),
        'cache_b_kv': nrm(ks[3], (DEC_BATCH, N_EVEN, 2, PAST_LEN, B_KV_HEADS, HEAD_DIM), 1.0),
        'cache_c_kv': nrm(ks[4], (DEC_BATCH, N_ODD, 2, PAST_LEN, C_KV_HEADS, HEAD_DIM), 1.0),
        'c': nrm(ks[5], (DEC_BATCH, D_MODEL), 1.0),
        'c_ctx': nrm(ks[6], (D_MODEL,), 1.0),
        'ada_w': nrm(ks[7], (DEPTH, D_MODEL, 6 * D_MODEL), 0.5 * D_MODEL ** -0.5),
        'ada_b': nrm(ks[8], (DEPTH, 6 * D_MODEL), 0.02),
        'norm1_g': 1.0 + nrm(ks[9], (DEPTH, D_MODEL), 0.02),
        'norm2_g': 1.0 + nrm(ks[10], (DEPTH, D_MODEL), 0.02),
        'final_g': 1.0 + nrm(ks[11], (D_MODEL,), 0.02),
        'mlp_w1': nrm(ks[12], (DEPTH, D_MODEL, D_FF), D_MODEL ** -0.5),
        'mlp_w2': nrm(ks[13], (DEPTH, D_FF, D_MODEL), D_FF ** -0.5),
        'ev_w_in': nrm(ks[15], (N_EVEN, D_MODEL, EVEN_IN), D_MODEL ** -0.5),
        'a_conv': nrm(ks[16], (N_EVEN, CONV_K, 3 * A_WIDTH), CONV_K ** -0.5),
        'a_log': jnp.log(jax.random.uniform(ks[17], (N_EVEN, 2, A_HEADS), F32, 1.0, 16.0)),
        'a_dt_bias': jnp.log(jnp.expm1(dt)),
        'a_norm_g': 1.0 + nrm(ks[18], (N_EVEN, HEAD_DIM), 0.02),
        'b_sink': nrm(ks[19], (N_EVEN, B_HEADS), 1.0),
        'ev_w_out': nrm(ks[20], (N_EVEN, A_WIDTH + B_WIDTH, D_MODEL), (A_WIDTH + B_WIDTH) ** -0.5),
        'od_w_in': nrm(ks[21], (N_ODD, D_MODEL, ODD_IN), D_MODEL ** -0.5),
        'c_qnorm_g': 1.0 + nrm(ks[22], (N_ODD, HEAD_DIM), 0.02),
        'c_knorm_g': 1.0 + nrm(ks[23], (N_ODD, HEAD_DIM), 0.02),
        'od_w_out': nrm(ks[24], (N_ODD, C_WIDTH, D_MODEL), C_WIDTH ** -0.5),
    }


def reference(x_prompt, x_sample, state_a, cache_b_kv, cache_c_kv, c, c_ctx, ada_w, ada_b, norm1_g, norm2_g,
              final_g, mlp_w1, mlp_w2, ev_w_in, a_conv, a_log, a_dt_bias, a_norm_g, b_sink, ev_w_out,
              od_w_in, c_qnorm_g, c_knorm_g, od_w_out):
    xp, xs = x_prompt, x_sample
    cond_p = c_ctx[None, :]
    new_a, new_b, new_c = [], [], []
    for l in range(DEPTH):
        i = l // 2
        mp = _adaln(cond_p, ada_w[l], ada_b[l])
        ms = _adaln(c, ada_w[l], ada_b[l])
        hp = _modulate(xp, norm1_g[l], mp[0], mp[1])
        hs = _modulate(xs, norm1_g[l], ms[0], ms[1])
        if l % 2 == 0:
            ev = (ev_w_in[i], a_conv[i], a_log[i], a_dt_bias[i], a_norm_g[i], b_sink[i], ev_w_out[i])
            op, sa, kvb = _even_mixer(hp, *ev, None, None)
            os_, _, _ = _even_mixer(hs, *ev, state_a[:, i], cache_b_kv[:, i])
            new_a.append(sa)
            new_b.append(kvb)
        else:
            od = (od_w_in[i], c_qnorm_g[i], c_knorm_g[i], od_w_out[i])
            op, kvc = _odd_mixer(hp, *od, None)
            os_, _ = _odd_mixer(hs, *od, cache_c_kv[:, i])
            new_c.append(kvc)
        xp = xp + mp[2] * op
        xs = xs + ms[2] * os_
        xp = xp + mp[5] * _mlp(_modulate(xp, norm2_g[l], mp[3], mp[4]), mlp_w1[l], mlp_w2[l])
        xs = xs + ms[5] * _mlp(_modulate(xs, norm2_g[l], ms[3], ms[4]), mlp_w1[l], mlp_w2[l])
    y_prompt = _rms(xp, final_g)
    y_sample = _rms(xs, final_g)
    return (y_prompt, y_sample, jnp.stack(new_a, axis=1), jnp.stack(new_b, axis=1), jnp.stack(new_c, axis=1))
```

```python
import functools

import numpy as np
import jax
import jax.numpy as jnp
from jax import lax
from jax.experimental import pallas as pl
from jax.experimental.pallas import tpu as pltpu

F32 = jnp.float32
BF16 = jnp.bfloat16
HIGHEST = lax.Precision.HIGHEST

HEAD_DIM = 128
GRID_W = 64
A_HEADS = 4
CHUNK = 64
B_HEADS = 4
B_KV_HEADS = 2
WINDOW = 128
C_HEADS = 8
C_KV_HEADS = 2
ROT_FREQS = HEAD_DIM // 4
ROPE_THETA = 10000.0
EPS = 1e-6
NEG = -1e30
A_WIDTH = A_HEADS * HEAD_DIM
B_WIDTH = B_HEADS * HEAD_DIM
C_WIDTH = C_HEADS * HEAD_DIM

LANES = 128
VMEM_LIMIT_BYTES = 56 * 1024 * 1024

EV_GATE_OFF = 3 * A_WIDTH
EV_QB_OFF = 4 * A_WIDTH
EV_KVB_OFF = EV_QB_OFF + B_WIDTH
EV_BG_OFF = EV_KVB_OFF + 2 * B_KV_HEADS * HEAD_DIM
EV_COLS = EV_BG_OFF + LANES
OD_K_OFF = C_WIDTH
OD_V_OFF = C_WIDTH + C_KV_HEADS * HEAD_DIM


def _sigmoid(x):
    return 1.0 / (1.0 + jnp.exp(-x))


def _silu(x):
    return x * _sigmoid(x)


def _softplus(x):
    return jnp.maximum(x, 0.0) + jnp.log1p(jnp.exp(-jnp.abs(x)))


def _rms_rows(x, g):
    return x * lax.rsqrt(jnp.mean(x * x, axis=-1, keepdims=True) + EPS) * g


def _dot(a, b, precision=None):
    return jnp.dot(a, b, preferred_element_type=F32, precision=precision)


def _dot_nt(a, b, precision=None):
    return lax.dot_general(a, b, (((1,), (1,)), ((), ())), preferred_element_type=F32, precision=precision)


def _dot_tn(a, b, precision=None):
    return lax.dot_general(a, b, (((0,), (0,)), ((), ())), preferred_element_type=F32, precision=precision)


def _cond_row(i, n_prompt_tiles, tiles_per_sample):
    return jnp.where(i < n_prompt_tiles, 0, 1 + (i - n_prompt_tiles) // tiles_per_sample)


def _adaln_kernel(cond_ref, w_ref, b_ref, o_ref):
    s = _silu(cond_ref[...])
    o_ref[0] = _dot(s, w_ref[0], HIGHEST) + b_ref[0]


def _adaln(cond, ada_w, ada_b, tn=1024):
    depth, d, n = ada_w.shape
    rows = cond.shape[0]
    return pl.pallas_call(
        _adaln_kernel,
        out_shape=jax.ShapeDtypeStruct((depth, rows, n), F32),
        grid=(depth, n // tn),
        in_specs=[
            pl.BlockSpec((rows, d), lambda l, j: (0, 0)),
            pl.BlockSpec((1, d, tn), lambda l, j: (l, 0, j)),
            pl.BlockSpec((1, 1, tn), lambda l, j: (l, 0, j)),
        ],
        out_specs=pl.BlockSpec((1, rows, tn), lambda l, j: (l, 0, j)),
        compiler_params=pltpu.CompilerParams(dimension_semantics=("arbitrary", "arbitrary"),
                                             vmem_limit_bytes=VMEM_LIMIT_BYTES),
        name="adaln",
    )(cond, ada_w, ada_b.reshape(depth, 1, n))


def _inproj_kernel(x_ref, mod_ref, g_ref, w_ref, o_ref):
    h = _rms_rows(x_ref[...], g_ref[...]) * (1.0 + mod_ref[1:2, :]) + mod_ref[0:1, :]
    o_ref[...] = _dot(h.astype(BF16), w_ref[...])


def _in_proj(x, mod_l, g, w, n_prompt, sample_len, tm=512):
    n_tok, d = x.shape
    n = w.shape[1]
    row = functools.partial(_cond_row, n_prompt_tiles=n_prompt // tm, tiles_per_sample=sample_len // tm)
    return pl.pallas_call(
        _inproj_kernel,
        out_shape=jax.ShapeDtypeStruct((n_tok, n), F32),
        grid=(n_tok // tm,),
        in_specs=[
            pl.BlockSpec((tm, d), lambda i: (i, 0)),
            pl.BlockSpec((None, 6, d), lambda i: (row(i), 0, 0)),
            pl.BlockSpec((1, d), lambda i: (0, 0)),
            pl.BlockSpec((d, n), lambda i: (0, 0)),
        ],
        out_specs=pl.BlockSpec((tm, n), lambda i: (i, 0)),
        compiler_params=pltpu.CompilerParams(dimension_semantics=("arbitrary",),
                                             vmem_limit_bytes=VMEM_LIMIT_BYTES),
        name="in_proj",
    )(x, mod_l, g.reshape(1, d), w)


def _unit_tri_inverse(strict):
    n = strict.shape[0]
    r = lax.broadcasted_iota(jnp.int32, (n, n), 0)
    c = lax.broadcasted_iota(jnp.int32, (n, n), 1)
    p = -strict
    t = jnp.where(r == c, 1.0, 0.0) + p
    k = 1
    while 2 * k < n:
        p = _dot(p, p, HIGHEST)
        t = t + _dot(t, p, HIGHEST)
        k *= 2
    return t


def _gdn_kernel(q_ref, k_ref, v_ref, gate_ref, bg_ref, cwq_ref, cwk_ref, cwv_ref, arow_ref, dtrow_ref,
                ng_ref, s0_ref, o_ref, sout_ref,
                qn_ref, kn_ref, vn_ref, beta_ref, gc_ref, gcrow_ref,
                u_ref, w_ref, qe_ref, kd_ref, qk_ref, eg_ref, oacc_ref, *, seq_len):
    L = seq_len
    nc = L // CHUNK
    h = pl.program_id(1)
    row_idx = lax.broadcasted_iota(jnp.int32, (L, LANES), 0)
    lane_idx = lax.broadcasted_iota(jnp.int32, (L, LANES), 1)

    def conv_silu(x_ref, cw_ref):
        x = x_ref[...]
        prev = jnp.where(row_idx == 0, 0.0, pltpu.roll(x, 1, 0))
        nxt = jnp.where(row_idx == L - 1, 0.0, pltpu.roll(x, L - 1, 0))
        return _silu(prev * cw_ref[0:1, :] + x * cw_ref[1:2, :] + nxt * cw_ref[2:3, :])

    def l2norm(x):
        return x * lax.rsqrt(jnp.sum(x * x, axis=-1, keepdims=True) + EPS)

    qn_ref[...] = l2norm(conv_silu(q_ref, cwq_ref)) * (HEAD_DIM ** -0.5)
    kn_ref[...] = l2norm(conv_silu(k_ref, cwk_ref))
    vn_ref[...] = conv_silu(v_ref, cwv_ref)

    bg = bg_ref[...]
    beta_all = _sigmoid(bg)
    g_all = -jnp.exp(arow_ref[...]) * _softplus(bg + dtrow_ref[...])

    def pick(x, col):
        return jnp.sum(jnp.where(lane_idx == col, x, 0.0), axis=1, keepdims=True)

    beta_ref[0] = jnp.broadcast_to(pick(beta_all, h), (L, LANES))
    beta_ref[1] = jnp.broadcast_to(pick(beta_all, A_HEADS + h), (L, LANES))
    g_f = jnp.broadcast_to(pick(g_all, 2 * A_HEADS + h), (L, LANES))
    g_b = jnp.broadcast_to(pick(g_all, 3 * A_HEADS + h), (L, LANES))

    cr = lax.broadcasted_iota(jnp.int32, (CHUNK, CHUNK), 0)
    cc = lax.broadcasted_iota(jnp.int32, (CHUNK, CHUNK), 1)
    low_incl = cr >= cc
    up_incl = cr <= cc
    tri_low = jnp.where(low_incl, 1.0, 0.0)
    tri_up = jnp.where(up_incl, 1.0, 0.0)
    for n in range(nc):
        sl = slice(n * CHUNK, (n + 1) * CHUNK)
        gcf = _dot(tri_low, g_f[sl], HIGHEST)
        gcb = _dot(tri_up, g_b[sl], HIGHEST)
        gc_ref[0, sl, :] = gcf
        gc_ref[1, sl, :] = gcb
        gcrow_ref[0, n] = jnp.transpose(jnp.concatenate([gcf, gcf], axis=0))[0:8, 0:CHUNK]
        gcrow_ref[1, n] = jnp.transpose(jnp.concatenate([gcb, gcb], axis=0))[0:8, 0:CHUNK]

    def prep(n, carry):
        off = pl.multiple_of(n * CHUNK, CHUNK)
        qc = qn_ref[pl.ds(off, CHUNK), :]
        kc = kn_ref[pl.ds(off, CHUNK), :]
        vc = vn_ref[pl.ds(off, CHUNK), :]
        for d in range(2):
            incl = low_incl if d == 0 else up_incl
            strict = (cr > cc) if d == 0 else (cr < cc)
            beta = beta_ref[d, pl.ds(off, CHUNK), :]
            gc = gc_ref[d, pl.ds(off, CHUNK), :]
            gcrow = gcrow_ref[d, n][0:1, :]
            diff = gc[:, 0:CHUNK] - gcrow
            decay = jnp.where(incl, jnp.exp(jnp.where(incl, diff, 0.0)), 0.0)
            kb = kc * beta
            lower = jnp.where(strict, _dot_nt(kb, kc) * decay, 0.0)
            t = _unit_tri_inverse(lower)
            u_ref[d, n] = _dot(t, vc * beta)
            w_ref[d, n] = _dot(t, kb * jnp.exp(gc))
            qk_ref[d, n] = jnp.where(incl, _dot_nt(qc, kc) * decay, 0.0)
            qe_ref[d, n] = qc * jnp.exp(gc)
            glast = gc[CHUNK - 1:CHUNK, :] if d == 0 else gc[0:1, :]
            kd_ref[d, n] = kc * jnp.exp(glast - gc)
            eg_ref[d, n] = jnp.broadcast_to(jnp.exp(glast), (8, LANES))
        return carry

    lax.fori_loop(0, nc, prep, 0)

    def scan(i, carry):
        s_f, s_b = carry
        outs = []
        for d, s, n in ((0, s_f, i), (1, s_b, nc - 1 - i)):
            v_new = u_ref[d, n] - _dot(w_ref[d, n], s)
            o = _dot(qe_ref[d, n], s) + _dot(qk_ref[d, n], v_new)
            s = s * eg_ref[d, n][0:1, :] + _dot_tn(kd_ref[d, n], v_new)
            off = pl.multiple_of(n * CHUNK, CHUNK)
            oacc_ref[d, pl.ds(off, CHUNK), :] = o
            outs.append(s)
        return tuple(outs)

    s_f, s_b = lax.fori_loop(0, nc, scan, (s0_ref[0, 0], s0_ref[1, 0]))
    sout_ref[0, 0] = s_f
    sout_ref[1, 0] = s_b

    o = oacc_ref[0] + oacc_ref[1]
    o = _rms_rows(o, ng_ref[...]) * _silu(gate_ref[...])
    o_ref[...] = o.astype(o_ref.dtype)


def _gdn(proj, conv_w, arow, dtrow, norm_g, s0, row_block0, n_seq, seq_len):
    L = seq_len
    nc = L // CHUNK
    n_tok = proj.shape[0]
    qcol = lambda base: (lambda b, h: (row_block0 + b, base + h))
    wcol = lambda base: (lambda b, h: (0, base + h))
    blk = lambda base: pl.BlockSpec((L, HEAD_DIM), qcol(base))
    kern = functools.partial(_gdn_kernel, seq_len=L)
    out, s_out = pl.pallas_call(
        kern,
        out_shape=(jax.ShapeDtypeStruct((n_seq * L, A_WIDTH), BF16),
                   jax.ShapeDtypeStruct((n_seq, 2, A_HEADS, HEAD_DIM, HEAD_DIM), F32)),
        grid=(n_seq, A_HEADS),
        in_specs=[
            blk(0), blk(A_HEADS), blk(2 * A_HEADS), blk(3 * A_HEADS),
            pl.BlockSpec((L, LANES), lambda b, h: (row_block0 + b, EV_BG_OFF // LANES)),
            pl.BlockSpec((3, HEAD_DIM), wcol(0)), pl.BlockSpec((3, HEAD_DIM), wcol(A_HEADS)),
            pl.BlockSpec((3, HEAD_DIM), wcol(2 * A_HEADS)),
            pl.BlockSpec((1, LANES), lambda b, h: (0, 0)),
            pl.BlockSpec((1, LANES), lambda b, h: (0, 0)),
            pl.BlockSpec((1, HEAD_DIM), lambda b, h: (0, 0)),
            pl.BlockSpec((None, 2, 1, HEAD_DIM, HEAD_DIM), lambda b, h: (b, 0, h, 0, 0)),
        ],
        out_specs=(pl.BlockSpec((L, HEAD_DIM), lambda b, h: (b, h)),
                   pl.BlockSpec((None, 2, 1, HEAD_DIM, HEAD_DIM), lambda b, h: (b, 0, h, 0, 0))),
        scratch_shapes=[
            pltpu.VMEM((L, HEAD_DIM), F32), pltpu.VMEM((L, HEAD_DIM), F32), pltpu.VMEM((L, HEAD_DIM), F32),
            pltpu.VMEM((2, L, LANES), F32), pltpu.VMEM((2, L, LANES), F32),
            pltpu.VMEM((2, nc, 8, CHUNK), F32),
            pltpu.VMEM((2, nc, CHUNK, HEAD_DIM), F32), pltpu.VMEM((2, nc, CHUNK, HEAD_DIM), F32),
            pltpu.VMEM((2, nc, CHUNK, HEAD_DIM), F32), pltpu.VMEM((2, nc, CHUNK, HEAD_DIM), F32),
            pltpu.VMEM((2, nc, CHUNK, CHUNK), F32), pltpu.VMEM((2, nc, 8, LANES), F32),
            pltpu.VMEM((2, L, HEAD_DIM), F32),
        ],
        compiler_params=pltpu.CompilerParams(dimension_semantics=("arbitrary", "arbitrary"),
                                             vmem_limit_bytes=VMEM_LIMIT_BYTES),
        name="gdn",
    )(proj, proj, proj, proj, proj, conv_w, conv_w, conv_w, arow, dtrow, norm_g.reshape(1, HEAD_DIM), s0)
    return out, s_out


def _rope_rows(x, cos, sin_signed):
    lane = lax.broadcasted_iota(jnp.int32, x.shape, 1)
    swapped = jnp.where((lane % (2 * ROT_FREQS)) < ROT_FREQS,
                        pltpu.roll(x, HEAD_DIM - ROT_FREQS, 1), pltpu.roll(x, ROT_FREQS, 1))
    return x * cos + swapped * sin_signed


def _attn_kernel(*refs, tq, seq_len, group, n_ctx, use_sink, window, qk_norm, rope, emit_k):
    it = iter(refs)
    q_ref, k_ref, v_ref = next(it), next(it), next(it)
    ck_ref = cv_ref = cos_ref = sin_ref = qg_ref = kg_ref = sink_ref = kn_out_ref = None
    if n_ctx:
        ck_ref, cv_ref = next(it), next(it)
    if rope:
        cos_ref, sin_ref = next(it), next(it)
    if qk_norm:
        qg_ref, kg_ref = next(it), next(it)
    if use_sink:
        sink_ref = next(it)
    o_ref = next(it)
    if emit_k:
        kn_out_ref = next(it)
    kbf_ref, vbf_ref = next(it), next(it)
    ckbf_ref = cvbf_ref = None
    if n_ctx:
        ckbf_ref, cvbf_ref = next(it), next(it)

    L = seq_len
    kvh = pl.program_id(1)
    qi = pl.program_id(2)

    @pl.when(qi == 0)
    def _():
        k = k_ref[...]
        if qk_norm:
            k = _rms_rows(k, kg_ref[...])
        if emit_k:
            kn_out_ref[...] = k
        if rope:
            k = _rope_rows(k, cos_ref[...], sin_ref[...])
        kbf_ref[...] = k.astype(BF16)
        vbf_ref[...] = v_ref[...].astype(BF16)
        if n_ctx:
            ckbf_ref[...] = ck_ref[...].astype(BF16)
            cvbf_ref[...] = cv_ref[...].astype(BF16)

    q0 = pl.multiple_of(qi * tq, tq)
    if window:
        wk = tq + 2 * window
        ws = pl.multiple_of(jnp.clip(q0 - window, 0, L - wk), LANES)
        k_loc = kbf_ref[pl.ds(ws, wk), :]
        v_loc = vbf_ref[pl.ds(ws, wk), :]
        qpos = q0 + lax.broadcasted_iota(jnp.int32, (tq, wk), 0)
        kpos = ws + lax.broadcasted_iota(jnp.int32, (tq, wk), 1)
        valid = jnp.abs(qpos - kpos) <= window
    else:
        k_loc = kbf_ref[...]
        v_loc = vbf_ref[...]

    for g in range(group):
        q = q_ref[:, g * HEAD_DIM:(g + 1) * HEAD_DIM]
        if qk_norm:
            q = _rms_rows(q, qg_ref[...])
        if rope:
            q = _rope_rows(q, cos_ref[pl.ds(q0, tq), :], sin_ref[pl.ds(q0, tq), :])
        q = (q * (HEAD_DIM ** -0.5)).astype(BF16)
        s_loc = _dot_nt(q, k_loc)
        if window:
            s_loc = jnp.where(valid, s_loc, NEG)
        m = jnp.max(s_loc, axis=-1, keepdims=True)
        if n_ctx:
            s_ctx = _dot_nt(q, ckbf_ref[...])
            m = jnp.maximum(m, jnp.max(s_ctx, axis=-1, keepdims=True))
        if use_sink:
            sink = sink_ref[kvh * group + g]
            m = jnp.maximum(m, sink)
        p_loc = jnp.exp(s_loc - m)
        denom = jnp.sum(p_loc, axis=-1, keepdims=True)
        acc = _dot(p_loc.astype(BF16), v_loc)
        if n_ctx:
            p_ctx = jnp.exp(s_ctx - m)
            denom = denom + jnp.sum(p_ctx, axis=-1, keepdims=True)
            acc = acc + _dot(p_ctx.astype(BF16), cvbf_ref[...])
        if use_sink:
            denom = denom + jnp.exp(sink - m)
        o_ref[:, g * HEAD_DIM:(g + 1) * HEAD_DIM] = (acc / denom).astype(o_ref.dtype)


def _attention(proj, *, q_off, k_off, v_off, n_heads, n_kv, row_block0, n_seq, seq_len, tq,
               ctx_k=None, ctx_v=None, cos=None, sin=None, q_gain=None, k_gain=None, sink=None,
               window=0, emit_k=False):
    L = seq_len
    group = n_heads // n_kv
    n_ctx = 0 if ctx_k is None else ctx_k.shape[1]
    rope = cos is not None
    qk_norm = q_gain is not None
    use_sink = sink is not None
    nq = L // tq
    qb, kb, vb = q_off // (group * HEAD_DIM), k_off // HEAD_DIM, v_off // HEAD_DIM
    in_specs = [
        pl.BlockSpec((tq, group * HEAD_DIM), lambda b, h, i: ((row_block0 + b) * nq + i, qb + h)),
        pl.BlockSpec((L, HEAD_DIM), lambda b, h, i: (row_block0 + b, kb + h)),
        pl.BlockSpec((L, HEAD_DIM), lambda b, h, i: (row_block0 + b, vb + h)),
    ]
    args = [proj, proj, proj]
    scratch = [pltpu.VMEM((L, HEAD_DIM), BF16), pltpu.VMEM((L, HEAD_DIM), BF16)]
    if n_ctx:
        in_specs += [pl.BlockSpec((None, n_ctx, HEAD_DIM), lambda b, h, i: (b, 0, h))] * 2
        args += [ctx_k, ctx_v]
        scratch += [pltpu.VMEM((n_ctx, HEAD_DIM), BF16), pltpu.VMEM((n_ctx, HEAD_DIM), BF16)]
    if rope:
        in_specs += [pl.BlockSpec((L, HEAD_DIM), lambda b, h, i: (0, 0))] * 2
        args += [cos, sin]
    if qk_norm:
        in_specs += [pl.BlockSpec((1, HEAD_DIM), lambda b, h, i: (0, 0))] * 2
        args += [q_gain.reshape(1, HEAD_DIM), k_gain.reshape(1, HEAD_DIM)]
    if use_sink:
        in_specs += [pl.BlockSpec(memory_space=pltpu.SMEM)]
        args += [sink]
    out_shape = [jax.ShapeDtypeStruct((n_seq * L, n_heads * HEAD_DIM), BF16)]
    out_specs = [pl.BlockSpec((tq, group * HEAD_DIM), lambda b, h, i: (b * nq + i, h))]
    if emit_k:
        out_shape += [jax.ShapeDtypeStruct((n_seq * L, n_kv * HEAD_DIM), F32)]
        out_specs += [pl.BlockSpec((L, HEAD_DIM), lambda b, h, i: (b, h))]
    kern = functools.partial(_attn_kernel, tq=tq, seq_len=L, group=group, n_ctx=n_ctx, use_sink=use_sink,
                             window=window, qk_norm=qk_norm, rope=rope, emit_k=emit_k)
    res = pl.pallas_call(
        kern,
        out_shape=tuple(out_shape),
        grid=(n_seq, n_kv, nq),
        in_specs=in_specs,
        out_specs=tuple(out_specs),
        scratch_shapes=scratch,
        compiler_params=pltpu.CompilerParams(dimension_semantics=("arbitrary", "arbitrary", "arbitrary"),
                                             vmem_limit_bytes=VMEM_LIMIT_BYTES),
        name="attention",
    )(*args)
    return res if emit_k else res[0]


def _out_mlp_kernel(*refs, n_parts, final):
    it = iter(refs)
    x_ref = next(it)
    a_refs = [next(it) for _ in range(n_parts)]
    wo_refs = [next(it) for _ in range(n_parts)]
    mod_ref, g2_ref, w1_ref, w2_ref = next(it), next(it), next(it), next(it)
    fg_ref = next(it) if final else None
    o_ref = next(it)
    x1_ref, h2_ref, acc_ref = next(it), next(it), next(it)
    j = pl.program_id(1)

    @pl.when(j == 0)
    def _():
        mix = _dot(a_refs[0][...], wo_refs[0][...])
        for a_ref, wo_ref in zip(a_refs[1:], wo_refs[1:]):
            mix = mix + _dot(a_ref[...], wo_ref[...])
        x1 = x_ref[...] + mod_ref[2:3, :] * mix
        x1_ref[...] = x1
        h2 = _rms_rows(x1, g2_ref[...]) * (1.0 + mod_ref[4:5, :]) + mod_ref[3:4, :]
        h2_ref[...] = h2.astype(BF16)
        acc_ref[...] = jnp.zeros_like(acc_ref)

    hid = jnp.maximum(_dot(h2_ref[...], w1_ref[...]), 0.0)
    acc_ref[...] += _dot((hid * hid).astype(BF16), w2_ref[...])

    @pl.when(j == pl.num_programs(1) - 1)
    def _():
        y = x1_ref[...] + mod_ref[5:6, :] * acc_ref[...]
        if final:
            y = _rms_rows(y, fg_ref[...])
        o_ref[...] = y


def _out_mlp(x, parts, wo_parts, mod_l, g2, w1, w2, n_prompt, sample_len, final_g=None, tm=512, tf=1024):
    n_tok, d = x.shape
    d_ff = w1.shape[1]
    final = final_g is not None
    row = functools.partial(_cond_row, n_prompt_tiles=n_prompt // tm, tiles_per_sample=sample_len // tm)
    in_specs = [pl.BlockSpec((tm, d), lambda i, j: (i, 0))]
    in_specs += [pl.BlockSpec((tm, p.shape[1]), lambda i, j: (i, 0)) for p in parts]
    in_specs += [pl.BlockSpec(w.shape, lambda i, j: (0, 0)) for w in wo_parts]
    in_specs += [
        pl.BlockSpec((None, 6, d), lambda i, j: (row(i), 0, 0)),
        pl.BlockSpec((1, d), lambda i, j: (0, 0)),
        pl.BlockSpec((d, tf), lambda i, j: (0, j)),
        pl.BlockSpec((tf, d), lambda i, j: (j, 0)),
    ]
    args = [x, *parts, *wo_parts, mod_l, g2.reshape(1, d), w1, w2]
    if final:
        in_specs += [pl.BlockSpec((1, d), lambda i, j: (0, 0))]
        args += [final_g.reshape(1, d)]
    kern = functools.partial(_out_mlp_kernel, n_parts=len(parts), final=final)
    return pl.pallas_call(
        kern,
        out_shape=jax.ShapeDtypeStruct((n_tok, d), F32),
        grid=(n_tok // tm, d_ff // tf),
        in_specs=in_specs,
        out_specs=pl.BlockSpec((tm, d), lambda i, j: (i, 0)),
        scratch_shapes=[pltpu.VMEM((tm, d), F32), pltpu.VMEM((tm, d), BF16), pltpu.VMEM((tm, d), F32)],
        compiler_params=pltpu.CompilerParams(dimension_semantics=("arbitrary", "arbitrary"),
                                             vmem_limit_bytes=VMEM_LIMIT_BYTES),
        name="out_mlp",
    )(*args)


def _rope_tables(seq_len):
    rows = seq_len // GRID_W
    inv = ROPE_THETA ** (-jnp.arange(ROT_FREQS, dtype=F32) / ROT_FREQS)
    ang_r = jnp.repeat(jnp.arange(rows, dtype=F32), GRID_W)[:, None] * inv
    ang_c = jnp.tile(jnp.arange(GRID_W, dtype=F32), rows)[:, None] * inv
    cos = jnp.concatenate([jnp.cos(ang_r), jnp.cos(ang_r), jnp.cos(ang_c), jnp.cos(ang_c)], axis=-1)
    sin = jnp.concatenate([-jnp.sin(ang_r), jnp.sin(ang_r), -jnp.sin(ang_c), jnp.sin(ang_c)], axis=-1)
    return cos.astype(F32), sin.astype(F32)


def kernel(x_prompt, x_sample, state_a, cache_b_kv, cache_c_kv, c, c_ctx, ada_w, ada_b, norm1_g, norm2_g,
           final_g, mlp_w1, mlp_w2, ev_w_in, a_conv, a_log, a_dt_bias, a_norm_g, b_sink, ev_w_out,
           od_w_in, c_qnorm_g, c_knorm_g, od_w_out):
    batch, seq, d = x_prompt.shape
    dec_batch, dec_seq, _ = x_sample.shape
    depth = ada_w.shape[0]
    n_prompt, n_sample = batch * seq, dec_batch * dec_seq
    past = cache_b_kv.shape[3]

    x = jnp.concatenate([x_prompt.reshape(n_prompt, d), x_sample.reshape(n_sample, d)], axis=0)
    cond = jnp.concatenate([c_ctx[None, :], c, jnp.zeros((8 - 1 - dec_batch, d), F32)], axis=0)
    mod = _adaln(cond, ada_w, ada_b).reshape(depth, 8, 6, d)

    cos, sin = _rope_tables(dec_seq)
    s_blk0 = n_prompt // dec_seq
    zero_state = jnp.zeros((batch, 2, A_HEADS, HEAD_DIM, HEAD_DIM), F32)
    lane_pad = lambda v: jnp.zeros((1, LANES), F32).at[0, 2 * A_HEADS:4 * A_HEADS].set(v.reshape(-1))

    new_a, new_b, new_c = [], [], []
    for l in range(depth):
        i = l // 2
        fin = final_g if l == depth - 1 else None
        w1 = mlp_w1[l].astype(BF16)
        w2 = mlp_w2[l].astype(BF16)
        if l % 2 == 0:
            w_in = ev_w_in[i]
            bg_off = 4 * A_WIDTH
            qb_off = bg_off + 4 * A_HEADS
            w_in = jnp.concatenate([w_in[:, :bg_off], w_in[:, qb_off:], w_in[:, bg_off:qb_off],
                                    jnp.zeros((d, LANES - 4 * A_HEADS), F32)], axis=1).astype(BF16)
            proj = _in_proj(x, mod[l], norm1_g[l], w_in, n_prompt, dec_seq)
            arow, dtrow = lane_pad(a_log[i]), lane_pad(a_dt_bias[i])
            oa_p, st_p = _gdn(proj, a_conv[i], arow, dtrow, a_norm_g[i], zero_state, 0, batch, seq)
            oa_s, _ = _gdn(proj, a_conv[i], arow, dtrow, a_norm_g[i], state_a[:, i], s_blk0, dec_batch, dec_seq)
            kv_cols = B_KV_HEADS * HEAD_DIM
            ob_p = _attention(proj, q_off=EV_QB_OFF, k_off=EV_KVB_OFF, v_off=EV_KVB_OFF + kv_cols,
                              n_heads=B_HEADS, n_kv=B_KV_HEADS, row_block0=0, n_seq=batch, seq_len=seq,
                              tq=seq, sink=b_sink[i])
            ob_s = _attention(proj, q_off=EV_QB_OFF, k_off=EV_KVB_OFF, v_off=EV_KVB_OFF + kv_cols,
                              n_heads=B_HEADS, n_kv=B_KV_HEADS, row_block0=s_blk0, n_seq=dec_batch,
                              seq_len=dec_seq, tq=256,
                              ctx_k=cache_b_kv[:, i, 0].reshape(dec_batch, past, kv_cols),
                              ctx_v=cache_b_kv[:, i, 1].reshape(dec_batch, past, kv_cols),
                              cos=cos, sin=sin, sink=b_sink[i], window=WINDOW)
            new_a.append(st_p)
            kvb = proj[:n_prompt, EV_KVB_OFF:EV_KVB_OFF + 2 * kv_cols]
            new_b.append(jnp.transpose(kvb.reshape(batch, seq, 2, B_KV_HEADS, HEAD_DIM), (0, 2, 1, 3, 4)))
            parts = [jnp.concatenate([oa_p, oa_s], axis=0), jnp.concatenate([ob_p, ob_s], axis=0)]
            w_out = ev_w_out[i].astype(BF16)
            wo_parts = [w_out[:A_WIDTH], w_out[A_WIDTH:]]
        else:
            proj = _in_proj(x, mod[l], norm1_g[l], od_w_in[i].astype(BF16), n_prompt, dec_seq)
            kv_cols = C_KV_HEADS * HEAD_DIM
            oc_p, kn_p = _attention(proj, q_off=0, k_off=OD_K_OFF, v_off=OD_V_OFF, n_heads=C_HEADS,
                                    n_kv=C_KV_HEADS, row_block0=0, n_seq=batch, seq_len=seq, tq=seq,
                                    q_gain=c_qnorm_g[i], k_gain=c_knorm_g[i], emit_k=True)
            oc_s = _attention(proj, q_off=0, k_off=OD_K_OFF, v_off=OD_V_OFF, n_heads=C_HEADS,
                              n_kv=C_KV_HEADS, row_block0=s_blk0, n_seq=dec_batch, seq_len=dec_seq, tq=256,
                              ctx_k=cache_c_kv[:, i, 0].reshape(dec_batch, past, kv_cols),
                              ctx_v=cache_c_kv[:, i, 1].reshape(dec_batch, past, kv_cols),
                              cos=cos, sin=sin, q_gain=c_qnorm_g[i], k_gain=c_knorm_g[i])
            v_p = proj[:n_prompt, OD_V_OFF:OD_V_OFF + kv_cols]
            new_c.append(jnp.stack([kn_p.reshape(batch, seq, C_KV_HEADS, HEAD_DIM),
                                    v_p.reshape(batch, seq, C_KV_HEADS, HEAD_DIM)], axis=1))
            parts = [jnp.concatenate([oc_p, oc_s], axis=0)]
            wo_parts = [od_w_out[i].astype(BF16)]
        x = _out_mlp(x, parts, wo_parts, mod[l], norm2_g[l], w1, w2, n_prompt, dec_seq, final_g=fin)

    y_prompt = x[:n_prompt].reshape(batch, seq, d)
    y_sample = x[n_prompt:].reshape(dec_batch, dec_seq, d)
    return (y_prompt, y_sample, jnp.stack(new_a, axis=1), jnp.stack(new_b, axis=1), jnp.stack(new_c, axis=1))
```

```python
import functools

import numpy as np
import jax
import jax.numpy as jnp
from jax import lax
from jax.experimental import pallas as pl
from jax.experimental.pallas import tpu as pltpu

F32 = jnp.float32
BF16 = jnp.bfloat16
HIGHEST = lax.Precision.HIGHEST

HEAD_DIM = 128
GRID_W = 64
A_HEADS = 4
CHUNK = 64
B_HEADS = 4
B_KV_HEADS = 2
WINDOW = 128
C_HEADS = 8
C_KV_HEADS = 2
ROT_FREQS = HEAD_DIM // 4
ROPE_THETA = 10000.0
EPS = 1e-6
NEG = -1e30
A_WIDTH = A_HEADS * HEAD_DIM
B_WIDTH = B_HEADS * HEAD_DIM
C_WIDTH = C_HEADS * HEAD_DIM

LANES = 128
VMEM_LIMIT_BYTES = 56 * 1024 * 1024

EV_GATE_OFF = 3 * A_WIDTH
EV_QB_OFF = 4 * A_WIDTH
EV_KVB_OFF = EV_QB_OFF + B_WIDTH
EV_BG_OFF = EV_KVB_OFF + 2 * B_KV_HEADS * HEAD_DIM
EV_COLS = EV_BG_OFF + LANES
OD_K_OFF = C_WIDTH
OD_V_OFF = C_WIDTH + C_KV_HEADS * HEAD_DIM


def _sigmoid(x):
    return 1.0 / (1.0 + jnp.exp(-x))


def _silu(x):
    return x * _sigmoid(x)


def _softplus(x):
    return jnp.maximum(x, 0.0) + jnp.log1p(jnp.exp(-jnp.abs(x)))


def _rms_rows(x, g):
    return x * lax.rsqrt(jnp.mean(x * x, axis=-1, keepdims=True) + EPS) * g


def _dot(a, b, precision=None):
    return jnp.dot(a, b, preferred_element_type=F32, precision=precision)


def _dot_nt(a, b, precision=None):
    return lax.dot_general(a, b, (((1,), (1,)), ((), ())), preferred_element_type=F32, precision=precision)


def _cond_row(i, n_prompt_tiles, tiles_per_sample):
    return jnp.where(i < n_prompt_tiles, 0, 1 + (i - n_prompt_tiles) // tiles_per_sample)


def _adaln_kernel(cond_ref, w_ref, b_ref, o_ref):
    s = _silu(cond_ref[...])
    o_ref[0] = _dot(s, w_ref[0], HIGHEST) + b_ref[0]


def _adaln(cond, ada_w, ada_b, tn=1024):
    depth, d, n = ada_w.shape
    rows = cond.shape[0]
    return pl.pallas_call(
        _adaln_kernel,
        out_shape=jax.ShapeDtypeStruct((depth, rows, n), F32),
        grid=(depth, n // tn),
        in_specs=[
            pl.BlockSpec((rows, d), lambda l, j: (0, 0)),
            pl.BlockSpec((1, d, tn), lambda l, j: (l, 0, j)),
            pl.BlockSpec((1, 1, tn), lambda l, j: (l, 0, j)),
        ],
        out_specs=pl.BlockSpec((1, rows, tn), lambda l, j: (l, 0, j)),
        compiler_params=pltpu.CompilerParams(dimension_semantics=("arbitrary", "arbitrary"),
                                             vmem_limit_bytes=VMEM_LIMIT_BYTES),
        name="adaln",
    )(cond, ada_w, ada_b.reshape(depth, 1, n))


def _inproj_kernel(x_ref, mod_ref, g_ref, w_ref, o_ref):
    h = _rms_rows(x_ref[...], g_ref[...]) * (1.0 + mod_ref[1:2, :]) + mod_ref[0:1, :]
    o_ref[...] = _dot(h.astype(BF16), w_ref[...])


def _in_proj(x, mod_l, g, w, n_prompt, sample_len, tm=512):
    n_tok, d = x.shape
    n = w.shape[1]
    row = functools.partial(_cond_row, n_prompt_tiles=n_prompt // tm, tiles_per_sample=sample_len // tm)
    return pl.pallas_call(
        _inproj_kernel,
        out_shape=jax.ShapeDtypeStruct((n_tok, n), F32),
        grid=(n_tok // tm,),
        in_specs=[
            pl.BlockSpec((tm, d), lambda i: (i, 0)),
            pl.BlockSpec((None, 6, d), lambda i: (row(i), 0, 0)),
            pl.BlockSpec((1, d), lambda i: (0, 0)),
            pl.BlockSpec((d, n), lambda i: (0, 0)),
        ],
        out_specs=pl.BlockSpec((tm, n), lambda i: (i, 0)),
        compiler_params=pltpu.CompilerParams(dimension_semantics=("arbitrary",),
                                             vmem_limit_bytes=VMEM_LIMIT_BYTES),
        name="in_proj",
    )(x, mod_l, g.reshape(1, d), w)


def _split_bf16(x):
    hi = x.astype(BF16)
    lo = (x - hi.astype(F32)).astype(BF16)
    return hi, lo


def _gdn_kernel(q_ref, k_ref, v_ref, gate_ref, bg_ref, cwq_ref, cwk_ref, cwv_ref, arow_ref, dtrow_ref,
                ng_ref, s0_ref, o_ref, sout_ref,
                qn_ref, kn_ref, vn_ref, beta_ref, g_ref, s_ref, oacc_ref,
                mq_a, c_a, oo_a, eg_a, mq_b, c_b, oo_b, eg_b, *, seq_len, heads, steps):
    L = seq_len
    nc = L // CHUNK
    nblk = nc // steps
    h0 = pl.program_id(1) * heads
    row_idx = lax.broadcasted_iota(jnp.int32, (L, HEAD_DIM), 0)

    def conv_silu(x, cw):
        prev = jnp.where(row_idx == 0, 0.0, pltpu.roll(x, 1, 0))
        nxt = jnp.where(row_idx == L - 1, 0.0, pltpu.roll(x, L - 1, 0))
        return _silu(prev * cw[0:1, :] + x * cw[1:2, :] + nxt * cw[2:3, :])

    def l2norm(x):
        return x * lax.rsqrt(jnp.sum(x * x, axis=-1, keepdims=True) + EPS)

    for hh in range(heads):
        cols = slice(hh * HEAD_DIM, (hh + 1) * HEAD_DIM)
        qn_ref[:, cols] = l2norm(conv_silu(q_ref[:, cols], cwq_ref[:, cols])) * (HEAD_DIM ** -0.5)
        kn_ref[:, cols] = l2norm(conv_silu(k_ref[:, cols], cwk_ref[:, cols]))
        vn_ref[:, cols] = conv_silu(v_ref[:, cols], cwv_ref[:, cols])
        s_ref[hh, 0] = s0_ref[0, hh]
        s_ref[hh, 1] = s0_ref[1, hh]

    bg = bg_ref[...]
    beta_ref[...] = _sigmoid(bg)
    g_ref[...] = -jnp.exp(arow_ref[...]) * _softplus(bg + dtrow_ref[...])

    r64 = lax.broadcasted_iota(jnp.int32, (CHUNK, LANES), 0)
    l64 = lax.broadcasted_iota(jnp.int32, (CHUNK, LANES), 1)
    is_f = l64 < CHUNK
    cpos = jnp.where(is_f, l64, l64 - CHUNK)
    ahead = jnp.where(is_f, r64 - cpos, cpos - r64)
    incl = ahead >= 0
    strict = ahead > 0
    eye = jnp.where(r64 == cpos, 1.0, 0.0)
    is_f2 = lax.broadcasted_iota(jnp.int32, (2 * CHUNK, LANES), 1) < CHUNK
    cr = lax.broadcasted_iota(jnp.int32, (CHUNK, CHUNK), 0)
    cc = lax.broadcasted_iota(jnp.int32, (CHUNK, CHUNK), 1)
    tri_low = jnp.where(cr >= cc, 1.0, 0.0).astype(BF16)
    tri_up = jnp.where(cr <= cc, 1.0, 0.0).astype(BF16)
    zeros_c = jnp.zeros((CHUNK, HEAD_DIM), F32)

    def aligned(off):
        return off if isinstance(off, int) else pl.multiple_of(off, CHUNK)

    def pick(x, col):
        col_vals = jnp.sum(jnp.where(l64 == col, x, 0.0), axis=1, keepdims=True)
        return jnp.broadcast_to(col_vals, (CHUNK, LANES))

    def block_diag(xp):
        z = jnp.zeros_like(xp)
        return jnp.concatenate([jnp.where(is_f, xp, z), jnp.where(is_f, z, xp)], axis=0)

    def cumsum(tri, g):
        g_hi = g.astype(BF16)
        r1 = g - g_hi.astype(F32)
        g_mid = r1.astype(BF16)
        g_lo = (r1 - g_mid.astype(F32)).astype(BF16)
        s = _dot(tri, jnp.concatenate([g_hi, g_mid, g_lo], axis=1))
        return s[:, :LANES] + s[:, LANES:2 * LANES] + s[:, 2 * LANES:]

    def prep(j, hh, slot, si):
        mq_ref, c_ref, oo_ref, eg_ref = slot
        cols = slice(hh * HEAD_DIM, (hh + 1) * HEAD_DIM)
        off_f = aligned(j * CHUNK)
        off_b = aligned((nc - 1 - j) * CHUNK)
        q_f, k_f, v_f = (r[pl.ds(off_f, CHUNK), cols] for r in (qn_ref, kn_ref, vn_ref))
        q_b, k_b, v_b = (r[pl.ds(off_b, CHUNK), cols] for r in (qn_ref, kn_ref, vn_ref))
        head = h0 + hh
        beta_f = pick(beta_ref[pl.ds(off_f, CHUNK), :], head)
        beta_b = pick(beta_ref[pl.ds(off_b, CHUNK), :], A_HEADS + head)
        gc_f = cumsum(tri_low, pick(g_ref[pl.ds(off_f, CHUNK), :], 2 * A_HEADS + head))
        gc_b = cumsum(tri_up, pick(g_ref[pl.ds(off_b, CHUNK), :], 3 * A_HEADS + head))
        gcrow = jnp.transpose(jnp.concatenate([gc_f, gc_b], axis=0))[0:1, :]
        diff = jnp.where(is_f, gc_f, gc_b) - gcrow
        decay = jnp.where(incl, jnp.exp(jnp.where(incl, diff, 0.0)), 0.0)
        kb_f = k_f * beta_f
        kb_b = k_b * beta_b
        lhs = jnp.concatenate([jnp.concatenate([kb_f, kb_b], axis=1),
                               jnp.concatenate([q_f, q_b], axis=1)], axis=0).astype(BF16)
        rhs = jnp.concatenate([jnp.concatenate([k_f, zeros_c], axis=1),
                               jnp.concatenate([zeros_c, k_b], axis=1)], axis=0).astype(BF16)
        aq = _dot_nt(lhs, rhs)
        x = -jnp.where(strict, aq[:CHUNK] * decay, 0.0)
        qk = jnp.where(incl, aq[CHUNK:] * decay, 0.0)

        p_hi, p_lo = _split_bf16(x)
        p = _dot(jnp.concatenate([p_hi, p_lo], axis=0), block_diag(p_hi))
        p = p[:CHUNK] + p[CHUNK:] + _dot(p_hi, block_diag(p_lo))
        t = eye + x
        span = 2
        while span < CHUNK:
            last = 2 * span >= CHUNK
            p_hi, p_lo = _split_bf16(p)
            t_hi, t_lo = _split_bf16(t)
            lhs_p = jnp.concatenate([p_hi, p_lo], axis=0)
            if last:
                r = _dot(lhs_p, block_diag(t_hi))
                t = t + r[:CHUNK] + r[CHUNK:] + _dot(p_hi, block_diag(t_lo))
            else:
                r = _dot(lhs_p, jnp.concatenate([block_diag(t_hi), block_diag(p_hi)], axis=1))
                r = r[:CHUNK] + r[CHUNK:] + _dot(p_hi, jnp.concatenate([block_diag(t_lo), block_diag(p_lo)], axis=1))
                t = t + r[:, :LANES]
                p = r[:, LANES:]
            span *= 2

        e_f = jnp.exp(gc_f)
        e_b = jnp.exp(gc_b)
        rhs_uw = jnp.concatenate([jnp.concatenate([v_f * beta_f, kb_f * e_f], axis=1),
                                  jnp.concatenate([v_b * beta_b, kb_b * e_b], axis=1)], axis=0).astype(BF16)
        uw = _dot(block_diag(t).astype(BF16), rhs_uw)
        glast_f = gc_f[CHUNK - 1:CHUNK, :]
        glast_b = gc_b[0:1, :]
        kdt = jnp.transpose(jnp.concatenate([k_f * jnp.exp(glast_f - gc_f), k_b * jnp.exp(glast_b - gc_b)], axis=0))
        zk = jnp.zeros_like(kdt)
        lhs3 = jnp.concatenate([jnp.where(is_f2, kdt, zk), jnp.where(is_f2, zk, kdt), block_diag(qk)], axis=0)
        big = _dot(lhs3.astype(BF16), uw.astype(BF16))
        for d, e_d, q_d in ((0, e_f, q_f), (1, e_b, q_b)):
            rows_s = slice(d * 2 * CHUNK, (d + 1) * 2 * CHUNK)
            rows_o = slice(4 * CHUNK + d * CHUNK, 4 * CHUNK + (d + 1) * CHUNK)
            c_ref[si, hh, d] = big[rows_s, :HEAD_DIM]
            oo_ref[si, hh, d] = big[rows_o, :HEAD_DIM]
            mq_ref[si, hh, d] = jnp.concatenate([big[rows_s, HEAD_DIM:], q_d * e_d - big[rows_o, HEAD_DIM:]],
                                                axis=0).astype(BF16)
        eg_ref[si, hh, 0] = jnp.broadcast_to(jnp.exp(glast_f), (8, LANES))
        eg_ref[si, hh, 1] = jnp.broadcast_to(jnp.exp(glast_b), (8, LANES))

    def scan(j, hh, slot, si):
        mq_ref, c_ref, oo_ref, eg_ref = slot
        cols = slice(hh * HEAD_DIM, (hh + 1) * HEAD_DIM)
        for d, n in ((0, j), (1, nc - 1 - j)):
            s = s_ref[hh, d]
            r = _dot(mq_ref[si, hh, d], s.astype(BF16))
            s_ref[hh, d] = s * eg_ref[si, hh, d][0:1, :] - r[:2 * CHUNK] + c_ref[si, hh, d]
            off = aligned(n * CHUNK)
            oacc_ref[d, pl.ds(off, CHUNK), cols] = r[2 * CHUNK:] + oo_ref[si, hh, d]

    def prep_block(blk, slot):
        for si in range(steps):
            for hh in range(heads):
                prep(blk * steps + si, hh, slot, si)

    def scan_block(blk, slot):
        for si in range(steps):
            for hh in range(heads):
                scan(blk * steps + si, hh, slot, si)

    slot_a = (mq_a, c_a, oo_a, eg_a)
    slot_b = (mq_b, c_b, oo_b, eg_b)
    prep_block(0, slot_a)
    if nblk == 1:
        scan_block(0, slot_a)
    else:
        def body(i, carry):
            blk = 2 * i
            prep_block(blk + 1, slot_b)
            scan_block(blk, slot_a)
            prep_block(blk + 2, slot_a)
            scan_block(blk + 1, slot_b)
            return carry

        lax.fori_loop(0, nblk // 2 - 1, body, 0)
        prep_block(nblk - 1, slot_b)
        scan_block(nblk - 2, slot_a)
        scan_block(nblk - 1, slot_b)

    for hh in range(heads):
        cols = slice(hh * HEAD_DIM, (hh + 1) * HEAD_DIM)
        sout_ref[0, hh] = s_ref[hh, 0]
        sout_ref[1, hh] = s_ref[hh, 1]
        o = oacc_ref[0, :, cols] + oacc_ref[1, :, cols]
        o = _rms_rows(o, ng_ref[...]) * _silu(gate_ref[:, cols])
        o_ref[:, cols] = o.astype(o_ref.dtype)


def _gdn(proj, conv_w, arow, dtrow, norm_g, s0, row_block0, n_seq, seq_len, heads, steps):
    L = seq_len
    nblk = L // CHUNK // steps
    assert L % (CHUNK * steps) == 0 and 2 * CHUNK == LANES and A_HEADS % heads == 0 and (nblk == 1 or nblk % 2 == 0)
    width = heads * HEAD_DIM
    groups = A_HEADS // heads
    blk = lambda base: pl.BlockSpec((L, width), lambda b, h: (row_block0 + b, base * groups + h))
    wblk = lambda base: pl.BlockSpec((3, width), lambda b, h: (0, base * groups + h))
    state_spec = pl.BlockSpec((None, 2, heads, HEAD_DIM, HEAD_DIM), lambda b, h: (b, 0, h, 0, 0))
    slot = [pltpu.VMEM((steps, heads, 2, 3 * CHUNK, HEAD_DIM), BF16),
            pltpu.VMEM((steps, heads, 2, 2 * CHUNK, HEAD_DIM), F32),
            pltpu.VMEM((steps, heads, 2, CHUNK, HEAD_DIM), F32),
            pltpu.VMEM((steps, heads, 2, 8, LANES), F32)]
    kern = functools.partial(_gdn_kernel, seq_len=L, heads=heads, steps=steps)
    out, s_out = pl.pallas_call(
        kern,
        out_shape=(jax.ShapeDtypeStruct((n_seq * L, A_WIDTH), BF16),
                   jax.ShapeDtypeStruct((n_seq, 2, A_HEADS, HEAD_DIM, HEAD_DIM), F32)),
        grid=(n_seq, groups),
        in_specs=[
            blk(0), blk(1), blk(2), blk(3),
            pl.BlockSpec((L, LANES), lambda b, h: (row_block0 + b, EV_BG_OFF // LANES)),
            wblk(0), wblk(1), wblk(2),
            pl.BlockSpec((1, LANES), lambda b, h: (0, 0)),
            pl.BlockSpec((1, LANES), lambda b, h: (0, 0)),
            pl.BlockSpec((1, HEAD_DIM), lambda b, h: (0, 0)),
            state_spec,
        ],
        out_specs=(pl.BlockSpec((L, width), lambda b, h: (b, h)), state_spec),
        scratch_shapes=[
            pltpu.VMEM((L, width), F32), pltpu.VMEM((L, width), F32), pltpu.VMEM((L, width), F32),
            pltpu.VMEM((L, LANES), F32), pltpu.VMEM((L, LANES), F32),
            pltpu.VMEM((heads, 2, HEAD_DIM, HEAD_DIM), F32),
            pltpu.VMEM((2, L, width), F32),
        ] + slot + slot,
        compiler_params=pltpu.CompilerParams(dimension_semantics=("arbitrary", "arbitrary"),
                                             vmem_limit_bytes=VMEM_LIMIT_BYTES),
        name="gdn",
    )(proj, proj, proj, proj, proj, conv_w, conv_w, conv_w, arow, dtrow, norm_g.reshape(1, HEAD_DIM), s0)
    return out, s_out


def _rope_rows(x, cos, sin_signed):
    lane = lax.broadcasted_iota(jnp.int32, x.shape, 1)
    swapped = jnp.where((lane % (2 * ROT_FREQS)) < ROT_FREQS,
                        pltpu.roll(x, HEAD_DIM - ROT_FREQS, 1), pltpu.roll(x, ROT_FREQS, 1))
    return x * cos + swapped * sin_signed


def _attn_kernel(*refs, tq, seq_len, group, n_ctx, use_sink, window, qk_norm, rope, emit_k):
    it = iter(refs)
    q_ref, k_ref, v_ref = next(it), next(it), next(it)
    ck_ref = cv_ref = cos_ref = sin_ref = qg_ref = kg_ref = sink_ref = kn_out_ref = None
    if n_ctx:
        ck_ref, cv_ref = next(it), next(it)
    if rope:
        cos_ref, sin_ref = next(it), next(it)
    if qk_norm:
        qg_ref, kg_ref = next(it), next(it)
    if use_sink:
        sink_ref = next(it)
    o_ref = next(it)
    if emit_k:
        kn_out_ref = next(it)
    kbf_ref, vbf_ref = next(it), next(it)
    ckbf_ref = cvbf_ref = None
    if n_ctx:
        ckbf_ref, cvbf_ref = next(it), next(it)

    L = seq_len
    kvh = pl.program_id(1)
    qi = pl.program_id(2)

    @pl.when(qi == 0)
    def _():
        k = k_ref[...]
        if qk_norm:
            k = _rms_rows(k, kg_ref[...])
        if emit_k:
            kn_out_ref[...] = k
        if rope:
            k = _rope_rows(k, cos_ref[...], sin_ref[...])
        kbf_ref[...] = k.astype(BF16)
        vbf_ref[...] = v_ref[...].astype(BF16)
        if n_ctx:
            ckbf_ref[...] = ck_ref[...].astype(BF16)
            cvbf_ref[...] = cv_ref[...].astype(BF16)

    q0 = pl.multiple_of(qi * tq, tq)
    if window:
        wk = tq + 2 * window
        ws = pl.multiple_of(jnp.clip(q0 - window, 0, L - wk), LANES)
        k_loc = kbf_ref[pl.ds(ws, wk), :]
        v_loc = vbf_ref[pl.ds(ws, wk), :]
        qpos = q0 + lax.broadcasted_iota(jnp.int32, (tq, wk), 0)
        kpos = ws + lax.broadcasted_iota(jnp.int32, (tq, wk), 1)
        valid = jnp.abs(qpos - kpos) <= window
    else:
        k_loc = kbf_ref[...]
        v_loc = vbf_ref[...]

    for g in range(group):
        q = q_ref[:, g * HEAD_DIM:(g + 1) * HEAD_DIM]
        if qk_norm:
            q = _rms_rows(q, qg_ref[...])
        if rope:
            q = _rope_rows(q, cos_ref[pl.ds(q0, tq), :], sin_ref[pl.ds(q0, tq), :])
        q = (q * (HEAD_DIM ** -0.5)).astype(BF16)
        s_loc = _dot_nt(q, k_loc)
        if window:
            s_loc = jnp.where(valid, s_loc, NEG)
        m = jnp.max(s_loc, axis=-1, keepdims=True)
        if n_ctx:
            s_ctx = _dot_nt(q, ckbf_ref[...])
            m = jnp.maximum(m, jnp.max(s_ctx, axis=-1, keepdims=True))
        if use_sink:
            sink = sink_ref[kvh * group + g]
            m = jnp.maximum(m, sink)
        p_loc = jnp.exp(s_loc - m)
        denom = jnp.sum(p_loc, axis=-1, keepdims=True)
        acc = _dot(p_loc.astype(BF16), v_loc)
        if n_ctx:
            p_ctx = jnp.exp(s_ctx - m)
            denom = denom + jnp.sum(p_ctx, axis=-1, keepdims=True)
            acc = acc + _dot(p_ctx.astype(BF16), cvbf_ref[...])
        if use_sink:
            denom = denom + jnp.exp(sink - m)
        o_ref[:, g * HEAD_DIM:(g + 1) * HEAD_DIM] = (acc / denom).astype(o_ref.dtype)


def _attention(proj, *, q_off, k_off, v_off, n_heads, n_kv, row_block0, n_seq, seq_len, tq,
               ctx_k=None, ctx_v=None, cos=None, sin=None, q_gain=None, k_gain=None, sink=None,
               window=0, emit_k=False):
    L = seq_len
    group = n_heads // n_kv
    n_ctx = 0 if ctx_k is None else ctx_k.shape[1]
    rope = cos is not None
    qk_norm = q_gain is not None
    use_sink = sink is not None
    nq = L // tq
    qb, kb, vb = q_off // (group * HEAD_DIM), k_off // HEAD_DIM, v_off // HEAD_DIM
    in_specs = [
        pl.BlockSpec((tq, group * HEAD_DIM), lambda b, h, i: ((row_block0 + b) * nq + i, qb + h)),
        pl.BlockSpec((L, HEAD_DIM), lambda b, h, i: (row_block0 + b, kb + h)),
        pl.BlockSpec((L, HEAD_DIM), lambda b, h, i: (row_block0 + b, vb + h)),
    ]
    args = [proj, proj, proj]
    scratch = [pltpu.VMEM((L, HEAD_DIM), BF16), pltpu.VMEM((L, HEAD_DIM), BF16)]
    if n_ctx:
        in_specs += [pl.BlockSpec((None, n_ctx, HEAD_DIM), lambda b, h, i: (b, 0, h))] * 2
        args += [ctx_k, ctx_v]
        scratch += [pltpu.VMEM((n_ctx, HEAD_DIM), BF16), pltpu.VMEM((n_ctx, HEAD_DIM), BF16)]
    if rope:
        in_specs += [pl.BlockSpec((L, HEAD_DIM), lambda b, h, i: (0, 0))] * 2
        args += [cos, sin]
    if qk_norm:
        in_specs += [pl.BlockSpec((1, HEAD_DIM), lambda b, h, i: (0, 0))] * 2
        args += [q_gain.reshape(1, HEAD_DIM), k_gain.reshape(1, HEAD_DIM)]
    if use_sink:
        in_specs += [pl.BlockSpec(memory_space=pltpu.SMEM)]
        args += [sink]
    out_shape = [jax.ShapeDtypeStruct((n_seq * L, n_heads * HEAD_DIM), BF16)]
    out_specs = [pl.BlockSpec((tq, group * HEAD_DIM), lambda b, h, i: (b * nq + i, h))]
    if emit_k:
        out_shape += [jax.ShapeDtypeStruct((n_seq * L, n_kv * HEAD_DIM), F32)]
        out_specs += [pl.BlockSpec((L, HEAD_DIM), lambda b, h, i: (b, h))]
    kern = functools.partial(_attn_kernel, tq=tq, seq_len=L, group=group, n_ctx=n_ctx, use_sink=use_sink,
                             window=window, qk_norm=qk_norm, rope=rope, emit_k=emit_k)
    res = pl.pallas_call(
        kern,
        out_shape=tuple(out_shape),
        grid=(n_seq, n_kv, nq),
        in_specs=in_specs,
        out_specs=tuple(out_specs),
        scratch_shapes=scratch,
        compiler_params=pltpu.CompilerParams(dimension_semantics=("arbitrary", "arbitrary", "arbitrary"),
                                             vmem_limit_bytes=VMEM_LIMIT_BYTES),
        name="attention",
    )(*args)
    return res if emit_k else res[0]


def _out_mlp_kernel(*refs, n_parts, final):
    it = iter(refs)
    x_ref = next(it)
    a_refs = [next(it) for _ in range(n_parts)]
    wo_refs = [next(it) for _ in range(n_parts)]
    mod_ref, g2_ref, w1_ref, w2_ref = next(it), next(it), next(it), next(it)
    fg_ref = next(it) if final else None
    o_ref = next(it)
    x1_ref, h2_ref, acc_ref = next(it), next(it), next(it)
    j = pl.program_id(1)

    @pl.when(j == 0)
    def _():
        mix = _dot(a_refs[0][...], wo_refs[0][...])
        for a_ref, wo_ref in zip(a_refs[1:], wo_refs[1:]):
            mix = mix + _dot(a_ref[...], wo_ref[...])
        x1 = x_ref[...] + mod_ref[2:3, :] * mix
        x1_ref[...] = x1
        h2 = _rms_rows(x1, g2_ref[...]) * (1.0 + mod_ref[4:5, :]) + mod_ref[3:4, :]
        h2_ref[...] = h2.astype(BF16)
        acc_ref[...] = jnp.zeros_like(acc_ref)

    hid = jnp.maximum(_dot(h2_ref[...], w1_ref[...]), 0.0)
    acc_ref[...] += _dot((hid * hid).astype(BF16), w2_ref[...])

    @pl.when(j == pl.num_programs(1) - 1)
    def _():
        y = x1_ref[...] + mod_ref[5:6, :] * acc_ref[...]
        if final:
            y = _rms_rows(y, fg_ref[...])
        o_ref[...] = y


def _out_mlp(x, parts, wo_parts, mod_l, g2, w1, w2, n_prompt, sample_len, final_g=None, tm=512, tf=1024):
    n_tok, d = x.shape
    d_ff = w1.shape[1]
    final = final_g is not None
    row = functools.partial(_cond_row, n_prompt_tiles=n_prompt // tm, tiles_per_sample=sample_len // tm)
    in_specs = [pl.BlockSpec((tm, d), lambda i, j: (i, 0))]
    in_specs += [pl.BlockSpec((tm, p.shape[1]), lambda i, j: (i, 0)) for p in parts]
    in_specs += [pl.BlockSpec(w.shape, lambda i, j: (0, 0)) for w in wo_parts]
    in_specs += [
        pl.BlockSpec((None, 6, d), lambda i, j: (row(i), 0, 0)),
        pl.BlockSpec((1, d), lambda i, j: (0, 0)),
        pl.BlockSpec((d, tf), lambda i, j: (0, j)),
        pl.BlockSpec((tf, d), lambda i, j: (j, 0)),
    ]
    args = [x, *parts, *wo_parts, mod_l, g2.reshape(1, d), w1, w2]
    if final:
        in_specs += [pl.BlockSpec((1, d), lambda i, j: (0, 0))]
        args += [final_g.reshape(1, d)]
    kern = functools.partial(_out_mlp_kernel, n_parts=len(parts), final=final)
    return pl.pallas_call(
        kern,
        out_shape=jax.ShapeDtypeStruct((n_tok, d), F32),
        grid=(n_tok // tm, d_ff // tf),
        in_specs=in_specs,
        out_specs=pl.BlockSpec((tm, d), lambda i, j: (i, 0)),
        scratch_shapes=[pltpu.VMEM((tm, d), F32), pltpu.VMEM((tm, d), BF16), pltpu.VMEM((tm, d), F32)],
        compiler_params=pltpu.CompilerParams(dimension_semantics=("arbitrary", "arbitrary"),
                                             vmem_limit_bytes=VMEM_LIMIT_BYTES),
        name="out_mlp",
    )(*args)


def _rope_tables(seq_len):
    rows = seq_len // GRID_W
    inv = ROPE_THETA ** (-jnp.arange(ROT_FREQS, dtype=F32) / ROT_FREQS)
    ang_r = jnp.repeat(jnp.arange(rows, dtype=F32), GRID_W)[:, None] * inv
    ang_c = jnp.tile(jnp.arange(GRID_W, dtype=F32), rows)[:, None] * inv
    cos = jnp.concatenate([jnp.cos(ang_r), jnp.cos(ang_r), jnp.cos(ang_c), jnp.cos(ang_c)], axis=-1)
    sin = jnp.concatenate([-jnp.sin(ang_r), jnp.sin(ang_r), -jnp.sin(ang_c), jnp.sin(ang_c)], axis=-1)
    return cos.astype(F32), sin.astype(F32)


def kernel(x_prompt, x_sample, state_a, cache_b_kv, cache_c_kv, c, c_ctx, ada_w, ada_b, norm1_g, norm2_g,
           final_g, mlp_w1, mlp_w2, ev_w_in, a_conv, a_log, a_dt_bias, a_norm_g, b_sink, ev_w_out,
           od_w_in, c_qnorm_g, c_knorm_g, od_w_out):
    batch, seq, d = x_prompt.shape
    dec_batch, dec_seq, _ = x_sample.shape
    depth = ada_w.shape[0]
    n_prompt, n_sample = batch * seq, dec_batch * dec_seq
    past = cache_b_kv.shape[3]

    x = jnp.concatenate([x_prompt.reshape(n_prompt, d), x_sample.reshape(n_sample, d)], axis=0)
    cond = jnp.concatenate([c_ctx[None, :], c, jnp.zeros((8 - 1 - dec_batch, d), F32)], axis=0)
    mod = _adaln(cond, ada_w, ada_b).reshape(depth, 8, 6, d)

    cos, sin = _rope_tables(dec_seq)
    s_blk0 = n_prompt // dec_seq
    zero_state = jnp.zeros((batch, 2, A_HEADS, HEAD_DIM, HEAD_DIM), F32)
    lane_pad = lambda v: jnp.zeros((1, LANES), F32).at[0, 2 * A_HEADS:4 * A_HEADS].set(v.reshape(-1))

    new_a, new_b, new_c = [], [], []
    for l in range(depth):
        i = l // 2
        fin = final_g if l == depth - 1 else None
        w1 = mlp_w1[l].astype(BF16)
        w2 = mlp_w2[l].astype(BF16)
        if l % 2 == 0:
            w_in = ev_w_in[i]
            bg_off = 4 * A_WIDTH
            qb_off = bg_off + 4 * A_HEADS
            w_in = jnp.concatenate([w_in[:, :bg_off], w_in[:, qb_off:], w_in[:, bg_off:qb_off],
                                    jnp.zeros((d, LANES - 4 * A_HEADS), F32)], axis=1).astype(BF16)
            proj = _in_proj(x, mod[l], norm1_g[l], w_in, n_prompt, dec_seq)
            arow, dtrow = lane_pad(a_log[i]), lane_pad(a_dt_bias[i])
            oa_p, st_p = _gdn(proj, a_conv[i], arow, dtrow, a_norm_g[i], zero_state, 0, batch, seq, heads=A_HEADS,
                              steps=seq // CHUNK)
            oa_s, _ = _gdn(proj, a_conv[i], arow, dtrow, a_norm_g[i], state_a[:, i], s_blk0, dec_batch, dec_seq,
                           heads=1, steps=8)
            kv_cols = B_KV_HEADS * HEAD_DIM
            ob_p = _attention(proj, q_off=EV_QB_OFF, k_off=EV_KVB_OFF, v_off=EV_KVB_OFF + kv_cols,
                              n_heads=B_HEADS, n_kv=B_KV_HEADS, row_block0=0, n_seq=batch, seq_len=seq,
                              tq=seq, sink=b_sink[i])
            ob_s = _attention(proj, q_off=EV_QB_OFF, k_off=EV_KVB_OFF, v_off=EV_KVB_OFF + kv_cols,
                              n_heads=B_HEADS, n_kv=B_KV_HEADS, row_block0=s_blk0, n_seq=dec_batch,
                              seq_len=dec_seq, tq=256,
                              ctx_k=cache_b_kv[:, i, 0].reshape(dec_batch, past, kv_cols),
                              ctx_v=cache_b_kv[:, i, 1].reshape(dec_batch, past, kv_cols),
                              cos=cos, sin=sin, sink=b_sink[i], window=WINDOW)
            new_a.append(st_p)
            kvb = proj[:n_prompt, EV_KVB_OFF:EV_KVB_OFF + 2 * kv_cols]
            new_b.append(jnp.transpose(kvb.reshape(batch, seq, 2, B_KV_HEADS, HEAD_DIM), (0, 2, 1, 3, 4)))
            parts = [jnp.concatenate([oa_p, oa_s], axis=0), jnp.concatenate([ob_p, ob_s], axis=0)]
            w_out = ev_w_out[i].astype(BF16)
            wo_parts = [w_out[:A_WIDTH], w_out[A_WIDTH:]]
        else:
            proj = _in_proj(x, mod[l], norm1_g[l], od_w_in[i].astype(BF16), n_prompt, dec_seq)
            kv_cols = C_KV_HEADS * HEAD_DIM
            oc_p, kn_p = _attention(proj, q_off=0, k_off=OD_K_OFF, v_off=OD_V_OFF, n_heads=C_HEADS,
                                    n_kv=C_KV_HEADS, row_block0=0, n_seq=batch, seq_len=seq, tq=seq,
                                    q_gain=c_qnorm_g[i], k_gain=c_knorm_g[i], emit_k=True)
            oc_s = _attention(proj, q_off=0, k_off=OD_K_OFF, v_off=OD_V_OFF, n_heads=C_HEADS,
                              n_kv=C_KV_HEADS, row_block0=s_blk0, n_seq=dec_batch, seq_len=dec_seq, tq=256,
                              ctx_k=cache_c_kv[:, i, 0].reshape(dec_batch, past, kv_cols),
                              ctx_v=cache_c_kv[:, i, 1].reshape(dec_batch, past, kv_cols),
                              cos=cos, sin=sin, q_gain=c_qnorm_g[i], k_gain=c_knorm_g[i])
            v_p = proj[:n_prompt, OD_V_OFF:OD_V_OFF + kv_cols]
            new_c.append(jnp.stack([kn_p.reshape(batch, seq, C_KV_HEADS, HEAD_DIM),
                                    v_p.reshape(batch, seq, C_KV_HEADS, HEAD_DIM)], axis=1))
            parts = [jnp.concatenate([oc_p, oc_s], axis=0)]
            wo_parts = [od_w_out[i].astype(BF16)]
        x = _out_mlp(x, parts, wo_parts, mod[l], norm2_g[l], w1, w2, n_prompt, dec_seq, final_g=fin)

    y_prompt = x[:n_prompt].reshape(batch, seq, d)
    y_sample = x[n_prompt:].reshape(dec_batch, dec_seq, d)
    return (y_prompt, y_sample, jnp.stack(new_a, axis=1), jnp.stack(new_b, axis=1), jnp.stack(new_c, axis=1))
```

```python
import functools

import numpy as np
import jax
import jax.numpy as jnp
from jax import lax
from jax.experimental import pallas as pl
from jax.experimental.pallas import tpu as pltpu

F32 = jnp.float32
BF16 = jnp.bfloat16
HIGHEST = lax.Precision.HIGHEST

HEAD_DIM = 128
GRID_W = 64
A_HEADS = 4
CHUNK = 64
B_HEADS = 4
B_KV_HEADS = 2
WINDOW = 128
C_HEADS = 8
C_KV_HEADS = 2
ROT_FREQS = HEAD_DIM // 4
ROPE_THETA = 10000.0
EPS = 1e-6
NEG = -1e30
A_WIDTH = A_HEADS * HEAD_DIM
B_WIDTH = B_HEADS * HEAD_DIM
C_WIDTH = C_HEADS * HEAD_DIM

LANES = 128
VMEM_LIMIT_BYTES = 56 * 1024 * 1024

EV_GATE_OFF = 3 * A_WIDTH
EV_QB_OFF = 4 * A_WIDTH
EV_KVB_OFF = EV_QB_OFF + B_WIDTH
EV_BG_OFF = EV_KVB_OFF + 2 * B_KV_HEADS * HEAD_DIM
EV_COLS = EV_BG_OFF + LANES
OD_K_OFF = C_WIDTH
OD_V_OFF = C_WIDTH + C_KV_HEADS * HEAD_DIM


def _sigmoid(x):
    return 1.0 / (1.0 + jnp.exp(-x))


def _silu(x):
    return x * _sigmoid(x)


def _softplus(x):
    return jnp.maximum(x, 0.0) + jnp.log1p(jnp.exp(-jnp.abs(x)))


def _rms_rows(x, g):
    return x * lax.rsqrt(jnp.mean(x * x, axis=-1, keepdims=True) + EPS) * g


def _dot(a, b, precision=None):
    return jnp.dot(a, b, preferred_element_type=F32, precision=precision)


def _dot_nt(a, b, precision=None):
    return lax.dot_general(a, b, (((1,), (1,)), ((), ())), preferred_element_type=F32, precision=precision)


def _cond_row(i, n_prompt_tiles, tiles_per_sample):
    return jnp.where(i < n_prompt_tiles, 0, 1 + (i - n_prompt_tiles) // tiles_per_sample)


def _adaln_kernel(cond_ref, w_ref, b_ref, o_ref):
    s = _silu(cond_ref[...])
    o_ref[0] = _dot(s, w_ref[0], HIGHEST) + b_ref[0]


def _adaln(cond, ada_w, ada_b, tn=1024):
    depth, d, n = ada_w.shape
    rows = cond.shape[0]
    return pl.pallas_call(
        _adaln_kernel,
        out_shape=jax.ShapeDtypeStruct((depth, rows, n), F32),
        grid=(depth, n // tn),
        in_specs=[
            pl.BlockSpec((rows, d), lambda l, j: (0, 0)),
            pl.BlockSpec((1, d, tn), lambda l, j: (l, 0, j)),
            pl.BlockSpec((1, 1, tn), lambda l, j: (l, 0, j)),
        ],
        out_specs=pl.BlockSpec((1, rows, tn), lambda l, j: (l, 0, j)),
        compiler_params=pltpu.CompilerParams(dimension_semantics=("arbitrary", "arbitrary"),
                                             vmem_limit_bytes=VMEM_LIMIT_BYTES),
        name="adaln",
    )(cond, ada_w, ada_b.reshape(depth, 1, n))


def _inproj_kernel(x_ref, mod_ref, g_ref, w_ref, o_ref):
    h = _rms_rows(x_ref[...], g_ref[...]) * (1.0 + mod_ref[1:2, :]) + mod_ref[0:1, :]
    o_ref[...] = _dot(h.astype(BF16), w_ref[...])


def _in_proj(x, mod_l, g, w, n_prompt, sample_len, tm=512):
    n_tok, d = x.shape
    n = w.shape[1]
    row = functools.partial(_cond_row, n_prompt_tiles=n_prompt // tm, tiles_per_sample=sample_len // tm)
    return pl.pallas_call(
        _inproj_kernel,
        out_shape=jax.ShapeDtypeStruct((n_tok, n), F32),
        grid=(n_tok // tm,),
        in_specs=[
            pl.BlockSpec((tm, d), lambda i: (i, 0)),
            pl.BlockSpec((None, 6, d), lambda i: (row(i), 0, 0)),
            pl.BlockSpec((1, d), lambda i: (0, 0)),
            pl.BlockSpec((d, n), lambda i: (0, 0)),
        ],
        out_specs=pl.BlockSpec((tm, n), lambda i: (i, 0)),
        compiler_params=pltpu.CompilerParams(dimension_semantics=("arbitrary",),
                                             vmem_limit_bytes=VMEM_LIMIT_BYTES),
        name="in_proj",
    )(x, mod_l, g.reshape(1, d), w)


def _split_bf16(x):
    hi = x.astype(BF16)
    lo = (x - hi.astype(F32)).astype(BF16)
    return hi, lo


def _gdn_kernel(q_ref, k_ref, v_ref, gate_ref, bg_ref, cwq_ref, cwk_ref, cwv_ref, arow_ref, dtrow_ref,
                ng_ref, s0_ref, o_ref, sout_ref,
                qn_ref, kn_ref, vn_ref, beta_ref, gc_ref, gcrow_ref, s_ref, oacc_ref,
                mq_a, c_a, oo_a, eg_a, mq_b, c_b, oo_b, eg_b, *, seq_len, heads, steps):
    L = seq_len
    nc = L // CHUNK
    nblk = nc // steps
    h0 = pl.program_id(1) * heads
    row_idx = lax.broadcasted_iota(jnp.int32, (L, LANES), 0)
    lane_idx = lax.broadcasted_iota(jnp.int32, (L, LANES), 1)
    chunk_row = row_idx % CHUNK

    def conv_silu(x, cw):
        prev = jnp.where(row_idx == 0, 0.0, pltpu.roll(x, 1, 0))
        nxt = jnp.where(row_idx == L - 1, 0.0, pltpu.roll(x, L - 1, 0))
        return _silu(prev * cw[0:1, :] + x * cw[1:2, :] + nxt * cw[2:3, :])

    def l2norm(x):
        return x * lax.rsqrt(jnp.sum(x * x, axis=-1, keepdims=True) + EPS)

    bg = bg_ref[...]
    beta_all = _sigmoid(bg)
    g_all = -jnp.exp(arow_ref[...]) * _softplus(bg + dtrow_ref[...])

    def pick(x, col):
        col_vals = jnp.sum(jnp.where(lane_idx == col, x, 0.0), axis=1, keepdims=True)
        return jnp.broadcast_to(col_vals, (L, LANES))

    def chunk_cumsum(g, reverse):
        shift = 1
        while shift < CHUNK:
            if reverse:
                g = g + jnp.where(chunk_row < CHUNK - shift, pltpu.roll(g, L - shift, 0), 0.0)
            else:
                g = g + jnp.where(chunk_row >= shift, pltpu.roll(g, shift, 0), 0.0)
            shift *= 2
        return g

    for hh in range(heads):
        cols = slice(hh * HEAD_DIM, (hh + 1) * HEAD_DIM)
        qn_ref[:, cols] = l2norm(conv_silu(q_ref[:, cols], cwq_ref[:, cols])) * (HEAD_DIM ** -0.5)
        kn_ref[:, cols] = l2norm(conv_silu(k_ref[:, cols], cwk_ref[:, cols]))
        vn_ref[:, cols] = conv_silu(v_ref[:, cols], cwv_ref[:, cols])
        s_ref[hh, 0] = s0_ref[0, hh]
        s_ref[hh, 1] = s0_ref[1, hh]
        head = h0 + hh
        beta_ref[hh, 0] = pick(beta_all, head)
        beta_ref[hh, 1] = pick(beta_all, A_HEADS + head)
        for d in range(2):
            gc = chunk_cumsum(pick(g_all, (2 + d) * A_HEADS + head), reverse=(d == 1))
            gc_ref[hh, d] = gc
            for r in range(L // LANES):
                rows = jnp.transpose(gc[r * LANES:(r + 1) * LANES, :])[0:8, :]
                gcrow_ref[hh, d, 0, r] = rows
                gcrow_ref[hh, d, 1, r] = pltpu.roll(rows, CHUNK, 1)

    r64 = lax.broadcasted_iota(jnp.int32, (CHUNK, LANES), 0)
    l64 = lax.broadcasted_iota(jnp.int32, (CHUNK, LANES), 1)
    is_f = l64 < CHUNK
    is_f_row = lax.broadcasted_iota(jnp.int32, (1, LANES), 1) < CHUNK
    cpos = jnp.where(is_f, l64, l64 - CHUNK)
    ahead = jnp.where(is_f, r64 - cpos, cpos - r64)
    incl = ahead >= 0
    strict = ahead > 0
    eye = jnp.where(r64 == cpos, 1.0, 0.0)
    is_f2 = lax.broadcasted_iota(jnp.int32, (2 * CHUNK, LANES), 1) < CHUNK
    zeros_c = jnp.zeros((CHUNK, HEAD_DIM), F32)

    def aligned(off, m):
        return off if isinstance(off, int) else pl.multiple_of(off, m)

    def half(n):
        return n // 2 if isinstance(n, int) else lax.shift_right_logical(n, 1)

    def block_diag(xp):
        z = jnp.zeros_like(xp)
        return jnp.concatenate([jnp.where(is_f, xp, z), jnp.where(is_f, z, xp)], axis=0)

    def prep_stages(blk, slot):
        mq_ref, c_ref, oo_ref, eg_ref = slot
        probs = [(si, hh) for si in range(steps) for hh in range(heads)]
        st = {p: {} for p in probs}

        def load(p):
            si, hh = p
            j = blk * steps + si
            n_b = nc - 1 - j
            cols = slice(hh * HEAD_DIM, (hh + 1) * HEAD_DIM)
            off_f = aligned(j * CHUNK, CHUNK)
            off_b = aligned(n_b * CHUNK, CHUNK)
            q_f, k_f, v_f = (r[pl.ds(off_f, CHUNK), cols] for r in (qn_ref, kn_ref, vn_ref))
            q_b, k_b, v_b = (r[pl.ds(off_b, CHUNK), cols] for r in (qn_ref, kn_ref, vn_ref))
            beta_f = beta_ref[hh, 0, pl.ds(off_f, CHUNK), :]
            beta_b = beta_ref[hh, 1, pl.ds(off_b, CHUNK), :]
            gc_f = gc_ref[hh, 0, pl.ds(off_f, CHUNK), :]
            gc_b = gc_ref[hh, 1, pl.ds(off_b, CHUNK), :]
            row_f = gcrow_ref[hh, 0, si % 2, half(j)][0:1, :]
            row_b = gcrow_ref[hh, 1, si % 2, half(n_b)][0:1, :]
            gcrow = jnp.where(is_f_row, row_f, row_b)
            diff = jnp.where(is_f, gc_f, gc_b) - gcrow
            decay = jnp.where(incl, jnp.exp(jnp.where(incl, diff, 0.0)), 0.0)
            kb_f = k_f * beta_f
            kb_b = k_b * beta_b
            e_f = jnp.exp(gc_f)
            e_b = jnp.exp(gc_b)
            glast_f = gc_f[CHUNK - 1:CHUNK, :]
            glast_b = gc_b[0:1, :]
            kdt = jnp.transpose(jnp.concatenate([k_f * jnp.exp(glast_f - gc_f), k_b * jnp.exp(glast_b - gc_b)],
                                                axis=0))
            zk = jnp.zeros_like(kdt)
            eg_ref[si, hh, 0] = jnp.broadcast_to(jnp.exp(glast_f), (8, LANES))
            eg_ref[si, hh, 1] = jnp.broadcast_to(jnp.exp(glast_b), (8, LANES))
            st[p].update(
                decay=decay,
                lhs=jnp.concatenate([jnp.concatenate([kb_f, kb_b], axis=1),
                                     jnp.concatenate([q_f, q_b], axis=1)], axis=0).astype(BF16),
                rhs=jnp.concatenate([jnp.concatenate([k_f, zeros_c], axis=1),
                                     jnp.concatenate([zeros_c, k_b], axis=1)], axis=0).astype(BF16),
                rhs_uw=jnp.concatenate([jnp.concatenate([v_f * beta_f, kb_f * e_f], axis=1),
                                        jnp.concatenate([v_b * beta_b, kb_b * e_b], axis=1)], axis=0).astype(BF16),
                kd2=jnp.concatenate([jnp.where(is_f2, kdt, zk), jnp.where(is_f2, zk, kdt)], axis=0).astype(BF16),
                qe=(q_f * e_f, q_b * e_b))

        def gram(p):
            s = st[p]
            aq = _dot_nt(s.pop("lhs"), s.pop("rhs"))
            decay = s.pop("decay")
            x = -jnp.where(strict, aq[:CHUNK] * decay, 0.0)
            s["qk"] = jnp.where(incl, aq[CHUNK:] * decay, 0.0)
            s["t"] = eye + x
            s["p"] = x

        def square(p):
            s = st[p]
            p_hi, p_lo = _split_bf16(s["p"])
            r = _dot(jnp.concatenate([p_hi, p_lo], axis=0), block_diag(p_hi))
            s["p"] = r[:CHUNK] + r[CHUNK:] + _dot(p_hi, block_diag(p_lo))

        def double(p):
            s = st[p]
            p_hi, p_lo = _split_bf16(s["p"])
            t_hi, t_lo = _split_bf16(s["t"])
            r = _dot(jnp.concatenate([p_hi, p_lo], axis=0),
                     jnp.concatenate([block_diag(t_hi), block_diag(p_hi)], axis=1))
            r = r[:CHUNK] + r[CHUNK:] + _dot(p_hi, jnp.concatenate([block_diag(t_lo), block_diag(p_lo)], axis=1))
            s["t"] = s["t"] + r[:, :LANES]
            s["p"] = r[:, LANES:]

        def last(p):
            s = st[p]
            p_hi, p_lo = _split_bf16(s.pop("p"))
            t_hi, t_lo = _split_bf16(s["t"])
            r = _dot(jnp.concatenate([p_hi, p_lo], axis=0), block_diag(t_hi))
            s["t"] = s["t"] + r[:CHUNK] + r[CHUNK:] + _dot(p_hi, block_diag(t_lo))

        def solve(p):
            s = st[p]
            s["uw"] = _dot(block_diag(s.pop("t")).astype(BF16), s.pop("rhs_uw"))

        def fold(p):
            si, hh = p
            s = st[p]
            lhs3 = jnp.concatenate([s.pop("kd2"), block_diag(s.pop("qk")).astype(BF16)], axis=0)
            big = _dot(lhs3, s.pop("uw").astype(BF16))
            qe = s.pop("qe")
            for d in range(2):
                rows_s = slice(d * 2 * CHUNK, (d + 1) * 2 * CHUNK)
                rows_o = slice(4 * CHUNK + d * CHUNK, 4 * CHUNK + (d + 1) * CHUNK)
                c_ref[si, hh, d] = big[rows_s, :HEAD_DIM]
                oo_ref[si, hh, d] = big[rows_o, :HEAD_DIM]
                mq_ref[si, hh, d] = jnp.concatenate([big[rows_s, HEAD_DIM:], qe[d] - big[rows_o, HEAD_DIM:]],
                                                    axis=0).astype(BF16)

        n_double = 0
        span = 2
        while 2 * span < CHUNK:
            n_double += 1
            span *= 2
        stage_fns = [load, gram, square] + [double] * n_double + [last, solve, fold]
        return [functools.partial(lambda fn: [fn(p) for p in probs], fn) for fn in stage_fns]

    def scan_step(blk, slot, si):
        mq_ref, c_ref, oo_ref, eg_ref = slot
        j = blk * steps + si
        for hh in range(heads):
            cols = slice(hh * HEAD_DIM, (hh + 1) * HEAD_DIM)
            for d, n in ((0, j), (1, nc - 1 - j)):
                s = s_ref[hh, d]
                r = _dot(mq_ref[si, hh, d], s.astype(BF16))
                s_ref[hh, d] = s * eg_ref[si, hh, d][0:1, :] - r[:2 * CHUNK] + c_ref[si, hh, d]
                oacc_ref[d, pl.ds(aligned(n * CHUNK, CHUNK), CHUNK), cols] = r[2 * CHUNK:] + oo_ref[si, hh, d]

    def run(prep, scan):
        stages = prep_stages(*prep) if prep else []
        n_scan = steps if scan else 0
        for k in range(max(len(stages), n_scan)):
            if k < len(stages):
                stages[k]()
            if k < n_scan:
                scan_step(scan[0], scan[1], k)

    slot_a = (mq_a, c_a, oo_a, eg_a)
    slot_b = (mq_b, c_b, oo_b, eg_b)
    run((0, slot_a), None)
    if nblk == 1:
        run(None, (0, slot_a))
    else:
        def body(i, carry):
            blk = 2 * i
            run((blk + 1, slot_b), (blk, slot_a))
            run((blk + 2, slot_a), (blk + 1, slot_b))
            return carry

        lax.fori_loop(0, nblk // 2 - 1, body, 0)
        run((nblk - 1, slot_b), (nblk - 2, slot_a))
        run(None, (nblk - 1, slot_b))

    for hh in range(heads):
        cols = slice(hh * HEAD_DIM, (hh + 1) * HEAD_DIM)
        sout_ref[0, hh] = s_ref[hh, 0]
        sout_ref[1, hh] = s_ref[hh, 1]
        o = oacc_ref[0, :, cols] + oacc_ref[1, :, cols]
        o = _rms_rows(o, ng_ref[...]) * _silu(gate_ref[:, cols])
        o_ref[:, cols] = o.astype(o_ref.dtype)


def _gdn(proj, conv_w, arow, dtrow, norm_g, s0, row_block0, n_seq, seq_len, heads, steps):
    L = seq_len
    nblk = L // CHUNK // steps
    assert L % (CHUNK * steps) == 0 and steps % 2 == 0 and 2 * CHUNK == LANES and A_HEADS % heads == 0
    assert nblk == 1 or nblk % 2 == 0
    width = heads * HEAD_DIM
    groups = A_HEADS // heads
    blk = lambda base: pl.BlockSpec((L, width), lambda b, h: (row_block0 + b, base * groups + h))
    wblk = lambda base: pl.BlockSpec((3, width), lambda b, h: (0, base * groups + h))
    state_spec = pl.BlockSpec((None, 2, heads, HEAD_DIM, HEAD_DIM), lambda b, h: (b, 0, h, 0, 0))
    slot = [pltpu.VMEM((steps, heads, 2, 3 * CHUNK, HEAD_DIM), BF16),
            pltpu.VMEM((steps, heads, 2, 2 * CHUNK, HEAD_DIM), F32),
            pltpu.VMEM((steps, heads, 2, CHUNK, HEAD_DIM), F32),
            pltpu.VMEM((steps, heads, 2, 8, LANES), F32)]
    kern = functools.partial(_gdn_kernel, seq_len=L, heads=heads, steps=steps)
    out, s_out = pl.pallas_call(
        kern,
        out_shape=(jax.ShapeDtypeStruct((n_seq * L, A_WIDTH), BF16),
                   jax.ShapeDtypeStruct((n_seq, 2, A_HEADS, HEAD_DIM, HEAD_DIM), F32)),
        grid=(n_seq, groups),
        in_specs=[
            blk(0), blk(1), blk(2), blk(3),
            pl.BlockSpec((L, LANES), lambda b, h: (row_block0 + b, EV_BG_OFF // LANES)),
            wblk(0), wblk(1), wblk(2),
            pl.BlockSpec((1, LANES), lambda b, h: (0, 0)),
            pl.BlockSpec((1, LANES), lambda b, h: (0, 0)),
            pl.BlockSpec((1, HEAD_DIM), lambda b, h: (0, 0)),
            state_spec,
        ],
        out_specs=(pl.BlockSpec((L, width), lambda b, h: (b, h)), state_spec),
        scratch_shapes=[
            pltpu.VMEM((L, width), F32), pltpu.VMEM((L, width), F32), pltpu.VMEM((L, width), F32),
            pltpu.VMEM((heads, 2, L, LANES), F32), pltpu.VMEM((heads, 2, L, LANES), F32),
            pltpu.VMEM((heads, 2, 2, L // LANES, 8, LANES), F32),
            pltpu.VMEM((heads, 2, HEAD_DIM, HEAD_DIM), F32),
            pltpu.VMEM((2, L, width), F32),
        ] + slot + slot,
        compiler_params=pltpu.CompilerParams(dimension_semantics=("arbitrary", "arbitrary"),
                                             vmem_limit_bytes=VMEM_LIMIT_BYTES),
        name="gdn",
    )(proj, proj, proj, proj, proj, conv_w, conv_w, conv_w, arow, dtrow, norm_g.reshape(1, HEAD_DIM), s0)
    return out, s_out


def _rope_rows(x, cos, sin_signed):
    lane = lax.broadcasted_iota(jnp.int32, x.shape, 1)
    swapped = jnp.where((lane % (2 * ROT_FREQS)) < ROT_FREQS,
                        pltpu.roll(x, HEAD_DIM - ROT_FREQS, 1), pltpu.roll(x, ROT_FREQS, 1))
    return x * cos + swapped * sin_signed


def _attn_kernel(*refs, tq, seq_len, group, n_ctx, use_sink, window, qk_norm, rope, emit_k):
    it = iter(refs)
    q_ref, k_ref, v_ref = next(it), next(it), next(it)
    ck_ref = cv_ref = cos_ref = sin_ref = qg_ref = kg_ref = sink_ref = kn_out_ref = None
    if n_ctx:
        ck_ref, cv_ref = next(it), next(it)
    if rope:
        cos_ref, sin_ref = next(it), next(it)
    if qk_norm:
        qg_ref, kg_ref = next(it), next(it)
    if use_sink:
        sink_ref = next(it)
    o_ref = next(it)
    if emit_k:
        kn_out_ref = next(it)
    kbf_ref, vbf_ref = next(it), next(it)
    ckbf_ref = cvbf_ref = None
    if n_ctx:
        ckbf_ref, cvbf_ref = next(it), next(it)

    L = seq_len
    kvh = pl.program_id(1)
    qi = pl.program_id(2)

    @pl.when(qi == 0)
    def _():
        k = k_ref[...]
        if qk_norm:
            k = _rms_rows(k, kg_ref[...])
        if emit_k:
            kn_out_ref[...] = k
        if rope:
            k = _rope_rows(k, cos_ref[...], sin_ref[...])
        kbf_ref[...] = k.astype(BF16)
        vbf_ref[...] = v_ref[...].astype(BF16)
        if n_ctx:
            ckbf_ref[...] = ck_ref[...].astype(BF16)
            cvbf_ref[...] = cv_ref[...].astype(BF16)

    q0 = pl.multiple_of(qi * tq, tq)
    if window:
        wk = tq + 2 * window
        ws = pl.multiple_of(jnp.clip(q0 - window, 0, L - wk), LANES)
        k_loc = kbf_ref[pl.ds(ws, wk), :]
        v_loc = vbf_ref[pl.ds(ws, wk), :]
        qpos = q0 + lax.broadcasted_iota(jnp.int32, (tq, wk), 0)
        kpos = ws + lax.broadcasted_iota(jnp.int32, (tq, wk), 1)
        valid = jnp.abs(qpos - kpos) <= window
    else:
        k_loc = kbf_ref[...]
        v_loc = vbf_ref[...]

    for g in range(group):
        q = q_ref[:, g * HEAD_DIM:(g + 1) * HEAD_DIM]
        if qk_norm:
            q = _rms_rows(q, qg_ref[...])
        if rope:
            q = _rope_rows(q, cos_ref[pl.ds(q0, tq), :], sin_ref[pl.ds(q0, tq), :])
        q = (q * (HEAD_DIM ** -0.5)).astype(BF16)
        s_loc = _dot_nt(q, k_loc)
        if window:
            s_loc = jnp.where(valid, s_loc, NEG)
        m = jnp.max(s_loc, axis=-1, keepdims=True)
        if n_ctx:
            s_ctx = _dot_nt(q, ckbf_ref[...])
            m = jnp.maximum(m, jnp.max(s_ctx, axis=-1, keepdims=True))
        if use_sink:
            sink = sink_ref[kvh * group + g]
            m = jnp.maximum(m, sink)
        p_loc = jnp.exp(s_loc - m)
        denom = jnp.sum(p_loc, axis=-1, keepdims=True)
        acc = _dot(p_loc.astype(BF16), v_loc)
        if n_ctx:
            p_ctx = jnp.exp(s_ctx - m)
            denom = denom + jnp.sum(p_ctx, axis=-1, keepdims=True)
            acc = acc + _dot(p_ctx.astype(BF16), cvbf_ref[...])
        if use_sink:
            denom = denom + jnp.exp(sink - m)
        o_ref[:, g * HEAD_DIM:(g + 1) * HEAD_DIM] = (acc / denom).astype(o_ref.dtype)


def _attention(proj, *, q_off, k_off, v_off, n_heads, n_kv, row_block0, n_seq, seq_len, tq,
               ctx_k=None, ctx_v=None, cos=None, sin=None, q_gain=None, k_gain=None, sink=None,
               window=0, emit_k=False):
    L = seq_len
    group = n_heads // n_kv
    n_ctx = 0 if ctx_k is None else ctx_k.shape[1]
    rope = cos is not None
    qk_norm = q_gain is not None
    use_sink = sink is not None
    nq = L // tq
    qb, kb, vb = q_off // (group * HEAD_DIM), k_off // HEAD_DIM, v_off // HEAD_DIM
    in_specs = [
        pl.BlockSpec((tq, group * HEAD_DIM), lambda b, h, i: ((row_block0 + b) * nq + i, qb + h)),
        pl.BlockSpec((L, HEAD_DIM), lambda b, h, i: (row_block0 + b, kb + h)),
        pl.BlockSpec((L, HEAD_DIM), lambda b, h, i: (row_block0 + b, vb + h)),
    ]
    args = [proj, proj, proj]
    scratch = [pltpu.VMEM((L, HEAD_DIM), BF16), pltpu.VMEM((L, HEAD_DIM), BF16)]
    if n_ctx:
        in_specs += [pl.BlockSpec((None, n_ctx, HEAD_DIM), lambda b, h, i: (b, 0, h))] * 2
        args += [ctx_k, ctx_v]
        scratch += [pltpu.VMEM((n_ctx, HEAD_DIM), BF16), pltpu.VMEM((n_ctx, HEAD_DIM), BF16)]
    if rope:
        in_specs += [pl.BlockSpec((L, HEAD_DIM), lambda b, h, i: (0, 0))] * 2
        args += [cos, sin]
    if qk_norm:
        in_specs += [pl.BlockSpec((1, HEAD_DIM), lambda b, h, i: (0, 0))] * 2
        args += [q_gain.reshape(1, HEAD_DIM), k_gain.reshape(1, HEAD_DIM)]
    if use_sink:
        in_specs += [pl.BlockSpec(memory_space=pltpu.SMEM)]
        args += [sink]
    out_shape = [jax.ShapeDtypeStruct((n_seq * L, n_heads * HEAD_DIM), BF16)]
    out_specs = [pl.BlockSpec((tq, group * HEAD_DIM), lambda b, h, i: (b * nq + i, h))]
    if emit_k:
        out_shape += [jax.ShapeDtypeStruct((n_seq * L, n_kv * HEAD_DIM), F32)]
        out_specs += [pl.BlockSpec((L, HEAD_DIM), lambda b, h, i: (b, h))]
    kern = functools.partial(_attn_kernel, tq=tq, seq_len=L, group=group, n_ctx=n_ctx, use_sink=use_sink,
                             window=window, qk_norm=qk_norm, rope=rope, emit_k=emit_k)
    res = pl.pallas_call(
        kern,
        out_shape=tuple(out_shape),
        grid=(n_seq, n_kv, nq),
        in_specs=in_specs,
        out_specs=tuple(out_specs),
        scratch_shapes=scratch,
        compiler_params=pltpu.CompilerParams(dimension_semantics=("arbitrary", "arbitrary", "arbitrary"),
                                             vmem_limit_bytes=VMEM_LIMIT_BYTES),
        name="attention",
    )(*args)
    return res if emit_k else res[0]


def _out_mlp_kernel(*refs, n_parts, final):
    it = iter(refs)
    x_ref = next(it)
    a_refs = [next(it) for _ in range(n_parts)]
    wo_refs = [next(it) for _ in range(n_parts)]
    mod_ref, g2_ref, w1_ref, w2_ref = next(it), next(it), next(it), next(it)
    fg_ref = next(it) if final else None
    o_ref = next(it)
    x1_ref, h2_ref, acc_ref = next(it), next(it), next(it)
    j = pl.program_id(1)

    @pl.when(j == 0)
    def _():
        mix = _dot(a_refs[0][...], wo_refs[0][...])
        for a_ref, wo_ref in zip(a_refs[1:], wo_refs[1:]):
            mix = mix + _dot(a_ref[...], wo_ref[...])
        x1 = x_ref[...] + mod_ref[2:3, :] * mix
        x1_ref[...] = x1
        h2 = _rms_rows(x1, g2_ref[...]) * (1.0 + mod_ref[4:5, :]) + mod_ref[3:4, :]
        h2_ref[...] = h2.astype(BF16)
        acc_ref[...] = jnp.zeros_like(acc_ref)

    hid = jnp.maximum(_dot(h2_ref[...], w1_ref[...]), 0.0)
    acc_ref[...] += _dot((hid * hid).astype(BF16), w2_ref[...])

    @pl.when(j == pl.num_programs(1) - 1)
    def _():
        y = x1_ref[...] + mod_ref[5:6, :] * acc_ref[...]
        if final:
            y = _rms_rows(y, fg_ref[...])
        o_ref[...] = y


def _out_mlp(x, parts, wo_parts, mod_l, g2, w1, w2, n_prompt, sample_len, final_g=None, tm=512, tf=1024):
    n_tok, d = x.shape
    d_ff = w1.shape[1]
    final = final_g is not None
    row = functools.partial(_cond_row, n_prompt_tiles=n_prompt // tm, tiles_per_sample=sample_len // tm)
    in_specs = [pl.BlockSpec((tm, d), lambda i, j: (i, 0))]
    in_specs += [pl.BlockSpec((tm, p.shape[1]), lambda i, j: (i, 0)) for p in parts]
    in_specs += [pl.BlockSpec(w.shape, lambda i, j: (0, 0)) for w in wo_parts]
    in_specs += [
        pl.BlockSpec((None, 6, d), lambda i, j: (row(i), 0, 0)),
        pl.BlockSpec((1, d), lambda i, j: (0, 0)),
        pl.BlockSpec((d, tf), lambda i, j: (0, j)),
        pl.BlockSpec((tf, d), lambda i, j: (j, 0)),
    ]
    args = [x, *parts, *wo_parts, mod_l, g2.reshape(1, d), w1, w2]
    if final:
        in_specs += [pl.BlockSpec((1, d), lambda i, j: (0, 0))]
        args += [final_g.reshape(1, d)]
    kern = functools.partial(_out_mlp_kernel, n_parts=len(parts), final=final)
    return pl.pallas_call(
        kern,
        out_shape=jax.ShapeDtypeStruct((n_tok, d), F32),
        grid=(n_tok // tm, d_ff // tf),
        in_specs=in_specs,
        out_specs=pl.BlockSpec((tm, d), lambda i, j: (i, 0)),
        scratch_shapes=[pltpu.VMEM((tm, d), F32), pltpu.VMEM((tm, d), BF16), pltpu.VMEM((tm, d), F32)],
        compiler_params=pltpu.CompilerParams(dimension_semantics=("arbitrary", "arbitrary"),
                                             vmem_limit_bytes=VMEM_LIMIT_BYTES),
        name="out_mlp",
    )(*args)


def _rope_tables(seq_len):
    rows = seq_len // GRID_W
    inv = ROPE_THETA ** (-jnp.arange(ROT_FREQS, dtype=F32) / ROT_FREQS)
    ang_r = jnp.repeat(jnp.arange(rows, dtype=F32), GRID_W)[:, None] * inv
    ang_c = jnp.tile(jnp.arange(GRID_W, dtype=F32), rows)[:, None] * inv
    cos = jnp.concatenate([jnp.cos(ang_r), jnp.cos(ang_r), jnp.cos(ang_c), jnp.cos(ang_c)], axis=-1)
    sin = jnp.concatenate([-jnp.sin(ang_r), jnp.sin(ang_r), -jnp.sin(ang_c), jnp.sin(ang_c)], axis=-1)
    return cos.astype(F32), sin.astype(F32)


def kernel(x_prompt, x_sample, state_a, cache_b_kv, cache_c_kv, c, c_ctx, ada_w, ada_b, norm1_g, norm2_g,
           final_g, mlp_w1, mlp_w2, ev_w_in, a_conv, a_log, a_dt_bias, a_norm_g, b_sink, ev_w_out,
           od_w_in, c_qnorm_g, c_knorm_g, od_w_out):
    batch, seq, d = x_prompt.shape
    dec_batch, dec_seq, _ = x_sample.shape
    depth = ada_w.shape[0]
    n_prompt, n_sample = batch * seq, dec_batch * dec_seq
    past = cache_b_kv.shape[3]

    x = jnp.concatenate([x_prompt.reshape(n_prompt, d), x_sample.reshape(n_sample, d)], axis=0)
    cond = jnp.concatenate([c_ctx[None, :], c, jnp.zeros((8 - 1 - dec_batch, d), F32)], axis=0)
    mod = _adaln(cond, ada_w, ada_b).reshape(depth, 8, 6, d)

    cos, sin = _rope_tables(dec_seq)
    s_blk0 = n_prompt // dec_seq
    zero_state = jnp.zeros((batch, 2, A_HEADS, HEAD_DIM, HEAD_DIM), F32)
    lane_pad = lambda v: jnp.zeros((1, LANES), F32).at[0, 2 * A_HEADS:4 * A_HEADS].set(v.reshape(-1))

    new_a, new_b, new_c = [], [], []
    for l in range(depth):
        i = l // 2
        fin = final_g if l == depth - 1 else None
        w1 = mlp_w1[l].astype(BF16)
        w2 = mlp_w2[l].astype(BF16)
        if l % 2 == 0:
            w_in = ev_w_in[i]
            bg_off = 4 * A_WIDTH
            qb_off = bg_off + 4 * A_HEADS
            w_in = jnp.concatenate([w_in[:, :bg_off], w_in[:, qb_off:], w_in[:, bg_off:qb_off],
                                    jnp.zeros((d, LANES - 4 * A_HEADS), F32)], axis=1).astype(BF16)
            proj = _in_proj(x, mod[l], norm1_g[l], w_in, n_prompt, dec_seq)
            arow, dtrow = lane_pad(a_log[i]), lane_pad(a_dt_bias[i])
            oa_p, st_p = _gdn(proj, a_conv[i], arow, dtrow, a_norm_g[i], zero_state, 0, batch, seq, heads=A_HEADS,
                              steps=seq // CHUNK)
            oa_s, _ = _gdn(proj, a_conv[i], arow, dtrow, a_norm_g[i], state_a[:, i], s_blk0, dec_batch, dec_seq,
                           heads=1, steps=8)
            kv_cols = B_KV_HEADS * HEAD_DIM
            ob_p = _attention(proj, q_off=EV_QB_OFF, k_off=EV_KVB_OFF, v_off=EV_KVB_OFF + kv_cols,
                              n_heads=B_HEADS, n_kv=B_KV_HEADS, row_block0=0, n_seq=batch, seq_len=seq,
                              tq=seq, sink=b_sink[i])
            ob_s = _attention(proj, q_off=EV_QB_OFF, k_off=EV_KVB_OFF, v_off=EV_KVB_OFF + kv_cols,
                              n_heads=B_HEADS, n_kv=B_KV_HEADS, row_block0=s_blk0, n_seq=dec_batch,
                              seq_len=dec_seq, tq=256,
                              ctx_k=cache_b_kv[:, i, 0].reshape(dec_batch, past, kv_cols),
                              ctx_v=cache_b_kv[:, i, 1].reshape(dec_batch, past, kv_cols),
                              cos=cos, sin=sin, sink=b_sink[i], window=WINDOW)
            new_a.append(st_p)
            kvb = proj[:n_prompt, EV_KVB_OFF:EV_KVB_OFF + 2 * kv_cols]
            new_b.append(jnp.transpose(kvb.reshape(batch, seq, 2, B_KV_HEADS, HEAD_DIM), (0, 2, 1, 3, 4)))
            parts = [jnp.concatenate([oa_p, oa_s], axis=0), jnp.concatenate([ob_p, ob_s], axis=0)]
            w_out = ev_w_out[i].astype(BF16)
            wo_parts = [w_out[:A_WIDTH], w_out[A_WIDTH:]]
        else:
            proj = _in_proj(x, mod[l], norm1_g[l], od_w_in[i].astype(BF16), n_prompt, dec_seq)
            kv_cols = C_KV_HEADS * HEAD_DIM
            oc_p, kn_p = _attention(proj, q_off=0, k_off=OD_K_OFF, v_off=OD_V_OFF, n_heads=C_HEADS,
                                    n_kv=C_KV_HEADS, row_block0=0, n_seq=batch, seq_len=seq, tq=seq,
                                    q_gain=c_qnorm_g[i], k_gain=c_knorm_g[i], emit_k=True)
            oc_s = _attention(proj, q_off=0, k_off=OD_K_OFF, v_off=OD_V_OFF, n_heads=C_HEADS,
                              n_kv=C_KV_HEADS, row_block0=s_blk0, n_seq=dec_batch, seq_len=dec_seq, tq=256,
                              ctx_k=cache_c_kv[:, i, 0].reshape(dec_batch, past, kv_cols),
                              ctx_v=cache_c_kv[:, i, 1].reshape(dec_batch, past, kv_cols),
                              cos=cos, sin=sin, q_gain=c_qnorm_g[i], k_gain=c_knorm_g[i])
            v_p = proj[:n_prompt, OD_V_OFF:OD_V_OFF + kv_cols]
            new_c.append(jnp.stack([kn_p.reshape(batch, seq, C_KV_HEADS, HEAD_DIM),
                                    v_p.reshape(batch, seq, C_KV_HEADS, HEAD_DIM)], axis=1))
            parts = [jnp.concatenate([oc_p, oc_s], axis=0)]
            wo_parts = [od_w_out[i].astype(BF16)]
        x = _out_mlp(x, parts, wo_parts, mod[l], norm2_g[l], w1, w2, n_prompt, dec_seq, final_g=fin)

    y_prompt = x[:n_prompt].reshape(batch, seq, d)
    y_sample = x[n_prompt:].reshape(dec_batch, dec_seq, d)
    return (y_prompt, y_sample, jnp.stack(new_a, axis=1), jnp.stack(new_b, axis=1), jnp.stack(new_c, axis=1))
```

```python
import functools

import numpy as np
import jax
import jax.numpy as jnp
from jax import lax
from jax.experimental import pallas as pl
from jax.experimental.pallas import tpu as pltpu

F32 = jnp.float32
BF16 = jnp.bfloat16
HIGHEST = lax.Precision.HIGHEST

HEAD_DIM = 128
GRID_W = 64
A_HEADS = 4
CHUNK = 64
B_HEADS = 4
B_KV_HEADS = 2
WINDOW = 128
C_HEADS = 8
C_KV_HEADS = 2
ROT_FREQS = HEAD_DIM // 4
ROPE_THETA = 10000.0
EPS = 1e-6
NEG = -1e30
LOG2_E = 1.4426950408889634
A_WIDTH = A_HEADS * HEAD_DIM
B_WIDTH = B_HEADS * HEAD_DIM
C_WIDTH = C_HEADS * HEAD_DIM

LANES = 128
VMEM_LIMIT_BYTES = 56 * 1024 * 1024

EV_GATE_OFF = 3 * A_WIDTH
EV_QB_OFF = 4 * A_WIDTH
EV_KVB_OFF = EV_QB_OFF + B_WIDTH
EV_BG_OFF = EV_KVB_OFF + 2 * B_KV_HEADS * HEAD_DIM
EV_COLS = EV_BG_OFF + LANES
OD_K_OFF = C_WIDTH
OD_V_OFF = C_WIDTH + C_KV_HEADS * HEAD_DIM


def _sigmoid(x):
    return 1.0 / (1.0 + jnp.exp(-x))


def _silu(x):
    return x * _sigmoid(x)


def _softplus(x):
    return jnp.maximum(x, 0.0) + jnp.log1p(jnp.exp(-jnp.abs(x)))


def _rms_rows(x, g):
    return x * lax.rsqrt(jnp.mean(x * x, axis=-1, keepdims=True) + EPS) * g


def _dot(a, b, precision=None):
    return jnp.dot(a, b, preferred_element_type=F32, precision=precision)


def _dot_nt(a, b, precision=None):
    return lax.dot_general(a, b, (((1,), (1,)), ((), ())), preferred_element_type=F32, precision=precision)


def _cond_row(i, n_prompt_tiles, tiles_per_sample):
    return jnp.where(i < n_prompt_tiles, 0, 1 + (i - n_prompt_tiles) // tiles_per_sample)


def _adaln_kernel(cond_ref, w_ref, b_ref, o_ref):
    s = _silu(cond_ref[...])
    o_ref[0] = _dot(s, w_ref[0], HIGHEST) + b_ref[0]


def _adaln(cond, ada_w, ada_b, tn=1024):
    depth, d, n = ada_w.shape
    rows = cond.shape[0]
    return pl.pallas_call(
        _adaln_kernel,
        out_shape=jax.ShapeDtypeStruct((depth, rows, n), F32),
        grid=(depth, n // tn),
        in_specs=[
            pl.BlockSpec((rows, d), lambda l, j: (0, 0)),
            pl.BlockSpec((1, d, tn), lambda l, j: (l, 0, j)),
            pl.BlockSpec((1, 1, tn), lambda l, j: (l, 0, j)),
        ],
        out_specs=pl.BlockSpec((1, rows, tn), lambda l, j: (l, 0, j)),
        compiler_params=pltpu.CompilerParams(dimension_semantics=("arbitrary", "arbitrary"),
                                             vmem_limit_bytes=VMEM_LIMIT_BYTES),
        name="adaln",
    )(cond, ada_w, ada_b.reshape(depth, 1, n))


def _token_specs(arrays, tm, n_prompt_tiles):
    if len(arrays) == 1:
        return [pl.BlockSpec((tm, arrays[0].shape[1]), lambda i, *_: (i, 0))]
    return [pl.BlockSpec((tm, arrays[0].shape[1]), lambda i, *_: (jnp.minimum(i, n_prompt_tiles - 1), 0)),
            pl.BlockSpec((tm, arrays[1].shape[1]), lambda i, *_: (jnp.maximum(i - n_prompt_tiles, 0), 0))]


def _token_tile(refs, n_prompt_tiles):
    if len(refs) == 1:
        return refs[0][...]
    return jnp.where(pl.program_id(0) < n_prompt_tiles, refs[0][...], refs[1][...])


def _inproj_kernel(*refs, n_x, n_prompt_tiles):
    x_refs = refs[:n_x]
    mod_ref, g_ref, w_ref, o_ref = refs[n_x:]
    x = _token_tile(x_refs, n_prompt_tiles)
    h = _rms_rows(x, g_ref[...]) * (1.0 + mod_ref[1:2, :]) + mod_ref[0:1, :]
    o_ref[...] = _dot(h.astype(BF16), w_ref[...])


def _in_proj(xs, mod_l, g, w, n_prompt, sample_len, tm=512):
    n_tok = sum(a.shape[0] for a in xs)
    d = xs[0].shape[1]
    n = w.shape[1]
    npt = n_prompt // tm
    row = functools.partial(_cond_row, n_prompt_tiles=npt, tiles_per_sample=sample_len // tm)
    return pl.pallas_call(
        functools.partial(_inproj_kernel, n_x=len(xs), n_prompt_tiles=npt),
        out_shape=jax.ShapeDtypeStruct((n_tok, n), F32),
        grid=(n_tok // tm,),
        in_specs=_token_specs(xs, tm, npt) + [
            pl.BlockSpec((None, 6, d), lambda i: (row(i), 0, 0)),
            pl.BlockSpec((1, d), lambda i: (0, 0)),
            pl.BlockSpec((d, n), lambda i: (0, 0)),
        ],
        out_specs=pl.BlockSpec((tm, n), lambda i: (i, 0)),
        compiler_params=pltpu.CompilerParams(dimension_semantics=("arbitrary",),
                                             vmem_limit_bytes=VMEM_LIMIT_BYTES),
        name="in_proj",
    )(*xs, mod_l, g.reshape(1, d), w)


def _split_bf16(x):
    hi = x.astype(BF16)
    lo = (x - hi.astype(F32)).astype(BF16)
    return hi, lo


def _gdn_kernel(q_ref, k_ref, v_ref, gate_ref, bg_ref, cwq_ref, cwk_ref, cwv_ref, arow_ref, dtrow_ref,
                ng_ref, s0_ref, o_ref, sout_ref,
                qn_ref, kn_ref, vn_ref, beta_ref, gc_ref, gcrow_ref, s_ref, oacc_ref,
                mq_a, c_a, oo_a, eg_a, mq_b, c_b, oo_b, eg_b, *, seq_len, heads, steps):
    L = seq_len
    nc = L // CHUNK
    nblk = nc // steps
    h0 = pl.program_id(1) * heads
    row_idx = lax.broadcasted_iota(jnp.int32, (L, LANES), 0)
    lane_idx = lax.broadcasted_iota(jnp.int32, (L, LANES), 1)
    chunk_row = row_idx % CHUNK

    def conv_silu(x, cw):
        prev = jnp.where(row_idx == 0, 0.0, pltpu.roll(x, 1, 0))
        nxt = jnp.where(row_idx == L - 1, 0.0, pltpu.roll(x, L - 1, 0))
        return _silu(prev * cw[0:1, :] + x * cw[1:2, :] + nxt * cw[2:3, :])

    def l2norm(x):
        return x * lax.rsqrt(jnp.sum(x * x, axis=-1, keepdims=True) + EPS)

    bg = bg_ref[...]
    beta_all = _sigmoid(bg)
    g_all = -jnp.exp(arow_ref[...]) * _softplus(bg + dtrow_ref[...])

    def pick(x, col):
        col_vals = jnp.sum(jnp.where(lane_idx == col, x, 0.0), axis=1, keepdims=True)
        return jnp.broadcast_to(col_vals, (L, LANES))

    def chunk_cumsum(g, reverse):
        shift = 1
        while shift < CHUNK:
            if reverse:
                g = g + jnp.where(chunk_row < CHUNK - shift, pltpu.roll(g, L - shift, 0), 0.0)
            else:
                g = g + jnp.where(chunk_row >= shift, pltpu.roll(g, shift, 0), 0.0)
            shift *= 2
        return g

    for hh in range(heads):
        cols = slice(hh * HEAD_DIM, (hh + 1) * HEAD_DIM)
        qn_ref[:, cols] = l2norm(conv_silu(q_ref[:, cols], cwq_ref[:, cols])) * (HEAD_DIM ** -0.5)
        kn_ref[:, cols] = l2norm(conv_silu(k_ref[:, cols], cwk_ref[:, cols]))
        vn_ref[:, cols] = conv_silu(v_ref[:, cols], cwv_ref[:, cols])
        s_ref[hh, 0] = s0_ref[0, hh]
        s_ref[hh, 1] = s0_ref[1, hh]
        head = h0 + hh
        beta_ref[hh, 0] = pick(beta_all, head)
        beta_ref[hh, 1] = pick(beta_all, A_HEADS + head)
        for d in range(2):
            gc = chunk_cumsum(pick(g_all, (2 + d) * A_HEADS + head), reverse=(d == 1))
            gc_ref[hh, d] = gc
            for r in range(L // LANES):
                rows = jnp.transpose(gc[r * LANES:(r + 1) * LANES, :])[0:8, :]
                gcrow_ref[hh, d, 0, r] = rows
                gcrow_ref[hh, d, 1, r] = pltpu.roll(rows, CHUNK, 1)

    r64 = lax.broadcasted_iota(jnp.int32, (CHUNK, LANES), 0)
    l64 = lax.broadcasted_iota(jnp.int32, (CHUNK, LANES), 1)
    is_f = l64 < CHUNK
    is_f_row = lax.broadcasted_iota(jnp.int32, (1, LANES), 1) < CHUNK
    cpos = jnp.where(is_f, l64, l64 - CHUNK)
    ahead = jnp.where(is_f, r64 - cpos, cpos - r64)
    incl = ahead >= 0
    strict = ahead > 0
    eye = jnp.where(r64 == cpos, 1.0, 0.0)
    is_f2 = lax.broadcasted_iota(jnp.int32, (2 * CHUNK, LANES), 1) < CHUNK
    zeros_c = jnp.zeros((CHUNK, HEAD_DIM), F32)

    def aligned(off, m):
        return off if isinstance(off, int) else pl.multiple_of(off, m)

    def half(n):
        return n // 2 if isinstance(n, int) else lax.shift_right_logical(n, 1)

    def block_diag(xp):
        z = jnp.zeros_like(xp)
        return jnp.concatenate([jnp.where(is_f, xp, z), jnp.where(is_f, z, xp)], axis=0)

    def prep_stages(blk, slot):
        mq_ref, c_ref, oo_ref, eg_ref = slot
        probs = [(si, hh) for si in range(steps) for hh in range(heads)]
        st = {p: {} for p in probs}

        def load(p):
            si, hh = p
            j = blk * steps + si
            n_b = nc - 1 - j
            cols = slice(hh * HEAD_DIM, (hh + 1) * HEAD_DIM)
            off_f = aligned(j * CHUNK, CHUNK)
            off_b = aligned(n_b * CHUNK, CHUNK)
            q_f, k_f, v_f = (r[pl.ds(off_f, CHUNK), cols] for r in (qn_ref, kn_ref, vn_ref))
            q_b, k_b, v_b = (r[pl.ds(off_b, CHUNK), cols] for r in (qn_ref, kn_ref, vn_ref))
            beta_f = beta_ref[hh, 0, pl.ds(off_f, CHUNK), :]
            beta_b = beta_ref[hh, 1, pl.ds(off_b, CHUNK), :]
            gc_f = gc_ref[hh, 0, pl.ds(off_f, CHUNK), :]
            gc_b = gc_ref[hh, 1, pl.ds(off_b, CHUNK), :]
            row_f = gcrow_ref[hh, 0, si % 2, half(j)][0:1, :]
            row_b = gcrow_ref[hh, 1, si % 2, half(n_b)][0:1, :]
            gcrow = jnp.where(is_f_row, row_f, row_b)
            diff = jnp.where(is_f, gc_f, gc_b) - gcrow
            decay = jnp.where(incl, jnp.exp(jnp.where(incl, diff, 0.0)), 0.0)
            kb_f = k_f * beta_f
            kb_b = k_b * beta_b
            e_f = jnp.exp(gc_f)
            e_b = jnp.exp(gc_b)
            glast_f = gc_f[CHUNK - 1:CHUNK, :]
            glast_b = gc_b[0:1, :]
            kdt = jnp.transpose(jnp.concatenate([k_f * jnp.exp(glast_f - gc_f), k_b * jnp.exp(glast_b - gc_b)],
                                                axis=0))
            zk = jnp.zeros_like(kdt)
            eg_ref[si, hh, 0] = jnp.broadcast_to(jnp.exp(glast_f), (8, LANES))
            eg_ref[si, hh, 1] = jnp.broadcast_to(jnp.exp(glast_b), (8, LANES))
            st[p].update(
                decay=decay,
                lhs=jnp.concatenate([jnp.concatenate([kb_f, kb_b], axis=1),
                                     jnp.concatenate([q_f, q_b], axis=1)], axis=0).astype(BF16),
                rhs=jnp.concatenate([jnp.concatenate([k_f, zeros_c], axis=1),
                                     jnp.concatenate([zeros_c, k_b], axis=1)], axis=0).astype(BF16),
                rhs_uw=jnp.concatenate([jnp.concatenate([v_f * beta_f, kb_f * e_f], axis=1),
                                        jnp.concatenate([v_b * beta_b, kb_b * e_b], axis=1)], axis=0).astype(BF16),
                kd2=jnp.concatenate([jnp.where(is_f2, kdt, zk), jnp.where(is_f2, zk, kdt)], axis=0).astype(BF16),
                qe=(q_f * e_f, q_b * e_b))

        def gram(p):
            s = st[p]
            aq = _dot_nt(s.pop("lhs"), s.pop("rhs"))
            decay = s.pop("decay")
            x = -jnp.where(strict, aq[:CHUNK] * decay, 0.0)
            s["qk"] = jnp.where(incl, aq[CHUNK:] * decay, 0.0)
            s["t"] = eye + x
            s["p"] = x

        def square(p):
            s = st[p]
            p_hi, p_lo = _split_bf16(s["p"])
            r = _dot(jnp.concatenate([p_hi, p_lo], axis=0), block_diag(p_hi))
            s["p"] = r[:CHUNK] + r[CHUNK:] + _dot(p_hi, block_diag(p_lo))

        def double(p):
            s = st[p]
            p_hi, p_lo = _split_bf16(s["p"])
            t_hi, t_lo = _split_bf16(s["t"])
            r = _dot(jnp.concatenate([p_hi, p_lo], axis=0),
                     jnp.concatenate([block_diag(t_hi), block_diag(p_hi)], axis=1))
            r = r[:CHUNK] + r[CHUNK:] + _dot(p_hi, jnp.concatenate([block_diag(t_lo), block_diag(p_lo)], axis=1))
            s["t"] = s["t"] + r[:, :LANES]
            s["p"] = r[:, LANES:]

        def last(p):
            s = st[p]
            p_hi, p_lo = _split_bf16(s.pop("p"))
            t_hi, t_lo = _split_bf16(s["t"])
            r = _dot(jnp.concatenate([p_hi, p_lo], axis=0), block_diag(t_hi))
            s["t"] = s["t"] + r[:CHUNK] + r[CHUNK:] + _dot(p_hi, block_diag(t_lo))

        def solve(p):
            s = st[p]
            s["uw"] = _dot(block_diag(s.pop("t")).astype(BF16), s.pop("rhs_uw"))

        def fold(p):
            si, hh = p
            s = st[p]
            lhs3 = jnp.concatenate([s.pop("kd2"), block_diag(s.pop("qk")).astype(BF16)], axis=0)
            big = _dot(lhs3, s.pop("uw").astype(BF16))
            qe = s.pop("qe")
            for d in range(2):
                rows_s = slice(d * 2 * CHUNK, (d + 1) * 2 * CHUNK)
                rows_o = slice(4 * CHUNK + d * CHUNK, 4 * CHUNK + (d + 1) * CHUNK)
                c_ref[si, hh, d] = big[rows_s, :HEAD_DIM]
                oo_ref[si, hh, d] = big[rows_o, :HEAD_DIM]
                mq_ref[si, hh, d] = jnp.concatenate([big[rows_s, HEAD_DIM:], qe[d] - big[rows_o, HEAD_DIM:]],
                                                    axis=0).astype(BF16)

        n_double = 0
        span = 2
        while 2 * span < CHUNK:
            n_double += 1
            span *= 2
        stage_fns = [load, gram, square] + [double] * n_double + [last, solve, fold]
        return [functools.partial(lambda fn: [fn(p) for p in probs], fn) for fn in stage_fns]

    def scan_step(blk, slot, si):
        mq_ref, c_ref, oo_ref, eg_ref = slot
        j = blk * steps + si
        for hh in range(heads):
            cols = slice(hh * HEAD_DIM, (hh + 1) * HEAD_DIM)
            for d, n in ((0, j), (1, nc - 1 - j)):
                s = s_ref[hh, d]
                r = _dot(mq_ref[si, hh, d], s.astype(BF16))
                s_ref[hh, d] = s * eg_ref[si, hh, d][0:1, :] - r[:2 * CHUNK] + c_ref[si, hh, d]
                oacc_ref[d, pl.ds(aligned(n * CHUNK, CHUNK), CHUNK), cols] = r[2 * CHUNK:] + oo_ref[si, hh, d]

    def run(prep, scan):
        stages = prep_stages(*prep) if prep else []
        n_scan = steps if scan else 0
        for k in range(max(len(stages), n_scan)):
            if k < len(stages):
                stages[k]()
            if k < n_scan:
                scan_step(scan[0], scan[1], k)

    slot_a = (mq_a, c_a, oo_a, eg_a)
    slot_b = (mq_b, c_b, oo_b, eg_b)
    run((0, slot_a), None)
    if nblk == 1:
        run(None, (0, slot_a))
    else:
        def body(i, carry):
            blk = 2 * i
            run((blk + 1, slot_b), (blk, slot_a))
            run((blk + 2, slot_a), (blk + 1, slot_b))
            return carry

        lax.fori_loop(0, nblk // 2 - 1, body, 0)
        run((nblk - 1, slot_b), (nblk - 2, slot_a))
        run(None, (nblk - 1, slot_b))

    for hh in range(heads):
        cols = slice(hh * HEAD_DIM, (hh + 1) * HEAD_DIM)
        sout_ref[0, hh] = s_ref[hh, 0]
        sout_ref[1, hh] = s_ref[hh, 1]
        o = oacc_ref[0, :, cols] + oacc_ref[1, :, cols]
        o = _rms_rows(o, ng_ref[...]) * _silu(gate_ref[:, cols])
        o_ref[:, cols] = o.astype(o_ref.dtype)


def _gdn(proj, conv_w, arow, dtrow, norm_g, s0, row_block0, n_seq, seq_len, heads, steps):
    L = seq_len
    nblk = L // CHUNK // steps
    assert L % (CHUNK * steps) == 0 and steps % 2 == 0 and 2 * CHUNK == LANES and A_HEADS % heads == 0
    assert nblk == 1 or nblk % 2 == 0
    width = heads * HEAD_DIM
    groups = A_HEADS // heads
    blk = lambda base: pl.BlockSpec((L, width), lambda b, h: (row_block0 + b, base * groups + h))
    wblk = lambda base: pl.BlockSpec((3, width), lambda b, h: (0, base * groups + h))
    state_spec = pl.BlockSpec((None, 2, heads, HEAD_DIM, HEAD_DIM), lambda b, h: (b, 0, h, 0, 0))
    slot = [pltpu.VMEM((steps, heads, 2, 3 * CHUNK, HEAD_DIM), BF16),
            pltpu.VMEM((steps, heads, 2, 2 * CHUNK, HEAD_DIM), F32),
            pltpu.VMEM((steps, heads, 2, CHUNK, HEAD_DIM), F32),
            pltpu.VMEM((steps, heads, 2, 8, LANES), F32)]
    kern = functools.partial(_gdn_kernel, seq_len=L, heads=heads, steps=steps)
    out, s_out = pl.pallas_call(
        kern,
        out_shape=(jax.ShapeDtypeStruct((n_seq * L, A_WIDTH), BF16),
                   jax.ShapeDtypeStruct((n_seq, 2, A_HEADS, HEAD_DIM, HEAD_DIM), F32)),
        grid=(n_seq, groups),
        in_specs=[
            blk(0), blk(1), blk(2), blk(3),
            pl.BlockSpec((L, LANES), lambda b, h: (row_block0 + b, EV_BG_OFF // LANES)),
            wblk(0), wblk(1), wblk(2),
            pl.BlockSpec((1, LANES), lambda b, h: (0, 0)),
            pl.BlockSpec((1, LANES), lambda b, h: (0, 0)),
            pl.BlockSpec((1, HEAD_DIM), lambda b, h: (0, 0)),
            state_spec,
        ],
        out_specs=(pl.BlockSpec((L, width), lambda b, h: (b, h)), state_spec),
        scratch_shapes=[
            pltpu.VMEM((L, width), F32), pltpu.VMEM((L, width), F32), pltpu.VMEM((L, width), F32),
            pltpu.VMEM((heads, 2, L, LANES), F32), pltpu.VMEM((heads, 2, L, LANES), F32),
            pltpu.VMEM((heads, 2, 2, L // LANES, 8, LANES), F32),
            pltpu.VMEM((heads, 2, HEAD_DIM, HEAD_DIM), F32),
            pltpu.VMEM((2, L, width), F32),
        ] + slot + slot,
        compiler_params=pltpu.CompilerParams(dimension_semantics=("arbitrary", "arbitrary"),
                                             vmem_limit_bytes=VMEM_LIMIT_BYTES),
        name="gdn",
    )(proj, proj, proj, proj, proj, conv_w, conv_w, conv_w, arow, dtrow, norm_g.reshape(1, HEAD_DIM), s0)
    return out, s_out


def _rope_rows(x, cos, sin_signed):
    lane = lax.broadcasted_iota(jnp.int32, x.shape, 1)
    swapped = jnp.where((lane % (2 * ROT_FREQS)) < ROT_FREQS,
                        pltpu.roll(x, HEAD_DIM - ROT_FREQS, 1), pltpu.roll(x, ROT_FREQS, 1))
    return x * cos + swapped * sin_signed


def _attn_kernel(*refs, tq, seq_len, group, n_ctx, use_sink, window, qk_norm, rope, emit_k):
    it = iter(refs)
    q_ref, k_ref, v_ref = next(it), next(it), next(it)
    ck_ref = cv_ref = cos_ref = sin_ref = qg_ref = kg_ref = sink_ref = kn_out_ref = None
    if n_ctx:
        ck_ref, cv_ref = next(it), next(it)
    if rope:
        cos_ref, sin_ref = next(it), next(it)
    if qk_norm:
        qg_ref, kg_ref = next(it), next(it)
    if use_sink:
        sink_ref = next(it)
    o_ref = next(it)
    if emit_k:
        kn_out_ref = next(it)
    kbf_ref, vbf_ref = next(it), next(it)
    ckbf_ref = cvbf_ref = None
    if n_ctx:
        ckbf_ref, cvbf_ref = next(it), next(it)

    L = seq_len
    kvh = pl.program_id(1)
    qi = pl.program_id(2)

    def ones_column(rows):
        lane = lax.broadcasted_iota(jnp.int32, (rows, HEAD_DIM), 1)
        return jnp.where(lane == 0, 1.0, 0.0).astype(BF16)

    @pl.when(qi == 0)
    def _():
        k = k_ref[...]
        if qk_norm:
            k = _rms_rows(k, kg_ref[...])
        if emit_k:
            kn_out_ref[...] = k
        if rope:
            k = _rope_rows(k, cos_ref[...], sin_ref[...])
        kbf_ref[...] = k.astype(BF16)
        vbf_ref[:, :HEAD_DIM] = v_ref[...].astype(BF16)
        vbf_ref[:, HEAD_DIM:] = ones_column(L)
        if n_ctx:
            ckbf_ref[...] = ck_ref[...].astype(BF16)
            cvbf_ref[:, :HEAD_DIM] = cv_ref[...].astype(BF16)
            cvbf_ref[:, HEAD_DIM:] = ones_column(n_ctx)

    q0 = pl.multiple_of(qi * tq, tq)
    if window:
        wk = tq + 2 * window
        ws = pl.multiple_of(jnp.clip(q0 - window, 0, L - wk), LANES)
        k_loc = kbf_ref[pl.ds(ws, wk), :]
        v_loc = vbf_ref[pl.ds(ws, wk), :]
        qpos = q0 + lax.broadcasted_iota(jnp.int32, (tq, wk), 0)
        kpos = ws + lax.broadcasted_iota(jnp.int32, (tq, wk), 1)
        valid = jnp.abs(qpos - kpos) <= window
    else:
        k_loc = kbf_ref[...]
        v_loc = vbf_ref[...]

    for g in range(group):
        q = q_ref[:, g * HEAD_DIM:(g + 1) * HEAD_DIM]
        if qk_norm:
            q = _rms_rows(q, qg_ref[...])
        if rope:
            q = _rope_rows(q, cos_ref[pl.ds(q0, tq), :], sin_ref[pl.ds(q0, tq), :])
        q = (q * (HEAD_DIM ** -0.5 * LOG2_E)).astype(BF16)
        s_loc = _dot_nt(q, k_loc)
        if window:
            s_loc = jnp.where(valid, s_loc, NEG)
        m = jnp.max(s_loc, axis=-1, keepdims=True)
        if n_ctx:
            s_ctx = _dot_nt(q, ckbf_ref[...])
            m = jnp.maximum(m, jnp.max(s_ctx, axis=-1, keepdims=True))
        if use_sink:
            sink = sink_ref[kvh * group + g] * LOG2_E
            m = jnp.maximum(m, sink)
        acc = _dot(jnp.exp2(s_loc - m).astype(BF16), v_loc)
        if n_ctx:
            acc = acc + _dot(jnp.exp2(s_ctx - m).astype(BF16), cvbf_ref[...])
        denom = acc[:, HEAD_DIM:HEAD_DIM + 1]
        if use_sink:
            denom = denom + jnp.exp2(sink - m)
        o_ref[:, g * HEAD_DIM:(g + 1) * HEAD_DIM] = (acc[:, :HEAD_DIM] / denom).astype(o_ref.dtype)


def _attention(proj, *, q_off, k_off, v_off, n_heads, n_kv, row_block0, n_seq, seq_len, tq,
               ctx_k=None, ctx_v=None, cos=None, sin=None, q_gain=None, k_gain=None, sink=None,
               window=0, emit_k=False):
    L = seq_len
    group = n_heads // n_kv
    n_ctx = 0 if ctx_k is None else ctx_k.shape[1]
    rope = cos is not None
    qk_norm = q_gain is not None
    use_sink = sink is not None
    nq = L // tq
    qb, kb, vb = q_off // (group * HEAD_DIM), k_off // HEAD_DIM, v_off // HEAD_DIM
    in_specs = [
        pl.BlockSpec((tq, group * HEAD_DIM), lambda b, h, i: ((row_block0 + b) * nq + i, qb + h)),
        pl.BlockSpec((L, HEAD_DIM), lambda b, h, i: (row_block0 + b, kb + h)),
        pl.BlockSpec((L, HEAD_DIM), lambda b, h, i: (row_block0 + b, vb + h)),
    ]
    args = [proj, proj, proj]
    scratch = [pltpu.VMEM((L, HEAD_DIM), BF16), pltpu.VMEM((L, 2 * HEAD_DIM), BF16)]
    if n_ctx:
        in_specs += [pl.BlockSpec((None, n_ctx, HEAD_DIM), lambda b, h, i: (b, 0, h))] * 2
        args += [ctx_k, ctx_v]
        scratch += [pltpu.VMEM((n_ctx, HEAD_DIM), BF16), pltpu.VMEM((n_ctx, 2 * HEAD_DIM), BF16)]
    if rope:
        in_specs += [pl.BlockSpec((L, HEAD_DIM), lambda b, h, i: (0, 0))] * 2
        args += [cos, sin]
    if qk_norm:
        in_specs += [pl.BlockSpec((1, HEAD_DIM), lambda b, h, i: (0, 0))] * 2
        args += [q_gain.reshape(1, HEAD_DIM), k_gain.reshape(1, HEAD_DIM)]
    if use_sink:
        in_specs += [pl.BlockSpec(memory_space=pltpu.SMEM)]
        args += [sink]
    out_shape = [jax.ShapeDtypeStruct((n_seq * L, n_heads * HEAD_DIM), BF16)]
    out_specs = [pl.BlockSpec((tq, group * HEAD_DIM), lambda b, h, i: (b * nq + i, h))]
    if emit_k:
        out_shape += [jax.ShapeDtypeStruct((n_seq * L, n_kv * HEAD_DIM), F32)]
        out_specs += [pl.BlockSpec((L, HEAD_DIM), lambda b, h, i: (b, h))]
    kern = functools.partial(_attn_kernel, tq=tq, seq_len=L, group=group, n_ctx=n_ctx, use_sink=use_sink,
                             window=window, qk_norm=qk_norm, rope=rope, emit_k=emit_k)
    res = pl.pallas_call(
        kern,
        out_shape=tuple(out_shape),
        grid=(n_seq, n_kv, nq),
        in_specs=in_specs,
        out_specs=tuple(out_specs),
        scratch_shapes=scratch,
        compiler_params=pltpu.CompilerParams(dimension_semantics=("arbitrary", "arbitrary", "arbitrary"),
                                             vmem_limit_bytes=VMEM_LIMIT_BYTES),
        name="attention",
    )(*args)
    return res if emit_k else res[0]


def _out_mlp_kernel(*refs, n_x, n_parts, final, n_prompt_tiles):
    it = iter(refs)
    x_refs = [next(it) for _ in range(n_x)]
    a_refs = [[next(it), next(it)] for _ in range(n_parts)]
    wo_refs = [next(it) for _ in range(n_parts)]
    mod_ref, g2_ref, w1_ref, w2_ref = next(it), next(it), next(it), next(it)
    fg_ref = next(it) if final else None
    o_refs = [next(it) for _ in range(2 if final else 1)]
    x1_ref, h2_ref, acc_ref = next(it), next(it), next(it)
    i = pl.program_id(0)
    j = pl.program_id(1)

    @pl.when(j == 0)
    def _():
        mix = None
        for pair, wo_ref in zip(a_refs, wo_refs):
            term = _dot(_token_tile(pair, n_prompt_tiles), wo_ref[...])
            mix = term if mix is None else mix + term
        x1 = _token_tile(x_refs, n_prompt_tiles) + mod_ref[2:3, :] * mix
        x1_ref[...] = x1
        h2 = _rms_rows(x1, g2_ref[...]) * (1.0 + mod_ref[4:5, :]) + mod_ref[3:4, :]
        h2_ref[...] = h2.astype(BF16)
        acc_ref[...] = jnp.zeros_like(acc_ref)

    hid = jnp.maximum(_dot(h2_ref[...], w1_ref[...]), 0.0)
    acc_ref[...] += _dot((hid * hid).astype(BF16), w2_ref[...])

    last = j == pl.num_programs(1) - 1
    if final:
        @pl.when(last & (i < n_prompt_tiles))
        def _():
            o_refs[0][...] = _rms_rows(x1_ref[...] + mod_ref[5:6, :] * acc_ref[...], fg_ref[...])

        @pl.when(last & (i >= n_prompt_tiles))
        def _():
            o_refs[1][...] = _rms_rows(x1_ref[...] + mod_ref[5:6, :] * acc_ref[...], fg_ref[...])
    else:
        @pl.when(last)
        def _():
            o_refs[0][...] = x1_ref[...] + mod_ref[5:6, :] * acc_ref[...]


def _out_mlp(xs, parts, wo_parts, mod_l, g2, w1, w2, n_prompt, sample_len, final_g=None, tm=512, tf=2048):
    n_tok = sum(a.shape[0] for a in xs)
    d = xs[0].shape[1]
    d_ff = w1.shape[1]
    final = final_g is not None
    npt = n_prompt // tm
    row = functools.partial(_cond_row, n_prompt_tiles=npt, tiles_per_sample=sample_len // tm)
    in_specs = _token_specs(xs, tm, npt)
    for pair in parts:
        in_specs += _token_specs(pair, tm, npt)
    in_specs += [pl.BlockSpec(w.shape, lambda i, j: (0, 0)) for w in wo_parts]
    in_specs += [
        pl.BlockSpec((None, 6, d), lambda i, j: (row(i), 0, 0)),
        pl.BlockSpec((1, d), lambda i, j: (0, 0)),
        pl.BlockSpec((d, tf), lambda i, j: (0, j)),
        pl.BlockSpec((tf, d), lambda i, j: (j, 0)),
    ]
    args = [*xs, *[a for pair in parts for a in pair], *wo_parts, mod_l, g2.reshape(1, d), w1, w2]
    if final:
        in_specs += [pl.BlockSpec((1, d), lambda i, j: (0, 0))]
        args += [final_g.reshape(1, d)]
        out_shape = (jax.ShapeDtypeStruct((n_prompt, d), F32), jax.ShapeDtypeStruct((n_tok - n_prompt, d), F32))
        out_specs = (pl.BlockSpec((tm, d), lambda i, j: (jnp.minimum(i, npt - 1), 0)),
                     pl.BlockSpec((tm, d), lambda i, j: (jnp.maximum(i - npt, 0), 0)))
    else:
        out_shape = jax.ShapeDtypeStruct((n_tok, d), F32)
        out_specs = pl.BlockSpec((tm, d), lambda i, j: (i, 0))
    kern = functools.partial(_out_mlp_kernel, n_x=len(xs), n_parts=len(parts), final=final, n_prompt_tiles=npt)
    return pl.pallas_call(
        kern,
        out_shape=out_shape,
        grid=(n_tok // tm, d_ff // tf),
        in_specs=in_specs,
        out_specs=out_specs,
        scratch_shapes=[pltpu.VMEM((tm, d), F32), pltpu.VMEM((tm, d), BF16), pltpu.VMEM((tm, d), F32)],
        compiler_params=pltpu.CompilerParams(dimension_semantics=("arbitrary", "arbitrary"),
                                             vmem_limit_bytes=VMEM_LIMIT_BYTES),
        name="out_mlp",
    )(*args)


def _rope_tables(seq_len):
    rows = seq_len // GRID_W
    inv = ROPE_THETA ** (-jnp.arange(ROT_FREQS, dtype=F32) / ROT_FREQS)
    ang_r = jnp.repeat(jnp.arange(rows, dtype=F32), GRID_W)[:, None] * inv
    ang_c = jnp.tile(jnp.arange(GRID_W, dtype=F32), rows)[:, None] * inv
    cos = jnp.concatenate([jnp.cos(ang_r), jnp.cos(ang_r), jnp.cos(ang_c), jnp.cos(ang_c)], axis=-1)
    sin = jnp.concatenate([-jnp.sin(ang_r), jnp.sin(ang_r), -jnp.sin(ang_c), jnp.sin(ang_c)], axis=-1)
    return cos.astype(F32), sin.astype(F32)


def kernel(x_prompt, x_sample, state_a, cache_b_kv, cache_c_kv, c, c_ctx, ada_w, ada_b, norm1_g, norm2_g,
           final_g, mlp_w1, mlp_w2, ev_w_in, a_conv, a_log, a_dt_bias, a_norm_g, b_sink, ev_w_out,
           od_w_in, c_qnorm_g, c_knorm_g, od_w_out):
    batch, seq, d = x_prompt.shape
    dec_batch, dec_seq, _ = x_sample.shape
    depth = ada_w.shape[0]
    n_prompt, n_sample = batch * seq, dec_batch * dec_seq
    past = cache_b_kv.shape[3]

    xs = [x_prompt.reshape(n_prompt, d), x_sample.reshape(n_sample, d)]
    cond = jnp.concatenate([c_ctx[None, :], c, jnp.zeros((8 - 1 - dec_batch, d), F32)], axis=0)
    mod = _adaln(cond, ada_w, ada_b).reshape(depth, 8, 6, d)

    cos, sin = _rope_tables(dec_seq)
    s_blk0 = n_prompt // dec_seq
    zero_state = jnp.zeros((batch, 2, A_HEADS, HEAD_DIM, HEAD_DIM), F32)
    lane_pad = lambda v: jnp.zeros((1, LANES), F32).at[0, 2 * A_HEADS:4 * A_HEADS].set(v.reshape(-1))

    new_a, new_b, new_c = [], [], []
    for l in range(depth):
        i = l // 2
        fin = final_g if l == depth - 1 else None
        w1 = mlp_w1[l].astype(BF16)
        w2 = mlp_w2[l].astype(BF16)
        if l % 2 == 0:
            w_in = ev_w_in[i]
            bg_off = 4 * A_WIDTH
            qb_off = bg_off + 4 * A_HEADS
            w_in = jnp.concatenate([w_in[:, :bg_off], w_in[:, qb_off:], w_in[:, bg_off:qb_off],
                                    jnp.zeros((d, LANES - 4 * A_HEADS), F32)], axis=1).astype(BF16)
            proj = _in_proj(xs, mod[l], norm1_g[l], w_in, n_prompt, dec_seq)
            arow, dtrow = lane_pad(a_log[i]), lane_pad(a_dt_bias[i])
            oa_p, st_p = _gdn(proj, a_conv[i], arow, dtrow, a_norm_g[i], zero_state, 0, batch, seq, heads=A_HEADS,
                              steps=seq // CHUNK)
            oa_s, _ = _gdn(proj, a_conv[i], arow, dtrow, a_norm_g[i], state_a[:, i], s_blk0, dec_batch, dec_seq,
                           heads=1, steps=8)
            kv_cols = B_KV_HEADS * HEAD_DIM
            ob_p = _attention(proj, q_off=EV_QB_OFF, k_off=EV_KVB_OFF, v_off=EV_KVB_OFF + kv_cols,
                              n_heads=B_HEADS, n_kv=B_KV_HEADS, row_block0=0, n_seq=batch, seq_len=seq,
                              tq=seq, sink=b_sink[i])
            ob_s = _attention(proj, q_off=EV_QB_OFF, k_off=EV_KVB_OFF, v_off=EV_KVB_OFF + kv_cols,
                              n_heads=B_HEADS, n_kv=B_KV_HEADS, row_block0=s_blk0, n_seq=dec_batch,
                              seq_len=dec_seq, tq=256,
                              ctx_k=cache_b_kv[:, i, 0].reshape(dec_batch, past, kv_cols),
                              ctx_v=cache_b_kv[:, i, 1].reshape(dec_batch, past, kv_cols),
                              cos=cos, sin=sin, sink=b_sink[i], window=WINDOW)
            new_a.append(st_p)
            kvb = proj[:n_prompt, EV_KVB_OFF:EV_KVB_OFF + 2 * kv_cols]
            new_b.append(jnp.transpose(kvb.reshape(batch, seq, 2, B_KV_HEADS, HEAD_DIM), (0, 2, 1, 3, 4)))
            parts = [[oa_p, oa_s], [ob_p, ob_s]]
            w_out = ev_w_out[i].astype(BF16)
            wo_parts = [w_out[:A_WIDTH], w_out[A_WIDTH:]]
        else:
            proj = _in_proj(xs, mod[l], norm1_g[l], od_w_in[i].astype(BF16), n_prompt, dec_seq)
            kv_cols = C_KV_HEADS * HEAD_DIM
            oc_p, kn_p = _attention(proj, q_off=0, k_off=OD_K_OFF, v_off=OD_V_OFF, n_heads=C_HEADS,
                                    n_kv=C_KV_HEADS, row_block0=0, n_seq=batch, seq_len=seq, tq=seq,
                                    q_gain=c_qnorm_g[i], k_gain=c_knorm_g[i], emit_k=True)
            oc_s = _attention(proj, q_off=0, k_off=OD_K_OFF, v_off=OD_V_OFF, n_heads=C_HEADS,
                              n_kv=C_KV_HEADS, row_block0=s_blk0, n_seq=dec_batch, seq_len=dec_seq, tq=256,
                              ctx_k=cache_c_kv[:, i, 0].reshape(dec_batch, past, kv_cols),
                              ctx_v=cache_c_kv[:, i, 1].reshape(dec_batch, past, kv_cols),
                              cos=cos, sin=sin, q_gain=c_qnorm_g[i], k_gain=c_knorm_g[i])
            v_p = proj[:n_prompt, OD_V_OFF:OD_V_OFF + kv_cols]
            new_c.append(jnp.stack([kn_p.reshape(batch, seq, C_KV_HEADS, HEAD_DIM),
                                    v_p.reshape(batch, seq, C_KV_HEADS, HEAD_DIM)], axis=1))
            parts = [[oc_p, oc_s]]
            wo_parts = [od_w_out[i].astype(BF16)]
        out = _out_mlp(xs, parts, wo_parts, mod[l], norm2_g[l], w1, w2, n_prompt, dec_seq, final_g=fin)
        xs = list(out) if fin is not None else [out]

    y_prompt = xs[0].reshape(batch, seq, d)
    y_sample = xs[1].reshape(dec_batch, dec_seq, d)
    return (y_prompt, y_sample, jnp.stack(new_a, axis=1), jnp.stack(new_b, axis=1), jnp.stack(new_c, axis=1))
```

```python
import functools

import numpy as np
import jax
import jax.numpy as jnp
from jax import lax
from jax.experimental import pallas as pl
from jax.experimental.pallas import tpu as pltpu

F32 = jnp.float32
BF16 = jnp.bfloat16
HIGHEST = lax.Precision.HIGHEST

HEAD_DIM = 128
GRID_W = 64
A_HEADS = 4
CHUNK = 64
B_HEADS = 4
B_KV_HEADS = 2
WINDOW = 128
C_HEADS = 8
C_KV_HEADS = 2
ROT_FREQS = HEAD_DIM // 4
ROPE_THETA = 10000.0
EPS = 1e-6
NEG = -1e30
LOG2_E = 1.4426950408889634
A_WIDTH = A_HEADS * HEAD_DIM
B_WIDTH = B_HEADS * HEAD_DIM
C_WIDTH = C_HEADS * HEAD_DIM

LANES = 128
VMEM_LIMIT_BYTES = 56 * 1024 * 1024

EV_GATE_OFF = 3 * A_WIDTH
EV_QB_OFF = 4 * A_WIDTH
EV_KVB_OFF = EV_QB_OFF + B_WIDTH
EV_BG_OFF = EV_KVB_OFF + 2 * B_KV_HEADS * HEAD_DIM
EV_COLS = EV_BG_OFF + LANES
OD_K_OFF = C_WIDTH
OD_V_OFF = C_WIDTH + C_KV_HEADS * HEAD_DIM


def _sigmoid(x):
    return 1.0 / (1.0 + jnp.exp(-x))


def _silu(x):
    return x * _sigmoid(x)


def _softplus(x):
    return jnp.maximum(x, 0.0) + jnp.log1p(jnp.exp(-jnp.abs(x)))


def _rms_rows(x, g):
    return x * lax.rsqrt(jnp.mean(x * x, axis=-1, keepdims=True) + EPS) * g


def _dot(a, b, precision=None):
    return jnp.dot(a, b, preferred_element_type=F32, precision=precision)


def _dot_nt(a, b, precision=None):
    return lax.dot_general(a, b, (((1,), (1,)), ((), ())), preferred_element_type=F32, precision=precision)


def _cond_row(i, n_prompt_tiles, tiles_per_sample):
    return jnp.where(i < n_prompt_tiles, 0, 1 + (i - n_prompt_tiles) // tiles_per_sample)


def _adaln_kernel(cond_ref, w_ref, b_ref, o_ref):
    s = _silu(cond_ref[...])
    o_ref[0] = _dot(s, w_ref[0], HIGHEST) + b_ref[0]


def _adaln(cond, ada_w, ada_b, tn=1024):
    depth, d, n = ada_w.shape
    rows = cond.shape[0]
    return pl.pallas_call(
        _adaln_kernel,
        out_shape=jax.ShapeDtypeStruct((depth, rows, n), F32),
        grid=(depth, n // tn),
        in_specs=[
            pl.BlockSpec((rows, d), lambda l, j: (0, 0)),
            pl.BlockSpec((1, d, tn), lambda l, j: (l, 0, j)),
            pl.BlockSpec((1, 1, tn), lambda l, j: (l, 0, j)),
        ],
        out_specs=pl.BlockSpec((1, rows, tn), lambda l, j: (l, 0, j)),
        compiler_params=pltpu.CompilerParams(dimension_semantics=("arbitrary", "arbitrary"),
                                             vmem_limit_bytes=VMEM_LIMIT_BYTES),
        name="adaln",
    )(cond, ada_w, ada_b.reshape(depth, 1, n))


def _token_specs(arrays, tm, n_prompt_tiles):
    if len(arrays) == 1:
        return [pl.BlockSpec((tm, arrays[0].shape[1]), lambda i, *_: (i, 0))]
    return [pl.BlockSpec((tm, arrays[0].shape[1]), lambda i, *_: (jnp.minimum(i, n_prompt_tiles - 1), 0)),
            pl.BlockSpec((tm, arrays[1].shape[1]), lambda i, *_: (jnp.maximum(i - n_prompt_tiles, 0), 0))]


def _token_tile(refs, n_prompt_tiles):
    if len(refs) == 1:
        return refs[0][...]
    return jnp.where(pl.program_id(0) < n_prompt_tiles, refs[0][...], refs[1][...])


def _inproj_kernel(*refs, n_x, n_prompt_tiles):
    x_refs = refs[:n_x]
    mod_ref, g_ref, w_ref, o_ref = refs[n_x:]
    x = _token_tile(x_refs, n_prompt_tiles)
    h = _rms_rows(x, g_ref[...]) * (1.0 + mod_ref[1:2, :]) + mod_ref[0:1, :]
    o_ref[...] = _dot(h.astype(BF16), w_ref[...])


def _in_proj(xs, mod_l, g, w, n_prompt, sample_len, tm=512):
    n_tok = sum(a.shape[0] for a in xs)
    d = xs[0].shape[1]
    n = w.shape[1]
    npt = n_prompt // tm
    row = functools.partial(_cond_row, n_prompt_tiles=npt, tiles_per_sample=sample_len // tm)
    return pl.pallas_call(
        functools.partial(_inproj_kernel, n_x=len(xs), n_prompt_tiles=npt),
        out_shape=jax.ShapeDtypeStruct((n_tok, n), F32),
        grid=(n_tok // tm,),
        in_specs=_token_specs(xs, tm, npt) + [
            pl.BlockSpec((None, 6, d), lambda i: (row(i), 0, 0)),
            pl.BlockSpec((1, d), lambda i: (0, 0)),
            pl.BlockSpec((d, n), lambda i: (0, 0)),
        ],
        out_specs=pl.BlockSpec((tm, n), lambda i: (i, 0)),
        compiler_params=pltpu.CompilerParams(dimension_semantics=("arbitrary",),
                                             vmem_limit_bytes=VMEM_LIMIT_BYTES),
        name="in_proj",
    )(*xs, mod_l, g.reshape(1, d), w)


def _split_bf16(x):
    hi = x.astype(BF16)
    lo = (x - hi.astype(F32)).astype(BF16)
    return hi, lo


def _gdn_kernel(q_ref, k_ref, v_ref, gate_ref, betaall_ref, gcall_ref, cwq_ref, cwk_ref, cwv_ref,
                ng_ref, s0_ref, o_ref, sout_ref,
                qn_ref, kn_ref, vn_ref, beta_ref, gc_ref, gcrow_ref, s_ref, oacc_ref,
                mq_a, c_a, oo_a, eg_a, mq_b, c_b, oo_b, eg_b, *, seq_len, heads, steps):
    L = seq_len
    nc = L // CHUNK
    nblk = nc // steps
    h0 = pl.program_id(1) * heads
    row_idx = lax.broadcasted_iota(jnp.int32, (L, LANES), 0)
    lane_idx = lax.broadcasted_iota(jnp.int32, (L, LANES), 1)

    def conv_silu(x, cw):
        prev = jnp.where(row_idx == 0, 0.0, pltpu.roll(x, 1, 0))
        nxt = jnp.where(row_idx == L - 1, 0.0, pltpu.roll(x, L - 1, 0))
        return _silu(prev * cw[0:1, :] + x * cw[1:2, :] + nxt * cw[2:3, :])

    def l2norm(x):
        return x * lax.rsqrt(jnp.sum(x * x, axis=-1, keepdims=True) + EPS)

    beta_all = betaall_ref[...]
    gc_all = gcall_ref[...]

    def pick(x, col):
        col_vals = jnp.sum(jnp.where(lane_idx == col, x, 0.0), axis=1, keepdims=True)
        return jnp.broadcast_to(col_vals, (L, LANES))

    for r in range(L // LANES):
        rows = jnp.transpose(gc_all[r * LANES:(r + 1) * LANES, :])[2 * A_HEADS:4 * A_HEADS, :]
        gcrow_ref[0, r] = rows
        gcrow_ref[1, r] = pltpu.roll(rows, CHUNK, 1)

    for hh in range(heads):
        cols = slice(hh * HEAD_DIM, (hh + 1) * HEAD_DIM)
        qn_ref[:, cols] = l2norm(conv_silu(q_ref[:, cols], cwq_ref[:, cols])) * (HEAD_DIM ** -0.5)
        kn_ref[:, cols] = l2norm(conv_silu(k_ref[:, cols], cwk_ref[:, cols]))
        vn_ref[:, cols] = conv_silu(v_ref[:, cols], cwv_ref[:, cols])
        s_ref[hh, 0] = s0_ref[0, hh]
        s_ref[hh, 1] = s0_ref[1, hh]
        head = h0 + hh
        for d in range(2):
            beta_ref[hh, d] = pick(beta_all, d * A_HEADS + head)
            gc_ref[hh, d] = pick(gc_all, (2 + d) * A_HEADS + head)

    r64 = lax.broadcasted_iota(jnp.int32, (CHUNK, LANES), 0)
    l64 = lax.broadcasted_iota(jnp.int32, (CHUNK, LANES), 1)
    is_f = l64 < CHUNK
    is_f_row = lax.broadcasted_iota(jnp.int32, (1, LANES), 1) < CHUNK
    cpos = jnp.where(is_f, l64, l64 - CHUNK)
    ahead = jnp.where(is_f, r64 - cpos, cpos - r64)
    incl = ahead >= 0
    strict = ahead > 0
    eye = jnp.where(r64 == cpos, 1.0, 0.0)
    is_f2 = lax.broadcasted_iota(jnp.int32, (2 * CHUNK, LANES), 1) < CHUNK
    zeros_c = jnp.zeros((CHUNK, HEAD_DIM), F32)

    def aligned(off, m):
        return off if isinstance(off, int) else pl.multiple_of(off, m)

    def half(n):
        return n // 2 if isinstance(n, int) else lax.shift_right_logical(n, 1)

    def block_diag(xp):
        z = jnp.zeros_like(xp)
        return jnp.concatenate([jnp.where(is_f, xp, z), jnp.where(is_f, z, xp)], axis=0)

    def split_dot(a_hi, a_lo, b_hi, b_lo):
        lhs = jnp.concatenate([a_hi, a_lo, a_hi], axis=1)
        rhs = jnp.concatenate([b_hi, b_hi, b_lo], axis=0)
        return _dot(lhs, rhs)

    def prep_stages(blk, slot):
        mq_ref, c_ref, oo_ref, eg_ref = slot
        probs = [(si, hh) for si in range(steps) for hh in range(heads)]
        st = {p: {} for p in probs}

        def load(p):
            si, hh = p
            j = blk * steps + si
            n_b = nc - 1 - j
            cols = slice(hh * HEAD_DIM, (hh + 1) * HEAD_DIM)
            off_f = aligned(j * CHUNK, CHUNK)
            off_b = aligned(n_b * CHUNK, CHUNK)
            q_f, k_f, v_f = (r[pl.ds(off_f, CHUNK), cols] for r in (qn_ref, kn_ref, vn_ref))
            q_b, k_b, v_b = (r[pl.ds(off_b, CHUNK), cols] for r in (qn_ref, kn_ref, vn_ref))
            beta_f = beta_ref[hh, 0, pl.ds(off_f, CHUNK), :]
            beta_b = beta_ref[hh, 1, pl.ds(off_b, CHUNK), :]
            gc_f = gc_ref[hh, 0, pl.ds(off_f, CHUNK), :]
            gc_b = gc_ref[hh, 1, pl.ds(off_b, CHUNK), :]
            row_f = gcrow_ref[si % 2, half(j), pl.ds(h0 + hh, 1), :]
            row_b = gcrow_ref[si % 2, half(n_b), pl.ds(A_HEADS + h0 + hh, 1), :]
            gcrow = jnp.where(is_f_row, row_f, row_b)
            diff = jnp.where(is_f, gc_f, gc_b) - gcrow
            decay = jnp.where(incl, jnp.exp(jnp.where(incl, diff, 0.0)), 0.0)
            kb_f = k_f * beta_f
            kb_b = k_b * beta_b
            e_f = jnp.exp(gc_f)
            e_b = jnp.exp(gc_b)
            glast_f = gc_f[CHUNK - 1:CHUNK, :]
            glast_b = gc_b[0:1, :]
            kdt = jnp.transpose(jnp.concatenate([k_f * jnp.exp(glast_f - gc_f), k_b * jnp.exp(glast_b - gc_b)],
                                                axis=0))
            zk = jnp.zeros_like(kdt)
            eg_ref[si, hh, 0] = jnp.broadcast_to(jnp.exp(glast_f), (8, LANES))
            eg_ref[si, hh, 1] = jnp.broadcast_to(jnp.exp(glast_b), (8, LANES))
            st[p].update(
                decay=decay,
                lhs=jnp.concatenate([jnp.concatenate([kb_f, kb_b], axis=1),
                                     jnp.concatenate([q_f, q_b], axis=1)], axis=0).astype(BF16),
                rhs=jnp.concatenate([jnp.concatenate([k_f, zeros_c], axis=1),
                                     jnp.concatenate([zeros_c, k_b], axis=1)], axis=0).astype(BF16),
                rhs_uw=jnp.concatenate([jnp.concatenate([v_f * beta_f, kb_f * e_f], axis=1),
                                        jnp.concatenate([v_b * beta_b, kb_b * e_b], axis=1)], axis=0).astype(BF16),
                kd2=jnp.concatenate([jnp.where(is_f2, kdt, zk), jnp.where(is_f2, zk, kdt)], axis=0).astype(BF16),
                qe=(q_f * e_f, q_b * e_b))

        def gram(p):
            s = st[p]
            aq = _dot_nt(s.pop("lhs"), s.pop("rhs"))
            decay = s.pop("decay")
            x = -jnp.where(strict, aq[:CHUNK] * decay, 0.0)
            s["qk"] = jnp.where(incl, aq[CHUNK:] * decay, 0.0)
            s["t"] = eye + x
            s["p"] = x

        def square(p):
            s = st[p]
            p_hi, p_lo = _split_bf16(s["p"])
            s["p"] = split_dot(p_hi, p_lo, block_diag(p_hi), block_diag(p_lo))

        def double(p):
            s = st[p]
            p_hi, p_lo = _split_bf16(s["p"])
            t_hi, t_lo = _split_bf16(s["t"])
            r = split_dot(p_hi, p_lo, jnp.concatenate([block_diag(t_hi), block_diag(p_hi)], axis=1),
                          jnp.concatenate([block_diag(t_lo), block_diag(p_lo)], axis=1))
            s["t"] = s["t"] + r[:, :LANES]
            s["p"] = r[:, LANES:]

        def last(p):
            s = st[p]
            p_hi, p_lo = _split_bf16(s.pop("p"))
            t_hi, t_lo = _split_bf16(s["t"])
            s["t"] = s["t"] + split_dot(p_hi, p_lo, block_diag(t_hi), block_diag(t_lo))

        def solve(p):
            s = st[p]
            s["uw"] = _dot(block_diag(s.pop("t")).astype(BF16), s.pop("rhs_uw"))

        def fold(p):
            si, hh = p
            s = st[p]
            lhs3 = jnp.concatenate([s.pop("kd2"), block_diag(s.pop("qk")).astype(BF16)], axis=0)
            big = _dot(lhs3, s.pop("uw").astype(BF16))
            qe = s.pop("qe")
            for d in range(2):
                rows_s = slice(d * 2 * CHUNK, (d + 1) * 2 * CHUNK)
                rows_o = slice(4 * CHUNK + d * CHUNK, 4 * CHUNK + (d + 1) * CHUNK)
                c_ref[si, hh, d] = big[rows_s, :HEAD_DIM]
                oo_ref[si, hh, d] = big[rows_o, :HEAD_DIM]
                mq_ref[si, hh, d] = jnp.concatenate([big[rows_s, HEAD_DIM:], qe[d] - big[rows_o, HEAD_DIM:]],
                                                    axis=0).astype(BF16)

        n_double = 0
        span = 2
        while 2 * span < CHUNK:
            n_double += 1
            span *= 2
        stage_fns = [load, gram, square] + [double] * n_double + [last, solve, fold]
        return [functools.partial(lambda fn: [fn(p) for p in probs], fn) for fn in stage_fns]

    def scan_step(blk, slot, si):
        mq_ref, c_ref, oo_ref, eg_ref = slot
        j = blk * steps + si
        for hh in range(heads):
            cols = slice(hh * HEAD_DIM, (hh + 1) * HEAD_DIM)
            for d, n in ((0, j), (1, nc - 1 - j)):
                s = s_ref[hh, d]
                r = _dot(mq_ref[si, hh, d], s.astype(BF16))
                s_ref[hh, d] = s * eg_ref[si, hh, d][0:1, :] - r[:2 * CHUNK] + c_ref[si, hh, d]
                oacc_ref[d, pl.ds(aligned(n * CHUNK, CHUNK), CHUNK), cols] = r[2 * CHUNK:] + oo_ref[si, hh, d]

    def run(prep, scan):
        stages = prep_stages(*prep) if prep else []
        n_scan = steps if scan else 0
        for k in range(max(len(stages), n_scan)):
            if k < len(stages):
                stages[k]()
            if k < n_scan:
                scan_step(scan[0], scan[1], k)

    slot_a = (mq_a, c_a, oo_a, eg_a)
    slot_b = (mq_b, c_b, oo_b, eg_b)
    run((0, slot_a), None)
    if nblk == 1:
        run(None, (0, slot_a))
    else:
        def body(i, carry):
            blk = 2 * i
            run((blk + 1, slot_b), (blk, slot_a))
            run((blk + 2, slot_a), (blk + 1, slot_b))
            return carry

        lax.fori_loop(0, nblk // 2 - 1, body, 0)
        run((nblk - 1, slot_b), (nblk - 2, slot_a))
        run(None, (nblk - 1, slot_b))

    for hh in range(heads):
        cols = slice(hh * HEAD_DIM, (hh + 1) * HEAD_DIM)
        sout_ref[0, hh] = s_ref[hh, 0]
        sout_ref[1, hh] = s_ref[hh, 1]
        o = oacc_ref[0, :, cols] + oacc_ref[1, :, cols]
        o = _rms_rows(o, ng_ref[...]) * _silu(gate_ref[:, cols])
        o_ref[:, cols] = o.astype(o_ref.dtype)


def _gates_kernel(bg_ref, arow_ref, dtrow_ref, beta_ref, gc_ref):
    rows = bg_ref.shape[0]
    bg = bg_ref[...]
    beta_ref[...] = _sigmoid(bg)
    g = -jnp.exp(arow_ref[...]) * _softplus(bg + dtrow_ref[...])
    chunk_row = lax.broadcasted_iota(jnp.int32, (rows, LANES), 0) % CHUNK
    lane = lax.broadcasted_iota(jnp.int32, (rows, LANES), 1)
    pre, suf = g, g
    shift = 1
    while shift < CHUNK:
        pre = pre + jnp.where(chunk_row >= shift, pltpu.roll(pre, shift, 0), 0.0)
        suf = suf + jnp.where(chunk_row < CHUNK - shift, pltpu.roll(suf, rows - shift, 0), 0.0)
        shift *= 2
    gc_ref[...] = jnp.where(lane >= 3 * A_HEADS, suf, pre)


def _gates(proj, arow, dtrow, tm=1024):
    n_tok = proj.shape[0]
    assert tm % CHUNK == 0
    spec = pl.BlockSpec((tm, LANES), lambda i: (i, 0))
    return pl.pallas_call(
        _gates_kernel,
        out_shape=(jax.ShapeDtypeStruct((n_tok, LANES), F32), jax.ShapeDtypeStruct((n_tok, LANES), F32)),
        grid=(n_tok // tm,),
        in_specs=[pl.BlockSpec((tm, LANES), lambda i: (i, EV_BG_OFF // LANES)),
                  pl.BlockSpec((1, LANES), lambda i: (0, 0)), pl.BlockSpec((1, LANES), lambda i: (0, 0))],
        out_specs=(spec, spec),
        compiler_params=pltpu.CompilerParams(dimension_semantics=("arbitrary",)),
        name="gates",
    )(proj, arow, dtrow)


def _gdn(proj, beta_all, gc_all, conv_w, norm_g, s0, row_block0, n_seq, seq_len, heads, steps):
    L = seq_len
    nblk = L // CHUNK // steps
    assert L % (CHUNK * steps) == 0 and steps % 2 == 0 and 2 * CHUNK == LANES and A_HEADS % heads == 0
    assert nblk == 1 or nblk % 2 == 0
    width = heads * HEAD_DIM
    groups = A_HEADS // heads
    blk = lambda base: pl.BlockSpec((L, width), lambda b, h: (row_block0 + b, base * groups + h))
    wblk = lambda base: pl.BlockSpec((3, width), lambda b, h: (0, base * groups + h))
    state_spec = pl.BlockSpec((None, 2, heads, HEAD_DIM, HEAD_DIM), lambda b, h: (b, 0, h, 0, 0))
    slot = [pltpu.VMEM((steps, heads, 2, 3 * CHUNK, HEAD_DIM), BF16),
            pltpu.VMEM((steps, heads, 2, 2 * CHUNK, HEAD_DIM), F32),
            pltpu.VMEM((steps, heads, 2, CHUNK, HEAD_DIM), F32),
            pltpu.VMEM((steps, heads, 2, 8, LANES), F32)]
    kern = functools.partial(_gdn_kernel, seq_len=L, heads=heads, steps=steps)
    out, s_out = pl.pallas_call(
        kern,
        out_shape=(jax.ShapeDtypeStruct((n_seq * L, A_WIDTH), BF16),
                   jax.ShapeDtypeStruct((n_seq, 2, A_HEADS, HEAD_DIM, HEAD_DIM), F32)),
        grid=(n_seq, groups),
        in_specs=[
            blk(0), blk(1), blk(2), blk(3),
            pl.BlockSpec((L, LANES), lambda b, h: (row_block0 + b, 0)),
            pl.BlockSpec((L, LANES), lambda b, h: (row_block0 + b, 0)),
            wblk(0), wblk(1), wblk(2),
            pl.BlockSpec((1, HEAD_DIM), lambda b, h: (0, 0)),
            state_spec,
        ],
        out_specs=(pl.BlockSpec((L, width), lambda b, h: (b, h)), state_spec),
        scratch_shapes=[
            pltpu.VMEM((L, width), F32), pltpu.VMEM((L, width), F32), pltpu.VMEM((L, width), F32),
            pltpu.VMEM((heads, 2, L, LANES), F32), pltpu.VMEM((heads, 2, L, LANES), F32),
            pltpu.VMEM((2, L // LANES, 2 * A_HEADS, LANES), F32),
            pltpu.VMEM((heads, 2, HEAD_DIM, HEAD_DIM), F32),
            pltpu.VMEM((2, L, width), F32),
        ] + slot + slot,
        compiler_params=pltpu.CompilerParams(dimension_semantics=("arbitrary", "arbitrary"),
                                             vmem_limit_bytes=VMEM_LIMIT_BYTES),
        name="gdn",
    )(proj, proj, proj, proj, beta_all, gc_all, conv_w, conv_w, conv_w, norm_g.reshape(1, HEAD_DIM), s0)
    return out, s_out


def _rope_rows(x, cos, sin_signed):
    lane = lax.broadcasted_iota(jnp.int32, x.shape, 1)
    swapped = jnp.where((lane % (2 * ROT_FREQS)) < ROT_FREQS,
                        pltpu.roll(x, HEAD_DIM - ROT_FREQS, 1), pltpu.roll(x, ROT_FREQS, 1))
    return x * cos + swapped * sin_signed


def _attn_kernel(*refs, tq, seq_len, group, n_ctx, use_sink, window, qk_norm, rope, emit_k):
    it = iter(refs)
    q_ref, k_ref, v_ref = next(it), next(it), next(it)
    ck_ref = cv_ref = cos_ref = sin_ref = qg_ref = kg_ref = sink_ref = kn_out_ref = None
    if n_ctx:
        ck_ref, cv_ref = next(it), next(it)
    if rope:
        cos_ref, sin_ref = next(it), next(it)
    if qk_norm:
        qg_ref, kg_ref = next(it), next(it)
    if use_sink:
        sink_ref = next(it)
    o_ref = next(it)
    if emit_k:
        kn_out_ref = next(it)
    kbf_ref, vbf_ref = next(it), next(it)
    ckbf_ref = cvbf_ref = None
    if n_ctx:
        ckbf_ref, cvbf_ref = next(it), next(it)

    L = seq_len
    kvh = pl.program_id(1)
    qi = pl.program_id(2)

    def ones_column(rows):
        lane = lax.broadcasted_iota(jnp.int32, (rows, HEAD_DIM), 1)
        return jnp.where(lane == 0, 1.0, 0.0).astype(BF16)

    @pl.when(qi == 0)
    def _():
        k = k_ref[...]
        if qk_norm:
            k = _rms_rows(k, kg_ref[...])
        if emit_k:
            kn_out_ref[...] = k
        if rope:
            k = _rope_rows(k, cos_ref[...], sin_ref[...])
        kbf_ref[...] = k.astype(BF16)
        vbf_ref[:, :HEAD_DIM] = v_ref[...].astype(BF16)
        vbf_ref[:, HEAD_DIM:] = ones_column(L)
        if n_ctx:
            ckbf_ref[...] = ck_ref[...].astype(BF16)
            cvbf_ref[:, :HEAD_DIM] = cv_ref[...].astype(BF16)
            cvbf_ref[:, HEAD_DIM:] = ones_column(n_ctx)

    q0 = pl.multiple_of(qi * tq, tq)
    if window:
        wk = tq + 2 * window
        ws = pl.multiple_of(jnp.clip(q0 - window, 0, L - wk), LANES)
        k_loc = kbf_ref[pl.ds(ws, wk), :]
        v_loc = vbf_ref[pl.ds(ws, wk), :]
        qpos = q0 + lax.broadcasted_iota(jnp.int32, (tq, wk), 0)
        kpos = ws + lax.broadcasted_iota(jnp.int32, (tq, wk), 1)
        valid = jnp.abs(qpos - kpos) <= window
    else:
        k_loc = kbf_ref[...]
        v_loc = vbf_ref[...]

    def scores(g):
        q = q_ref[:, g * HEAD_DIM:(g + 1) * HEAD_DIM]
        if qk_norm:
            q = _rms_rows(q, qg_ref[...])
        if rope:
            q = _rope_rows(q, cos_ref[pl.ds(q0, tq), :], sin_ref[pl.ds(q0, tq), :])
        q = (q * (HEAD_DIM ** -0.5 * LOG2_E)).astype(BF16)
        s_loc = _dot_nt(q, k_loc)
        if window:
            s_loc = jnp.where(valid, s_loc, NEG)
        pieces = [(s_loc, v_loc)]
        if n_ctx:
            pieces.append((_dot_nt(q, ckbf_ref[...]), cvbf_ref[...]))
        return pieces

    def finish(g, pieces):
        m = None
        for s, _ in pieces:
            row_max = jnp.max(s, axis=-1, keepdims=True)
            m = row_max if m is None else jnp.maximum(m, row_max)
        if use_sink:
            sink = sink_ref[kvh * group + g] * LOG2_E
            m = jnp.maximum(m, sink)
        acc = None
        for s, v in pieces:
            term = _dot(jnp.exp2(s - m).astype(BF16), v)
            acc = term if acc is None else acc + term
        denom = acc[:, HEAD_DIM:HEAD_DIM + 1]
        if use_sink:
            denom = denom + jnp.exp2(sink - m)
        o_ref[:, g * HEAD_DIM:(g + 1) * HEAD_DIM] = (acc[:, :HEAD_DIM] / denom).astype(o_ref.dtype)

    ahead = scores(0)
    for g in range(group):
        cur = ahead
        if g + 1 < group:
            ahead = scores(g + 1)
        finish(g, cur)


def _attention(proj, *, q_off, k_off, v_off, n_heads, n_kv, row_block0, n_seq, seq_len, tq,
               ctx_k=None, ctx_v=None, cos=None, sin=None, q_gain=None, k_gain=None, sink=None,
               window=0, emit_k=False):
    L = seq_len
    group = n_heads // n_kv
    n_ctx = 0 if ctx_k is None else ctx_k.shape[1]
    rope = cos is not None
    qk_norm = q_gain is not None
    use_sink = sink is not None
    nq = L // tq
    qb, kb, vb = q_off // (group * HEAD_DIM), k_off // HEAD_DIM, v_off // HEAD_DIM
    in_specs = [
        pl.BlockSpec((tq, group * HEAD_DIM), lambda b, h, i: ((row_block0 + b) * nq + i, qb + h)),
        pl.BlockSpec((L, HEAD_DIM), lambda b, h, i: (row_block0 + b, kb + h)),
        pl.BlockSpec((L, HEAD_DIM), lambda b, h, i: (row_block0 + b, vb + h)),
    ]
    args = [proj, proj, proj]
    scratch = [pltpu.VMEM((L, HEAD_DIM), BF16), pltpu.VMEM((L, 2 * HEAD_DIM), BF16)]
    if n_ctx:
        in_specs += [pl.BlockSpec((None, n_ctx, HEAD_DIM), lambda b, h, i: (b, 0, h))] * 2
        args += [ctx_k, ctx_v]
        scratch += [pltpu.VMEM((n_ctx, HEAD_DIM), BF16), pltpu.VMEM((n_ctx, 2 * HEAD_DIM), BF16)]
    if rope:
        in_specs += [pl.BlockSpec((L, HEAD_DIM), lambda b, h, i: (0, 0))] * 2
        args += [cos, sin]
    if qk_norm:
        in_specs += [pl.BlockSpec((1, HEAD_DIM), lambda b, h, i: (0, 0))] * 2
        args += [q_gain.reshape(1, HEAD_DIM), k_gain.reshape(1, HEAD_DIM)]
    if use_sink:
        in_specs += [pl.BlockSpec(memory_space=pltpu.SMEM)]
        args += [sink]
    out_shape = [jax.ShapeDtypeStruct((n_seq * L, n_heads * HEAD_DIM), BF16)]
    out_specs = [pl.BlockSpec((tq, group * HEAD_DIM), lambda b, h, i: (b * nq + i, h))]
    if emit_k:
        out_shape += [jax.ShapeDtypeStruct((n_seq * L, n_kv * HEAD_DIM), F32)]
        out_specs += [pl.BlockSpec((L, HEAD_DIM), lambda b, h, i: (b, h))]
    kern = functools.partial(_attn_kernel, tq=tq, seq_len=L, group=group, n_ctx=n_ctx, use_sink=use_sink,
                             window=window, qk_norm=qk_norm, rope=rope, emit_k=emit_k)
    res = pl.pallas_call(
        kern,
        out_shape=tuple(out_shape),
        grid=(n_seq, n_kv, nq),
        in_specs=in_specs,
        out_specs=tuple(out_specs),
        scratch_shapes=scratch,
        compiler_params=pltpu.CompilerParams(dimension_semantics=("arbitrary", "arbitrary", "arbitrary"),
                                             vmem_limit_bytes=VMEM_LIMIT_BYTES),
        name="attention",
    )(*args)
    return res if emit_k else res[0]


def _out_mlp_kernel(*refs, n_x, n_parts, final, n_prompt_tiles):
    it = iter(refs)
    x_refs = [next(it) for _ in range(n_x)]
    a_refs = [[next(it), next(it)] for _ in range(n_parts)]
    wo_refs = [next(it) for _ in range(n_parts)]
    mod_ref, g2_ref, w1_ref, w2_ref = next(it), next(it), next(it), next(it)
    fg_ref = next(it) if final else None
    o_refs = [next(it) for _ in range(2 if final else 1)]
    x1_ref, h2_ref, acc_ref = next(it), next(it), next(it)
    i = pl.program_id(0)
    j = pl.program_id(1)

    @pl.when(j == 0)
    def _():
        mix = None
        for pair, wo_ref in zip(a_refs, wo_refs):
            term = _dot(_token_tile(pair, n_prompt_tiles), wo_ref[...])
            mix = term if mix is None else mix + term
        x1 = _token_tile(x_refs, n_prompt_tiles) + mod_ref[2:3, :] * mix
        x1_ref[...] = x1
        h2 = _rms_rows(x1, g2_ref[...]) * (1.0 + mod_ref[4:5, :]) + mod_ref[3:4, :]
        h2_ref[...] = h2.astype(BF16)
        acc_ref[...] = jnp.zeros_like(acc_ref)

    hid = jnp.maximum(_dot(h2_ref[...], w1_ref[...]), 0.0)
    acc_ref[...] += _dot((hid * hid).astype(BF16), w2_ref[...])

    last = j == pl.num_programs(1) - 1
    if final:
        @pl.when(last & (i < n_prompt_tiles))
        def _():
            o_refs[0][...] = _rms_rows(x1_ref[...] + mod_ref[5:6, :] * acc_ref[...], fg_ref[...])

        @pl.when(last & (i >= n_prompt_tiles))
        def _():
            o_refs[1][...] = _rms_rows(x1_ref[...] + mod_ref[5:6, :] * acc_ref[...], fg_ref[...])
    else:
        @pl.when(last)
        def _():
            o_refs[0][...] = x1_ref[...] + mod_ref[5:6, :] * acc_ref[...]


def _out_mlp(xs, parts, wo_parts, mod_l, g2, w1, w2, n_prompt, sample_len, final_g=None, tm=512, tf=2048):
    n_tok = sum(a.shape[0] for a in xs)
    d = xs[0].shape[1]
    d_ff = w1.shape[1]
    final = final_g is not None
    npt = n_prompt // tm
    row = functools.partial(_cond_row, n_prompt_tiles=npt, tiles_per_sample=sample_len // tm)
    in_specs = _token_specs(xs, tm, npt)
    for pair in parts:
        in_specs += _token_specs(pair, tm, npt)
    in_specs += [pl.BlockSpec(w.shape, lambda i, j: (0, 0)) for w in wo_parts]
    in_specs += [
        pl.BlockSpec((None, 6, d), lambda i, j: (row(i), 0, 0)),
        pl.BlockSpec((1, d), lambda i, j: (0, 0)),
        pl.BlockSpec((d, tf), lambda i, j: (0, j)),
        pl.BlockSpec((tf, d), lambda i, j: (j, 0)),
    ]
    args = [*xs, *[a for pair in parts for a in pair], *wo_parts, mod_l, g2.reshape(1, d), w1, w2]
    if final:
        in_specs += [pl.BlockSpec((1, d), lambda i, j: (0, 0))]
        args += [final_g.reshape(1, d)]
        out_shape = (jax.ShapeDtypeStruct((n_prompt, d), F32), jax.ShapeDtypeStruct((n_tok - n_prompt, d), F32))
        out_specs = (pl.BlockSpec((tm, d), lambda i, j: (jnp.minimum(i, npt - 1), 0)),
                     pl.BlockSpec((tm, d), lambda i, j: (jnp.maximum(i - npt, 0), 0)))
    else:
        out_shape = jax.ShapeDtypeStruct((n_tok, d), F32)
        out_specs = pl.BlockSpec((tm, d), lambda i, j: (i, 0))
    kern = functools.partial(_out_mlp_kernel, n_x=len(xs), n_parts=len(parts), final=final, n_prompt_tiles=npt)
    return pl.pallas_call(
        kern,
        out_shape=out_shape,
        grid=(n_tok // tm, d_ff // tf),
        in_specs=in_specs,
        out_specs=out_specs,
        scratch_shapes=[pltpu.VMEM((tm, d), F32), pltpu.VMEM((tm, d), BF16), pltpu.VMEM((tm, d), F32)],
        compiler_params=pltpu.CompilerParams(dimension_semantics=("arbitrary", "arbitrary"),
                                             vmem_limit_bytes=VMEM_LIMIT_BYTES),
        name="out_mlp",
    )(*args)


def _rope_tables(seq_len):
    rows = seq_len // GRID_W
    inv = ROPE_THETA ** (-jnp.arange(ROT_FREQS, dtype=F32) / ROT_FREQS)
    ang_r = jnp.repeat(jnp.arange(rows, dtype=F32), GRID_W)[:, None] * inv
    ang_c = jnp.tile(jnp.arange(GRID_W, dtype=F32), rows)[:, None] * inv
    cos = jnp.concatenate([jnp.cos(ang_r), jnp.cos(ang_r), jnp.cos(ang_c), jnp.cos(ang_c)], axis=-1)
    sin = jnp.concatenate([-jnp.sin(ang_r), jnp.sin(ang_r), -jnp.sin(ang_c), jnp.sin(ang_c)], axis=-1)
    return cos.astype(F32), sin.astype(F32)


def kernel(x_prompt, x_sample, state_a, cache_b_kv, cache_c_kv, c, c_ctx, ada_w, ada_b, norm1_g, norm2_g,
           final_g, mlp_w1, mlp_w2, ev_w_in, a_conv, a_log, a_dt_bias, a_norm_g, b_sink, ev_w_out,
           od_w_in, c_qnorm_g, c_knorm_g, od_w_out):
    batch, seq, d = x_prompt.shape
    dec_batch, dec_seq, _ = x_sample.shape
    depth = ada_w.shape[0]
    n_prompt, n_sample = batch * seq, dec_batch * dec_seq
    past = cache_b_kv.shape[3]

    xs = [x_prompt.reshape(n_prompt, d), x_sample.reshape(n_sample, d)]
    cond = jnp.concatenate([c_ctx[None, :], c, jnp.zeros((8 - 1 - dec_batch, d), F32)], axis=0)
    mod = _adaln(cond, ada_w, ada_b).reshape(depth, 8, 6, d)

    cos, sin = _rope_tables(dec_seq)
    s_blk0 = n_prompt // dec_seq
    zero_state = jnp.zeros((batch, 2, A_HEADS, HEAD_DIM, HEAD_DIM), F32)
    lane_pad = lambda v: jnp.zeros((1, LANES), F32).at[0, 2 * A_HEADS:4 * A_HEADS].set(v.reshape(-1))

    new_a, new_b, new_c = [], [], []
    for l in range(depth):
        i = l // 2
        fin = final_g if l == depth - 1 else None
        w1 = mlp_w1[l].astype(BF16)
        w2 = mlp_w2[l].astype(BF16)
        if l % 2 == 0:
            w_in = ev_w_in[i]
            bg_off = 4 * A_WIDTH
            qb_off = bg_off + 4 * A_HEADS
            w_in = jnp.concatenate([w_in[:, :bg_off], w_in[:, qb_off:], w_in[:, bg_off:qb_off],
                                    jnp.zeros((d, LANES - 4 * A_HEADS), F32)], axis=1).astype(BF16)
            proj = _in_proj(xs, mod[l], norm1_g[l], w_in, n_prompt, dec_seq)
            beta_all, gc_all = _gates(proj, lane_pad(a_log[i]), lane_pad(a_dt_bias[i]))
            oa_p, st_p = _gdn(proj, beta_all, gc_all, a_conv[i], a_norm_g[i], zero_state, 0, batch, seq,
                              heads=A_HEADS, steps=seq // CHUNK)
            oa_s, _ = _gdn(proj, beta_all, gc_all, a_conv[i], a_norm_g[i], state_a[:, i], s_blk0, dec_batch,
                           dec_seq, heads=1, steps=8)
            kv_cols = B_KV_HEADS * HEAD_DIM
            ob_p = _attention(proj, q_off=EV_QB_OFF, k_off=EV_KVB_OFF, v_off=EV_KVB_OFF + kv_cols,
                              n_heads=B_HEADS, n_kv=B_KV_HEADS, row_block0=0, n_seq=batch, seq_len=seq,
                              tq=seq, sink=b_sink[i])
            ob_s = _attention(proj, q_off=EV_QB_OFF, k_off=EV_KVB_OFF, v_off=EV_KVB_OFF + kv_cols,
                              n_heads=B_HEADS, n_kv=B_KV_HEADS, row_block0=s_blk0, n_seq=dec_batch,
                              seq_len=dec_seq, tq=256,
                              ctx_k=cache_b_kv[:, i, 0].reshape(dec_batch, past, kv_cols),
                              ctx_v=cache_b_kv[:, i, 1].reshape(dec_batch, past, kv_cols),
                              cos=cos, sin=sin, sink=b_sink[i], window=WINDOW)
            new_a.append(st_p)
            kvb = proj[:n_prompt, EV_KVB_OFF:EV_KVB_OFF + 2 * kv_cols]
            new_b.append(jnp.transpose(kvb.reshape(batch, seq, 2, B_KV_HEADS, HEAD_DIM), (0, 2, 1, 3, 4)))
            parts = [[oa_p, oa_s], [ob_p, ob_s]]
            w_out = ev_w_out[i].astype(BF16)
            wo_parts = [w_out[:A_WIDTH], w_out[A_WIDTH:]]
        else:
            proj = _in_proj(xs, mod[l], norm1_g[l], od_w_in[i].astype(BF16), n_prompt, dec_seq)
            kv_cols = C_KV_HEADS * HEAD_DIM
            oc_p, kn_p = _attention(proj, q_off=0, k_off=OD_K_OFF, v_off=OD_V_OFF, n_heads=C_HEADS,
                                    n_kv=C_KV_HEADS, row_block0=0, n_seq=batch, seq_len=seq, tq=seq,
                                    q_gain=c_qnorm_g[i], k_gain=c_knorm_g[i], emit_k=True)
            oc_s = _attention(proj, q_off=0, k_off=OD_K_OFF, v_off=OD_V_OFF, n_heads=C_HEADS,
                              n_kv=C_KV_HEADS, row_block0=s_blk0, n_seq=dec_batch, seq_len=dec_seq, tq=256,
                              ctx_k=cache_c_kv[:, i, 0].reshape(dec_batch, past, kv_cols),
                              ctx_v=cache_c_kv[:, i, 1].reshape(dec_batch, past, kv_cols),
                              cos=cos, sin=sin, q_gain=c_qnorm_g[i], k_gain=c_knorm_g[i])
            v_p = proj[:n_prompt, OD_V_OFF:OD_V_OFF + kv_cols]
            new_c.append(jnp.stack([kn_p.reshape(batch, seq, C_KV_HEADS, HEAD_DIM),
                                    v_p.reshape(batch, seq, C_KV_HEADS, HEAD_DIM)], axis=1))
            parts = [[oc_p, oc_s]]
            wo_parts = [od_w_out[i].astype(BF16)]
        out = _out_mlp(xs, parts, wo_parts, mod[l], norm2_g[l], w1, w2, n_prompt, dec_seq, final_g=fin)
        xs = list(out) if fin is not None else [out]

    y_prompt = xs[0].reshape(batch, seq, d)
    y_sample = xs[1].reshape(dec_batch, dec_seq, d)
    return (y_prompt, y_sample, jnp.stack(new_a, axis=1), jnp.stack(new_b, axis=1), jnp.stack(new_c, axis=1))
```

```python
import functools

import jax
import jax.numpy as jnp
from jax import lax
from jax.experimental import pallas as pl
from jax.experimental.pallas import tpu as pltpu

F32 = jnp.float32
BF16 = jnp.bfloat16

HEAD_DIM = 128
GRID_W = 64
A_HEADS = 4
CHUNK = 64
B_HEADS = 4
B_KV_HEADS = 2
WINDOW = 128
C_HEADS = 8
C_KV_HEADS = 2
ROT_FREQS = HEAD_DIM // 4
ROPE_THETA = 10000.0
EPS = 1e-6
NEG = -1e30
LOG2_E = 1.4426950408889634
A_WIDTH = A_HEADS * HEAD_DIM
B_WIDTH = B_HEADS * HEAD_DIM
C_WIDTH = C_HEADS * HEAD_DIM

LANES = 128
VMEM_LIMIT_BYTES = 56 * 1024 * 1024

EV_GATE_OFF = 3 * A_WIDTH
EV_QB_OFF = 4 * A_WIDTH
EV_KVB_OFF = EV_QB_OFF + B_WIDTH
EV_BG_OFF = EV_KVB_OFF + 2 * B_KV_HEADS * HEAD_DIM
EV_COLS = EV_BG_OFF + LANES
OD_K_OFF = C_WIDTH
OD_V_OFF = C_WIDTH + C_KV_HEADS * HEAD_DIM


def _sigmoid(x):
    return 1.0 / (1.0 + jnp.exp(-x))


def _silu(x):
    return x * _sigmoid(x)


def _softplus(x):
    return jnp.maximum(x, 0.0) + jnp.log1p(jnp.exp(-jnp.abs(x)))


def _rms_rows(x, g):
    return x * lax.rsqrt(jnp.mean(x * x, axis=-1, keepdims=True) + EPS) * g


def _dot(a, b):
    return jnp.dot(a, b, preferred_element_type=F32)


def _dot_nt(a, b):
    return lax.dot_general(a, b, (((1,), (1,)), ((), ())), preferred_element_type=F32)


def _cond_row(i, n_prompt_tiles, tiles_per_sample):
    return jnp.where(i < n_prompt_tiles, 0, 1 + (i - n_prompt_tiles) // tiles_per_sample)


def _split_bf16(x):
    hi = x.astype(BF16)
    lo = (x - hi.astype(F32)).astype(BF16)
    return hi, lo


def _adaln_kernel(cond_ref, w_ref, b_ref, o_ref):
    rows = cond_ref.shape[0]
    s_hi, s_lo = _split_bf16(_silu(cond_ref[...]))
    w_hi, w_lo = _split_bf16(w_ref[0])
    r = _dot(jnp.concatenate([s_hi, s_lo], axis=0), w_hi)
    o_ref[0] = r[:rows] + r[rows:] + _dot(s_hi, w_lo) + b_ref[0]


def _adaln(cond, ada_w, ada_b, tn=1024):
    depth, d, n = ada_w.shape
    rows = cond.shape[0]
    return pl.pallas_call(
        _adaln_kernel,
        out_shape=jax.ShapeDtypeStruct((depth, rows, n), F32),
        grid=(depth, n // tn),
        in_specs=[
            pl.BlockSpec((rows, d), lambda l, j: (0, 0)),
            pl.BlockSpec((1, d, tn), lambda l, j: (l, 0, j)),
            pl.BlockSpec((1, 1, tn), lambda l, j: (l, 0, j)),
        ],
        out_specs=pl.BlockSpec((1, rows, tn), lambda l, j: (l, 0, j)),
        compiler_params=pltpu.CompilerParams(dimension_semantics=("arbitrary", "arbitrary"),
                                             vmem_limit_bytes=VMEM_LIMIT_BYTES),
        name="adaln",
    )(cond, ada_w, ada_b.reshape(depth, 1, n))


def _token_specs(arrays, tm, n_prompt_tiles):
    if len(arrays) == 1:
        return [pl.BlockSpec((tm, arrays[0].shape[1]), lambda i, *_: (i, 0))]
    return [pl.BlockSpec((tm, arrays[0].shape[1]), lambda i, *_: (jnp.minimum(i, n_prompt_tiles - 1), 0)),
            pl.BlockSpec((tm, arrays[1].shape[1]), lambda i, *_: (jnp.maximum(i - n_prompt_tiles, 0), 0))]


def _token_tile(refs, n_prompt_tiles):
    if len(refs) == 1:
        return refs[0][...]
    return jnp.where(pl.program_id(0) < n_prompt_tiles, refs[0][...], refs[1][...])


def _inproj_kernel(*refs, n_x, n_prompt_tiles, cache_cols, n_kv):
    x_refs = refs[:n_x]
    mod_ref, g_ref, w_ref, o_ref = refs[n_x:n_x + 4]
    cache_refs = refs[n_x + 4:]
    x = _token_tile(x_refs, n_prompt_tiles)
    h = _rms_rows(x, g_ref[...]) * (1.0 + mod_ref[1:2, :]) + mod_ref[0:1, :]
    o = _dot(h.astype(BF16), w_ref[...])
    o_ref[...] = o

    @pl.when(pl.program_id(0) < n_prompt_tiles)
    def _():
        tm = o.shape[0]
        for c_ref, col in zip(cache_refs, cache_cols):
            for hd in range(n_kv):
                c_ref[pl.ds(hd, tm, stride=n_kv), :] = o[:, col + hd * HEAD_DIM:col + (hd + 1) * HEAD_DIM]


def _in_proj(xs, mod_l, g, w, n_prompt, sample_len, cache_cols, n_kv, tm=512):
    n_tok = sum(a.shape[0] for a in xs)
    d = xs[0].shape[1]
    n = w.shape[1]
    npt = n_prompt // tm
    row = functools.partial(_cond_row, n_prompt_tiles=npt, tiles_per_sample=sample_len // tm)
    cache_spec = pl.BlockSpec((tm * n_kv, HEAD_DIM), lambda i: (jnp.minimum(i, npt - 1), 0))
    res = pl.pallas_call(
        functools.partial(_inproj_kernel, n_x=len(xs), n_prompt_tiles=npt, cache_cols=cache_cols, n_kv=n_kv),
        out_shape=(jax.ShapeDtypeStruct((n_tok, n), F32),
                   *[jax.ShapeDtypeStruct((n_prompt * n_kv, HEAD_DIM), F32) for _ in cache_cols]),
        grid=(n_tok // tm,),
        in_specs=_token_specs(xs, tm, npt) + [
            pl.BlockSpec((None, 6, d), lambda i: (row(i), 0, 0)),
            pl.BlockSpec((1, d), lambda i: (0, 0)),
            pl.BlockSpec((d, n), lambda i: (0, 0)),
        ],
        out_specs=(pl.BlockSpec((tm, n), lambda i: (i, 0)), *[cache_spec for _ in cache_cols]),
        compiler_params=pltpu.CompilerParams(dimension_semantics=("arbitrary",),
                                             vmem_limit_bytes=VMEM_LIMIT_BYTES),
        name="in_proj",
    )(*xs, mod_l, g.reshape(1, d), w)
    return res[0], res[1:]


def _gdn_kernel(q_ref, k_ref, v_ref, gate_ref, betaall_ref, gcall_ref, cwq_ref, cwk_ref, cwv_ref,
                ng_ref, s0_ref, o_ref, sout_ref,
                qn_ref, kn_ref, vn_ref, beta_ref, gc_ref, gcrow_ref, s_ref, oacc_ref,
                mq_a, c_a, oo_a, eg_a, mq_b, c_b, oo_b, eg_b, *, seq_len, heads, steps):
    L = seq_len
    nc = L // CHUNK
    nblk = nc // steps
    h0 = pl.program_id(1) * heads
    row_idx = lax.broadcasted_iota(jnp.int32, (L, LANES), 0)
    lane_idx = lax.broadcasted_iota(jnp.int32, (L, LANES), 1)

    def conv_silu(x, cw):
        prev = jnp.where(row_idx == 0, 0.0, pltpu.roll(x, 1, 0))
        nxt = jnp.where(row_idx == L - 1, 0.0, pltpu.roll(x, L - 1, 0))
        return _silu(prev * cw[0:1, :] + x * cw[1:2, :] + nxt * cw[2:3, :])

    def l2norm(x):
        return x * lax.rsqrt(jnp.sum(x * x, axis=-1, keepdims=True) + EPS)

    beta_all = betaall_ref[...]
    gc_all = gcall_ref[...]

    def pick(x, col):
        col_vals = jnp.sum(jnp.where(lane_idx == col, x, 0.0), axis=1, keepdims=True)
        return jnp.broadcast_to(col_vals, (L, LANES))

    for r in range(L // LANES):
        rows = jnp.transpose(gc_all[r * LANES:(r + 1) * LANES, :])[2 * A_HEADS:4 * A_HEADS, :]
        gcrow_ref[0, r] = rows
        gcrow_ref[1, r] = pltpu.roll(rows, CHUNK, 1)

    for hh in range(heads):
        cols = slice(hh * HEAD_DIM, (hh + 1) * HEAD_DIM)
        qn_ref[:, cols] = l2norm(conv_silu(q_ref[:, cols], cwq_ref[:, cols])) * (HEAD_DIM ** -0.5)
        kn_ref[:, cols] = l2norm(conv_silu(k_ref[:, cols], cwk_ref[:, cols]))
        vn_ref[:, cols] = conv_silu(v_ref[:, cols], cwv_ref[:, cols])
        s_ref[hh, 0] = s0_ref[0, hh]
        s_ref[hh, 1] = s0_ref[1, hh]
        head = h0 + hh
        for d in range(2):
            beta_ref[hh, d] = pick(beta_all, d * A_HEADS + head)
            gc_ref[hh, d] = pick(gc_all, (2 + d) * A_HEADS + head)

    r64 = lax.broadcasted_iota(jnp.int32, (CHUNK, LANES), 0)
    l64 = lax.broadcasted_iota(jnp.int32, (CHUNK, LANES), 1)
    is_f = l64 < CHUNK
    is_f_row = lax.broadcasted_iota(jnp.int32, (1, LANES), 1) < CHUNK
    cpos = jnp.where(is_f, l64, l64 - CHUNK)
    ahead = jnp.where(is_f, r64 - cpos, cpos - r64)
    incl = ahead >= 0
    strict = ahead > 0
    eye = jnp.where(r64 == cpos, 1.0, 0.0)
    is_f2 = lax.broadcasted_iota(jnp.int32, (2 * CHUNK, LANES), 1) < CHUNK
    zeros_c = jnp.zeros((CHUNK, HEAD_DIM), F32)

    def aligned(off, m):
        return off if isinstance(off, int) else pl.multiple_of(off, m)

    def half(n):
        return n // 2 if isinstance(n, int) else lax.shift_right_logical(n, 1)

    def block_diag(xp):
        z = jnp.zeros_like(xp)
        return jnp.concatenate([jnp.where(is_f, xp, z), jnp.where(is_f, z, xp)], axis=0)

    def split_dot(a_hi, a_lo, b_hi, b_lo):
        lhs = jnp.concatenate([a_hi, a_lo, a_hi], axis=1)
        rhs = jnp.concatenate([b_hi, b_hi, b_lo], axis=0)
        return _dot(lhs, rhs)

    def prep_stages(blk, slot):
        mq_ref, c_ref, oo_ref, eg_ref = slot
        probs = [(si, hh) for si in range(steps) for hh in range(heads)]
        st = {p: {} for p in probs}

        def load(p):
            si, hh = p
            j = blk * steps + si
            n_b = nc - 1 - j
            cols = slice(hh * HEAD_DIM, (hh + 1) * HEAD_DIM)
            off_f = aligned(j * CHUNK, CHUNK)
            off_b = aligned(n_b * CHUNK, CHUNK)
            q_f, k_f, v_f = (r[pl.ds(off_f, CHUNK), cols] for r in (qn_ref, kn_ref, vn_ref))
            q_b, k_b, v_b = (r[pl.ds(off_b, CHUNK), cols] for r in (qn_ref, kn_ref, vn_ref))
            beta_f = beta_ref[hh, 0, pl.ds(off_f, CHUNK), :]
            beta_b = beta_ref[hh, 1, pl.ds(off_b, CHUNK), :]
            gc_f = gc_ref[hh, 0, pl.ds(off_f, CHUNK), :]
            gc_b = gc_ref[hh, 1, pl.ds(off_b, CHUNK), :]
            row_f = gcrow_ref[si % 2, half(j), pl.ds(h0 + hh, 1), :]
            row_b = gcrow_ref[si % 2, half(n_b), pl.ds(A_HEADS + h0 + hh, 1), :]
            gcrow = jnp.where(is_f_row, row_f, row_b)
            diff = jnp.where(is_f, gc_f, gc_b) - gcrow
            decay = jnp.where(incl, jnp.exp(jnp.where(incl, diff, 0.0)), 0.0)
            kb_f = k_f * beta_f
            kb_b = k_b * beta_b
            e_f = jnp.exp(gc_f)
            e_b = jnp.exp(gc_b)
            glast_f = gc_f[CHUNK - 1:CHUNK, :]
            glast_b = gc_b[0:1, :]
            kdt = jnp.transpose(jnp.concatenate([k_f * jnp.exp(glast_f - gc_f), k_b * jnp.exp(glast_b - gc_b)],
                                                axis=0))
            zk = jnp.zeros_like(kdt)
            eg_ref[si, hh, 0] = jnp.broadcast_to(jnp.exp(glast_f), (8, LANES))
            eg_ref[si, hh, 1] = jnp.broadcast_to(jnp.exp(glast_b), (8, LANES))
            st[p].update(
                decay=decay,
                lhs=jnp.concatenate([jnp.concatenate([kb_f, kb_b], axis=1),
                                     jnp.concatenate([q_f, q_b], axis=1)], axis=0).astype(BF16),
                rhs=jnp.concatenate([jnp.concatenate([k_f, zeros_c], axis=1),
                                     jnp.concatenate([zeros_c, k_b], axis=1)], axis=0).astype(BF16),
                rhs_uw=jnp.concatenate([jnp.concatenate([v_f * beta_f, kb_f * e_f], axis=1),
                                        jnp.concatenate([v_b * beta_b, kb_b * e_b], axis=1)], axis=0).astype(BF16),
                kd2=jnp.concatenate([jnp.where(is_f2, kdt, zk), jnp.where(is_f2, zk, kdt)], axis=0).astype(BF16),
                qe=(q_f * e_f, q_b * e_b))

        def gram(p):
            s = st[p]
            aq = _dot_nt(s.pop("lhs"), s.pop("rhs"))
            decay = s.pop("decay")
            x = -jnp.where(strict, aq[:CHUNK] * decay, 0.0)
            s["qk"] = jnp.where(incl, aq[CHUNK:] * decay, 0.0)
            s["t"] = eye + x
            s["p"] = x

        def square(p):
            s = st[p]
            p_hi, p_lo = _split_bf16(s["p"])
            s["p"] = split_dot(p_hi, p_lo, block_diag(p_hi), block_diag(p_lo))

        def double(p):
            s = st[p]
            p_hi, p_lo = _split_bf16(s["p"])
            t_hi, t_lo = _split_bf16(s["t"])
            r = split_dot(p_hi, p_lo, jnp.concatenate([block_diag(t_hi), block_diag(p_hi)], axis=1),
                          jnp.concatenate([block_diag(t_lo), block_diag(p_lo)], axis=1))
            s["t"] = s["t"] + r[:, :LANES]
            s["p"] = r[:, LANES:]

        def last(p):
            s = st[p]
            p_hi, p_lo = _split_bf16(s.pop("p"))
            t_hi, t_lo = _split_bf16(s["t"])
            s["t"] = s["t"] + split_dot(p_hi, p_lo, block_diag(t_hi), block_diag(t_lo))

        def solve(p):
            s = st[p]
            s["uw"] = _dot(block_diag(s.pop("t")).astype(BF16), s.pop("rhs_uw"))

        def fold(p):
            si, hh = p
            s = st[p]
            lhs3 = jnp.concatenate([s.pop("kd2"), block_diag(s.pop("qk")).astype(BF16)], axis=0)
            big = _dot(lhs3, s.pop("uw").astype(BF16))
            qe = s.pop("qe")
            for d in range(2):
                rows_s = slice(d * 2 * CHUNK, (d + 1) * 2 * CHUNK)
                rows_o = slice(4 * CHUNK + d * CHUNK, 4 * CHUNK + (d + 1) * CHUNK)
                c_ref[si, hh, d] = big[rows_s, :HEAD_DIM]
                oo_ref[si, hh, d] = big[rows_o, :HEAD_DIM]
                mq_ref[si, hh, d] = jnp.concatenate([big[rows_s, HEAD_DIM:], qe[d] - big[rows_o, HEAD_DIM:]],
                                                    axis=0).astype(BF16)

        n_double = 0
        span = 2
        while 2 * span < CHUNK:
            n_double += 1
            span *= 2
        stage_fns = [load, gram, square] + [double] * n_double + [last, solve, fold]
        return [functools.partial(lambda fn: [fn(p) for p in probs], fn) for fn in stage_fns]

    def scan_step(blk, slot, si):
        mq_ref, c_ref, oo_ref, eg_ref = slot
        j = blk * steps + si
        for hh in range(heads):
            cols = slice(hh * HEAD_DIM, (hh + 1) * HEAD_DIM)
            for d, n in ((0, j), (1, nc - 1 - j)):
                s = s_ref[hh, d]
                r = _dot(mq_ref[si, hh, d], s.astype(BF16))
                s_ref[hh, d] = s * eg_ref[si, hh, d][0:1, :] - r[:2 * CHUNK] + c_ref[si, hh, d]
                oacc_ref[d, pl.ds(aligned(n * CHUNK, CHUNK), CHUNK), cols] = r[2 * CHUNK:] + oo_ref[si, hh, d]

    def run(prep, scan):
        stages = prep_stages(*prep) if prep else []
        n_scan = steps if scan else 0
        for k in range(max(len(stages), n_scan)):
            if k < len(stages):
                stages[k]()
            if k < n_scan:
                scan_step(scan[0], scan[1], k)

    slot_a = (mq_a, c_a, oo_a, eg_a)
    slot_b = (mq_b, c_b, oo_b, eg_b)
    run((0, slot_a), None)
    if nblk == 1:
        run(None, (0, slot_a))
    else:
        def body(i, carry):
            blk = 2 * i
            run((blk + 1, slot_b), (blk, slot_a))
            run((blk + 2, slot_a), (blk + 1, slot_b))
            return carry

        lax.fori_loop(0, nblk // 2 - 1, body, 0)
        run((nblk - 1, slot_b), (nblk - 2, slot_a))
        run(None, (nblk - 1, slot_b))

    for hh in range(heads):
        cols = slice(hh * HEAD_DIM, (hh + 1) * HEAD_DIM)
        sout_ref[0, hh] = s_ref[hh, 0]
        sout_ref[1, hh] = s_ref[hh, 1]
        o = oacc_ref[0, :, cols] + oacc_ref[1, :, cols]
        o = _rms_rows(o, ng_ref[...]) * _silu(gate_ref[:, cols])
        o_ref[:, cols] = o.astype(o_ref.dtype)


def _gates_kernel(bg_ref, arow_ref, dtrow_ref, beta_ref, gc_ref):
    rows = bg_ref.shape[0]
    bg = bg_ref[...]
    beta_ref[...] = _sigmoid(bg)
    g = -jnp.exp(arow_ref[...]) * _softplus(bg + dtrow_ref[...])
    chunk_row = lax.broadcasted_iota(jnp.int32, (rows, LANES), 0) % CHUNK
    lane = lax.broadcasted_iota(jnp.int32, (rows, LANES), 1)
    pre, suf = g, g
    shift = 1
    while shift < CHUNK:
        pre = pre + jnp.where(chunk_row >= shift, pltpu.roll(pre, shift, 0), 0.0)
        suf = suf + jnp.where(chunk_row < CHUNK - shift, pltpu.roll(suf, rows - shift, 0), 0.0)
        shift *= 2
    gc_ref[...] = jnp.where(lane >= 3 * A_HEADS, suf, pre)


def _gates(proj, arow, dtrow, tm=1024):
    n_tok = proj.shape[0]
    assert tm % CHUNK == 0
    spec = pl.BlockSpec((tm, LANES), lambda i: (i, 0))
    return pl.pallas_call(
        _gates_kernel,
        out_shape=(jax.ShapeDtypeStruct((n_tok, LANES), F32), jax.ShapeDtypeStruct((n_tok, LANES), F32)),
        grid=(n_tok // tm,),
        in_specs=[pl.BlockSpec((tm, LANES), lambda i: (i, EV_BG_OFF // LANES)),
                  pl.BlockSpec((1, LANES), lambda i: (0, 0)), pl.BlockSpec((1, LANES), lambda i: (0, 0))],
        out_specs=(spec, spec),
        compiler_params=pltpu.CompilerParams(dimension_semantics=("arbitrary",)),
        name="gates",
    )(proj, arow, dtrow)


def _gdn(proj, beta_all, gc_all, conv_w, norm_g, s0, row_block0, n_seq, seq_len, heads, steps):
    L = seq_len
    nblk = L // CHUNK // steps
    assert L % (CHUNK * steps) == 0 and steps % 2 == 0 and 2 * CHUNK == LANES and A_HEADS % heads == 0
    assert nblk == 1 or nblk % 2 == 0
    width = heads * HEAD_DIM
    groups = A_HEADS // heads
    blk = lambda base: pl.BlockSpec((L, width), lambda b, h: (row_block0 + b, base * groups + h))
    wblk = lambda base: pl.BlockSpec((3, width), lambda b, h: (0, base * groups + h))
    state_spec = pl.BlockSpec((None, 2, heads, HEAD_DIM, HEAD_DIM), lambda b, h: (b, 0, h, 0, 0))
    slot = [pltpu.VMEM((steps, heads, 2, 3 * CHUNK, HEAD_DIM), BF16),
            pltpu.VMEM((steps, heads, 2, 2 * CHUNK, HEAD_DIM), F32),
            pltpu.VMEM((steps, heads, 2, CHUNK, HEAD_DIM), F32),
            pltpu.VMEM((steps, heads, 2, 8, LANES), F32)]
    kern = functools.partial(_gdn_kernel, seq_len=L, heads=heads, steps=steps)
    out, s_out = pl.pallas_call(
        kern,
        out_shape=(jax.ShapeDtypeStruct((n_seq * L, A_WIDTH), BF16),
                   jax.ShapeDtypeStruct((n_seq, 2, A_HEADS, HEAD_DIM, HEAD_DIM), F32)),
        grid=(n_seq, groups),
        in_specs=[
            blk(0), blk(1), blk(2), blk(3),
            pl.BlockSpec((L, LANES), lambda b, h: (row_block0 + b, 0)),
            pl.BlockSpec((L, LANES), lambda b, h: (row_block0 + b, 0)),
            wblk(0), wblk(1), wblk(2),
            pl.BlockSpec((1, HEAD_DIM), lambda b, h: (0, 0)),
            state_spec,
        ],
        out_specs=(pl.BlockSpec((L, width), lambda b, h: (b, h)), state_spec),
        scratch_shapes=[
            pltpu.VMEM((L, width), F32), pltpu.VMEM((L, width), F32), pltpu.VMEM((L, width), F32),
            pltpu.VMEM((heads, 2, L, LANES), F32), pltpu.VMEM((heads, 2, L, LANES), F32),
            pltpu.VMEM((2, L // LANES, 2 * A_HEADS, LANES), F32),
            pltpu.VMEM((heads, 2, HEAD_DIM, HEAD_DIM), F32),
            pltpu.VMEM((2, L, width), F32),
        ] + slot + slot,
        compiler_params=pltpu.CompilerParams(dimension_semantics=("arbitrary", "arbitrary"),
                                             vmem_limit_bytes=VMEM_LIMIT_BYTES),
        name="gdn",
    )(proj, proj, proj, proj, beta_all, gc_all, conv_w, conv_w, conv_w, norm_g.reshape(1, HEAD_DIM), s0)
    return out, s_out


def _rope_rows(x, cos, sin_signed):
    lane = lax.broadcasted_iota(jnp.int32, x.shape, 1)
    swapped = jnp.where((lane % (2 * ROT_FREQS)) < ROT_FREQS,
                        pltpu.roll(x, HEAD_DIM - ROT_FREQS, 1), pltpu.roll(x, ROT_FREQS, 1))
    return x * cos + swapped * sin_signed


def _attn_kernel(*refs, tq, seq_len, group, n_kv, n_ctx, use_sink, window, qk_norm, rope, emit_k):
    it = iter(refs)
    q_ref, k_ref, v_ref = next(it), next(it), next(it)
    ck_ref = cv_ref = cos_ref = sin_ref = qg_ref = kg_ref = sink_ref = kn_out_ref = None
    if n_ctx:
        ck_ref, cv_ref = next(it), next(it)
    if rope:
        cos_ref, sin_ref = next(it), next(it)
    if qk_norm:
        qg_ref, kg_ref = next(it), next(it)
    if use_sink:
        sink_ref = next(it)
    o_ref = next(it)
    if emit_k:
        kn_out_ref = next(it)
    kbf_ref, vbf_ref = next(it), next(it)
    ckbf_ref = cvbf_ref = None
    if n_ctx:
        ckbf_ref, cvbf_ref = next(it), next(it)

    L = seq_len
    kvh = pl.program_id(1)
    qi = pl.program_id(2)

    def ones_column(rows):
        lane = lax.broadcasted_iota(jnp.int32, (rows, HEAD_DIM), 1)
        return jnp.where(lane == 0, 1.0, 0.0).astype(BF16)

    @pl.when(qi == 0)
    def _():
        k = k_ref[...]
        if qk_norm:
            k = _rms_rows(k, kg_ref[...])
        if emit_k:
            for hd in range(n_kv):
                @pl.when(kvh == hd)
                def _():
                    kn_out_ref[pl.ds(hd, L, stride=n_kv), :] = k
        if rope:
            k = _rope_rows(k, cos_ref[...], sin_ref[...])
        kbf_ref[...] = k.astype(BF16)
        vbf_ref[:, :HEAD_DIM] = v_ref[...].astype(BF16)
        vbf_ref[:, HEAD_DIM:] = ones_column(L)
        if n_ctx:
            ckbf_ref[...] = ck_ref[...].astype(BF16)
            cvbf_ref[:, :HEAD_DIM] = cv_ref[...].astype(BF16)
            cvbf_ref[:, HEAD_DIM:] = ones_column(n_ctx)

    q0 = pl.multiple_of(qi * tq, tq)
    if window:
        wk = tq + 2 * window
        ws = pl.multiple_of(jnp.clip(q0 - window, 0, L - wk), LANES)
        k_loc = kbf_ref[pl.ds(ws, wk), :]
        v_loc = vbf_ref[pl.ds(ws, wk), :]
        qpos = q0 + lax.broadcasted_iota(jnp.int32, (tq, wk), 0)
        kpos = ws + lax.broadcasted_iota(jnp.int32, (tq, wk), 1)
        valid = jnp.abs(qpos - kpos) <= window
    else:
        k_loc = kbf_ref[...]
        v_loc = vbf_ref[...]

    def scores(g):
        q = q_ref[:, g * HEAD_DIM:(g + 1) * HEAD_DIM]
        if qk_norm:
            q = _rms_rows(q, qg_ref[...])
        if rope:
            q = _rope_rows(q, cos_ref[pl.ds(q0, tq), :], sin_ref[pl.ds(q0, tq), :])
        q = (q * (HEAD_DIM ** -0.5 * LOG2_E)).astype(BF16)
        s_loc = _dot_nt(q, k_loc)
        if window:
            s_loc = jnp.where(valid, s_loc, NEG)
        pieces = [(s_loc, v_loc)]
        if n_ctx:
            pieces.append((_dot_nt(q, ckbf_ref[...]), cvbf_ref[...]))
        return pieces

    def finish(g, pieces):
        m = None
        for s, _ in pieces:
            row_max = jnp.max(s, axis=-1, keepdims=True)
            m = row_max if m is None else jnp.maximum(m, row_max)
        if use_sink:
            sink = sink_ref[kvh * group + g] * LOG2_E
            m = jnp.maximum(m, sink)
        acc = None
        for s, v in pieces:
            term = _dot(jnp.exp2(s - m).astype(BF16), v)
            acc = term if acc is None else acc + term
        denom = acc[:, HEAD_DIM:HEAD_DIM + 1]
        if use_sink:
            denom = denom + jnp.exp2(sink - m)
        o_ref[:, g * HEAD_DIM:(g + 1) * HEAD_DIM] = (acc[:, :HEAD_DIM] / denom).astype(o_ref.dtype)

    ahead = scores(0)
    for g in range(group):
        cur = ahead
        if g + 1 < group:
            ahead = scores(g + 1)
        finish(g, cur)


def _attention(proj, *, q_off, k_off, v_off, n_heads, n_kv, row_block0, n_seq, seq_len, tq,
               ctx_k=None, ctx_v=None, cos=None, sin=None, q_gain=None, k_gain=None, sink=None,
               window=0, emit_k=False):
    L = seq_len
    group = n_heads // n_kv
    n_ctx = 0 if ctx_k is None else ctx_k.shape[1]
    rope = cos is not None
    qk_norm = q_gain is not None
    use_sink = sink is not None
    nq = L // tq
    qb, kb, vb = q_off // (group * HEAD_DIM), k_off // HEAD_DIM, v_off // HEAD_DIM
    in_specs = [
        pl.BlockSpec((tq, group * HEAD_DIM), lambda b, h, i: ((row_block0 + b) * nq + i, qb + h)),
        pl.BlockSpec((L, HEAD_DIM), lambda b, h, i: (row_block0 + b, kb + h)),
        pl.BlockSpec((L, HEAD_DIM), lambda b, h, i: (row_block0 + b, vb + h)),
    ]
    args = [proj, proj, proj]
    scratch = [pltpu.VMEM((L, HEAD_DIM), BF16), pltpu.VMEM((L, 2 * HEAD_DIM), BF16)]
    if n_ctx:
        in_specs += [pl.BlockSpec((None, n_ctx, HEAD_DIM), lambda b, h, i: (b, 0, h))] * 2
        args += [ctx_k, ctx_v]
        scratch += [pltpu.VMEM((n_ctx, HEAD_DIM), BF16), pltpu.VMEM((n_ctx, 2 * HEAD_DIM), BF16)]
    if rope:
        in_specs += [pl.BlockSpec((L, HEAD_DIM), lambda b, h, i: (0, 0))] * 2
        args += [cos, sin]
    if qk_norm:
        in_specs += [pl.BlockSpec((1, HEAD_DIM), lambda b, h, i: (0, 0))] * 2
        args += [q_gain.reshape(1, HEAD_DIM), k_gain.reshape(1, HEAD_DIM)]
    if use_sink:
        in_specs += [pl.BlockSpec(memory_space=pltpu.SMEM)]
        args += [sink]
    out_shape = [jax.ShapeDtypeStruct((n_seq * L, n_heads * HEAD_DIM), BF16)]
    out_specs = [pl.BlockSpec((tq, group * HEAD_DIM), lambda b, h, i: (b * nq + i, h))]
    if emit_k:
        out_shape += [jax.ShapeDtypeStruct((n_seq * L * n_kv, HEAD_DIM), F32)]
        out_specs += [pl.BlockSpec((L * n_kv, HEAD_DIM), lambda b, h, i: (b, 0))]
    kern = functools.partial(_attn_kernel, tq=tq, seq_len=L, group=group, n_kv=n_kv, n_ctx=n_ctx,
                             use_sink=use_sink, window=window, qk_norm=qk_norm, rope=rope, emit_k=emit_k)
    res = pl.pallas_call(
        kern,
        out_shape=tuple(out_shape),
        grid=(n_seq, n_kv, nq),
        in_specs=in_specs,
        out_specs=tuple(out_specs),
        scratch_shapes=scratch,
        compiler_params=pltpu.CompilerParams(dimension_semantics=("arbitrary", "arbitrary", "arbitrary"),
                                             vmem_limit_bytes=VMEM_LIMIT_BYTES),
        name="attention",
    )(*args)
    return res if emit_k else res[0]


def _out_mlp_kernel(*refs, n_x, n_parts, final, n_prompt_tiles, tf):
    it = iter(refs)
    x_refs = [next(it) for _ in range(n_x)]
    a_refs = [[next(it), next(it)] for _ in range(n_parts)]
    wo_refs = [next(it) for _ in range(n_parts)]
    mod_ref, g2_ref, w1_ref, w2_ref = next(it), next(it), next(it), next(it)
    fg_ref = next(it) if final else None
    o_refs = [next(it) for _ in range(2 if final else 1)]
    i = pl.program_id(0)
    d_ff = w1_ref.shape[1]

    mix = None
    for pair, wo_ref in zip(a_refs, wo_refs):
        term = _dot(_token_tile(pair, n_prompt_tiles), wo_ref[...])
        mix = term if mix is None else mix + term
    x1 = _token_tile(x_refs, n_prompt_tiles) + mod_ref[2:3, :] * mix
    h2 = (_rms_rows(x1, g2_ref[...]) * (1.0 + mod_ref[4:5, :]) + mod_ref[3:4, :]).astype(BF16)
    acc = None
    for c in range(d_ff // tf):
        hid = jnp.maximum(_dot(h2, w1_ref[:, c * tf:(c + 1) * tf]), 0.0)
        term = _dot((hid * hid).astype(BF16), w2_ref[c * tf:(c + 1) * tf, :])
        acc = term if acc is None else acc + term
    y = x1 + mod_ref[5:6, :] * acc

    if final:
        y = _rms_rows(y, fg_ref[...])

        @pl.when(i < n_prompt_tiles)
        def _():
            o_refs[0][...] = y

        @pl.when(i >= n_prompt_tiles)
        def _():
            o_refs[1][...] = y
    else:
        o_refs[0][...] = y


def _out_mlp(xs, parts, wo_parts, mod_l, g2, w1, w2, n_prompt, sample_len, final_g=None, tm=512, tf=1024):
    n_tok = sum(a.shape[0] for a in xs)
    d = xs[0].shape[1]
    d_ff = w1.shape[1]
    final = final_g is not None
    npt = n_prompt // tm
    row = functools.partial(_cond_row, n_prompt_tiles=npt, tiles_per_sample=sample_len // tm)
    in_specs = _token_specs(xs, tm, npt)
    for pair in parts:
        in_specs += _token_specs(pair, tm, npt)
    resident = lambda shape: pl.BlockSpec(shape, lambda i: (0,) * len(shape), pipeline_mode=pl.Buffered(1))
    in_specs += [resident(w.shape) for w in wo_parts]
    in_specs += [
        pl.BlockSpec((None, 6, d), lambda i: (row(i), 0, 0)),
        pl.BlockSpec((1, d), lambda i: (0, 0)),
        resident((d, d_ff)),
        resident((d_ff, d)),
    ]
    args = [*xs, *[a for pair in parts for a in pair], *wo_parts, mod_l, g2.reshape(1, d), w1, w2]
    if final:
        in_specs += [pl.BlockSpec((1, d), lambda i: (0, 0))]
        args += [final_g.reshape(1, d)]
        out_shape = (jax.ShapeDtypeStruct((n_prompt, d), F32), jax.ShapeDtypeStruct((n_tok - n_prompt, d), F32))
        out_specs = (pl.BlockSpec((tm, d), lambda i: (jnp.minimum(i, npt - 1), 0)),
                     pl.BlockSpec((tm, d), lambda i: (jnp.maximum(i - npt, 0), 0)))
    else:
        out_shape = jax.ShapeDtypeStruct((n_tok, d), F32)
        out_specs = pl.BlockSpec((tm, d), lambda i: (i, 0))
    kern = functools.partial(_out_mlp_kernel, n_x=len(xs), n_parts=len(parts), final=final, n_prompt_tiles=npt,
                             tf=tf)
    return pl.pallas_call(
        kern,
        out_shape=out_shape,
        grid=(n_tok // tm,),
        in_specs=in_specs,
        out_specs=out_specs,
        compiler_params=pltpu.CompilerParams(dimension_semantics=("arbitrary",),
                                             vmem_limit_bytes=VMEM_LIMIT_BYTES),
        name="out_mlp",
    )(*args)


def _rope_tables(seq_len):
    rows = seq_len // GRID_W
    inv = ROPE_THETA ** (-jnp.arange(ROT_FREQS, dtype=F32) / ROT_FREQS)
    ang_r = jnp.repeat(jnp.arange(rows, dtype=F32), GRID_W)[:, None] * inv
    ang_c = jnp.tile(jnp.arange(GRID_W, dtype=F32), rows)[:, None] * inv
    cos = jnp.concatenate([jnp.cos(ang_r), jnp.cos(ang_r), jnp.cos(ang_c), jnp.cos(ang_c)], axis=-1)
    sin = jnp.concatenate([-jnp.sin(ang_r), jnp.sin(ang_r), -jnp.sin(ang_c), jnp.sin(ang_c)], axis=-1)
    return cos.astype(F32), sin.astype(F32)


def kernel(x_prompt, x_sample, state_a, cache_b_kv, cache_c_kv, c, c_ctx, ada_w, ada_b, norm1_g, norm2_g,
           final_g, mlp_w1, mlp_w2, ev_w_in, a_conv, a_log, a_dt_bias, a_norm_g, b_sink, ev_w_out,
           od_w_in, c_qnorm_g, c_knorm_g, od_w_out):
    batch, seq, d = x_prompt.shape
    dec_batch, dec_seq, _ = x_sample.shape
    depth = ada_w.shape[0]
    n_prompt, n_sample = batch * seq, dec_batch * dec_seq
    past = cache_b_kv.shape[3]

    xs = [x_prompt.reshape(n_prompt, d), x_sample.reshape(n_sample, d)]
    cond = jnp.concatenate([c_ctx[None, :], c, jnp.zeros((8 - 1 - dec_batch, d), F32)], axis=0)
    mod = _adaln(cond, ada_w, ada_b).reshape(depth, 8, 6, d)

    cos, sin = _rope_tables(dec_seq)
    s_blk0 = n_prompt // dec_seq
    zero_state = jnp.zeros((batch, 2, A_HEADS, HEAD_DIM, HEAD_DIM), F32)
    lane_pad = lambda v: jnp.zeros((1, LANES), F32).at[0, 2 * A_HEADS:4 * A_HEADS].set(v.reshape(-1))

    new_a, new_b, new_c = [], [], []
    for l in range(depth):
        i = l // 2
        fin = final_g if l == depth - 1 else None
        w1 = mlp_w1[l].astype(BF16)
        w2 = mlp_w2[l].astype(BF16)
        if l % 2 == 0:
            w_in = ev_w_in[i]
            bg_off = 4 * A_WIDTH
            qb_off = bg_off + 4 * A_HEADS
            w_in = jnp.concatenate([w_in[:, :bg_off], w_in[:, qb_off:], w_in[:, bg_off:qb_off],
                                    jnp.zeros((d, LANES - 4 * A_HEADS), F32)], axis=1).astype(BF16)
            kv_cols = B_KV_HEADS * HEAD_DIM
            proj, (cache_k, cache_v) = _in_proj(xs, mod[l], norm1_g[l], w_in, n_prompt, dec_seq,
                                                cache_cols=(EV_KVB_OFF, EV_KVB_OFF + kv_cols), n_kv=B_KV_HEADS)
            beta_all, gc_all = _gates(proj, lane_pad(a_log[i]), lane_pad(a_dt_bias[i]))
            oa_p, st_p = _gdn(proj, beta_all, gc_all, a_conv[i], a_norm_g[i], zero_state, 0, batch, seq,
                              heads=A_HEADS, steps=seq // CHUNK)
            oa_s, _ = _gdn(proj, beta_all, gc_all, a_conv[i], a_norm_g[i], state_a[:, i], s_blk0, dec_batch,
                           dec_seq, heads=1, steps=8)
            ob_p = _attention(proj, q_off=EV_QB_OFF, k_off=EV_KVB_OFF, v_off=EV_KVB_OFF + kv_cols,
                              n_heads=B_HEADS, n_kv=B_KV_HEADS, row_block0=0, n_seq=batch, seq_len=seq,
                              tq=seq, sink=b_sink[i])
            ob_s = _attention(proj, q_off=EV_QB_OFF, k_off=EV_KVB_OFF, v_off=EV_KVB_OFF + kv_cols,
                              n_heads=B_HEADS, n_kv=B_KV_HEADS, row_block0=s_blk0, n_seq=dec_batch,
                              seq_len=dec_seq, tq=256,
                              ctx_k=cache_b_kv[:, i, 0].reshape(dec_batch, past, kv_cols),
                              ctx_v=cache_b_kv[:, i, 1].reshape(dec_batch, past, kv_cols),
                              cos=cos, sin=sin, sink=b_sink[i], window=WINDOW)
            new_a.append(st_p)
            new_b.append(jnp.stack([cache_k.reshape(batch, seq, B_KV_HEADS, HEAD_DIM),
                                    cache_v.reshape(batch, seq, B_KV_HEADS, HEAD_DIM)], axis=1))
            parts = [[oa_p, oa_s], [ob_p, ob_s]]
            w_out = ev_w_out[i].astype(BF16)
            wo_parts = [w_out[:A_WIDTH], w_out[A_WIDTH:]]
        else:
            kv_cols = C_KV_HEADS * HEAD_DIM
            proj, (cache_v,) = _in_proj(xs, mod[l], norm1_g[l], od_w_in[i].astype(BF16), n_prompt, dec_seq,
                                        cache_cols=(OD_V_OFF,), n_kv=C_KV_HEADS)
            oc_p, kn_p = _attention(proj, q_off=0, k_off=OD_K_OFF, v_off=OD_V_OFF, n_heads=C_HEADS,
                                    n_kv=C_KV_HEADS, row_block0=0, n_seq=batch, seq_len=seq, tq=seq,
                                    q_gain=c_qnorm_g[i], k_gain=c_knorm_g[i], emit_k=True)
            oc_s = _attention(proj, q_off=0, k_off=OD_K_OFF, v_off=OD_V_OFF, n_heads=C_HEADS,
                              n_kv=C_KV_HEADS, row_block0=s_blk0, n_seq=dec_batch, seq_len=dec_seq, tq=256,
                              ctx_k=cache_c_kv[:, i, 0].reshape(dec_batch, past, kv_cols),
                              ctx_v=cache_c_kv[:, i, 1].reshape(dec_batch, past, kv_cols),
                              cos=cos, sin=sin, q_gain=c_qnorm_g[i], k_gain=c_knorm_g[i])
            new_c.append(jnp.stack([kn_p.reshape(batch, seq, C_KV_HEADS, HEAD_DIM),
                                    cache_v.reshape(batch, seq, C_KV_HEADS, HEAD_DIM)], axis=1))
            parts = [[oc_p, oc_s]]
            wo_parts = [od_w_out[i].astype(BF16)]
        out = _out_mlp(xs, parts, wo_parts, mod[l], norm2_g[l], w1, w2, n_prompt, dec_seq, final_g=fin)
        xs = list(out) if fin is not None else [out]

    y_prompt = xs[0].reshape(batch, seq, d)
    y_sample = xs[1].reshape(dec_batch, dec_seq, d)
    return (y_prompt, y_sample, jnp.stack(new_a, axis=1), jnp.stack(new_b, axis=1), jnp.stack(new_c, axis=1))
```

```python
import functools

import jax
import jax.numpy as jnp
from jax import lax
from jax.experimental import pallas as pl
from jax.experimental.pallas import tpu as pltpu

F32 = jnp.float32
BF16 = jnp.bfloat16

HEAD_DIM = 128
GRID_W = 64
A_HEADS = 4
CHUNK = 64
B_HEADS = 4
B_KV_HEADS = 2
WINDOW = 128
C_HEADS = 8
C_KV_HEADS = 2
ROT_FREQS = HEAD_DIM // 4
ROPE_THETA = 10000.0
EPS = 1e-6
NEG = -1e30
LOG2_E = 1.4426950408889634
A_WIDTH = A_HEADS * HEAD_DIM
B_WIDTH = B_HEADS * HEAD_DIM
C_WIDTH = C_HEADS * HEAD_DIM

LANES = 128
VMEM_LIMIT_BYTES = 56 * 1024 * 1024

EV_GATE_OFF = 3 * A_WIDTH
EV_QB_OFF = 4 * A_WIDTH
EV_KVB_OFF = EV_QB_OFF + B_WIDTH
EV_BG_OFF = EV_KVB_OFF + 2 * B_KV_HEADS * HEAD_DIM
EV_COLS = EV_BG_OFF + LANES
OD_K_OFF = C_WIDTH
OD_V_OFF = C_WIDTH + C_KV_HEADS * HEAD_DIM


def _sigmoid(x):
    return 1.0 / (1.0 + jnp.exp(-x))


def _silu(x):
    return x * _sigmoid(x)


def _softplus(x):
    return jnp.maximum(x, 0.0) + jnp.log1p(jnp.exp(-jnp.abs(x)))


def _rms_rows(x, g):
    return x * lax.rsqrt(jnp.mean(x * x, axis=-1, keepdims=True) + EPS) * g


def _dot(a, b):
    return jnp.dot(a, b, preferred_element_type=F32)


def _dot_nt(a, b):
    return lax.dot_general(a, b, (((1,), (1,)), ((), ())), preferred_element_type=F32)


def _cond_row(i, n_prompt_tiles, tiles_per_sample):
    return jnp.where(i < n_prompt_tiles, 0, 1 + (i - n_prompt_tiles) // tiles_per_sample)


def _split_bf16(x):
    hi = x.astype(BF16)
    lo = (x - hi.astype(F32)).astype(BF16)
    return hi, lo


def _adaln_kernel(cond_ref, w_ref, b_ref, o_ref):
    rows = cond_ref.shape[0]
    s_hi, s_lo = _split_bf16(_silu(cond_ref[...]))
    w_hi, w_lo = _split_bf16(w_ref[0])
    r = _dot(jnp.concatenate([s_hi, s_lo], axis=0), w_hi)
    o_ref[0] = r[:rows] + r[rows:] + _dot(s_hi, w_lo) + b_ref[0]


def _adaln(cond, ada_w, ada_b, tn=1024):
    depth, d, n = ada_w.shape
    rows = cond.shape[0]
    return pl.pallas_call(
        _adaln_kernel,
        out_shape=jax.ShapeDtypeStruct((depth, rows, n), F32),
        grid=(depth, n // tn),
        in_specs=[
            pl.BlockSpec((rows, d), lambda l, j: (0, 0)),
            pl.BlockSpec((1, d, tn), lambda l, j: (l, 0, j)),
            pl.BlockSpec((1, 1, tn), lambda l, j: (l, 0, j)),
        ],
        out_specs=pl.BlockSpec((1, rows, tn), lambda l, j: (l, 0, j)),
        compiler_params=pltpu.CompilerParams(dimension_semantics=("arbitrary", "arbitrary"),
                                             vmem_limit_bytes=VMEM_LIMIT_BYTES),
        name="adaln",
    )(cond, ada_w, ada_b.reshape(depth, 1, n))


def _ev_w_in_kernel(w_ref, o_ref):
    bg_off = 4 * A_WIDTH
    bg_end = bg_off + 4 * A_HEADS
    rows = w_ref.shape[0]
    o_ref[:, :bg_off] = w_ref[:, :bg_off].astype(BF16)
    o_ref[:, bg_off:EV_BG_OFF] = w_ref[:, bg_end:].astype(BF16)
    tail = jnp.concatenate([w_ref[:, bg_off:bg_end], jnp.zeros((rows, LANES - 4 * A_HEADS), F32)], axis=1)
    o_ref[:, EV_BG_OFF:] = tail.astype(BF16)


def _ev_w_in(w, tr=256):
    layers, d, n = w.shape
    return pl.pallas_call(
        _ev_w_in_kernel,
        out_shape=jax.ShapeDtypeStruct((layers, d, EV_COLS), BF16),
        grid=(layers, d // tr),
        in_specs=[pl.BlockSpec((None, tr, n), lambda l, r: (l, r, 0))],
        out_specs=pl.BlockSpec((None, tr, EV_COLS), lambda l, r: (l, r, 0)),
        compiler_params=pltpu.CompilerParams(dimension_semantics=("arbitrary", "arbitrary")),
        name="ev_w_in",
    )(w)


def _token_specs(arrays, tm, n_prompt_tiles):
    if len(arrays) == 1:
        return [pl.BlockSpec((tm, arrays[0].shape[1]), lambda i, *_: (i, 0))]
    return [pl.BlockSpec((tm, arrays[0].shape[1]), lambda i, *_: (jnp.minimum(i, n_prompt_tiles - 1), 0)),
            pl.BlockSpec((tm, arrays[1].shape[1]), lambda i, *_: (jnp.maximum(i - n_prompt_tiles, 0), 0))]


def _token_tile(refs, n_prompt_tiles):
    if len(refs) == 1:
        return refs[0][...]
    return jnp.where(pl.program_id(0) < n_prompt_tiles, refs[0][...], refs[1][...])


def _inproj_kernel(*refs, n_x, n_prompt_tiles, cache_cols, n_kv):
    x_refs = refs[:n_x]
    mod_ref, g_ref, w_ref, o_ref = refs[n_x:n_x + 4]
    cache_refs = refs[n_x + 4:]
    x = _token_tile(x_refs, n_prompt_tiles)
    h = _rms_rows(x, g_ref[...]) * (1.0 + mod_ref[1:2, :]) + mod_ref[0:1, :]
    o = _dot(h.astype(BF16), w_ref[...])
    o_ref[...] = o

    @pl.when(pl.program_id(0) < n_prompt_tiles)
    def _():
        tm = o.shape[0]
        for c_ref, col in zip(cache_refs, cache_cols):
            for hd in range(n_kv):
                c_ref[pl.ds(hd, tm, stride=n_kv), :] = o[:, col + hd * HEAD_DIM:col + (hd + 1) * HEAD_DIM]


def _in_proj(xs, mod_l, g, w, n_prompt, sample_len, cache_cols, n_kv, tm=512):
    n_tok = sum(a.shape[0] for a in xs)
    d = xs[0].shape[1]
    n = w.shape[1]
    npt = n_prompt // tm
    row = functools.partial(_cond_row, n_prompt_tiles=npt, tiles_per_sample=sample_len // tm)
    cache_spec = pl.BlockSpec((tm * n_kv, HEAD_DIM), lambda i: (jnp.minimum(i, npt - 1), 0))
    res = pl.pallas_call(
        functools.partial(_inproj_kernel, n_x=len(xs), n_prompt_tiles=npt, cache_cols=cache_cols, n_kv=n_kv),
        out_shape=(jax.ShapeDtypeStruct((n_tok, n), F32),
                   *[jax.ShapeDtypeStruct((n_prompt * n_kv, HEAD_DIM), F32) for _ in cache_cols]),
        grid=(n_tok // tm,),
        in_specs=_token_specs(xs, tm, npt) + [
            pl.BlockSpec((None, 6, d), lambda i: (row(i), 0, 0)),
            pl.BlockSpec((1, d), lambda i: (0, 0)),
            pl.BlockSpec((d, n), lambda i: (0, 0)),
        ],
        out_specs=(pl.BlockSpec((tm, n), lambda i: (i, 0)), *[cache_spec for _ in cache_cols]),
        compiler_params=pltpu.CompilerParams(dimension_semantics=("arbitrary",),
                                             vmem_limit_bytes=VMEM_LIMIT_BYTES),
        name="in_proj",
    )(*xs, mod_l, g.reshape(1, d), w)
    return res[0], res[1:]


def _gdn_kernel(q_ref, k_ref, v_ref, gate_ref, betaall_ref, gcall_ref, cwq_ref, cwk_ref, cwv_ref, ng_ref, *rest,
                seq_len, heads, steps, has_state):
    s0_ref = rest[0] if has_state else None
    (o_ref, sout_ref, qn_ref, kn_ref, vn_ref, beta_ref, gc_ref, gcrow_ref, s_ref, oacc_ref,
     mq_a, c_a, oo_a, eg_a, mq_b, c_b, oo_b, eg_b) = rest[1:] if has_state else rest
    L = seq_len
    nc = L // CHUNK
    nblk = nc // steps
    h0 = pl.program_id(1) * heads
    row_idx = lax.broadcasted_iota(jnp.int32, (L, LANES), 0)
    lane_idx = lax.broadcasted_iota(jnp.int32, (L, LANES), 1)

    def conv_silu(x, cw):
        prev = jnp.where(row_idx == 0, 0.0, pltpu.roll(x, 1, 0))
        nxt = jnp.where(row_idx == L - 1, 0.0, pltpu.roll(x, L - 1, 0))
        return _silu(prev * cw[0:1, :] + x * cw[1:2, :] + nxt * cw[2:3, :])

    def l2norm(x):
        return x * lax.rsqrt(jnp.sum(x * x, axis=-1, keepdims=True) + EPS)

    beta_all = betaall_ref[...]
    gc_all = gcall_ref[...]

    def pick(x, col):
        col_vals = jnp.sum(jnp.where(lane_idx == col, x, 0.0), axis=1, keepdims=True)
        return jnp.broadcast_to(col_vals, (L, LANES))

    for r in range(L // LANES):
        rows = jnp.transpose(gc_all[r * LANES:(r + 1) * LANES, :])[2 * A_HEADS:4 * A_HEADS, :]
        gcrow_ref[0, r] = rows
        gcrow_ref[1, r] = pltpu.roll(rows, CHUNK, 1)

    for hh in range(heads):
        cols = slice(hh * HEAD_DIM, (hh + 1) * HEAD_DIM)
        qn_ref[:, cols] = l2norm(conv_silu(q_ref[:, cols], cwq_ref[:, cols])) * (HEAD_DIM ** -0.5)
        kn_ref[:, cols] = l2norm(conv_silu(k_ref[:, cols], cwk_ref[:, cols]))
        vn_ref[:, cols] = conv_silu(v_ref[:, cols], cwv_ref[:, cols])
        for d in range(2):
            s_ref[hh, d] = s0_ref[d, hh] if has_state else jnp.zeros((HEAD_DIM, HEAD_DIM), F32)
        head = h0 + hh
        for d in range(2):
            beta_ref[hh, d] = pick(beta_all, d * A_HEADS + head)
            gc_ref[hh, d] = pick(gc_all, (2 + d) * A_HEADS + head)

    r64 = lax.broadcasted_iota(jnp.int32, (CHUNK, LANES), 0)
    l64 = lax.broadcasted_iota(jnp.int32, (CHUNK, LANES), 1)
    is_f = l64 < CHUNK
    is_f_row = lax.broadcasted_iota(jnp.int32, (1, LANES), 1) < CHUNK
    cpos = jnp.where(is_f, l64, l64 - CHUNK)
    ahead = jnp.where(is_f, r64 - cpos, cpos - r64)
    incl = ahead >= 0
    strict = ahead > 0
    eye = jnp.where(r64 == cpos, 1.0, 0.0)
    is_f2 = lax.broadcasted_iota(jnp.int32, (2 * CHUNK, LANES), 1) < CHUNK
    zeros_c = jnp.zeros((CHUNK, HEAD_DIM), F32)

    def aligned(off, m):
        return off if isinstance(off, int) else pl.multiple_of(off, m)

    def half(n):
        return n // 2 if isinstance(n, int) else lax.shift_right_logical(n, 1)

    def block_diag(xp):
        z = jnp.zeros_like(xp)
        return jnp.concatenate([jnp.where(is_f, xp, z), jnp.where(is_f, z, xp)], axis=0)

    def split_dot(a_hi, a_lo, b_hi, b_lo):
        lhs = jnp.concatenate([a_hi, a_lo, a_hi], axis=1)
        rhs = jnp.concatenate([b_hi, b_hi, b_lo], axis=0)
        return _dot(lhs, rhs)

    def prep_stages(blk, slot):
        mq_ref, c_ref, oo_ref, eg_ref = slot
        probs = [(si, hh) for si in range(steps) for hh in range(heads)]
        st = {p: {} for p in probs}

        def load(p):
            si, hh = p
            j = blk * steps + si
            n_b = nc - 1 - j
            cols = slice(hh * HEAD_DIM, (hh + 1) * HEAD_DIM)
            off_f = aligned(j * CHUNK, CHUNK)
            off_b = aligned(n_b * CHUNK, CHUNK)
            q_f, k_f, v_f = (r[pl.ds(off_f, CHUNK), cols] for r in (qn_ref, kn_ref, vn_ref))
            q_b, k_b, v_b = (r[pl.ds(off_b, CHUNK), cols] for r in (qn_ref, kn_ref, vn_ref))
            beta_f = beta_ref[hh, 0, pl.ds(off_f, CHUNK), :]
            beta_b = beta_ref[hh, 1, pl.ds(off_b, CHUNK), :]
            gc_f = gc_ref[hh, 0, pl.ds(off_f, CHUNK), :]
            gc_b = gc_ref[hh, 1, pl.ds(off_b, CHUNK), :]
            row_f = gcrow_ref[si % 2, half(j), pl.ds(h0 + hh, 1), :]
            row_b = gcrow_ref[si % 2, half(n_b), pl.ds(A_HEADS + h0 + hh, 1), :]
            gcrow = jnp.where(is_f_row, row_f, row_b)
            diff = jnp.where(is_f, gc_f, gc_b) - gcrow
            decay = jnp.where(incl, jnp.exp(jnp.where(incl, diff, 0.0)), 0.0)
            kb_f = k_f * beta_f
            kb_b = k_b * beta_b
            e_f = jnp.exp(gc_f)
            e_b = jnp.exp(gc_b)
            glast_f = gc_f[CHUNK - 1:CHUNK, :]
            glast_b = gc_b[0:1, :]
            kdt = jnp.transpose(jnp.concatenate([k_f * jnp.exp(glast_f - gc_f), k_b * jnp.exp(glast_b - gc_b)],
                                                axis=0))
            zk = jnp.zeros_like(kdt)
            eg_ref[si, hh, 0] = jnp.broadcast_to(jnp.exp(glast_f), (8, LANES))
            eg_ref[si, hh, 1] = jnp.broadcast_to(jnp.exp(glast_b), (8, LANES))
            st[p].update(
                decay=decay,
                lhs=jnp.concatenate([jnp.concatenate([kb_f, kb_b], axis=1),
                                     jnp.concatenate([q_f, q_b], axis=1)], axis=0).astype(BF16),
                rhs=jnp.concatenate([jnp.concatenate([k_f, zeros_c], axis=1),
                                     jnp.concatenate([zeros_c, k_b], axis=1)], axis=0).astype(BF16),
                rhs_uw=jnp.concatenate([jnp.concatenate([v_f * beta_f, kb_f * e_f], axis=1),
                                        jnp.concatenate([v_b * beta_b, kb_b * e_b], axis=1)], axis=0).astype(BF16),
                kd2=jnp.concatenate([jnp.where(is_f2, kdt, zk), jnp.where(is_f2, zk, kdt)], axis=0).astype(BF16),
                qe=(q_f * e_f, q_b * e_b))

        def gram(p):
            s = st[p]
            aq = _dot_nt(s.pop("lhs"), s.pop("rhs"))
            decay = s.pop("decay")
            x = -jnp.where(strict, aq[:CHUNK] * decay, 0.0)
            s["qk"] = jnp.where(incl, aq[CHUNK:] * decay, 0.0)
            s["t"] = eye + x
            s["p"] = x

        def square(p):
            s = st[p]
            p_hi, p_lo = _split_bf16(s["p"])
            s["p"] = split_dot(p_hi, p_lo, block_diag(p_hi), block_diag(p_lo))

        def double(p):
            s = st[p]
            p_hi, p_lo = _split_bf16(s["p"])
            t_hi, t_lo = _split_bf16(s["t"])
            r = split_dot(p_hi, p_lo, jnp.concatenate([block_diag(t_hi), block_diag(p_hi)], axis=1),
                          jnp.concatenate([block_diag(t_lo), block_diag(p_lo)], axis=1))
            s["t"] = s["t"] + r[:, :LANES]
            s["p"] = r[:, LANES:]

        def last(p):
            s = st[p]
            p_hi, p_lo = _split_bf16(s.pop("p"))
            t_hi, t_lo = _split_bf16(s["t"])
            s["t"] = s["t"] + split_dot(p_hi, p_lo, block_diag(t_hi), block_diag(t_lo))

        def solve(p):
            s = st[p]
            s["uw"] = _dot(block_diag(s.pop("t")).astype(BF16), s.pop("rhs_uw"))

        def fold(p):
            si, hh = p
            s = st[p]
            lhs3 = jnp.concatenate([s.pop("kd2"), block_diag(s.pop("qk")).astype(BF16)], axis=0)
            big = _dot(lhs3, s.pop("uw").astype(BF16))
            qe = s.pop("qe")
            for d in range(2):
                rows_s = slice(d * 2 * CHUNK, (d + 1) * 2 * CHUNK)
                rows_o = slice(4 * CHUNK + d * CHUNK, 4 * CHUNK + (d + 1) * CHUNK)
                c_ref[si, hh, d] = big[rows_s, :HEAD_DIM]
                oo_ref[si, hh, d] = big[rows_o, :HEAD_DIM]
                mq_ref[si, hh, d] = jnp.concatenate([big[rows_s, HEAD_DIM:], qe[d] - big[rows_o, HEAD_DIM:]],
                                                    axis=0).astype(BF16)

        n_double = 0
        span = 2
        while 2 * span < CHUNK:
            n_double += 1
            span *= 2
        stage_fns = [load, gram, square] + [double] * n_double + [last, solve, fold]
        return [functools.partial(lambda fn: [fn(p) for p in probs], fn) for fn in stage_fns]

    def scan_step(blk, slot, si):
        mq_ref, c_ref, oo_ref, eg_ref = slot
        j = blk * steps + si
        for hh in range(heads):
            cols = slice(hh * HEAD_DIM, (hh + 1) * HEAD_DIM)
            for d, n in ((0, j), (1, nc - 1 - j)):
                s = s_ref[hh, d]
                r = _dot(mq_ref[si, hh, d], s.astype(BF16))
                s_ref[hh, d] = s * eg_ref[si, hh, d][0:1, :] - r[:2 * CHUNK] + c_ref[si, hh, d]
                oacc_ref[d, pl.ds(aligned(n * CHUNK, CHUNK), CHUNK), cols] = r[2 * CHUNK:] + oo_ref[si, hh, d]

    def run(prep, scan):
        stages = prep_stages(*prep) if prep else []
        n_scan = steps if scan else 0
        for k in range(max(len(stages), n_scan)):
            if k < len(stages):
                stages[k]()
            if k < n_scan:
                scan_step(scan[0], scan[1], k)

    slot_a = (mq_a, c_a, oo_a, eg_a)
    slot_b = (mq_b, c_b, oo_b, eg_b)
    run((0, slot_a), None)
    if nblk == 1:
        run(None, (0, slot_a))
    else:
        def body(i, carry):
            blk = 2 * i
            run((blk + 1, slot_b), (blk, slot_a))
            run((blk + 2, slot_a), (blk + 1, slot_b))
            return carry

        lax.fori_loop(0, nblk // 2 - 1, body, 0)
        run((nblk - 1, slot_b), (nblk - 2, slot_a))
        run(None, (nblk - 1, slot_b))

    for hh in range(heads):
        cols = slice(hh * HEAD_DIM, (hh + 1) * HEAD_DIM)
        sout_ref[0, hh] = s_ref[hh, 0]
        sout_ref[1, hh] = s_ref[hh, 1]
        o = oacc_ref[0, :, cols] + oacc_ref[1, :, cols]
        o = _rms_rows(o, ng_ref[...]) * _silu(gate_ref[:, cols])
        o_ref[:, cols] = o.astype(o_ref.dtype)


def _gates_kernel(bg_ref, arow_ref, dtrow_ref, beta_ref, gc_ref):
    rows = bg_ref.shape[0]
    bg = bg_ref[...]
    beta_ref[...] = _sigmoid(bg)
    g = -jnp.exp(arow_ref[...]) * _softplus(bg + dtrow_ref[...])
    chunk_row = lax.broadcasted_iota(jnp.int32, (rows, LANES), 0) % CHUNK
    lane = lax.broadcasted_iota(jnp.int32, (rows, LANES), 1)
    pre, suf = g, g
    shift = 1
    while shift < CHUNK:
        pre = pre + jnp.where(chunk_row >= shift, pltpu.roll(pre, shift, 0), 0.0)
        suf = suf + jnp.where(chunk_row < CHUNK - shift, pltpu.roll(suf, rows - shift, 0), 0.0)
        shift *= 2
    gc_ref[...] = jnp.where(lane >= 3 * A_HEADS, suf, pre)


def _gates(proj, arow, dtrow, tm=1024):
    n_tok = proj.shape[0]
    assert tm % CHUNK == 0
    spec = pl.BlockSpec((tm, LANES), lambda i: (i, 0))
    return pl.pallas_call(
        _gates_kernel,
        out_shape=(jax.ShapeDtypeStruct((n_tok, LANES), F32), jax.ShapeDtypeStruct((n_tok, LANES), F32)),
        grid=(n_tok // tm,),
        in_specs=[pl.BlockSpec((tm, LANES), lambda i: (i, EV_BG_OFF // LANES)),
                  pl.BlockSpec((1, LANES), lambda i: (0, 0)), pl.BlockSpec((1, LANES), lambda i: (0, 0))],
        out_specs=(spec, spec),
        compiler_params=pltpu.CompilerParams(dimension_semantics=("arbitrary",)),
        name="gates",
    )(proj, arow, dtrow)


def _gdn(proj, beta_all, gc_all, conv_w, norm_g, row_block0, n_seq, seq_len, heads, steps, s0=None, s0_layer=0):
    L = seq_len
    nblk = L // CHUNK // steps
    assert L % (CHUNK * steps) == 0 and steps % 2 == 0 and 2 * CHUNK == LANES and A_HEADS % heads == 0
    assert nblk == 1 or nblk % 2 == 0
    width = heads * HEAD_DIM
    groups = A_HEADS // heads
    blk = lambda base: pl.BlockSpec((L, width), lambda b, h: (row_block0 + b, base * groups + h))
    wblk = lambda base: pl.BlockSpec((3, width), lambda b, h: (0, base * groups + h))
    state_spec = pl.BlockSpec((None, 2, heads, HEAD_DIM, HEAD_DIM), lambda b, h: (b, 0, h, 0, 0))
    slot = [pltpu.VMEM((steps, heads, 2, 3 * CHUNK, HEAD_DIM), BF16),
            pltpu.VMEM((steps, heads, 2, 2 * CHUNK, HEAD_DIM), F32),
            pltpu.VMEM((steps, heads, 2, CHUNK, HEAD_DIM), F32),
            pltpu.VMEM((steps, heads, 2, 8, LANES), F32)]
    kern = functools.partial(_gdn_kernel, seq_len=L, heads=heads, steps=steps, has_state=s0 is not None)
    state_in, state_arg = [], []
    if s0 is not None:
        state_in = [pl.BlockSpec((None, None, 2, heads, HEAD_DIM, HEAD_DIM), lambda b, h: (b, s0_layer, 0, h, 0, 0))]
        state_arg = [s0]
    out, s_out = pl.pallas_call(
        kern,
        out_shape=(jax.ShapeDtypeStruct((n_seq * L, A_WIDTH), BF16),
                   jax.ShapeDtypeStruct((n_seq, 2, A_HEADS, HEAD_DIM, HEAD_DIM), F32)),
        grid=(n_seq, groups),
        in_specs=[
            blk(0), blk(1), blk(2), blk(3),
            pl.BlockSpec((L, LANES), lambda b, h: (row_block0 + b, 0)),
            pl.BlockSpec((L, LANES), lambda b, h: (row_block0 + b, 0)),
            wblk(0), wblk(1), wblk(2),
            pl.BlockSpec((1, HEAD_DIM), lambda b, h: (0, 0)),
        ] + state_in,
        out_specs=(pl.BlockSpec((L, width), lambda b, h: (b, h)), state_spec),
        scratch_shapes=[
            pltpu.VMEM((L, width), F32), pltpu.VMEM((L, width), F32), pltpu.VMEM((L, width), F32),
            pltpu.VMEM((heads, 2, L, LANES), F32), pltpu.VMEM((heads, 2, L, LANES), F32),
            pltpu.VMEM((2, L // LANES, 2 * A_HEADS, LANES), F32),
            pltpu.VMEM((heads, 2, HEAD_DIM, HEAD_DIM), F32),
            pltpu.VMEM((2, L, width), F32),
        ] + slot + slot,
        compiler_params=pltpu.CompilerParams(dimension_semantics=("arbitrary", "arbitrary"),
                                             vmem_limit_bytes=VMEM_LIMIT_BYTES),
        name="gdn",
    )(proj, proj, proj, proj, beta_all, gc_all, conv_w, conv_w, conv_w, norm_g.reshape(1, HEAD_DIM), *state_arg)
    return out, s_out


def _rope_rows(x, cos, sin_signed):
    lane = lax.broadcasted_iota(jnp.int32, x.shape, 1)
    swapped = jnp.where((lane % (2 * ROT_FREQS)) < ROT_FREQS,
                        pltpu.roll(x, HEAD_DIM - ROT_FREQS, 1), pltpu.roll(x, ROT_FREQS, 1))
    return x * cos + swapped * sin_signed


def _attn_kernel(*refs, tq, seq_len, group, n_kv, n_ctx, use_sink, window, qk_norm, rope, emit_k):
    it = iter(refs)
    q_ref, k_ref, v_ref = next(it), next(it), next(it)
    ck_ref = cv_ref = cos_ref = sin_ref = qg_ref = kg_ref = sink_ref = kn_out_ref = None
    if n_ctx:
        ck_ref, cv_ref = next(it), next(it)
    if rope:
        cos_ref, sin_ref = next(it), next(it)
    if qk_norm:
        qg_ref, kg_ref = next(it), next(it)
    if use_sink:
        sink_ref = next(it)
    o_ref = next(it)
    if emit_k:
        kn_out_ref = next(it)
    kbf_ref, vbf_ref = next(it), next(it)
    ckbf_ref = cvbf_ref = None
    if n_ctx:
        ckbf_ref, cvbf_ref = next(it), next(it)

    L = seq_len
    kvh = pl.program_id(1)
    qi = pl.program_id(2)

    def ones_column(rows):
        lane = lax.broadcasted_iota(jnp.int32, (rows, HEAD_DIM), 1)
        return jnp.where(lane == 0, 1.0, 0.0).astype(BF16)

    @pl.when(qi == 0)
    def _():
        k = k_ref[...]
        if qk_norm:
            k = _rms_rows(k, kg_ref[...])
        if emit_k:
            for hd in range(n_kv):
                @pl.when(kvh == hd)
                def _():
                    kn_out_ref[pl.ds(hd, L, stride=n_kv), :] = k
        if rope:
            k = _rope_rows(k, cos_ref[...], sin_ref[...])
        kbf_ref[...] = k.astype(BF16)
        vbf_ref[:, :HEAD_DIM] = v_ref[...].astype(BF16)
        vbf_ref[:, HEAD_DIM:] = ones_column(L)
        if n_ctx:
            ckbf_ref[...] = ck_ref[...].astype(BF16)
            cvbf_ref[:, :HEAD_DIM] = cv_ref[...].astype(BF16)
            cvbf_ref[:, HEAD_DIM:] = ones_column(n_ctx)

    q0 = pl.multiple_of(qi * tq, tq)
    if window:
        wk = tq + 2 * window
        ws = pl.multiple_of(jnp.clip(q0 - window, 0, L - wk), LANES)
        k_loc = kbf_ref[pl.ds(ws, wk), :]
        v_loc = vbf_ref[pl.ds(ws, wk), :]
        qpos = q0 + lax.broadcasted_iota(jnp.int32, (tq, wk), 0)
        kpos = ws + lax.broadcasted_iota(jnp.int32, (tq, wk), 1)
        valid = jnp.abs(qpos - kpos) <= window
    else:
        k_loc = kbf_ref[...]
        v_loc = vbf_ref[...]

    def scores(g):
        q = q_ref[:, g * HEAD_DIM:(g + 1) * HEAD_DIM]
        if qk_norm:
            q = _rms_rows(q, qg_ref[...])
        if rope:
            q = _rope_rows(q, cos_ref[pl.ds(q0, tq), :], sin_ref[pl.ds(q0, tq), :])
        q = (q * (HEAD_DIM ** -0.5 * LOG2_E)).astype(BF16)
        s_loc = _dot_nt(q, k_loc)
        if window:
            s_loc = jnp.where(valid, s_loc, NEG)
        pieces = [(s_loc, v_loc)]
        if n_ctx:
            pieces.append((_dot_nt(q, ckbf_ref[...]), cvbf_ref[...]))
        return pieces

    def finish(g, pieces):
        m = None
        for s, _ in pieces:
            row_max = jnp.max(s, axis=-1, keepdims=True)
            m = row_max if m is None else jnp.maximum(m, row_max)
        if use_sink:
            sink = sink_ref[kvh * group + g] * LOG2_E
            m = jnp.maximum(m, sink)
        acc = None
        for s, v in pieces:
            term = _dot(jnp.exp2(s - m).astype(BF16), v)
            acc = term if acc is None else acc + term
        denom = acc[:, HEAD_DIM:HEAD_DIM + 1]
        if use_sink:
            denom = denom + jnp.exp2(sink - m)
        o_ref[:, g * HEAD_DIM:(g + 1) * HEAD_DIM] = (acc[:, :HEAD_DIM] / denom).astype(o_ref.dtype)

    ahead = scores(0)
    for g in range(group):
        cur = ahead
        if g + 1 < group:
            ahead = scores(g + 1)
        finish(g, cur)


def _attention(proj, *, q_off, k_off, v_off, n_heads, n_kv, row_block0, n_seq, seq_len, tq,
               ctx=None, ctx_layer=0, cos=None, sin=None, q_gain=None, k_gain=None, sink=None,
               window=0, emit_k=False):
    L = seq_len
    group = n_heads // n_kv
    n_ctx = 0 if ctx is None else ctx.shape[3]
    rope = cos is not None
    qk_norm = q_gain is not None
    use_sink = sink is not None
    nq = L // tq
    qb, kb, vb = q_off // (group * HEAD_DIM), k_off // HEAD_DIM, v_off // HEAD_DIM
    in_specs = [
        pl.BlockSpec((tq, group * HEAD_DIM), lambda b, h, i: ((row_block0 + b) * nq + i, qb + h)),
        pl.BlockSpec((L, HEAD_DIM), lambda b, h, i: (row_block0 + b, kb + h)),
        pl.BlockSpec((L, HEAD_DIM), lambda b, h, i: (row_block0 + b, vb + h)),
    ]
    args = [proj, proj, proj]
    scratch = [pltpu.VMEM((L, HEAD_DIM), BF16), pltpu.VMEM((L, 2 * HEAD_DIM), BF16)]
    if n_ctx:
        in_specs += [pl.BlockSpec((None, None, None, n_ctx, HEAD_DIM), lambda b, h, i: (b, ctx_layer, 0, 0, h)),
                     pl.BlockSpec((None, None, None, n_ctx, HEAD_DIM), lambda b, h, i: (b, ctx_layer, 1, 0, h))]
        args += [ctx, ctx]
        scratch += [pltpu.VMEM((n_ctx, HEAD_DIM), BF16), pltpu.VMEM((n_ctx, 2 * HEAD_DIM), BF16)]
    if rope:
        in_specs += [pl.BlockSpec((L, HEAD_DIM), lambda b, h, i: (0, 0))] * 2
        args += [cos, sin]
    if qk_norm:
        in_specs += [pl.BlockSpec((1, HEAD_DIM), lambda b, h, i: (0, 0))] * 2
        args += [q_gain.reshape(1, HEAD_DIM), k_gain.reshape(1, HEAD_DIM)]
    if use_sink:
        in_specs += [pl.BlockSpec(memory_space=pltpu.SMEM)]
        args += [sink]
    out_shape = [jax.ShapeDtypeStruct((n_seq * L, n_heads * HEAD_DIM), BF16)]
    out_specs = [pl.BlockSpec((tq, group * HEAD_DIM), lambda b, h, i: (b * nq + i, h))]
    if emit_k:
        out_shape += [jax.ShapeDtypeStruct((n_seq * L * n_kv, HEAD_DIM), F32)]
        out_specs += [pl.BlockSpec((L * n_kv, HEAD_DIM), lambda b, h, i: (b, 0))]
    kern = functools.partial(_attn_kernel, tq=tq, seq_len=L, group=group, n_kv=n_kv, n_ctx=n_ctx,
                             use_sink=use_sink, window=window, qk_norm=qk_norm, rope=rope, emit_k=emit_k)
    res = pl.pallas_call(
        kern,
        out_shape=tuple(out_shape),
        grid=(n_seq, n_kv, nq),
        in_specs=in_specs,
        out_specs=tuple(out_specs),
        scratch_shapes=scratch,
        compiler_params=pltpu.CompilerParams(dimension_semantics=("arbitrary", "arbitrary", "arbitrary"),
                                             vmem_limit_bytes=VMEM_LIMIT_BYTES),
        name="attention",
    )(*args)
    return res if emit_k else res[0]


def _out_mlp_kernel(*refs, n_x, n_parts, final, n_prompt_tiles, tf):
    it = iter(refs)
    x_refs = [next(it) for _ in range(n_x)]
    a_refs = [[next(it), next(it)] for _ in range(n_parts)]
    wo_refs = [next(it) for _ in range(n_parts)]
    mod_ref, g2_ref, w1_ref, w2_ref = next(it), next(it), next(it), next(it)
    fg_ref = next(it) if final else None
    o_refs = [next(it) for _ in range(2 if final else 1)]
    i = pl.program_id(0)
    d_ff = w1_ref.shape[1]

    mix = None
    for pair, wo_ref in zip(a_refs, wo_refs):
        term = _dot(_token_tile(pair, n_prompt_tiles), wo_ref[...])
        mix = term if mix is None else mix + term
    x1 = _token_tile(x_refs, n_prompt_tiles) + mod_ref[2:3, :] * mix
    h2 = (_rms_rows(x1, g2_ref[...]) * (1.0 + mod_ref[4:5, :]) + mod_ref[3:4, :]).astype(BF16)
    acc = None
    for c in range(d_ff // tf):
        hid = jnp.maximum(_dot(h2, w1_ref[:, c * tf:(c + 1) * tf]), 0.0)
        term = _dot((hid * hid).astype(BF16), w2_ref[c * tf:(c + 1) * tf, :])
        acc = term if acc is None else acc + term
    y = x1 + mod_ref[5:6, :] * acc

    if final:
        y = _rms_rows(y, fg_ref[...])

        @pl.when(i < n_prompt_tiles)
        def _():
            o_refs[0][...] = y

        @pl.when(i >= n_prompt_tiles)
        def _():
            o_refs[1][...] = y
    else:
        o_refs[0][...] = y


def _out_mlp(xs, parts, w_out, mod_l, g2, w1, w2, n_prompt, sample_len, final_g=None, tm=512, tf=1024):
    n_tok = sum(a.shape[0] for a in xs)
    d = xs[0].shape[1]
    d_ff = w1.shape[1]
    final = final_g is not None
    npt = n_prompt // tm
    row = functools.partial(_cond_row, n_prompt_tiles=npt, tiles_per_sample=sample_len // tm)
    in_specs = _token_specs(xs, tm, npt)
    for pair in parts:
        in_specs += _token_specs(pair, tm, npt)
    resident = lambda shape: pl.BlockSpec(shape, lambda i: (0,) * len(shape), pipeline_mode=pl.Buffered(1))
    width = w_out.shape[0] // len(parts)
    in_specs += [pl.BlockSpec((width, d), lambda i, p=p: (p, 0), pipeline_mode=pl.Buffered(1))
                 for p in range(len(parts))]
    in_specs += [
        pl.BlockSpec((None, 6, d), lambda i: (row(i), 0, 0)),
        pl.BlockSpec((1, d), lambda i: (0, 0)),
        resident((d, d_ff)),
        resident((d_ff, d)),
    ]
    args = [*xs, *[a for pair in parts for a in pair], *([w_out] * len(parts)), mod_l, g2.reshape(1, d), w1, w2]
    if final:
        in_specs += [pl.BlockSpec((1, d), lambda i: (0, 0))]
        args += [final_g.reshape(1, d)]
        out_shape = (jax.ShapeDtypeStruct((n_prompt, d), F32), jax.ShapeDtypeStruct((n_tok - n_prompt, d), F32))
        out_specs = (pl.BlockSpec((tm, d), lambda i: (jnp.minimum(i, npt - 1), 0)),
                     pl.BlockSpec((tm, d), lambda i: (jnp.maximum(i - npt, 0), 0)))
    else:
        out_shape = jax.ShapeDtypeStruct((n_tok, d), F32)
        out_specs = pl.BlockSpec((tm, d), lambda i: (i, 0))
    kern = functools.partial(_out_mlp_kernel, n_x=len(xs), n_parts=len(parts), final=final, n_prompt_tiles=npt,
                             tf=tf)
    return pl.pallas_call(
        kern,
        out_shape=out_shape,
        grid=(n_tok // tm,),
        in_specs=in_specs,
        out_specs=out_specs,
        compiler_params=pltpu.CompilerParams(dimension_semantics=("arbitrary",),
                                             vmem_limit_bytes=VMEM_LIMIT_BYTES),
        name="out_mlp",
    )(*args)


def _rope_tables(seq_len):
    rows = seq_len // GRID_W
    inv = ROPE_THETA ** (-jnp.arange(ROT_FREQS, dtype=F32) / ROT_FREQS)
    ang_r = jnp.repeat(jnp.arange(rows, dtype=F32), GRID_W)[:, None] * inv
    ang_c = jnp.tile(jnp.arange(GRID_W, dtype=F32), rows)[:, None] * inv
    cos = jnp.concatenate([jnp.cos(ang_r), jnp.cos(ang_r), jnp.cos(ang_c), jnp.cos(ang_c)], axis=-1)
    sin = jnp.concatenate([-jnp.sin(ang_r), jnp.sin(ang_r), -jnp.sin(ang_c), jnp.sin(ang_c)], axis=-1)
    return cos.astype(F32), sin.astype(F32)


def kernel(x_prompt, x_sample, state_a, cache_b_kv, cache_c_kv, c, c_ctx, ada_w, ada_b, norm1_g, norm2_g,
           final_g, mlp_w1, mlp_w2, ev_w_in, a_conv, a_log, a_dt_bias, a_norm_g, b_sink, ev_w_out,
           od_w_in, c_qnorm_g, c_knorm_g, od_w_out):
    batch, seq, d = x_prompt.shape
    dec_batch, dec_seq, _ = x_sample.shape
    depth = ada_w.shape[0]
    n_prompt, n_sample = batch * seq, dec_batch * dec_seq
    past = cache_b_kv.shape[3]

    xs = [x_prompt.reshape(n_prompt, d), x_sample.reshape(n_sample, d)]
    cond = jnp.concatenate([c_ctx[None, :], c, jnp.zeros((8 - 1 - dec_batch, d), F32)], axis=0)
    mod = _adaln(cond, ada_w, ada_b).reshape(depth, 8, 6, d)

    cos, sin = _rope_tables(dec_seq)
    s_blk0 = n_prompt // dec_seq
    ctx_b = cache_b_kv.reshape(*cache_b_kv.shape[:4], B_KV_HEADS * HEAD_DIM)
    ctx_c = cache_c_kv.reshape(*cache_c_kv.shape[:4], C_KV_HEADS * HEAD_DIM)
    ev_w_in_bf16 = _ev_w_in(ev_w_in)
    lane_pad = lambda v: jnp.zeros((1, LANES), F32).at[0, 2 * A_HEADS:4 * A_HEADS].set(v.reshape(-1))

    new_a, new_b, new_c = [], [], []
    for l in range(depth):
        i = l // 2
        fin = final_g if l == depth - 1 else None
        w1 = mlp_w1[l].astype(BF16)
        w2 = mlp_w2[l].astype(BF16)
        if l % 2 == 0:
            kv_cols = B_KV_HEADS * HEAD_DIM
            proj, (cache_k, cache_v) = _in_proj(xs, mod[l], norm1_g[l], ev_w_in_bf16[i], n_prompt, dec_seq,
                                                cache_cols=(EV_KVB_OFF, EV_KVB_OFF + kv_cols), n_kv=B_KV_HEADS)
            beta_all, gc_all = _gates(proj, lane_pad(a_log[i]), lane_pad(a_dt_bias[i]))
            oa_p, st_p = _gdn(proj, beta_all, gc_all, a_conv[i], a_norm_g[i], 0, batch, seq, heads=A_HEADS,
                              steps=seq // CHUNK)
            oa_s, _ = _gdn(proj, beta_all, gc_all, a_conv[i], a_norm_g[i], s_blk0, dec_batch, dec_seq, heads=1,
                           steps=8, s0=state_a, s0_layer=i)
            ob_p = _attention(proj, q_off=EV_QB_OFF, k_off=EV_KVB_OFF, v_off=EV_KVB_OFF + kv_cols,
                              n_heads=B_HEADS, n_kv=B_KV_HEADS, row_block0=0, n_seq=batch, seq_len=seq,
                              tq=seq, sink=b_sink[i])
            ob_s = _attention(proj, q_off=EV_QB_OFF, k_off=EV_KVB_OFF, v_off=EV_KVB_OFF + kv_cols,
                              n_heads=B_HEADS, n_kv=B_KV_HEADS, row_block0=s_blk0, n_seq=dec_batch,
                              seq_len=dec_seq, tq=256,
                              ctx=ctx_b, ctx_layer=i, cos=cos, sin=sin, sink=b_sink[i], window=WINDOW)
            new_a.append(st_p)
            new_b.append(jnp.stack([cache_k.reshape(batch, seq, B_KV_HEADS, HEAD_DIM),
                                    cache_v.reshape(batch, seq, B_KV_HEADS, HEAD_DIM)], axis=1))
            parts = [[oa_p, oa_s], [ob_p, ob_s]]
            w_out = ev_w_out[i].astype(BF16)
        else:
            kv_cols = C_KV_HEADS * HEAD_DIM
            proj, (cache_v,) = _in_proj(xs, mod[l], norm1_g[l], od_w_in[i].astype(BF16), n_prompt, dec_seq,
                                        cache_cols=(OD_V_OFF,), n_kv=C_KV_HEADS)
            oc_p, kn_p = _attention(proj, q_off=0, k_off=OD_K_OFF, v_off=OD_V_OFF, n_heads=C_HEADS,
                                    n_kv=C_KV_HEADS, row_block0=0, n_seq=batch, seq_len=seq, tq=seq,
                                    q_gain=c_qnorm_g[i], k_gain=c_knorm_g[i], emit_k=True)
            oc_s = _attention(proj, q_off=0, k_off=OD_K_OFF, v_off=OD_V_OFF, n_heads=C_HEADS,
                              n_kv=C_KV_HEADS, row_block0=s_blk0, n_seq=dec_batch, seq_len=dec_seq, tq=256,
                              ctx=ctx_c, ctx_layer=i, cos=cos, sin=sin, q_gain=c_qnorm_g[i],
                              k_gain=c_knorm_g[i])
            new_c.append(jnp.stack([kn_p.reshape(batch, seq, C_KV_HEADS, HEAD_DIM),
                                    cache_v.reshape(batch, seq, C_KV_HEADS, HEAD_DIM)], axis=1))
            parts = [[oc_p, oc_s]]
            w_out = od_w_out[i].astype(BF16)
        out = _out_mlp(xs, parts, w_out, mod[l], norm2_g[l], w1, w2, n_prompt, dec_seq, final_g=fin)
        xs = list(out) if fin is not None else [out]

    y_prompt = xs[0].reshape(batch, seq, d)
    y_sample = xs[1].reshape(dec_batch, dec_seq, d)
    return (y_prompt, y_sample, jnp.stack(new_a, axis=1), jnp.stack(new_b, axis=1), jnp.stack(new_c, axis=1))
```

```python
import functools

import numpy as np
import jax
import jax.numpy as jnp
from jax import lax
from jax.experimental import pallas as pl
from jax.experimental.pallas import tpu as pltpu

F32 = jnp.float32
BF16 = jnp.bfloat16

HEAD_DIM = 128
GRID_W = 64
A_HEADS = 4
CHUNK = 64
B_HEADS = 4
B_KV_HEADS = 2
WINDOW = 128
C_HEADS = 8
C_KV_HEADS = 2
ROT_FREQS = HEAD_DIM // 4
ROPE_THETA = 10000.0
EPS = 1e-6
NEG = -1e30
LOG2_E = 1.4426950408889634
A_WIDTH = A_HEADS * HEAD_DIM
B_WIDTH = B_HEADS * HEAD_DIM
C_WIDTH = C_HEADS * HEAD_DIM

LANES = 128
VMEM_LIMIT_BYTES = 56 * 1024 * 1024

EV_GATE_OFF = 3 * A_WIDTH
EV_QB_OFF = 4 * A_WIDTH
EV_KVB_OFF = EV_QB_OFF + B_WIDTH
EV_BG_OFF = EV_KVB_OFF + 2 * B_KV_HEADS * HEAD_DIM
EV_COLS = EV_BG_OFF + LANES
OD_K_OFF = C_WIDTH
OD_V_OFF = C_WIDTH + C_KV_HEADS * HEAD_DIM


def _sigmoid(x):
    return 1.0 / (1.0 + jnp.exp(-x))


def _silu(x):
    return x * _sigmoid(x)


def _softplus(x):
    return jnp.maximum(x, 0.0) + jnp.log1p(jnp.exp(-jnp.abs(x)))


def _rms_rows(x, g):
    return x * lax.rsqrt(jnp.mean(x * x, axis=-1, keepdims=True) + EPS) * g


def _dot(a, b):
    return jnp.dot(a, b, preferred_element_type=F32)


def _dot_nt(a, b):
    return lax.dot_general(a, b, (((1,), (1,)), ((), ())), preferred_element_type=F32)


def _cond_row(i, n_prompt_tiles, tiles_per_sample):
    return jnp.where(i < n_prompt_tiles, 0, 1 + (i - n_prompt_tiles) // tiles_per_sample)


def _split_bf16(x):
    hi = x.astype(BF16)
    lo = (x - hi.astype(F32)).astype(BF16)
    return hi, lo


def _adaln_kernel(cond_ref, w_ref, b_ref, o_ref):
    rows = cond_ref.shape[0]
    s_hi, s_lo = _split_bf16(_silu(cond_ref[...]))
    w_hi, w_lo = _split_bf16(w_ref[0])
    r = _dot(jnp.concatenate([s_hi, s_lo], axis=0), w_hi)
    o_ref[0] = r[:rows] + r[rows:] + _dot(s_hi, w_lo) + b_ref[0]


def _adaln(cond, ada_w, ada_b, tn=1024):
    depth, d, n = ada_w.shape
    rows = cond.shape[0]
    return pl.pallas_call(
        _adaln_kernel,
        out_shape=jax.ShapeDtypeStruct((depth, rows, n), F32),
        grid=(depth, n // tn),
        in_specs=[
            pl.BlockSpec((rows, d), lambda l, j: (0, 0)),
            pl.BlockSpec((1, d, tn), lambda l, j: (l, 0, j)),
            pl.BlockSpec((1, 1, tn), lambda l, j: (l, 0, j)),
        ],
        out_specs=pl.BlockSpec((1, rows, tn), lambda l, j: (l, 0, j)),
        compiler_params=pltpu.CompilerParams(dimension_semantics=("arbitrary", "arbitrary"),
                                             vmem_limit_bytes=VMEM_LIMIT_BYTES),
        name="adaln",
    )(cond, ada_w, ada_b.reshape(depth, 1, n))


def _ev_w_in_kernel(w_ref, o_ref):
    bg_off = 4 * A_WIDTH
    bg_end = bg_off + 4 * A_HEADS
    rows = w_ref.shape[0]
    o_ref[:, :bg_off] = w_ref[:, :bg_off].astype(BF16)
    o_ref[:, bg_off:EV_BG_OFF] = w_ref[:, bg_end:].astype(BF16)
    tail = jnp.concatenate([w_ref[:, bg_off:bg_end], jnp.zeros((rows, LANES - 4 * A_HEADS), F32)], axis=1)
    o_ref[:, EV_BG_OFF:] = tail.astype(BF16)


def _ev_w_in(w, tr=256):
    layers, d, n = w.shape
    return pl.pallas_call(
        _ev_w_in_kernel,
        out_shape=jax.ShapeDtypeStruct((layers, d, EV_COLS), BF16),
        grid=(layers, d // tr),
        in_specs=[pl.BlockSpec((None, tr, n), lambda l, r: (l, r, 0))],
        out_specs=pl.BlockSpec((None, tr, EV_COLS), lambda l, r: (l, r, 0)),
        compiler_params=pltpu.CompilerParams(dimension_semantics=("arbitrary", "arbitrary")),
        name="ev_w_in",
    )(w)


def _token_specs(arrays, tm, n_prompt_tiles):
    if len(arrays) == 1:
        return [pl.BlockSpec((tm, arrays[0].shape[1]), lambda i, *_: (i, 0))]
    return [pl.BlockSpec((tm, arrays[0].shape[1]), lambda i, *_: (jnp.minimum(i, n_prompt_tiles - 1), 0)),
            pl.BlockSpec((tm, arrays[1].shape[1]), lambda i, *_: (jnp.maximum(i - n_prompt_tiles, 0), 0))]


def _token_tile(refs, n_prompt_tiles):
    if len(refs) == 1:
        return refs[0][...]
    return jnp.where(pl.program_id(0) < n_prompt_tiles, refs[0][...], refs[1][...])


def _inproj_kernel(*refs, n_x, n_prompt_tiles, cache_cols, n_kv):
    x_refs = refs[:n_x]
    mod_ref, g_ref, w_ref, o_ref = refs[n_x:n_x + 4]
    cache_refs = refs[n_x + 4:]
    x = _token_tile(x_refs, n_prompt_tiles)
    h = _rms_rows(x, g_ref[...]) * (1.0 + mod_ref[1:2, :]) + mod_ref[0:1, :]
    o = _dot(h.astype(BF16), w_ref[...])
    o_ref[...] = o

    @pl.when(pl.program_id(0) < n_prompt_tiles)
    def _():
        tm = o.shape[0]
        for c_ref, col in zip(cache_refs, cache_cols):
            for hd in range(n_kv):
                c_ref[pl.ds(hd, tm, stride=n_kv), :] = o[:, col + hd * HEAD_DIM:col + (hd + 1) * HEAD_DIM]


def _in_proj(xs, mod, layer, g, w, w_layer, n_prompt, sample_len, cache_cols, n_kv, tm=512):
    n_tok = sum(a.shape[0] for a in xs)
    d = xs[0].shape[1]
    n = w.shape[2]
    npt = n_prompt // tm
    row = functools.partial(_cond_row, n_prompt_tiles=npt, tiles_per_sample=sample_len // tm)
    cache_spec = pl.BlockSpec((tm * n_kv, HEAD_DIM), lambda i: (jnp.minimum(i, npt - 1), 0))
    res = pl.pallas_call(
        functools.partial(_inproj_kernel, n_x=len(xs), n_prompt_tiles=npt, cache_cols=cache_cols, n_kv=n_kv),
        out_shape=(jax.ShapeDtypeStruct((n_tok, n), F32),
                   *[jax.ShapeDtypeStruct((n_prompt * n_kv, HEAD_DIM), F32) for _ in cache_cols]),
        grid=(n_tok // tm,),
        in_specs=_token_specs(xs, tm, npt) + [
            pl.BlockSpec((None, None, 6, d), lambda i: (layer, row(i), 0, 0)),
            pl.BlockSpec((None, 1, d), lambda i: (layer, 0, 0)),
            pl.BlockSpec((None, d, n), lambda i: (w_layer, 0, 0)),
        ],
        out_specs=(pl.BlockSpec((tm, n), lambda i: (i, 0)), *[cache_spec for _ in cache_cols]),
        compiler_params=pltpu.CompilerParams(dimension_semantics=("arbitrary",),
                                             vmem_limit_bytes=VMEM_LIMIT_BYTES),
        name="in_proj",
    )(*xs, mod, g, w)
    return res[0], res[1:]


def _gdn_kernel(q_ref, k_ref, v_ref, gate_ref, betaall_ref, gcall_ref, cwq_ref, cwk_ref, cwv_ref, ng_ref, *rest,
                seq_len, heads, steps, has_state):
    s0_ref = rest[0] if has_state else None
    (o_ref, sout_ref, qn_ref, kn_ref, vn_ref, beta_ref, gc_ref, gcrow_ref, s_ref, oacc_ref,
     mq_a, c_a, oo_a, eg_a, mq_b, c_b, oo_b, eg_b) = rest[1:] if has_state else rest
    L = seq_len
    nc = L // CHUNK
    nblk = nc // steps
    h0 = pl.program_id(1) * heads
    row_idx = lax.broadcasted_iota(jnp.int32, (L, LANES), 0)
    lane_idx = lax.broadcasted_iota(jnp.int32, (L, LANES), 1)

    def conv_silu(x, cw):
        prev = jnp.where(row_idx == 0, 0.0, pltpu.roll(x, 1, 0))
        nxt = jnp.where(row_idx == L - 1, 0.0, pltpu.roll(x, L - 1, 0))
        return _silu(prev * cw[0:1, :] + x * cw[1:2, :] + nxt * cw[2:3, :])

    def l2norm(x):
        return x * lax.rsqrt(jnp.sum(x * x, axis=-1, keepdims=True) + EPS)

    beta_all = betaall_ref[...]
    gc_all = gcall_ref[...]

    def pick(x, col):
        col_vals = jnp.sum(jnp.where(lane_idx == col, x, 0.0), axis=1, keepdims=True)
        return jnp.broadcast_to(col_vals, (L, LANES))

    for r in range(L // LANES):
        rows = jnp.transpose(gc_all[r * LANES:(r + 1) * LANES, :])[2 * A_HEADS:4 * A_HEADS, :]
        gcrow_ref[0, r] = rows
        gcrow_ref[1, r] = pltpu.roll(rows, CHUNK, 1)

    for hh in range(heads):
        cols = slice(hh * HEAD_DIM, (hh + 1) * HEAD_DIM)
        qn_ref[:, cols] = l2norm(conv_silu(q_ref[:, cols], cwq_ref[:, cols])) * (HEAD_DIM ** -0.5)
        kn_ref[:, cols] = l2norm(conv_silu(k_ref[:, cols], cwk_ref[:, cols]))
        vn_ref[:, cols] = conv_silu(v_ref[:, cols], cwv_ref[:, cols])
        for d in range(2):
            s_ref[hh, d] = s0_ref[d, hh] if has_state else jnp.zeros((HEAD_DIM, HEAD_DIM), F32)
        head = h0 + hh
        for d in range(2):
            beta_ref[hh, d] = pick(beta_all, d * A_HEADS + head)
            gc_ref[hh, d] = pick(gc_all, (2 + d) * A_HEADS + head)

    r64 = lax.broadcasted_iota(jnp.int32, (CHUNK, LANES), 0)
    l64 = lax.broadcasted_iota(jnp.int32, (CHUNK, LANES), 1)
    is_f = l64 < CHUNK
    is_f_row = lax.broadcasted_iota(jnp.int32, (1, LANES), 1) < CHUNK
    cpos = jnp.where(is_f, l64, l64 - CHUNK)
    ahead = jnp.where(is_f, r64 - cpos, cpos - r64)
    incl = ahead >= 0
    strict = ahead > 0
    eye = jnp.where(r64 == cpos, 1.0, 0.0)
    is_f2 = lax.broadcasted_iota(jnp.int32, (2 * CHUNK, LANES), 1) < CHUNK
    zeros_c = jnp.zeros((CHUNK, HEAD_DIM), F32)

    def aligned(off, m):
        return off if isinstance(off, int) else pl.multiple_of(off, m)

    def half(n):
        return n // 2 if isinstance(n, int) else lax.shift_right_logical(n, 1)

    def block_diag(xp):
        z = jnp.zeros_like(xp)
        return jnp.concatenate([jnp.where(is_f, xp, z), jnp.where(is_f, z, xp)], axis=0)

    def split_dot(a_hi, a_lo, b_hi, b_lo):
        lhs = jnp.concatenate([a_hi, a_lo, a_hi], axis=1)
        rhs = jnp.concatenate([b_hi, b_hi, b_lo], axis=0)
        return _dot(lhs, rhs)

    def prep_stages(blk, slot):
        mq_ref, c_ref, oo_ref, eg_ref = slot
        probs = [(si, hh) for si in range(steps) for hh in range(heads)]
        st = {p: {} for p in probs}

        def load(p):
            si, hh = p
            j = blk * steps + si
            n_b = nc - 1 - j
            cols = slice(hh * HEAD_DIM, (hh + 1) * HEAD_DIM)
            off_f = aligned(j * CHUNK, CHUNK)
            off_b = aligned(n_b * CHUNK, CHUNK)
            q_f, k_f, v_f = (r[pl.ds(off_f, CHUNK), cols] for r in (qn_ref, kn_ref, vn_ref))
            q_b, k_b, v_b = (r[pl.ds(off_b, CHUNK), cols] for r in (qn_ref, kn_ref, vn_ref))
            beta_f = beta_ref[hh, 0, pl.ds(off_f, CHUNK), :]
            beta_b = beta_ref[hh, 1, pl.ds(off_b, CHUNK), :]
            gc_f = gc_ref[hh, 0, pl.ds(off_f, CHUNK), :]
            gc_b = gc_ref[hh, 1, pl.ds(off_b, CHUNK), :]
            row_f = gcrow_ref[si % 2, half(j), pl.ds(h0 + hh, 1), :]
            row_b = gcrow_ref[si % 2, half(n_b), pl.ds(A_HEADS + h0 + hh, 1), :]
            gcrow = jnp.where(is_f_row, row_f, row_b)
            diff = jnp.where(is_f, gc_f, gc_b) - gcrow
            decay = jnp.where(incl, jnp.exp(jnp.where(incl, diff, 0.0)), 0.0)
            kb_f = k_f * beta_f
            kb_b = k_b * beta_b
            e_f = jnp.exp(gc_f)
            e_b = jnp.exp(gc_b)
            glast_f = gc_f[CHUNK - 1:CHUNK, :]
            glast_b = gc_b[0:1, :]
            kdt = jnp.transpose(jnp.concatenate([k_f * jnp.exp(glast_f - gc_f), k_b * jnp.exp(glast_b - gc_b)],
                                                axis=0))
            zk = jnp.zeros_like(kdt)
            eg_ref[si, hh, 0] = jnp.broadcast_to(jnp.exp(glast_f), (8, LANES))
            eg_ref[si, hh, 1] = jnp.broadcast_to(jnp.exp(glast_b), (8, LANES))
            st[p].update(
                decay=decay,
                lhs=jnp.concatenate([jnp.concatenate([kb_f, kb_b], axis=1),
                                     jnp.concatenate([q_f, q_b], axis=1)], axis=0).astype(BF16),
                rhs=jnp.concatenate([jnp.concatenate([k_f, zeros_c], axis=1),
                                     jnp.concatenate([zeros_c, k_b], axis=1)], axis=0).astype(BF16),
                rhs_uw=jnp.concatenate([jnp.concatenate([v_f * beta_f, kb_f * e_f], axis=1),
                                        jnp.concatenate([v_b * beta_b, kb_b * e_b], axis=1)], axis=0).astype(BF16),
                kd2=jnp.concatenate([jnp.where(is_f2, kdt, zk), jnp.where(is_f2, zk, kdt)], axis=0).astype(BF16),
                qe=(q_f * e_f, q_b * e_b))

        def gram(p):
            s = st[p]
            aq = _dot_nt(s.pop("lhs"), s.pop("rhs"))
            decay = s.pop("decay")
            x = -jnp.where(strict, aq[:CHUNK] * decay, 0.0)
            s["qk"] = jnp.where(incl, aq[CHUNK:] * decay, 0.0)
            s["t"] = eye + x
            s["p"] = x

        def square(p):
            s = st[p]
            p_hi, p_lo = _split_bf16(s["p"])
            s["p"] = split_dot(p_hi, p_lo, block_diag(p_hi), block_diag(p_lo))

        def double(p):
            s = st[p]
            p_hi, p_lo = _split_bf16(s["p"])
            t_hi, t_lo = _split_bf16(s["t"])
            r = split_dot(p_hi, p_lo, jnp.concatenate([block_diag(t_hi), block_diag(p_hi)], axis=1),
                          jnp.concatenate([block_diag(t_lo), block_diag(p_lo)], axis=1))
            s["t"] = s["t"] + r[:, :LANES]
            s["p"] = r[:, LANES:]

        def last(p):
            s = st[p]
            p_hi, p_lo = _split_bf16(s.pop("p"))
            t_hi, t_lo = _split_bf16(s["t"])
            s["t"] = s["t"] + split_dot(p_hi, p_lo, block_diag(t_hi), block_diag(t_lo))

        def solve(p):
            s = st[p]
            s["uw"] = _dot(block_diag(s.pop("t")).astype(BF16), s.pop("rhs_uw"))

        def fold(p):
            si, hh = p
            s = st[p]
            lhs3 = jnp.concatenate([s.pop("kd2"), block_diag(s.pop("qk")).astype(BF16)], axis=0)
            big = _dot(lhs3, s.pop("uw").astype(BF16))
            qe = s.pop("qe")
            for d in range(2):
                rows_s = slice(d * 2 * CHUNK, (d + 1) * 2 * CHUNK)
                rows_o = slice(4 * CHUNK + d * CHUNK, 4 * CHUNK + (d + 1) * CHUNK)
                c_ref[si, hh, d] = big[rows_s, :HEAD_DIM]
                oo_ref[si, hh, d] = big[rows_o, :HEAD_DIM]
                mq_ref[si, hh, d] = jnp.concatenate([big[rows_s, HEAD_DIM:], qe[d] - big[rows_o, HEAD_DIM:]],
                                                    axis=0).astype(BF16)

        n_double = 0
        span = 2
        while 2 * span < CHUNK:
            n_double += 1
            span *= 2
        stage_fns = [load, gram, square] + [double] * n_double + [last, solve, fold]
        return [functools.partial(lambda fn: [fn(p) for p in probs], fn) for fn in stage_fns]

    def scan_step(blk, slot, si):
        mq_ref, c_ref, oo_ref, eg_ref = slot
        j = blk * steps + si
        for hh in range(heads):
            cols = slice(hh * HEAD_DIM, (hh + 1) * HEAD_DIM)
            for d, n in ((0, j), (1, nc - 1 - j)):
                s = s_ref[hh, d]
                r = _dot(mq_ref[si, hh, d], s.astype(BF16))
                s_ref[hh, d] = s * eg_ref[si, hh, d][0:1, :] - r[:2 * CHUNK] + c_ref[si, hh, d]
                oacc_ref[d, pl.ds(aligned(n * CHUNK, CHUNK), CHUNK), cols] = r[2 * CHUNK:] + oo_ref[si, hh, d]

    def run(prep, scan):
        stages = prep_stages(*prep) if prep else []
        n_scan = steps if scan else 0
        for k in range(max(len(stages), n_scan)):
            if k < len(stages):
                stages[k]()
            if k < n_scan:
                scan_step(scan[0], scan[1], k)

    slot_a = (mq_a, c_a, oo_a, eg_a)
    slot_b = (mq_b, c_b, oo_b, eg_b)
    run((0, slot_a), None)
    if nblk == 1:
        run(None, (0, slot_a))
    else:
        def body(i, carry):
            blk = 2 * i
            run((blk + 1, slot_b), (blk, slot_a))
            run((blk + 2, slot_a), (blk + 1, slot_b))
            return carry

        lax.fori_loop(0, nblk // 2 - 1, body, 0)
        run((nblk - 1, slot_b), (nblk - 2, slot_a))
        run(None, (nblk - 1, slot_b))

    for hh in range(heads):
        cols = slice(hh * HEAD_DIM, (hh + 1) * HEAD_DIM)
        sout_ref[0, hh] = s_ref[hh, 0]
        sout_ref[1, hh] = s_ref[hh, 1]
        o = oacc_ref[0, :, cols] + oacc_ref[1, :, cols]
        o = _rms_rows(o, ng_ref[...]) * _silu(gate_ref[:, cols])
        o_ref[:, cols] = o.astype(o_ref.dtype)


def _gates_kernel(bg_ref, arow_ref, dtrow_ref, beta_ref, gc_ref):
    rows = bg_ref.shape[0]
    bg = bg_ref[...]
    beta_ref[...] = _sigmoid(bg)
    g = -jnp.exp(arow_ref[...]) * _softplus(bg + dtrow_ref[...])
    chunk_row = lax.broadcasted_iota(jnp.int32, (rows, LANES), 0) % CHUNK
    lane = lax.broadcasted_iota(jnp.int32, (rows, LANES), 1)
    pre, suf = g, g
    shift = 1
    while shift < CHUNK:
        pre = pre + jnp.where(chunk_row >= shift, pltpu.roll(pre, shift, 0), 0.0)
        suf = suf + jnp.where(chunk_row < CHUNK - shift, pltpu.roll(suf, rows - shift, 0), 0.0)
        shift *= 2
    gc_ref[...] = jnp.where(lane >= 3 * A_HEADS, suf, pre)


def _gates(proj, arow, dtrow, tm=1024):
    n_tok = proj.shape[0]
    assert tm % CHUNK == 0
    spec = pl.BlockSpec((tm, LANES), lambda i: (i, 0))
    return pl.pallas_call(
        _gates_kernel,
        out_shape=(jax.ShapeDtypeStruct((n_tok, LANES), F32), jax.ShapeDtypeStruct((n_tok, LANES), F32)),
        grid=(n_tok // tm,),
        in_specs=[pl.BlockSpec((tm, LANES), lambda i: (i, EV_BG_OFF // LANES)),
                  pl.BlockSpec((1, LANES), lambda i: (0, 0)), pl.BlockSpec((1, LANES), lambda i: (0, 0))],
        out_specs=(spec, spec),
        compiler_params=pltpu.CompilerParams(dimension_semantics=("arbitrary",)),
        name="gates",
    )(proj, arow, dtrow)


def _gdn(proj, beta_all, gc_all, conv_w, norm_g, row_block0, n_seq, seq_len, heads, steps, s0=None, s0_layer=0):
    L = seq_len
    nblk = L // CHUNK // steps
    assert L % (CHUNK * steps) == 0 and steps % 2 == 0 and 2 * CHUNK == LANES and A_HEADS % heads == 0
    assert nblk == 1 or nblk % 2 == 0
    width = heads * HEAD_DIM
    groups = A_HEADS // heads
    blk = lambda base: pl.BlockSpec((L, width), lambda b, h: (row_block0 + b, base * groups + h))
    wblk = lambda base: pl.BlockSpec((3, width), lambda b, h: (0, base * groups + h))
    state_spec = pl.BlockSpec((None, 2, heads, HEAD_DIM, HEAD_DIM), lambda b, h: (b, 0, h, 0, 0))
    slot = [pltpu.VMEM((steps, heads, 2, 3 * CHUNK, HEAD_DIM), BF16),
            pltpu.VMEM((steps, heads, 2, 2 * CHUNK, HEAD_DIM), F32),
            pltpu.VMEM((steps, heads, 2, CHUNK, HEAD_DIM), F32),
            pltpu.VMEM((steps, heads, 2, 8, LANES), F32)]
    kern = functools.partial(_gdn_kernel, seq_len=L, heads=heads, steps=steps, has_state=s0 is not None)
    state_in, state_arg = [], []
    if s0 is not None:
        state_in = [pl.BlockSpec((None, None, 2, heads, HEAD_DIM, HEAD_DIM), lambda b, h: (b, s0_layer, 0, h, 0, 0))]
        state_arg = [s0]
    out, s_out = pl.pallas_call(
        kern,
        out_shape=(jax.ShapeDtypeStruct((n_seq * L, A_WIDTH), BF16),
                   jax.ShapeDtypeStruct((n_seq, 2, A_HEADS, HEAD_DIM, HEAD_DIM), F32)),
        grid=(n_seq, groups),
        in_specs=[
            blk(0), blk(1), blk(2), blk(3),
            pl.BlockSpec((L, LANES), lambda b, h: (row_block0 + b, 0)),
            pl.BlockSpec((L, LANES), lambda b, h: (row_block0 + b, 0)),
            wblk(0), wblk(1), wblk(2),
            pl.BlockSpec((1, HEAD_DIM), lambda b, h: (0, 0)),
        ] + state_in,
        out_specs=(pl.BlockSpec((L, width), lambda b, h: (b, h)), state_spec),
        scratch_shapes=[
            pltpu.VMEM((L, width), F32), pltpu.VMEM((L, width), F32), pltpu.VMEM((L, width), F32),
            pltpu.VMEM((heads, 2, L, LANES), F32), pltpu.VMEM((heads, 2, L, LANES), F32),
            pltpu.VMEM((2, L // LANES, 2 * A_HEADS, LANES), F32),
            pltpu.VMEM((heads, 2, HEAD_DIM, HEAD_DIM), F32),
            pltpu.VMEM((2, L, width), F32),
        ] + slot + slot,
        compiler_params=pltpu.CompilerParams(dimension_semantics=("arbitrary", "arbitrary"),
                                             vmem_limit_bytes=VMEM_LIMIT_BYTES),
        name="gdn",
    )(proj, proj, proj, proj, beta_all, gc_all, conv_w, conv_w, conv_w, norm_g.reshape(1, HEAD_DIM), *state_arg)
    return out, s_out


def _rope_rows(x, cos, sin_signed):
    lane = lax.broadcasted_iota(jnp.int32, x.shape, 1)
    swapped = jnp.where((lane % (2 * ROT_FREQS)) < ROT_FREQS,
                        pltpu.roll(x, HEAD_DIM - ROT_FREQS, 1), pltpu.roll(x, ROT_FREQS, 1))
    return x * cos + swapped * sin_signed


def _attn_kernel(*refs, tq, seq_len, group, n_kv, n_ctx, use_sink, window, qk_norm, rope, emit_k):
    it = iter(refs)
    q_ref, k_ref, v_ref = next(it), next(it), next(it)
    ck_ref = cv_ref = cos_ref = sin_ref = qg_ref = kg_ref = sink_ref = kn_out_ref = None
    if n_ctx:
        ck_ref, cv_ref = next(it), next(it)
    if rope:
        cos_ref, sin_ref = next(it), next(it)
    if qk_norm:
        qg_ref, kg_ref = next(it), next(it)
    if use_sink:
        sink_ref = next(it)
    o_ref = next(it)
    if emit_k:
        kn_out_ref = next(it)
    kbf_ref, vbf_ref = next(it), next(it)
    ckbf_ref = cvbf_ref = None
    if n_ctx:
        ckbf_ref, cvbf_ref = next(it), next(it)

    L = seq_len
    kvh = pl.program_id(1)
    qi = pl.program_id(2)

    def ones_column(rows):
        lane = lax.broadcasted_iota(jnp.int32, (rows, HEAD_DIM), 1)
        return jnp.where(lane == 0, 1.0, 0.0).astype(BF16)

    @pl.when(qi == 0)
    def _():
        k = k_ref[...]
        if qk_norm:
            k = _rms_rows(k, kg_ref[...])
        if emit_k:
            for hd in range(n_kv):
                @pl.when(kvh == hd)
                def _():
                    kn_out_ref[pl.ds(hd, L, stride=n_kv), :] = k
        if rope:
            k = _rope_rows(k, cos_ref[...], sin_ref[...])
        kbf_ref[...] = k.astype(BF16)
        vbf_ref[:, :HEAD_DIM] = v_ref[...].astype(BF16)
        vbf_ref[:, HEAD_DIM:] = ones_column(L)
        if n_ctx:
            ckbf_ref[...] = ck_ref[...].astype(BF16)
            cvbf_ref[:, :HEAD_DIM] = cv_ref[...].astype(BF16)
            cvbf_ref[:, HEAD_DIM:] = ones_column(n_ctx)

    q0 = pl.multiple_of(qi * tq, tq)
    if window:
        wk = tq + 2 * window
        ws = pl.multiple_of(jnp.clip(q0 - window, 0, L - wk), LANES)
        k_loc = kbf_ref[pl.ds(ws, wk), :]
        v_loc = vbf_ref[pl.ds(ws, wk), :]
        qpos = q0 + lax.broadcasted_iota(jnp.int32, (tq, wk), 0)
        kpos = ws + lax.broadcasted_iota(jnp.int32, (tq, wk), 1)
        valid = jnp.abs(qpos - kpos) <= window
    else:
        k_loc = kbf_ref[...]
        v_loc = vbf_ref[...]

    def scores(g):
        q = q_ref[:, g * HEAD_DIM:(g + 1) * HEAD_DIM]
        if qk_norm:
            q = _rms_rows(q, qg_ref[...])
        if rope:
            q = _rope_rows(q, cos_ref[pl.ds(q0, tq), :], sin_ref[pl.ds(q0, tq), :])
        q = (q * (HEAD_DIM ** -0.5 * LOG2_E)).astype(BF16)
        s_loc = _dot_nt(q, k_loc)
        if window:
            s_loc = jnp.where(valid, s_loc, NEG)
        pieces = [(s_loc, v_loc)]
        if n_ctx:
            pieces.append((_dot_nt(q, ckbf_ref[...]), cvbf_ref[...]))
        return pieces

    def finish(g, pieces):
        m = None
        for s, _ in pieces:
            row_max = jnp.max(s, axis=-1, keepdims=True)
            m = row_max if m is None else jnp.maximum(m, row_max)
        if use_sink:
            sink = sink_ref[kvh * group + g] * LOG2_E
            m = jnp.maximum(m, sink)
        acc = None
        for s, v in pieces:
            term = _dot(jnp.exp2(s - m).astype(BF16), v)
            acc = term if acc is None else acc + term
        denom = acc[:, HEAD_DIM:HEAD_DIM + 1]
        if use_sink:
            denom = denom + jnp.exp2(sink - m)
        o_ref[:, g * HEAD_DIM:(g + 1) * HEAD_DIM] = (acc[:, :HEAD_DIM] / denom).astype(o_ref.dtype)

    ahead = scores(0)
    for g in range(group):
        cur = ahead
        if g + 1 < group:
            ahead = scores(g + 1)
        finish(g, cur)


def _attention(proj, *, q_off, k_off, v_off, n_heads, n_kv, row_block0, n_seq, seq_len, tq,
               ctx=None, ctx_layer=0, cos=None, sin=None, q_gain=None, k_gain=None, sink=None,
               window=0, emit_k=False):
    L = seq_len
    group = n_heads // n_kv
    n_ctx = 0 if ctx is None else ctx.shape[3]
    rope = cos is not None
    qk_norm = q_gain is not None
    use_sink = sink is not None
    nq = L // tq
    qb, kb, vb = q_off // (group * HEAD_DIM), k_off // HEAD_DIM, v_off // HEAD_DIM
    in_specs = [
        pl.BlockSpec((tq, group * HEAD_DIM), lambda b, h, i: ((row_block0 + b) * nq + i, qb + h)),
        pl.BlockSpec((L, HEAD_DIM), lambda b, h, i: (row_block0 + b, kb + h)),
        pl.BlockSpec((L, HEAD_DIM), lambda b, h, i: (row_block0 + b, vb + h)),
    ]
    args = [proj, proj, proj]
    scratch = [pltpu.VMEM((L, HEAD_DIM), BF16), pltpu.VMEM((L, 2 * HEAD_DIM), BF16)]
    if n_ctx:
        in_specs += [pl.BlockSpec((None, None, None, n_ctx, HEAD_DIM), lambda b, h, i: (b, ctx_layer, 0, 0, h)),
                     pl.BlockSpec((None, None, None, n_ctx, HEAD_DIM), lambda b, h, i: (b, ctx_layer, 1, 0, h))]
        args += [ctx, ctx]
        scratch += [pltpu.VMEM((n_ctx, HEAD_DIM), BF16), pltpu.VMEM((n_ctx, 2 * HEAD_DIM), BF16)]
    if rope:
        in_specs += [pl.BlockSpec((L, HEAD_DIM), lambda b, h, i: (0, 0))] * 2
        args += [cos, sin]
    if qk_norm:
        in_specs += [pl.BlockSpec((1, HEAD_DIM), lambda b, h, i: (0, 0))] * 2
        args += [q_gain.reshape(1, HEAD_DIM), k_gain.reshape(1, HEAD_DIM)]
    if use_sink:
        in_specs += [pl.BlockSpec(memory_space=pltpu.SMEM)]
        args += [sink]
    out_shape = [jax.ShapeDtypeStruct((n_seq * L, n_heads * HEAD_DIM), BF16)]
    out_specs = [pl.BlockSpec((tq, group * HEAD_DIM), lambda b, h, i: (b * nq + i, h))]
    if emit_k:
        out_shape += [jax.ShapeDtypeStruct((n_seq * L * n_kv, HEAD_DIM), F32)]
        out_specs += [pl.BlockSpec((L * n_kv, HEAD_DIM), lambda b, h, i: (b, 0))]
    kern = functools.partial(_attn_kernel, tq=tq, seq_len=L, group=group, n_kv=n_kv, n_ctx=n_ctx,
                             use_sink=use_sink, window=window, qk_norm=qk_norm, rope=rope, emit_k=emit_k)
    res = pl.pallas_call(
        kern,
        out_shape=tuple(out_shape),
        grid=(n_seq, n_kv, nq),
        in_specs=in_specs,
        out_specs=tuple(out_specs),
        scratch_shapes=scratch,
        compiler_params=pltpu.CompilerParams(dimension_semantics=("arbitrary", "arbitrary", "arbitrary"),
                                             vmem_limit_bytes=VMEM_LIMIT_BYTES),
        name="attention",
    )(*args)
    return res if emit_k else res[0]


def _out_mlp_kernel(*refs, n_x, n_parts, final, n_prompt_tiles, tf):
    it = iter(refs)
    x_refs = [next(it) for _ in range(n_x)]
    a_refs = [[next(it), next(it)] for _ in range(n_parts)]
    wo_refs = [next(it) for _ in range(n_parts)]
    mod_ref, g2_ref, w1_ref, w2_ref = next(it), next(it), next(it), next(it)
    fg_ref = next(it) if final else None
    o_refs = [next(it) for _ in range(2 if final else 1)]
    i = pl.program_id(0)
    d_ff = w1_ref.shape[1]

    mix = None
    for pair, wo_ref in zip(a_refs, wo_refs):
        term = _dot(_token_tile(pair, n_prompt_tiles), wo_ref[...])
        mix = term if mix is None else mix + term
    x1 = _token_tile(x_refs, n_prompt_tiles) + mod_ref[2:3, :] * mix
    h2 = (_rms_rows(x1, g2_ref[...]) * (1.0 + mod_ref[4:5, :]) + mod_ref[3:4, :]).astype(BF16)
    acc = None
    for c in range(d_ff // tf):
        hid = jnp.maximum(_dot(h2, w1_ref[:, c * tf:(c + 1) * tf]), 0.0)
        term = _dot((hid * hid).astype(BF16), w2_ref[c * tf:(c + 1) * tf, :])
        acc = term if acc is None else acc + term
    y = x1 + mod_ref[5:6, :] * acc

    if final:
        y = _rms_rows(y, fg_ref[...])

        @pl.when(i < n_prompt_tiles)
        def _():
            o_refs[0][...] = y

        @pl.when(i >= n_prompt_tiles)
        def _():
            o_refs[1][...] = y
    else:
        o_refs[0][...] = y


def _out_mlp(xs, parts, w_out, wo_layer, mod, layer, g2, w1, w2, n_prompt, sample_len, final_g=None, tm=512,
             tf=1024):
    n_tok = sum(a.shape[0] for a in xs)
    d = xs[0].shape[1]
    d_ff = w1.shape[2]
    final = final_g is not None
    npt = n_prompt // tm
    row = functools.partial(_cond_row, n_prompt_tiles=npt, tiles_per_sample=sample_len // tm)
    in_specs = _token_specs(xs, tm, npt)
    for pair in parts:
        in_specs += _token_specs(pair, tm, npt)
    resident = lambda shape: pl.BlockSpec((None, *shape), lambda i: (layer, 0, 0), pipeline_mode=pl.Buffered(1))
    width = w_out.shape[1] // len(parts)
    in_specs += [pl.BlockSpec((None, width, d), lambda i, p=p: (wo_layer, p, 0), pipeline_mode=pl.Buffered(1))
                 for p in range(len(parts))]
    in_specs += [
        pl.BlockSpec((None, None, 6, d), lambda i: (layer, row(i), 0, 0)),
        pl.BlockSpec((None, 1, d), lambda i: (layer, 0, 0)),
        resident((d, d_ff)),
        resident((d_ff, d)),
    ]
    args = [*xs, *[a for pair in parts for a in pair], *([w_out] * len(parts)), mod, g2, w1, w2]
    if final:
        in_specs += [pl.BlockSpec((1, d), lambda i: (0, 0))]
        args += [final_g.reshape(1, d)]
        out_shape = (jax.ShapeDtypeStruct((n_prompt, d), F32), jax.ShapeDtypeStruct((n_tok - n_prompt, d), F32))
        out_specs = (pl.BlockSpec((tm, d), lambda i: (jnp.minimum(i, npt - 1), 0)),
                     pl.BlockSpec((tm, d), lambda i: (jnp.maximum(i - npt, 0), 0)))
    else:
        out_shape = jax.ShapeDtypeStruct((n_tok, d), F32)
        out_specs = pl.BlockSpec((tm, d), lambda i: (i, 0))
    kern = functools.partial(_out_mlp_kernel, n_x=len(xs), n_parts=len(parts), final=final, n_prompt_tiles=npt,
                             tf=tf)
    return pl.pallas_call(
        kern,
        out_shape=out_shape,
        grid=(n_tok // tm,),
        in_specs=in_specs,
        out_specs=out_specs,
        compiler_params=pltpu.CompilerParams(dimension_semantics=("arbitrary",),
                                             vmem_limit_bytes=VMEM_LIMIT_BYTES),
        name="out_mlp",
    )(*args)


def _rope_tables(seq_len):
    rows = seq_len // GRID_W
    inv = ROPE_THETA ** (-np.arange(ROT_FREQS, dtype=np.float64) / ROT_FREQS)
    ang_r = np.repeat(np.arange(rows, dtype=np.float64), GRID_W)[:, None] * inv
    ang_c = np.tile(np.arange(GRID_W, dtype=np.float64), rows)[:, None] * inv
    cos = np.concatenate([np.cos(ang_r), np.cos(ang_r), np.cos(ang_c), np.cos(ang_c)], axis=-1)
    sin = np.concatenate([-np.sin(ang_r), np.sin(ang_r), -np.sin(ang_c), np.sin(ang_c)], axis=-1)
    return jnp.asarray(cos, dtype=F32), jnp.asarray(sin, dtype=F32)


def kernel(x_prompt, x_sample, state_a, cache_b_kv, cache_c_kv, c, c_ctx, ada_w, ada_b, norm1_g, norm2_g,
           final_g, mlp_w1, mlp_w2, ev_w_in, a_conv, a_log, a_dt_bias, a_norm_g, b_sink, ev_w_out,
           od_w_in, c_qnorm_g, c_knorm_g, od_w_out):
    batch, seq, d = x_prompt.shape
    dec_batch, dec_seq, _ = x_sample.shape
    depth = ada_w.shape[0]
    n_prompt, n_sample = batch * seq, dec_batch * dec_seq
    past = cache_b_kv.shape[3]

    xs = [x_prompt.reshape(n_prompt, d), x_sample.reshape(n_sample, d)]
    cond = jnp.concatenate([c_ctx[None, :], c, jnp.zeros((8 - 1 - dec_batch, d), F32)], axis=0)
    mod = _adaln(cond, ada_w, ada_b).reshape(depth, 8, 6, d)

    cos, sin = _rope_tables(dec_seq)
    s_blk0 = n_prompt // dec_seq
    ctx_b = cache_b_kv.reshape(*cache_b_kv.shape[:4], B_KV_HEADS * HEAD_DIM)
    ctx_c = cache_c_kv.reshape(*cache_c_kv.shape[:4], C_KV_HEADS * HEAD_DIM)
    ev_w_in_bf16 = _ev_w_in(ev_w_in)
    od_w_in_bf16 = od_w_in.astype(BF16)
    ev_w_out_bf16 = ev_w_out.astype(BF16)
    od_w_out_bf16 = od_w_out.astype(BF16)
    w1 = mlp_w1.astype(BF16)
    w2 = mlp_w2.astype(BF16)
    g1 = norm1_g.reshape(depth, 1, d)
    g2 = norm2_g.reshape(depth, 1, d)
    lane_pad = lambda v: jnp.zeros((1, LANES), F32).at[0, 2 * A_HEADS:4 * A_HEADS].set(v.reshape(-1))

    new_a, new_b, new_c = [], [], []
    for l in range(depth):
        i = l // 2
        fin = final_g if l == depth - 1 else None
        if l % 2 == 0:
            kv_cols = B_KV_HEADS * HEAD_DIM
            proj, (cache_k, cache_v) = _in_proj(xs, mod, l, g1, ev_w_in_bf16, i, n_prompt, dec_seq,
                                                cache_cols=(EV_KVB_OFF, EV_KVB_OFF + kv_cols), n_kv=B_KV_HEADS)
            beta_all, gc_all = _gates(proj, lane_pad(a_log[i]), lane_pad(a_dt_bias[i]))
            oa_p, st_p = _gdn(proj, beta_all, gc_all, a_conv[i], a_norm_g[i], 0, batch, seq, heads=A_HEADS,
                              steps=seq // CHUNK)
            oa_s, _ = _gdn(proj, beta_all, gc_all, a_conv[i], a_norm_g[i], s_blk0, dec_batch, dec_seq, heads=1,
                           steps=8, s0=state_a, s0_layer=i)
            ob_p = _attention(proj, q_off=EV_QB_OFF, k_off=EV_KVB_OFF, v_off=EV_KVB_OFF + kv_cols,
                              n_heads=B_HEADS, n_kv=B_KV_HEADS, row_block0=0, n_seq=batch, seq_len=seq,
                              tq=seq, sink=b_sink[i])
            ob_s = _attention(proj, q_off=EV_QB_OFF, k_off=EV_KVB_OFF, v_off=EV_KVB_OFF + kv_cols,
                              n_heads=B_HEADS, n_kv=B_KV_HEADS, row_block0=s_blk0, n_seq=dec_batch,
                              seq_len=dec_seq, tq=512,
                              ctx=ctx_b, ctx_layer=i, cos=cos, sin=sin, sink=b_sink[i], window=WINDOW)
            new_a.append(st_p)
            new_b.append(jnp.stack([cache_k.reshape(batch, seq, B_KV_HEADS, HEAD_DIM),
                                    cache_v.reshape(batch, seq, B_KV_HEADS, HEAD_DIM)], axis=1))
            parts = [[oa_p, oa_s], [ob_p, ob_s]]
            w_out = ev_w_out_bf16
        else:
            kv_cols = C_KV_HEADS * HEAD_DIM
            proj, (cache_v,) = _in_proj(xs, mod, l, g1, od_w_in_bf16, i, n_prompt, dec_seq,
                                        cache_cols=(OD_V_OFF,), n_kv=C_KV_HEADS)
            oc_p, kn_p = _attention(proj, q_off=0, k_off=OD_K_OFF, v_off=OD_V_OFF, n_heads=C_HEADS,
                                    n_kv=C_KV_HEADS, row_block0=0, n_seq=batch, seq_len=seq, tq=seq,
                                    q_gain=c_qnorm_g[i], k_gain=c_knorm_g[i], emit_k=True)
            oc_s = _attention(proj, q_off=0, k_off=OD_K_OFF, v_off=OD_V_OFF, n_heads=C_HEADS,
                              n_kv=C_KV_HEADS, row_block0=s_blk0, n_seq=dec_batch, seq_len=dec_seq, tq=256,
                              ctx=ctx_c, ctx_layer=i, cos=cos, sin=sin, q_gain=c_qnorm_g[i],
                              k_gain=c_knorm_g[i])
            new_c.append(jnp.stack([kn_p.reshape(batch, seq, C_KV_HEADS, HEAD_DIM),
                                    cache_v.reshape(batch, seq, C_KV_HEADS, HEAD_DIM)], axis=1))
            parts = [[oc_p, oc_s]]
            w_out = od_w_out_bf16
        out = _out_mlp(xs, parts, w_out, i, mod, l, g2, w1, w2, n_prompt, dec_seq, final_g=fin)
        xs = list(out) if fin is not None else [out]

    y_prompt = xs[0].reshape(batch, seq, d)
    y_sample = xs[1].reshape(dec_batch, dec_seq, d)
    return (y_prompt, y_sample, jnp.stack(new_a, axis=1), jnp.stack(new_b, axis=1), jnp.stack(new_c, axis=1))
```

```python
import functools

import numpy as np
import jax
import jax.numpy as jnp
from jax import lax
from jax.experimental import pallas as pl
from jax.experimental.pallas import tpu as pltpu

F32 = jnp.float32
BF16 = jnp.bfloat16

HEAD_DIM = 128
GRID_W = 64
A_HEADS = 4
CHUNK = 64
B_HEADS = 4
B_KV_HEADS = 2
WINDOW = 128
C_HEADS = 8
C_KV_HEADS = 2
ROT_FREQS = HEAD_DIM // 4
ROPE_THETA = 10000.0
EPS = 1e-6
NEG = -1e30
LOG2_E = 1.4426950408889634
A_WIDTH = A_HEADS * HEAD_DIM
B_WIDTH = B_HEADS * HEAD_DIM
C_WIDTH = C_HEADS * HEAD_DIM

LANES = 128
VMEM_LIMIT_BYTES = 56 * 1024 * 1024

EV_GATE_OFF = 3 * A_WIDTH
EV_QB_OFF = 4 * A_WIDTH
EV_KVB_OFF = EV_QB_OFF + B_WIDTH
EV_BG_OFF = EV_KVB_OFF + 2 * B_KV_HEADS * HEAD_DIM
EV_COLS = EV_BG_OFF + LANES
OD_K_OFF = C_WIDTH
OD_V_OFF = C_WIDTH + C_KV_HEADS * HEAD_DIM


def _sigmoid(x):
    return 1.0 / (1.0 + jnp.exp(-x))


def _silu(x):
    return x * _sigmoid(x)


def _softplus(x):
    return jnp.maximum(x, 0.0) + jnp.log1p(jnp.exp(-jnp.abs(x)))


def _rms_rows(x, g):
    return x * lax.rsqrt(jnp.mean(x * x, axis=-1, keepdims=True) + EPS) * g


def _dot(a, b):
    return jnp.dot(a, b, preferred_element_type=F32)


def _dot_nt(a, b):
    return lax.dot_general(a, b, (((1,), (1,)), ((), ())), preferred_element_type=F32)


def _cond_row(i, n_prompt_tiles, tiles_per_sample):
    return jnp.where(i < n_prompt_tiles, 0, 1 + (i - n_prompt_tiles) // tiles_per_sample)


def _split_bf16(x):
    hi = x.astype(BF16)
    lo = (x - hi.astype(F32)).astype(BF16)
    return hi, lo


def _adaln_kernel(cond_ref, w_ref, b_ref, o_ref):
    rows = cond_ref.shape[0]
    s_hi, s_lo = _split_bf16(_silu(cond_ref[...]))
    w_hi, w_lo = _split_bf16(w_ref[0])
    r = _dot(jnp.concatenate([s_hi, s_lo], axis=0), w_hi)
    o_ref[0] = r[:rows] + r[rows:] + _dot(s_hi, w_lo) + b_ref[0]


def _adaln(cond, ada_w, ada_b, tn=1024):
    depth, d, n = ada_w.shape
    rows = cond.shape[0]
    return pl.pallas_call(
        _adaln_kernel,
        out_shape=jax.ShapeDtypeStruct((depth, rows, n), F32),
        grid=(depth, n // tn),
        in_specs=[
            pl.BlockSpec((rows, d), lambda l, j: (0, 0)),
            pl.BlockSpec((1, d, tn), lambda l, j: (l, 0, j)),
            pl.BlockSpec((1, 1, tn), lambda l, j: (l, 0, j)),
        ],
        out_specs=pl.BlockSpec((1, rows, tn), lambda l, j: (l, 0, j)),
        compiler_params=pltpu.CompilerParams(dimension_semantics=("arbitrary", "arbitrary"),
                                             vmem_limit_bytes=VMEM_LIMIT_BYTES),
        name="adaln",
    )(cond, ada_w, ada_b.reshape(depth, 1, n))


def _ev_w_in_kernel(w_ref, o_ref):
    bg_off = 4 * A_WIDTH
    bg_end = bg_off + 4 * A_HEADS
    rows = w_ref.shape[0]
    o_ref[:, :bg_off] = w_ref[:, :bg_off]
    o_ref[:, bg_off:EV_BG_OFF] = w_ref[:, bg_end:]
    o_ref[:, EV_BG_OFF:] = jnp.concatenate([w_ref[:, bg_off:bg_end],
                                            jnp.zeros((rows, LANES - 4 * A_HEADS), BF16)], axis=1)


def _ev_w_in(w, tr=256):
    layers, d, n = w.shape
    return pl.pallas_call(
        _ev_w_in_kernel,
        out_shape=jax.ShapeDtypeStruct((layers, d, EV_COLS), BF16),
        grid=(layers, d // tr),
        in_specs=[pl.BlockSpec((None, tr, n), lambda l, r: (l, r, 0))],
        out_specs=pl.BlockSpec((None, tr, EV_COLS), lambda l, r: (l, r, 0)),
        compiler_params=pltpu.CompilerParams(dimension_semantics=("arbitrary", "arbitrary")),
        name="ev_w_in",
    )(w)


def _token_specs(arrays, tm, n_prompt_tiles):
    if len(arrays) == 1:
        return [pl.BlockSpec((tm, arrays[0].shape[1]), lambda i, *_: (i, 0))]
    return [pl.BlockSpec((tm, arrays[0].shape[1]), lambda i, *_: (jnp.minimum(i, n_prompt_tiles - 1), 0)),
            pl.BlockSpec((tm, arrays[1].shape[1]), lambda i, *_: (jnp.maximum(i - n_prompt_tiles, 0), 0))]


def _token_tile(refs, n_prompt_tiles):
    if len(refs) == 1:
        return refs[0][...]
    return jnp.where(pl.program_id(0) < n_prompt_tiles, refs[0][...], refs[1][...])


def _gate_columns(bg, a_row, dt_row):
    rows = bg.shape[0]
    g = -jnp.exp(a_row) * _softplus(bg + dt_row)
    chunk_row = lax.broadcasted_iota(jnp.int32, (rows, LANES), 0) % CHUNK
    lane = lax.broadcasted_iota(jnp.int32, (rows, LANES), 1)
    pre, suf = g, g
    shift = 1
    while shift < CHUNK:
        pre = pre + jnp.where(chunk_row >= shift, pltpu.roll(pre, shift, 0), 0.0)
        suf = suf + jnp.where(chunk_row < CHUNK - shift, pltpu.roll(suf, rows - shift, 0), 0.0)
        shift *= 2
    return _sigmoid(bg), jnp.where(lane >= 3 * A_HEADS, suf, pre)


def _inproj_kernel(*refs, n_x, n_prompt_tiles, cache_cols, n_kv, gates):
    it = iter(refs)
    x_refs = [next(it) for _ in range(n_x)]
    mod_ref, g_ref, w_ref = next(it), next(it), next(it)
    a_ref, dt_ref = (next(it), next(it)) if gates else (None, None)
    o_ref = next(it)
    cache_refs = [next(it) for _ in cache_cols]
    x = _token_tile(x_refs, n_prompt_tiles)
    h = _rms_rows(x, g_ref[...]) * (1.0 + mod_ref[1:2, :]) + mod_ref[0:1, :]
    o = _dot(h.astype(BF16), w_ref[...])
    o_ref[...] = o
    if gates:
        beta_ref, gc_ref = next(it), next(it)
        beta_ref[...], gc_ref[...] = _gate_columns(o[:, EV_BG_OFF:EV_BG_OFF + LANES], a_ref[...], dt_ref[...])

    @pl.when(pl.program_id(0) < n_prompt_tiles)
    def _():
        tm = o.shape[0]
        for c_ref, col in zip(cache_refs, cache_cols):
            for hd in range(n_kv):
                c_ref[pl.ds(hd, tm, stride=n_kv), :] = o[:, col + hd * HEAD_DIM:col + (hd + 1) * HEAD_DIM]


def _in_proj(xs, mod, layer, g, w, w_layer, n_prompt, sample_len, cache_cols, n_kv, gate_rows=None, tm=512):
    n_tok = sum(a.shape[0] for a in xs)
    d = xs[0].shape[1]
    n = w.shape[2]
    npt = n_prompt // tm
    assert tm % CHUNK == 0
    gates = gate_rows is not None
    row = functools.partial(_cond_row, n_prompt_tiles=npt, tiles_per_sample=sample_len // tm)
    cache_spec = pl.BlockSpec((tm * n_kv, HEAD_DIM), lambda i: (jnp.minimum(i, npt - 1), 0))
    lane_row = pl.BlockSpec((1, LANES), lambda i: (0, 0))
    gate_spec = pl.BlockSpec((tm, LANES), lambda i: (i, 0))
    res = pl.pallas_call(
        functools.partial(_inproj_kernel, n_x=len(xs), n_prompt_tiles=npt, cache_cols=cache_cols, n_kv=n_kv,
                          gates=gates),
        out_shape=(jax.ShapeDtypeStruct((n_tok, n), F32),
                   *[jax.ShapeDtypeStruct((n_prompt * n_kv, HEAD_DIM), F32) for _ in cache_cols],
                   *[jax.ShapeDtypeStruct((n_tok, LANES), F32) for _ in range(2 if gates else 0)]),
        grid=(n_tok // tm,),
        in_specs=_token_specs(xs, tm, npt) + [
            pl.BlockSpec((None, None, 6, d), lambda i: (layer, row(i), 0, 0)),
            pl.BlockSpec((None, 1, d), lambda i: (layer, 0, 0)),
            pl.BlockSpec((None, d, n), lambda i: (w_layer, 0, 0)),
        ] + ([lane_row, lane_row] if gates else []),
        out_specs=(pl.BlockSpec((tm, n), lambda i: (i, 0)), *[cache_spec for _ in cache_cols],
                   *([gate_spec, gate_spec] if gates else [])),
        compiler_params=pltpu.CompilerParams(dimension_semantics=("arbitrary",),
                                             vmem_limit_bytes=VMEM_LIMIT_BYTES),
        name="in_proj",
    )(*xs, mod, g, w, *(gate_rows if gates else ()))
    n_c = len(cache_cols)
    return res[0], res[1:1 + n_c], res[1 + n_c:]


def _gdn_kernel(q_ref, k_ref, v_ref, gate_ref, betaall_ref, gcall_ref, cwq_ref, cwk_ref, cwv_ref, ng_ref, *rest,
                seq_len, heads, steps, has_state):
    s0_ref = rest[0] if has_state else None
    (o_ref, sout_ref, qn_ref, kn_ref, vn_ref, beta_ref, gc_ref, gcrow_ref, s_ref, oacc_ref,
     mq_a, c_a, oo_a, eg_a, mq_b, c_b, oo_b, eg_b) = rest[1:] if has_state else rest
    L = seq_len
    nc = L // CHUNK
    nblk = nc // steps
    h0 = pl.program_id(1) * heads
    row_idx = lax.broadcasted_iota(jnp.int32, (L, LANES), 0)
    lane_idx = lax.broadcasted_iota(jnp.int32, (L, LANES), 1)

    def conv_silu(x, cw):
        prev = jnp.where(row_idx == 0, 0.0, pltpu.roll(x, 1, 0))
        nxt = jnp.where(row_idx == L - 1, 0.0, pltpu.roll(x, L - 1, 0))
        return _silu(prev * cw[0:1, :] + x * cw[1:2, :] + nxt * cw[2:3, :])

    def l2norm(x):
        return x * lax.rsqrt(jnp.sum(x * x, axis=-1, keepdims=True) + EPS)

    beta_all = betaall_ref[...]
    gc_all = gcall_ref[...]

    def pick(x, col):
        col_vals = jnp.sum(jnp.where(lane_idx == col, x, 0.0), axis=1, keepdims=True)
        return jnp.broadcast_to(col_vals, (L, LANES))

    for r in range(L // LANES):
        rows = jnp.transpose(gc_all[r * LANES:(r + 1) * LANES, :])[2 * A_HEADS:4 * A_HEADS, :]
        gcrow_ref[0, r] = rows
        gcrow_ref[1, r] = pltpu.roll(rows, CHUNK, 1)

    for hh in range(heads):
        cols = slice(hh * HEAD_DIM, (hh + 1) * HEAD_DIM)
        qn_ref[:, cols] = l2norm(conv_silu(q_ref[:, cols], cwq_ref[:, cols])) * (HEAD_DIM ** -0.5)
        kn_ref[:, cols] = l2norm(conv_silu(k_ref[:, cols], cwk_ref[:, cols]))
        vn_ref[:, cols] = conv_silu(v_ref[:, cols], cwv_ref[:, cols])
        for d in range(2):
            s_ref[hh, d] = s0_ref[d, hh] if has_state else jnp.zeros((HEAD_DIM, HEAD_DIM), F32)
        head = h0 + hh
        for d in range(2):
            beta_ref[hh, d] = pick(beta_all, d * A_HEADS + head)
            gc_ref[hh, d] = pick(gc_all, (2 + d) * A_HEADS + head)

    r64 = lax.broadcasted_iota(jnp.int32, (CHUNK, LANES), 0)
    l64 = lax.broadcasted_iota(jnp.int32, (CHUNK, LANES), 1)
    is_f = l64 < CHUNK
    is_f_row = lax.broadcasted_iota(jnp.int32, (1, LANES), 1) < CHUNK
    cpos = jnp.where(is_f, l64, l64 - CHUNK)
    ahead = jnp.where(is_f, r64 - cpos, cpos - r64)
    incl = ahead >= 0
    strict = ahead > 0
    eye = jnp.where(r64 == cpos, 1.0, 0.0)
    is_f2 = lax.broadcasted_iota(jnp.int32, (2 * CHUNK, LANES), 1) < CHUNK
    zeros_c = jnp.zeros((CHUNK, HEAD_DIM), F32)

    def aligned(off, m):
        return off if isinstance(off, int) else pl.multiple_of(off, m)

    def half(n):
        return n // 2 if isinstance(n, int) else lax.shift_right_logical(n, 1)

    def block_diag(xp):
        z = jnp.zeros_like(xp)
        return jnp.concatenate([jnp.where(is_f, xp, z), jnp.where(is_f, z, xp)], axis=0)

    def split_dot(a_hi, a_lo, b_hi, b_lo):
        lhs = jnp.concatenate([a_hi, a_lo, a_hi], axis=1)
        rhs = jnp.concatenate([b_hi, b_hi, b_lo], axis=0)
        return _dot(lhs, rhs)

    def prep_stages(blk, slot):
        mq_ref, c_ref, oo_ref, eg_ref = slot
        probs = [(si, hh) for si in range(steps) for hh in range(heads)]
        st = {p: {} for p in probs}

        def load(p):
            si, hh = p
            j = blk * steps + si
            n_b = nc - 1 - j
            cols = slice(hh * HEAD_DIM, (hh + 1) * HEAD_DIM)
            off_f = aligned(j * CHUNK, CHUNK)
            off_b = aligned(n_b * CHUNK, CHUNK)
            q_f, k_f, v_f = (r[pl.ds(off_f, CHUNK), cols] for r in (qn_ref, kn_ref, vn_ref))
            q_b, k_b, v_b = (r[pl.ds(off_b, CHUNK), cols] for r in (qn_ref, kn_ref, vn_ref))
            beta_f = beta_ref[hh, 0, pl.ds(off_f, CHUNK), :]
            beta_b = beta_ref[hh, 1, pl.ds(off_b, CHUNK), :]
            gc_f = gc_ref[hh, 0, pl.ds(off_f, CHUNK), :]
            gc_b = gc_ref[hh, 1, pl.ds(off_b, CHUNK), :]
            row_f = gcrow_ref[si % 2, half(j), pl.ds(h0 + hh, 1), :]
            row_b = gcrow_ref[si % 2, half(n_b), pl.ds(A_HEADS + h0 + hh, 1), :]
            gcrow = jnp.where(is_f_row, row_f, row_b)
            diff = jnp.where(is_f, gc_f, gc_b) - gcrow
            decay = jnp.where(incl, jnp.exp(jnp.where(incl, diff, 0.0)), 0.0)
            kb_f = k_f * beta_f
            kb_b = k_b * beta_b
            e_f = jnp.exp(gc_f)
            e_b = jnp.exp(gc_b)
            glast_f = gc_f[CHUNK - 1:CHUNK, :]
            glast_b = gc_b[0:1, :]
            kdt = jnp.transpose(jnp.concatenate([k_f * jnp.exp(glast_f - gc_f), k_b * jnp.exp(glast_b - gc_b)],
                                                axis=0))
            zk = jnp.zeros_like(kdt)
            eg_ref[si, hh, 0] = jnp.broadcast_to(jnp.exp(glast_f), (8, LANES))
            eg_ref[si, hh, 1] = jnp.broadcast_to(jnp.exp(glast_b), (8, LANES))
            st[p].update(
                decay=decay,
                lhs=jnp.concatenate([jnp.concatenate([kb_f, kb_b], axis=1),
                                     jnp.concatenate([q_f, q_b], axis=1)], axis=0).astype(BF16),
                rhs=jnp.concatenate([jnp.concatenate([k_f, zeros_c], axis=1),
                                     jnp.concatenate([zeros_c, k_b], axis=1)], axis=0).astype(BF16),
                rhs_uw=jnp.concatenate([jnp.concatenate([v_f * beta_f, kb_f * e_f], axis=1),
                                        jnp.concatenate([v_b * beta_b, kb_b * e_b], axis=1)], axis=0).astype(BF16),
                kd2=jnp.concatenate([jnp.where(is_f2, kdt, zk), jnp.where(is_f2, zk, kdt)], axis=0).astype(BF16),
                qe=(q_f * e_f, q_b * e_b))

        def gram(p):
            s = st[p]
            aq = _dot_nt(s.pop("lhs"), s.pop("rhs"))
            decay = s.pop("decay")
            x = -jnp.where(strict, aq[:CHUNK] * decay, 0.0)
            s["qk"] = jnp.where(incl, aq[CHUNK:] * decay, 0.0)
            s["t"] = eye + x
            s["p"] = x

        def square(p):
            s = st[p]
            p_hi, p_lo = _split_bf16(s["p"])
            s["p"] = split_dot(p_hi, p_lo, block_diag(p_hi), block_diag(p_lo))

        def double(p):
            s = st[p]
            p_hi, p_lo = _split_bf16(s["p"])
            t_hi, t_lo = _split_bf16(s["t"])
            r = split_dot(p_hi, p_lo, jnp.concatenate([block_diag(t_hi), block_diag(p_hi)], axis=1),
                          jnp.concatenate([block_diag(t_lo), block_diag(p_lo)], axis=1))
            s["t"] = s["t"] + r[:, :LANES]
            s["p"] = r[:, LANES:]

        def last(p):
            s = st[p]
            p_hi, p_lo = _split_bf16(s.pop("p"))
            t_hi, t_lo = _split_bf16(s["t"])
            s["t"] = s["t"] + split_dot(p_hi, p_lo, block_diag(t_hi), block_diag(t_lo))

        def solve(p):
            s = st[p]
            s["uw"] = _dot(block_diag(s.pop("t")).astype(BF16), s.pop("rhs_uw"))

        def fold(p):
            si, hh = p
            s = st[p]
            lhs3 = jnp.concatenate([s.pop("kd2"), block_diag(s.pop("qk")).astype(BF16)], axis=0)
            big = _dot(lhs3, s.pop("uw").astype(BF16))
            qe = s.pop("qe")
            for d in range(2):
                rows_s = slice(d * 2 * CHUNK, (d + 1) * 2 * CHUNK)
                rows_o = slice(4 * CHUNK + d * CHUNK, 4 * CHUNK + (d + 1) * CHUNK)
                c_ref[si, hh, d] = big[rows_s, :HEAD_DIM]
                oo_ref[si, hh, d] = big[rows_o, :HEAD_DIM]
                mq_ref[si, hh, d] = jnp.concatenate([big[rows_s, HEAD_DIM:], qe[d] - big[rows_o, HEAD_DIM:]],
                                                    axis=0).astype(BF16)

        n_double = 0
        span = 2
        while 2 * span < CHUNK:
            n_double += 1
            span *= 2
        stage_fns = [load, gram, square] + [double] * n_double + [last, solve, fold]
        return [functools.partial(lambda fn: [fn(p) for p in probs], fn) for fn in stage_fns]

    def scan_step(blk, slot, si):
        mq_ref, c_ref, oo_ref, eg_ref = slot
        j = blk * steps + si
        for hh in range(heads):
            cols = slice(hh * HEAD_DIM, (hh + 1) * HEAD_DIM)
            for d, n in ((0, j), (1, nc - 1 - j)):
                s = s_ref[hh, d]
                r = _dot(mq_ref[si, hh, d], s.astype(BF16))
                s_ref[hh, d] = s * eg_ref[si, hh, d][0:1, :] - r[:2 * CHUNK] + c_ref[si, hh, d]
                oacc_ref[d, pl.ds(aligned(n * CHUNK, CHUNK), CHUNK), cols] = r[2 * CHUNK:] + oo_ref[si, hh, d]

    def run(prep, scan):
        stages = prep_stages(*prep) if prep else []
        n_scan = steps if scan else 0
        for k in range(max(len(stages), n_scan)):
            if k < len(stages):
                stages[k]()
            if k < n_scan:
                scan_step(scan[0], scan[1], k)

    slot_a = (mq_a, c_a, oo_a, eg_a)
    slot_b = (mq_b, c_b, oo_b, eg_b)
    run((0, slot_a), None)
    if nblk == 1:
        run(None, (0, slot_a))
    else:
        def body(i, carry):
            blk = 2 * i
            run((blk + 1, slot_b), (blk, slot_a))
            run((blk + 2, slot_a), (blk + 1, slot_b))
            return carry

        lax.fori_loop(0, nblk // 2 - 1, body, 0)
        run((nblk - 1, slot_b), (nblk - 2, slot_a))
        run(None, (nblk - 1, slot_b))

    for hh in range(heads):
        cols = slice(hh * HEAD_DIM, (hh + 1) * HEAD_DIM)
        sout_ref[0, hh] = s_ref[hh, 0]
        sout_ref[1, hh] = s_ref[hh, 1]
        o = oacc_ref[0, :, cols] + oacc_ref[1, :, cols]
        o = _rms_rows(o, ng_ref[...]) * _silu(gate_ref[:, cols])
        o_ref[:, cols] = o.astype(o_ref.dtype)


def _gdn(proj, beta_all, gc_all, conv_w, norm_g, row_block0, n_seq, seq_len, heads, steps, s0=None, s0_layer=0):
    L = seq_len
    nblk = L // CHUNK // steps
    assert L % (CHUNK * steps) == 0 and steps % 2 == 0 and 2 * CHUNK == LANES and A_HEADS % heads == 0
    assert nblk == 1 or nblk % 2 == 0
    width = heads * HEAD_DIM
    groups = A_HEADS // heads
    blk = lambda base: pl.BlockSpec((L, width), lambda b, h: (row_block0 + b, base * groups + h))
    wblk = lambda base: pl.BlockSpec((3, width), lambda b, h: (0, base * groups + h))
    state_spec = pl.BlockSpec((None, 2, heads, HEAD_DIM, HEAD_DIM), lambda b, h: (b, 0, h, 0, 0))
    slot = [pltpu.VMEM((steps, heads, 2, 3 * CHUNK, HEAD_DIM), BF16),
            pltpu.VMEM((steps, heads, 2, 2 * CHUNK, HEAD_DIM), F32),
            pltpu.VMEM((steps, heads, 2, CHUNK, HEAD_DIM), F32),
            pltpu.VMEM((steps, heads, 2, 8, LANES), F32)]
    kern = functools.partial(_gdn_kernel, seq_len=L, heads=heads, steps=steps, has_state=s0 is not None)
    state_in, state_arg = [], []
    if s0 is not None:
        state_in = [pl.BlockSpec((None, None, 2, heads, HEAD_DIM, HEAD_DIM), lambda b, h: (b, s0_layer, 0, h, 0, 0))]
        state_arg = [s0]
    out, s_out = pl.pallas_call(
        kern,
        out_shape=(jax.ShapeDtypeStruct((n_seq * L, A_WIDTH), BF16),
                   jax.ShapeDtypeStruct((n_seq, 2, A_HEADS, HEAD_DIM, HEAD_DIM), F32)),
        grid=(n_seq, groups),
        in_specs=[
            blk(0), blk(1), blk(2), blk(3),
            pl.BlockSpec((L, LANES), lambda b, h: (row_block0 + b, 0)),
            pl.BlockSpec((L, LANES), lambda b, h: (row_block0 + b, 0)),
            wblk(0), wblk(1), wblk(2),
            pl.BlockSpec((1, HEAD_DIM), lambda b, h: (0, 0)),
        ] + state_in,
        out_specs=(pl.BlockSpec((L, width), lambda b, h: (b, h)), state_spec),
        scratch_shapes=[
            pltpu.VMEM((L, width), F32), pltpu.VMEM((L, width), F32), pltpu.VMEM((L, width), F32),
            pltpu.VMEM((heads, 2, L, LANES), F32), pltpu.VMEM((heads, 2, L, LANES), F32),
            pltpu.VMEM((2, L // LANES, 2 * A_HEADS, LANES), F32),
            pltpu.VMEM((heads, 2, HEAD_DIM, HEAD_DIM), F32),
            pltpu.VMEM((2, L, width), F32),
        ] + slot + slot,
        compiler_params=pltpu.CompilerParams(dimension_semantics=("arbitrary", "arbitrary"),
                                             vmem_limit_bytes=VMEM_LIMIT_BYTES),
        name="gdn",
    )(proj, proj, proj, proj, beta_all, gc_all, conv_w, conv_w, conv_w, norm_g.reshape(1, HEAD_DIM), *state_arg)
    return out, s_out


def _rope_rows(x, cos, sin_signed):
    lane = lax.broadcasted_iota(jnp.int32, x.shape, 1)
    swapped = jnp.where((lane % (2 * ROT_FREQS)) < ROT_FREQS,
                        pltpu.roll(x, HEAD_DIM - ROT_FREQS, 1), pltpu.roll(x, ROT_FREQS, 1))
    return x * cos + swapped * sin_signed


def _attn_kernel(*refs, tq, seq_len, group, n_kv, n_ctx, use_sink, window, qk_norm, rope, emit_k):
    it = iter(refs)
    q_ref, k_ref, v_ref = next(it), next(it), next(it)
    ck_ref = cv_ref = cos_ref = sin_ref = qg_ref = kg_ref = sink_ref = kn_out_ref = None
    if n_ctx:
        ck_ref, cv_ref = next(it), next(it)
    if rope:
        cos_ref, sin_ref = next(it), next(it)
    if qk_norm:
        qg_ref, kg_ref = next(it), next(it)
    if use_sink:
        sink_ref = next(it)
    o_ref = next(it)
    if emit_k:
        kn_out_ref = next(it)
    kbf_ref, vbf_ref = next(it), next(it)
    ckbf_ref = cvbf_ref = None
    if n_ctx:
        ckbf_ref, cvbf_ref = next(it), next(it)

    L = seq_len
    kvh = pl.program_id(1)
    qi = pl.program_id(2)

    def ones_column(rows):
        lane = lax.broadcasted_iota(jnp.int32, (rows, HEAD_DIM), 1)
        return jnp.where(lane == 0, 1.0, 0.0).astype(BF16)

    @pl.when(qi == 0)
    def _():
        k = k_ref[...]
        if qk_norm:
            k = _rms_rows(k, kg_ref[...])
        if emit_k:
            for hd in range(n_kv):
                @pl.when(kvh == hd)
                def _():
                    kn_out_ref[pl.ds(hd, L, stride=n_kv), :] = k
        if rope:
            k = _rope_rows(k, cos_ref[...], sin_ref[...])
        kbf_ref[...] = k.astype(BF16)
        vbf_ref[:, :HEAD_DIM] = v_ref[...].astype(BF16)
        vbf_ref[:, HEAD_DIM:] = ones_column(L)
        if n_ctx:
            ckbf_ref[...] = ck_ref[...].astype(BF16)
            cvbf_ref[:, :HEAD_DIM] = cv_ref[...].astype(BF16)
            cvbf_ref[:, HEAD_DIM:] = ones_column(n_ctx)

    q0 = pl.multiple_of(qi * tq, tq)
    if window:
        wk = tq + 2 * window
        ws = pl.multiple_of(jnp.clip(q0 - window, 0, L - wk), LANES)
        k_loc = kbf_ref[pl.ds(ws, wk), :]
        v_loc = vbf_ref[pl.ds(ws, wk), :]
        qpos = q0 + lax.broadcasted_iota(jnp.int32, (tq, wk), 0)
        kpos = ws + lax.broadcasted_iota(jnp.int32, (tq, wk), 1)
        valid = jnp.abs(qpos - kpos) <= window
    else:
        k_loc = kbf_ref[...]
        v_loc = vbf_ref[...]

    def scores(g):
        q = q_ref[:, g * HEAD_DIM:(g + 1) * HEAD_DIM]
        if qk_norm:
            q = _rms_rows(q, qg_ref[...])
        if rope:
            q = _rope_rows(q, cos_ref[pl.ds(q0, tq), :], sin_ref[pl.ds(q0, tq), :])
        q = (q * (HEAD_DIM ** -0.5 * LOG2_E)).astype(BF16)
        s_loc = _dot_nt(q, k_loc)
        if window:
            s_loc = jnp.where(valid, s_loc, NEG)
        pieces = [(s_loc, v_loc)]
        if n_ctx:
            pieces.append((_dot_nt(q, ckbf_ref[...]), cvbf_ref[...]))
        return pieces

    def finish(g, pieces):
        m = None
        for s, _ in pieces:
            row_max = jnp.max(s, axis=-1, keepdims=True)
            m = row_max if m is None else jnp.maximum(m, row_max)
        if use_sink:
            sink = sink_ref[kvh * group + g] * LOG2_E
            m = jnp.maximum(m, sink)
        acc = None
        for s, v in pieces:
            term = _dot(jnp.exp2(s - m).astype(BF16), v)
            acc = term if acc is None else acc + term
        denom = acc[:, HEAD_DIM:HEAD_DIM + 1]
        if use_sink:
            denom = denom + jnp.exp2(sink - m)
        o_ref[:, g * HEAD_DIM:(g + 1) * HEAD_DIM] = (acc[:, :HEAD_DIM] / denom).astype(o_ref.dtype)

    ahead = scores(0)
    for g in range(group):
        cur = ahead
        if g + 1 < group:
            ahead = scores(g + 1)
        finish(g, cur)


def _attention(proj, *, q_off, k_off, v_off, n_heads, n_kv, row_block0, n_seq, seq_len, tq,
               ctx=None, ctx_layer=0, cos=None, sin=None, q_gain=None, k_gain=None, sink=None,
               window=0, emit_k=False):
    L = seq_len
    group = n_heads // n_kv
    n_ctx = 0 if ctx is None else ctx.shape[3]
    rope = cos is not None
    qk_norm = q_gain is not None
    use_sink = sink is not None
    nq = L // tq
    qb, kb, vb = q_off // (group * HEAD_DIM), k_off // HEAD_DIM, v_off // HEAD_DIM
    in_specs = [
        pl.BlockSpec((tq, group * HEAD_DIM), lambda b, h, i: ((row_block0 + b) * nq + i, qb + h)),
        pl.BlockSpec((L, HEAD_DIM), lambda b, h, i: (row_block0 + b, kb + h)),
        pl.BlockSpec((L, HEAD_DIM), lambda b, h, i: (row_block0 + b, vb + h)),
    ]
    args = [proj, proj, proj]
    scratch = [pltpu.VMEM((L, HEAD_DIM), BF16), pltpu.VMEM((L, 2 * HEAD_DIM), BF16)]
    if n_ctx:
        in_specs += [pl.BlockSpec((None, None, None, n_ctx, HEAD_DIM), lambda b, h, i: (b, ctx_layer, 0, 0, h)),
                     pl.BlockSpec((None, None, None, n_ctx, HEAD_DIM), lambda b, h, i: (b, ctx_layer, 1, 0, h))]
        args += [ctx, ctx]
        scratch += [pltpu.VMEM((n_ctx, HEAD_DIM), BF16), pltpu.VMEM((n_ctx, 2 * HEAD_DIM), BF16)]
    if rope:
        in_specs += [pl.BlockSpec((L, HEAD_DIM), lambda b, h, i: (0, 0))] * 2
        args += [cos, sin]
    if qk_norm:
        in_specs += [pl.BlockSpec((1, HEAD_DIM), lambda b, h, i: (0, 0))] * 2
        args += [q_gain.reshape(1, HEAD_DIM), k_gain.reshape(1, HEAD_DIM)]
    if use_sink:
        in_specs += [pl.BlockSpec(memory_space=pltpu.SMEM)]
        args += [sink]
    out_shape = [jax.ShapeDtypeStruct((n_seq * L, n_heads * HEAD_DIM), BF16)]
    out_specs = [pl.BlockSpec((tq, group * HEAD_DIM), lambda b, h, i: (b * nq + i, h))]
    if emit_k:
        out_shape += [jax.ShapeDtypeStruct((n_seq * L * n_kv, HEAD_DIM), F32)]
        out_specs += [pl.BlockSpec((L * n_kv, HEAD_DIM), lambda b, h, i: (b, 0))]
    kern = functools.partial(_attn_kernel, tq=tq, seq_len=L, group=group, n_kv=n_kv, n_ctx=n_ctx,
                             use_sink=use_sink, window=window, qk_norm=qk_norm, rope=rope, emit_k=emit_k)
    res = pl.pallas_call(
        kern,
        out_shape=tuple(out_shape),
        grid=(n_seq, n_kv, nq),
        in_specs=in_specs,
        out_specs=tuple(out_specs),
        scratch_shapes=scratch,
        compiler_params=pltpu.CompilerParams(dimension_semantics=("arbitrary", "arbitrary", "arbitrary"),
                                             vmem_limit_bytes=VMEM_LIMIT_BYTES),
        name="attention",
    )(*args)
    return res if emit_k else res[0]


def _out_mlp_kernel(*refs, n_x, n_parts, final, n_prompt_tiles, tf):
    it = iter(refs)
    x_refs = [next(it) for _ in range(n_x)]
    a_refs = [[next(it), next(it)] for _ in range(n_parts)]
    wo_refs = [next(it) for _ in range(n_parts)]
    mod_ref, g2_ref, w1_ref, w2_ref = next(it), next(it), next(it), next(it)
    fg_ref = next(it) if final else None
    o_refs = [next(it) for _ in range(2 if final else 1)]
    i = pl.program_id(0)
    d_ff = w1_ref.shape[1]

    mix = None
    for pair, wo_ref in zip(a_refs, wo_refs):
        term = _dot(_token_tile(pair, n_prompt_tiles), wo_ref[...])
        mix = term if mix is None else mix + term
    x1 = _token_tile(x_refs, n_prompt_tiles) + mod_ref[2:3, :] * mix
    h2 = (_rms_rows(x1, g2_ref[...]) * (1.0 + mod_ref[4:5, :]) + mod_ref[3:4, :]).astype(BF16)
    acc = None
    for c in range(d_ff // tf):
        hid = jnp.maximum(_dot(h2, w1_ref[:, c * tf:(c + 1) * tf]), 0.0)
        term = _dot((hid * hid).astype(BF16), w2_ref[c * tf:(c + 1) * tf, :])
        acc = term if acc is None else acc + term
    y = x1 + mod_ref[5:6, :] * acc

    if final:
        y = _rms_rows(y, fg_ref[...])

        @pl.when(i < n_prompt_tiles)
        def _():
            o_refs[0][...] = y

        @pl.when(i >= n_prompt_tiles)
        def _():
            o_refs[1][...] = y
    else:
        o_refs[0][...] = y


def _out_mlp(xs, parts, w_out, wo_layer, mod, layer, g2, w1, w2, n_prompt, sample_len, final_g=None, tm=512,
             tf=1024):
    n_tok = sum(a.shape[0] for a in xs)
    d = xs[0].shape[1]
    d_ff = w1.shape[2]
    final = final_g is not None
    npt = n_prompt // tm
    row = functools.partial(_cond_row, n_prompt_tiles=npt, tiles_per_sample=sample_len // tm)
    in_specs = _token_specs(xs, tm, npt)
    for pair in parts:
        in_specs += _token_specs(pair, tm, npt)
    resident = lambda shape: pl.BlockSpec((None, *shape), lambda i: (layer, 0, 0), pipeline_mode=pl.Buffered(1))
    width = w_out.shape[1] // len(parts)
    in_specs += [pl.BlockSpec((None, width, d), lambda i, p=p: (wo_layer, p, 0), pipeline_mode=pl.Buffered(1))
                 for p in range(len(parts))]
    in_specs += [
        pl.BlockSpec((None, None, 6, d), lambda i: (layer, row(i), 0, 0)),
        pl.BlockSpec((None, 1, d), lambda i: (layer, 0, 0)),
        resident((d, d_ff)),
        resident((d_ff, d)),
    ]
    args = [*xs, *[a for pair in parts for a in pair], *([w_out] * len(parts)), mod, g2, w1, w2]
    if final:
        in_specs += [pl.BlockSpec((1, d), lambda i: (0, 0))]
        args += [final_g.reshape(1, d)]
        out_shape = (jax.ShapeDtypeStruct((n_prompt, d), F32), jax.ShapeDtypeStruct((n_tok - n_prompt, d), F32))
        out_specs = (pl.BlockSpec((tm, d), lambda i: (jnp.minimum(i, npt - 1), 0)),
                     pl.BlockSpec((tm, d), lambda i: (jnp.maximum(i - npt, 0), 0)))
    else:
        out_shape = jax.ShapeDtypeStruct((n_tok, d), F32)
        out_specs = pl.BlockSpec((tm, d), lambda i: (i, 0))
    kern = functools.partial(_out_mlp_kernel, n_x=len(xs), n_parts=len(parts), final=final, n_prompt_tiles=npt,
                             tf=tf)
    return pl.pallas_call(
        kern,
        out_shape=out_shape,
        grid=(n_tok // tm,),
        in_specs=in_specs,
        out_specs=out_specs,
        compiler_params=pltpu.CompilerParams(dimension_semantics=("arbitrary",),
                                             vmem_limit_bytes=VMEM_LIMIT_BYTES),
        name="out_mlp",
    )(*args)


def _rope_tables(seq_len):
    rows = seq_len // GRID_W
    inv = ROPE_THETA ** (-np.arange(ROT_FREQS, dtype=np.float64) / ROT_FREQS)
    ang_r = np.repeat(np.arange(rows, dtype=np.float64), GRID_W)[:, None] * inv
    ang_c = np.tile(np.arange(GRID_W, dtype=np.float64), rows)[:, None] * inv
    cos = np.concatenate([np.cos(ang_r), np.cos(ang_r), np.cos(ang_c), np.cos(ang_c)], axis=-1)
    sin = np.concatenate([-np.sin(ang_r), np.sin(ang_r), -np.sin(ang_c), np.sin(ang_c)], axis=-1)
    return jnp.asarray(cos, dtype=F32), jnp.asarray(sin, dtype=F32)


def kernel(x_prompt, x_sample, state_a, cache_b_kv, cache_c_kv, c, c_ctx, ada_w, ada_b, norm1_g, norm2_g,
           final_g, mlp_w1, mlp_w2, ev_w_in, a_conv, a_log, a_dt_bias, a_norm_g, b_sink, ev_w_out,
           od_w_in, c_qnorm_g, c_knorm_g, od_w_out):
    batch, seq, d = x_prompt.shape
    dec_batch, dec_seq, _ = x_sample.shape
    depth = ada_w.shape[0]
    n_prompt, n_sample = batch * seq, dec_batch * dec_seq
    past = cache_b_kv.shape[3]

    xs = [x_prompt.reshape(n_prompt, d), x_sample.reshape(n_sample, d)]
    cond = jnp.concatenate([c_ctx[None, :], c, jnp.zeros((8 - 1 - dec_batch, d), F32)], axis=0)
    mod = _adaln(cond, ada_w, ada_b).reshape(depth, 8, 6, d)

    cos, sin = _rope_tables(dec_seq)
    s_blk0 = n_prompt // dec_seq
    ctx_b = cache_b_kv.reshape(*cache_b_kv.shape[:4], B_KV_HEADS * HEAD_DIM)
    ctx_c = cache_c_kv.reshape(*cache_c_kv.shape[:4], C_KV_HEADS * HEAD_DIM)
    ev_w_in_bf16 = _ev_w_in(ev_w_in.astype(BF16))
    od_w_in_bf16 = od_w_in.astype(BF16)
    ev_w_out_bf16 = ev_w_out.astype(BF16)
    od_w_out_bf16 = od_w_out.astype(BF16)
    w1 = mlp_w1.astype(BF16)
    w2 = mlp_w2.astype(BF16)
    g1 = norm1_g.reshape(depth, 1, d)
    g2 = norm2_g.reshape(depth, 1, d)
    lane_pad = lambda v: jnp.zeros((1, LANES), F32).at[0, 2 * A_HEADS:4 * A_HEADS].set(v.reshape(-1))

    new_a, new_b, new_c = [], [], []
    for l in range(depth):
        i = l // 2
        fin = final_g if l == depth - 1 else None
        if l % 2 == 0:
            kv_cols = B_KV_HEADS * HEAD_DIM
            proj, (cache_k, cache_v), (beta_all, gc_all) = _in_proj(
                xs, mod, l, g1, ev_w_in_bf16, i, n_prompt, dec_seq,
                cache_cols=(EV_KVB_OFF, EV_KVB_OFF + kv_cols), n_kv=B_KV_HEADS,
                gate_rows=(lane_pad(a_log[i]), lane_pad(a_dt_bias[i])))
            oa_p, st_p = _gdn(proj, beta_all, gc_all, a_conv[i], a_norm_g[i], 0, batch, seq, heads=A_HEADS,
                              steps=seq // CHUNK)
            oa_s, _ = _gdn(proj, beta_all, gc_all, a_conv[i], a_norm_g[i], s_blk0, dec_batch, dec_seq, heads=1,
                           steps=8, s0=state_a, s0_layer=i)
            ob_p = _attention(proj, q_off=EV_QB_OFF, k_off=EV_KVB_OFF, v_off=EV_KVB_OFF + kv_cols,
                              n_heads=B_HEADS, n_kv=B_KV_HEADS, row_block0=0, n_seq=batch, seq_len=seq,
                              tq=seq, sink=b_sink[i])
            ob_s = _attention(proj, q_off=EV_QB_OFF, k_off=EV_KVB_OFF, v_off=EV_KVB_OFF + kv_cols,
                              n_heads=B_HEADS, n_kv=B_KV_HEADS, row_block0=s_blk0, n_seq=dec_batch,
                              seq_len=dec_seq, tq=512,
                              ctx=ctx_b, ctx_layer=i, cos=cos, sin=sin, sink=b_sink[i], window=WINDOW)
            new_a.append(st_p)
            new_b.append(jnp.stack([cache_k.reshape(batch, seq, B_KV_HEADS, HEAD_DIM),
                                    cache_v.reshape(batch, seq, B_KV_HEADS, HEAD_DIM)], axis=1))
            parts = [[oa_p, oa_s], [ob_p, ob_s]]
            w_out = ev_w_out_bf16
        else:
            kv_cols = C_KV_HEADS * HEAD_DIM
            proj, (cache_v,), _ = _in_proj(xs, mod, l, g1, od_w_in_bf16, i, n_prompt, dec_seq,
                                           cache_cols=(OD_V_OFF,), n_kv=C_KV_HEADS)
            oc_p, kn_p = _attention(proj, q_off=0, k_off=OD_K_OFF, v_off=OD_V_OFF, n_heads=C_HEADS,
                                    n_kv=C_KV_HEADS, row_block0=0, n_seq=batch, seq_len=seq, tq=seq,
                                    q_gain=c_qnorm_g[i], k_gain=c_knorm_g[i], emit_k=True)
            oc_s = _attention(proj, q_off=0, k_off=OD_K_OFF, v_off=OD_V_OFF, n_heads=C_HEADS,
                              n_kv=C_KV_HEADS, row_block0=s_blk0, n_seq=dec_batch, seq_len=dec_seq, tq=256,
                              ctx=ctx_c, ctx_layer=i, cos=cos, sin=sin, q_gain=c_qnorm_g[i],
                              k_gain=c_knorm_g[i])
            new_c.append(jnp.stack([kn_p.reshape(batch, seq, C_KV_HEADS, HEAD_DIM),
                                    cache_v.reshape(batch, seq, C_KV_HEADS, HEAD_DIM)], axis=1))
            parts = [[oc_p, oc_s]]
            w_out = od_w_out_bf16
        out = _out_mlp(xs, parts, w_out, i, mod, l, g2, w1, w2, n_prompt, dec_seq, final_g=fin)
        xs = list(out) if fin is not None else [out]

    y_prompt = xs[0].reshape(batch, seq, d)
    y_sample = xs[1].reshape(dec_batch, dec_seq, d)
    return (y_prompt, y_sample, jnp.stack(new_a, axis=1), jnp.stack(new_b, axis=1), jnp.stack(new_c, axis=1))
```

```python
import functools

import numpy as np
import jax
import jax.numpy as jnp
from jax import lax
from jax.experimental import pallas as pl
from jax.experimental.pallas import tpu as pltpu

F32 = jnp.float32
BF16 = jnp.bfloat16

HEAD_DIM = 128
GRID_W = 64
A_HEADS = 4
CHUNK = 64
B_HEADS = 4
B_KV_HEADS = 2
WINDOW = 128
C_HEADS = 8
C_KV_HEADS = 2
ROT_FREQS = HEAD_DIM // 4
ROPE_THETA = 10000.0
EPS = 1e-6
NEG = -1e30
LOG2_E = 1.4426950408889634
A_WIDTH = A_HEADS * HEAD_DIM
B_WIDTH = B_HEADS * HEAD_DIM
C_WIDTH = C_HEADS * HEAD_DIM

LANES = 128
VMEM_LIMIT_BYTES = 56 * 1024 * 1024

EV_GATE_OFF = 3 * A_WIDTH
EV_QB_OFF = 4 * A_WIDTH
EV_KVB_OFF = EV_QB_OFF + B_WIDTH
EV_BG_OFF = EV_KVB_OFF + 2 * B_KV_HEADS * HEAD_DIM
EV_COLS = EV_BG_OFF + LANES
OD_K_OFF = C_WIDTH
OD_V_OFF = C_WIDTH + C_KV_HEADS * HEAD_DIM


def _sigmoid(x):
    return 1.0 / (1.0 + jnp.exp(-x))


def _silu(x):
    return x * _sigmoid(x)


def _softplus(x):
    return jnp.maximum(x, 0.0) + jnp.log1p(jnp.exp(-jnp.abs(x)))


def _rms_rows(x, g):
    return x * lax.rsqrt(jnp.mean(x * x, axis=-1, keepdims=True) + EPS) * g


def _dot(a, b):
    return jnp.dot(a, b, preferred_element_type=F32)


def _dot_nt(a, b):
    return lax.dot_general(a, b, (((1,), (1,)), ((), ())), preferred_element_type=F32)


def _cond_row(i, n_prompt_tiles, tiles_per_sample):
    return jnp.where(i < n_prompt_tiles, 0, 1 + (i - n_prompt_tiles) // tiles_per_sample)


def _split_bf16(x):
    hi = x.astype(BF16)
    lo = (x - hi.astype(F32)).astype(BF16)
    return hi, lo


def _adaln_kernel(cond_ref, w_ref, b_ref, o_ref):
    rows = cond_ref.shape[0]
    s_hi, s_lo = _split_bf16(_silu(cond_ref[...]))
    w_hi, w_lo = _split_bf16(w_ref[0])
    r = _dot(jnp.concatenate([s_hi, s_lo], axis=0), w_hi)
    o_ref[0] = r[:rows] + r[rows:] + _dot(s_hi, w_lo) + b_ref[0]


def _adaln(cond, ada_w, ada_b, tn=1024):
    depth, d, n = ada_w.shape
    rows = cond.shape[0]
    return pl.pallas_call(
        _adaln_kernel,
        out_shape=jax.ShapeDtypeStruct((depth, rows, n), F32),
        grid=(depth, n // tn),
        in_specs=[
            pl.BlockSpec((rows, d), lambda l, j: (0, 0)),
            pl.BlockSpec((1, d, tn), lambda l, j: (l, 0, j)),
            pl.BlockSpec((1, 1, tn), lambda l, j: (l, 0, j)),
        ],
        out_specs=pl.BlockSpec((1, rows, tn), lambda l, j: (l, 0, j)),
        compiler_params=pltpu.CompilerParams(dimension_semantics=("arbitrary", "arbitrary"),
                                             vmem_limit_bytes=VMEM_LIMIT_BYTES),
        name="adaln",
    )(cond, ada_w, ada_b.reshape(depth, 1, n))


def _ev_w_in_kernel(w_ref, o_ref):
    bg_off = 4 * A_WIDTH
    bg_end = bg_off + 4 * A_HEADS
    rows = w_ref.shape[0]
    o_ref[:, :bg_off] = w_ref[:, :bg_off]
    o_ref[:, bg_off:EV_BG_OFF] = w_ref[:, bg_end:]
    o_ref[:, EV_BG_OFF:] = jnp.concatenate([w_ref[:, bg_off:bg_end],
                                            jnp.zeros((rows, LANES - 4 * A_HEADS), BF16)], axis=1)


def _ev_w_in(w, tr=256):
    layers, d, n = w.shape
    return pl.pallas_call(
        _ev_w_in_kernel,
        out_shape=jax.ShapeDtypeStruct((layers, d, EV_COLS), BF16),
        grid=(layers, d // tr),
        in_specs=[pl.BlockSpec((None, tr, n), lambda l, r: (l, r, 0))],
        out_specs=pl.BlockSpec((None, tr, EV_COLS), lambda l, r: (l, r, 0)),
        compiler_params=pltpu.CompilerParams(dimension_semantics=("arbitrary", "arbitrary")),
        name="ev_w_in",
    )(w)


def _token_specs(arrays, tm, n_prompt_tiles):
    if len(arrays) == 1:
        return [pl.BlockSpec((tm, arrays[0].shape[1]), lambda i, *_: (i, 0))]
    return [pl.BlockSpec((tm, arrays[0].shape[1]), lambda i, *_: (jnp.minimum(i, n_prompt_tiles - 1), 0)),
            pl.BlockSpec((tm, arrays[1].shape[1]), lambda i, *_: (jnp.maximum(i - n_prompt_tiles, 0), 0))]


def _token_tile(refs, n_prompt_tiles):
    if len(refs) == 1:
        return refs[0][...]
    return jnp.where(pl.program_id(0) < n_prompt_tiles, refs[0][...], refs[1][...])


def _gate_columns(bg, a_row, dt_row):
    rows = bg.shape[0]
    g = -jnp.exp(a_row) * _softplus(bg + dt_row)
    chunk_row = lax.broadcasted_iota(jnp.int32, (rows, LANES), 0) % CHUNK
    lane = lax.broadcasted_iota(jnp.int32, (rows, LANES), 1)
    pre, suf = g, g
    shift = 1
    while shift < CHUNK:
        pre = pre + jnp.where(chunk_row >= shift, pltpu.roll(pre, shift, 0), 0.0)
        suf = suf + jnp.where(chunk_row < CHUNK - shift, pltpu.roll(suf, rows - shift, 0), 0.0)
        shift *= 2
    return _sigmoid(bg), jnp.where(lane >= 3 * A_HEADS, suf, pre)


def _inproj_kernel(*refs, n_x, n_prompt_tiles, cache_cols, n_kv, gates):
    it = iter(refs)
    x_refs = [next(it) for _ in range(n_x)]
    mod_ref, g_ref, w_ref = next(it), next(it), next(it)
    a_ref, dt_ref = (next(it), next(it)) if gates else (None, None)
    o_ref = next(it)
    cache_refs = [next(it) for _ in cache_cols]
    x = _token_tile(x_refs, n_prompt_tiles)
    h = _rms_rows(x, g_ref[...]) * (1.0 + mod_ref[1:2, :]) + mod_ref[0:1, :]
    o = _dot(h.astype(BF16), w_ref[...])
    o_ref[...] = o
    if gates:
        beta_ref, gc_ref = next(it), next(it)
        beta_ref[...], gc_ref[...] = _gate_columns(o[:, EV_BG_OFF:EV_BG_OFF + LANES], a_ref[...], dt_ref[...])

    @pl.when(pl.program_id(0) < n_prompt_tiles)
    def _():
        tm = o.shape[0]
        for c_ref, col in zip(cache_refs, cache_cols):
            for hd in range(n_kv):
                c_ref[pl.ds(hd, tm, stride=n_kv), :] = o[:, col + hd * HEAD_DIM:col + (hd + 1) * HEAD_DIM]


def _in_proj(xs, mod, layer, g, w, w_layer, n_prompt, sample_len, cache_cols, n_kv, gate_rows=None, tm=512):
    n_tok = sum(a.shape[0] for a in xs)
    d = xs[0].shape[1]
    n = w.shape[2]
    npt = n_prompt // tm
    assert tm % CHUNK == 0
    gates = gate_rows is not None
    row = functools.partial(_cond_row, n_prompt_tiles=npt, tiles_per_sample=sample_len // tm)
    cache_spec = pl.BlockSpec((tm * n_kv, HEAD_DIM), lambda i: (jnp.minimum(i, npt - 1), 0))
    lane_row = pl.BlockSpec((1, LANES), lambda i: (0, 0))
    gate_spec = pl.BlockSpec((tm, LANES), lambda i: (i, 0))
    res = pl.pallas_call(
        functools.partial(_inproj_kernel, n_x=len(xs), n_prompt_tiles=npt, cache_cols=cache_cols, n_kv=n_kv,
                          gates=gates),
        out_shape=(jax.ShapeDtypeStruct((n_tok, n), F32),
                   *[jax.ShapeDtypeStruct((n_prompt * n_kv, HEAD_DIM), F32) for _ in cache_cols],
                   *[jax.ShapeDtypeStruct((n_tok, LANES), F32) for _ in range(2 if gates else 0)]),
        grid=(n_tok // tm,),
        in_specs=_token_specs(xs, tm, npt) + [
            pl.BlockSpec((None, None, 6, d), lambda i: (layer, row(i), 0, 0)),
            pl.BlockSpec((None, 1, d), lambda i: (layer, 0, 0)),
            pl.BlockSpec((None, d, n), lambda i: (w_layer, 0, 0)),
        ] + ([lane_row, lane_row] if gates else []),
        out_specs=(pl.BlockSpec((tm, n), lambda i: (i, 0)), *[cache_spec for _ in cache_cols],
                   *([gate_spec, gate_spec] if gates else [])),
        compiler_params=pltpu.CompilerParams(dimension_semantics=("arbitrary",),
                                             vmem_limit_bytes=VMEM_LIMIT_BYTES),
        name="in_proj",
    )(*xs, mod, g, w, *(gate_rows if gates else ()))
    n_c = len(cache_cols)
    return res[0], res[1:1 + n_c], res[1 + n_c:]


def _gdn_kernel(q_ref, k_ref, v_ref, gate_ref, betaall_ref, gcall_ref, cwq_ref, cwk_ref, cwv_ref, ng_ref, *rest,
                seq_len, heads, steps, has_state):
    s0_ref = rest[0] if has_state else None
    (o_ref, sout_ref, qn_ref, kn_ref, vn_ref, beta_ref, gc_ref, gcrow_ref, s_ref, oacc_ref,
     mq_a, c_a, oo_a, eg_a, mq_b, c_b, oo_b, eg_b) = rest[1:] if has_state else rest
    L = seq_len
    nc = L // CHUNK
    nblk = nc // steps
    h0 = pl.program_id(1) * heads
    row_idx = lax.broadcasted_iota(jnp.int32, (L, LANES), 0)
    lane_idx = lax.broadcasted_iota(jnp.int32, (L, LANES), 1)

    def conv_silu(x, cw):
        prev = jnp.where(row_idx == 0, 0.0, pltpu.roll(x, 1, 0))
        nxt = jnp.where(row_idx == L - 1, 0.0, pltpu.roll(x, L - 1, 0))
        return _silu(prev * cw[0:1, :] + x * cw[1:2, :] + nxt * cw[2:3, :])

    def l2norm(x):
        return x * lax.rsqrt(jnp.sum(x * x, axis=-1, keepdims=True) + EPS)

    beta_all = betaall_ref[...]
    gc_all = gcall_ref[...]

    def pick(x, col):
        col_vals = jnp.sum(jnp.where(lane_idx == col, x, 0.0), axis=1, keepdims=True)
        return jnp.broadcast_to(col_vals, (L, LANES))

    for r in range(L // LANES):
        rows = jnp.transpose(gc_all[r * LANES:(r + 1) * LANES, :])[2 * A_HEADS:4 * A_HEADS, :]
        gcrow_ref[0, r] = rows
        gcrow_ref[1, r] = pltpu.roll(rows, CHUNK, 1)

    for hh in range(heads):
        cols = slice(hh * HEAD_DIM, (hh + 1) * HEAD_DIM)
        qn_ref[:, cols] = l2norm(conv_silu(q_ref[:, cols], cwq_ref[:, cols])) * (HEAD_DIM ** -0.5)
        kn_ref[:, cols] = l2norm(conv_silu(k_ref[:, cols], cwk_ref[:, cols]))
        vn_ref[:, cols] = conv_silu(v_ref[:, cols], cwv_ref[:, cols])
        for d in range(2):
            s_ref[hh, d] = s0_ref[d, hh] if has_state else jnp.zeros((HEAD_DIM, HEAD_DIM), F32)
        head = h0 + hh
        for d in range(2):
            beta_ref[hh, d] = pick(beta_all, d * A_HEADS + head)
            gc_ref[hh, d] = pick(gc_all, (2 + d) * A_HEADS + head)

    r64 = lax.broadcasted_iota(jnp.int32, (CHUNK, LANES), 0)
    l64 = lax.broadcasted_iota(jnp.int32, (CHUNK, LANES), 1)
    is_f = l64 < CHUNK
    is_f_row = lax.broadcasted_iota(jnp.int32, (1, LANES), 1) < CHUNK
    cpos = jnp.where(is_f, l64, l64 - CHUNK)
    ahead = jnp.where(is_f, r64 - cpos, cpos - r64)
    incl = ahead >= 0
    strict = ahead > 0
    eye = jnp.where(r64 == cpos, 1.0, 0.0)
    is_f2 = lax.broadcasted_iota(jnp.int32, (2 * CHUNK, LANES), 1) < CHUNK
    zeros_c = jnp.zeros((CHUNK, HEAD_DIM), F32)

    def aligned(off, m):
        return off if isinstance(off, int) else pl.multiple_of(off, m)

    def half(n):
        return n // 2 if isinstance(n, int) else lax.shift_right_logical(n, 1)

    def block_diag(xp):
        z = jnp.zeros_like(xp)
        return jnp.concatenate([jnp.where(is_f, xp, z), jnp.where(is_f, z, xp)], axis=0)

    def split_dot(a_hi, a_lo, b_hi, b_lo):
        lhs = jnp.concatenate([a_hi, a_lo, a_hi], axis=1)
        rhs = jnp.concatenate([b_hi, b_hi, b_lo], axis=0)
        return _dot(lhs, rhs)

    def prep_stages(blk, slot):
        mq_ref, c_ref, oo_ref, eg_ref = slot
        probs = [(si, hh) for si in range(steps) for hh in range(heads)]
        st = {p: {} for p in probs}

        def load(p):
            si, hh = p
            j = blk * steps + si
            n_b = nc - 1 - j
            cols = slice(hh * HEAD_DIM, (hh + 1) * HEAD_DIM)
            off_f = aligned(j * CHUNK, CHUNK)
            off_b = aligned(n_b * CHUNK, CHUNK)
            q_f, k_f, v_f = (r[pl.ds(off_f, CHUNK), cols] for r in (qn_ref, kn_ref, vn_ref))
            q_b, k_b, v_b = (r[pl.ds(off_b, CHUNK), cols] for r in (qn_ref, kn_ref, vn_ref))
            beta_f = beta_ref[hh, 0, pl.ds(off_f, CHUNK), :]
            beta_b = beta_ref[hh, 1, pl.ds(off_b, CHUNK), :]
            gc_f = gc_ref[hh, 0, pl.ds(off_f, CHUNK), :]
            gc_b = gc_ref[hh, 1, pl.ds(off_b, CHUNK), :]
            row_f = gcrow_ref[si % 2, half(j), pl.ds(h0 + hh, 1), :]
            row_b = gcrow_ref[si % 2, half(n_b), pl.ds(A_HEADS + h0 + hh, 1), :]
            gcrow = jnp.where(is_f_row, row_f, row_b)
            diff = jnp.where(is_f, gc_f, gc_b) - gcrow
            decay = jnp.where(incl, jnp.exp(jnp.where(incl, diff, 0.0)), 0.0)
            kb_f = k_f * beta_f
            kb_b = k_b * beta_b
            e_f = jnp.exp(gc_f)
            e_b = jnp.exp(gc_b)
            glast_f = gc_f[CHUNK - 1:CHUNK, :]
            glast_b = gc_b[0:1, :]
            kdt = jnp.transpose(jnp.concatenate([k_f * jnp.exp(glast_f - gc_f), k_b * jnp.exp(glast_b - gc_b)],
                                                axis=0))
            zk = jnp.zeros_like(kdt)
            eg_ref[si, hh, 0] = jnp.broadcast_to(jnp.exp(glast_f), (8, LANES))
            eg_ref[si, hh, 1] = jnp.broadcast_to(jnp.exp(glast_b), (8, LANES))
            st[p].update(
                decay=decay,
                lhs=jnp.concatenate([jnp.concatenate([kb_f, kb_b], axis=1),
                                     jnp.concatenate([q_f, q_b], axis=1)], axis=0).astype(BF16),
                rhs=jnp.concatenate([jnp.concatenate([k_f, zeros_c], axis=1),
                                     jnp.concatenate([zeros_c, k_b], axis=1)], axis=0).astype(BF16),
                rhs_uw=jnp.concatenate([jnp.concatenate([v_f * beta_f, kb_f * e_f], axis=1),
                                        jnp.concatenate([v_b * beta_b, kb_b * e_b], axis=1)], axis=0).astype(BF16),
                kd2=jnp.concatenate([jnp.where(is_f2, kdt, zk), jnp.where(is_f2, zk, kdt)], axis=0).astype(BF16),
                qe=(q_f * e_f, q_b * e_b))

        def gram(p):
            s = st[p]
            aq = _dot_nt(s.pop("lhs"), s.pop("rhs"))
            decay = s.pop("decay")
            x = -jnp.where(strict, aq[:CHUNK] * decay, 0.0)
            s["qk"] = jnp.where(incl, aq[CHUNK:] * decay, 0.0)
            s["t"] = eye + x
            s["p"] = x

        def square(p):
            s = st[p]
            p_hi, p_lo = _split_bf16(s["p"])
            s["p"] = split_dot(p_hi, p_lo, block_diag(p_hi), block_diag(p_lo))

        def double(p):
            s = st[p]
            p_hi, p_lo = _split_bf16(s["p"])
            t_hi, t_lo = _split_bf16(s["t"])
            r = split_dot(p_hi, p_lo, jnp.concatenate([block_diag(t_hi), block_diag(p_hi)], axis=1),
                          jnp.concatenate([block_diag(t_lo), block_diag(p_lo)], axis=1))
            s["t"] = s["t"] + r[:, :LANES]
            s["p"] = r[:, LANES:]

        def last(p):
            s = st[p]
            p_hi, p_lo = _split_bf16(s.pop("p"))
            t_hi, t_lo = _split_bf16(s["t"])
            s["t"] = s["t"] + split_dot(p_hi, p_lo, block_diag(t_hi), block_diag(t_lo))

        def solve(p):
            s = st[p]
            s["uw"] = _dot(block_diag(s.pop("t")).astype(BF16), s.pop("rhs_uw"))

        def fold(p):
            si, hh = p
            s = st[p]
            lhs3 = jnp.concatenate([s.pop("kd2"), block_diag(s.pop("qk")).astype(BF16)], axis=0)
            big = _dot(lhs3, s.pop("uw").astype(BF16))
            qe = s.pop("qe")
            for d in range(2):
                rows_s = slice(d * 2 * CHUNK, (d + 1) * 2 * CHUNK)
                rows_o = slice(4 * CHUNK + d * CHUNK, 4 * CHUNK + (d + 1) * CHUNK)
                c_ref[si, hh, d] = big[rows_s, :HEAD_DIM]
                oo_ref[si, hh, d] = big[rows_o, :HEAD_DIM]
                mq_ref[si, hh, d] = jnp.concatenate([big[rows_s, HEAD_DIM:], qe[d] - big[rows_o, HEAD_DIM:]],
                                                    axis=0).astype(BF16)

        n_double = 0
        span = 2
        while 2 * span < CHUNK:
            n_double += 1
            span *= 2
        stage_fns = [load, gram, square] + [double] * n_double + [last, solve, fold]
        return [functools.partial(lambda fn: [fn(p) for p in probs], fn) for fn in stage_fns]

    def scan_step(blk, slot, si):
        mq_ref, c_ref, oo_ref, eg_ref = slot
        j = blk * steps + si
        for hh in range(heads):
            cols = slice(hh * HEAD_DIM, (hh + 1) * HEAD_DIM)
            for d, n in ((0, j), (1, nc - 1 - j)):
                s = s_ref[hh, d]
                r = _dot(mq_ref[si, hh, d], s.astype(BF16))
                s_ref[hh, d] = s * eg_ref[si, hh, d][0:1, :] - r[:2 * CHUNK] + c_ref[si, hh, d]
                oacc_ref[d, pl.ds(aligned(n * CHUNK, CHUNK), CHUNK), cols] = r[2 * CHUNK:] + oo_ref[si, hh, d]

    def run(prep, scan):
        stages = prep_stages(*prep) if prep else []
        n_scan = steps if scan else 0
        for k in range(max(len(stages), n_scan)):
            if k < len(stages):
                stages[k]()
            if k < n_scan:
                scan_step(scan[0], scan[1], k)

    slot_a = (mq_a, c_a, oo_a, eg_a)
    slot_b = (mq_b, c_b, oo_b, eg_b)
    run((0, slot_a), None)
    if nblk == 1:
        run(None, (0, slot_a))
    else:
        def body(i, carry):
            blk = 2 * i
            run((blk + 1, slot_b), (blk, slot_a))
            run((blk + 2, slot_a), (blk + 1, slot_b))
            return carry

        lax.fori_loop(0, nblk // 2 - 1, body, 0)
        run((nblk - 1, slot_b), (nblk - 2, slot_a))
        run(None, (nblk - 1, slot_b))

    for hh in range(heads):
        cols = slice(hh * HEAD_DIM, (hh + 1) * HEAD_DIM)
        sout_ref[0, hh] = s_ref[hh, 0]
        sout_ref[1, hh] = s_ref[hh, 1]
        o = oacc_ref[0, :, cols] + oacc_ref[1, :, cols]
        o = _rms_rows(o, ng_ref[...]) * _silu(gate_ref[:, cols])
        o_ref[:, cols] = o.astype(o_ref.dtype)


def _gdn(proj, beta_all, gc_all, conv_w, norm_g, row_block0, n_seq, seq_len, heads, steps, s0=None, s0_layer=0):
    L = seq_len
    nblk = L // CHUNK // steps
    assert L % (CHUNK * steps) == 0 and steps % 2 == 0 and 2 * CHUNK == LANES and A_HEADS % heads == 0
    assert nblk == 1 or nblk % 2 == 0
    width = heads * HEAD_DIM
    groups = A_HEADS // heads
    blk = lambda base: pl.BlockSpec((L, width), lambda b, h: (row_block0 + b, base * groups + h))
    wblk = lambda base: pl.BlockSpec((3, width), lambda b, h: (0, base * groups + h))
    state_spec = pl.BlockSpec((None, 2, heads, HEAD_DIM, HEAD_DIM), lambda b, h: (b, 0, h, 0, 0))
    slot = [pltpu.VMEM((steps, heads, 2, 3 * CHUNK, HEAD_DIM), BF16),
            pltpu.VMEM((steps, heads, 2, 2 * CHUNK, HEAD_DIM), F32),
            pltpu.VMEM((steps, heads, 2, CHUNK, HEAD_DIM), F32),
            pltpu.VMEM((steps, heads, 2, 8, LANES), F32)]
    kern = functools.partial(_gdn_kernel, seq_len=L, heads=heads, steps=steps, has_state=s0 is not None)
    state_in, state_arg = [], []
    if s0 is not None:
        state_in = [pl.BlockSpec((None, None, 2, heads, HEAD_DIM, HEAD_DIM), lambda b, h: (b, s0_layer, 0, h, 0, 0))]
        state_arg = [s0]
    out, s_out = pl.pallas_call(
        kern,
        out_shape=(jax.ShapeDtypeStruct((n_seq * L, A_WIDTH), BF16),
                   jax.ShapeDtypeStruct((n_seq, 2, A_HEADS, HEAD_DIM, HEAD_DIM), F32)),
        grid=(n_seq, groups),
        in_specs=[
            blk(0), blk(1), blk(2), blk(3),
            pl.BlockSpec((L, LANES), lambda b, h: (row_block0 + b, 0)),
            pl.BlockSpec((L, LANES), lambda b, h: (row_block0 + b, 0)),
            wblk(0), wblk(1), wblk(2),
            pl.BlockSpec((1, HEAD_DIM), lambda b, h: (0, 0)),
        ] + state_in,
        out_specs=(pl.BlockSpec((L, width), lambda b, h: (b, h)), state_spec),
        scratch_shapes=[
            pltpu.VMEM((L, width), F32), pltpu.VMEM((L, width), F32), pltpu.VMEM((L, width), F32),
            pltpu.VMEM((heads, 2, L, LANES), F32), pltpu.VMEM((heads, 2, L, LANES), F32),
            pltpu.VMEM((2, L // LANES, 2 * A_HEADS, LANES), F32),
            pltpu.VMEM((heads, 2, HEAD_DIM, HEAD_DIM), F32),
            pltpu.VMEM((2, L, width), F32),
        ] + slot + slot,
        compiler_params=pltpu.CompilerParams(dimension_semantics=("arbitrary", "arbitrary"),
                                             vmem_limit_bytes=VMEM_LIMIT_BYTES),
        name="gdn",
    )(proj, proj, proj, proj, beta_all, gc_all, conv_w, conv_w, conv_w, norm_g.reshape(1, HEAD_DIM), *state_arg)
    return out, s_out


def _rope_rows(x, cos, sin_signed):
    lane = lax.broadcasted_iota(jnp.int32, x.shape, 1)
    swapped = jnp.where((lane % (2 * ROT_FREQS)) < ROT_FREQS,
                        pltpu.roll(x, HEAD_DIM - ROT_FREQS, 1), pltpu.roll(x, ROT_FREQS, 1))
    return x * cos + swapped * sin_signed


def _attn_kernel(*refs, tq, seq_len, group, n_kv, kvs, n_ctx, use_sink, window, qk_norm, rope, emit_k):
    it = iter(refs)
    q_ref, k_ref, v_ref = next(it), next(it), next(it)
    ck_ref = cv_ref = cos_ref = sin_ref = qg_ref = kg_ref = sink_ref = kn_out_ref = None
    if n_ctx:
        ck_ref, cv_ref = next(it), next(it)
    if rope:
        cos_ref, sin_ref = next(it), next(it)
    if qk_norm:
        qg_ref, kg_ref = next(it), next(it)
    if use_sink:
        sink_ref = next(it)
    o_ref = next(it)
    if emit_k:
        kn_out_ref = next(it)
    kbf_ref, vbf_ref = next(it), next(it)
    ckbf_ref = cvbf_ref = None
    if n_ctx:
        ckbf_ref, cvbf_ref = next(it), next(it)

    L = seq_len
    kvg = pl.program_id(1)
    qi = pl.program_id(2)

    def ones_column(rows):
        lane = lax.broadcasted_iota(jnp.int32, (rows, HEAD_DIM), 1)
        return jnp.where(lane == 0, 1.0, 0.0).astype(BF16)

    @pl.when(qi == 0)
    def _():
        for kk in range(kvs):
            cols = slice(kk * HEAD_DIM, (kk + 1) * HEAD_DIM)
            k = k_ref[:, cols]
            if qk_norm:
                k = _rms_rows(k, kg_ref[...])
            if emit_k:
                for hd in range(n_kv):
                    @pl.when(kvg * kvs + kk == hd)
                    def _():
                        kn_out_ref[pl.ds(hd, L, stride=n_kv), :] = k
            if rope:
                k = _rope_rows(k, cos_ref[...], sin_ref[...])
            kbf_ref[kk] = k.astype(BF16)
            vbf_ref[kk, :, :HEAD_DIM] = v_ref[:, cols].astype(BF16)
            vbf_ref[kk, :, HEAD_DIM:] = ones_column(L)
            if n_ctx:
                ckbf_ref[kk] = ck_ref[:, cols].astype(BF16)
                cvbf_ref[kk, :, :HEAD_DIM] = cv_ref[:, cols].astype(BF16)
                cvbf_ref[kk, :, HEAD_DIM:] = ones_column(n_ctx)

    q0 = pl.multiple_of(qi * tq, tq)
    if window:
        wk = tq + 2 * window
        ws = pl.multiple_of(jnp.clip(q0 - window, 0, L - wk), LANES)
        qpos = q0 + lax.broadcasted_iota(jnp.int32, (tq, wk), 0)
        kpos = ws + lax.broadcasted_iota(jnp.int32, (tq, wk), 1)
        valid = jnp.abs(qpos - kpos) <= window

    def scores(g):
        kk = g // group
        q = q_ref[:, g * HEAD_DIM:(g + 1) * HEAD_DIM]
        if qk_norm:
            q = _rms_rows(q, qg_ref[...])
        if rope:
            q = _rope_rows(q, cos_ref[pl.ds(q0, tq), :], sin_ref[pl.ds(q0, tq), :])
        q = (q * (HEAD_DIM ** -0.5 * LOG2_E)).astype(BF16)
        if window:
            s_loc = jnp.where(valid, _dot_nt(q, kbf_ref[kk, pl.ds(ws, wk), :]), NEG)
            pieces = [(s_loc, vbf_ref[kk, pl.ds(ws, wk), :])]
        else:
            pieces = [(_dot_nt(q, kbf_ref[kk]), vbf_ref[kk])]
        if n_ctx:
            pieces.append((_dot_nt(q, ckbf_ref[kk]), cvbf_ref[kk]))
        return pieces

    def finish(g, pieces):
        m = None
        for s, _ in pieces:
            row_max = jnp.max(s, axis=-1, keepdims=True)
            m = row_max if m is None else jnp.maximum(m, row_max)
        if use_sink:
            sink = sink_ref[kvg * (kvs * group) + g] * LOG2_E
            m = jnp.maximum(m, sink)
        acc = None
        for s, v in pieces:
            term = _dot(jnp.exp2(s - m).astype(BF16), v)
            acc = term if acc is None else acc + term
        denom = acc[:, HEAD_DIM:HEAD_DIM + 1]
        if use_sink:
            denom = denom + jnp.exp2(sink - m)
        o_ref[:, g * HEAD_DIM:(g + 1) * HEAD_DIM] = (acc[:, :HEAD_DIM] / denom).astype(o_ref.dtype)

    ahead = scores(0)
    for g in range(kvs * group):
        cur = ahead
        if g + 1 < kvs * group:
            ahead = scores(g + 1)
        finish(g, cur)


def _attention(proj, *, q_off, k_off, v_off, n_heads, n_kv, row_block0, n_seq, seq_len, tq,
               kvs=1, ctx=None, ctx_layer=0, cos=None, sin=None, q_gain=None, k_gain=None, sink=None,
               window=0, emit_k=False):
    L = seq_len
    group = n_heads // n_kv
    n_ctx = 0 if ctx is None else ctx.shape[3]
    rope = cos is not None
    qk_norm = q_gain is not None
    use_sink = sink is not None
    nq = L // tq
    q_w, kv_w = kvs * group * HEAD_DIM, kvs * HEAD_DIM
    assert n_kv % kvs == 0 and q_off % q_w == 0 and k_off % kv_w == 0 and v_off % kv_w == 0
    qb, kb, vb = q_off // q_w, k_off // kv_w, v_off // kv_w
    in_specs = [
        pl.BlockSpec((tq, q_w), lambda b, h, i: ((row_block0 + b) * nq + i, qb + h)),
        pl.BlockSpec((L, kv_w), lambda b, h, i: (row_block0 + b, kb + h)),
        pl.BlockSpec((L, kv_w), lambda b, h, i: (row_block0 + b, vb + h)),
    ]
    args = [proj, proj, proj]
    scratch = [pltpu.VMEM((kvs, L, HEAD_DIM), BF16), pltpu.VMEM((kvs, L, 2 * HEAD_DIM), BF16)]
    if n_ctx:
        in_specs += [pl.BlockSpec((None, None, None, n_ctx, kv_w), lambda b, h, i: (b, ctx_layer, 0, 0, h)),
                     pl.BlockSpec((None, None, None, n_ctx, kv_w), lambda b, h, i: (b, ctx_layer, 1, 0, h))]
        args += [ctx, ctx]
        scratch += [pltpu.VMEM((kvs, n_ctx, HEAD_DIM), BF16), pltpu.VMEM((kvs, n_ctx, 2 * HEAD_DIM), BF16)]
    if rope:
        in_specs += [pl.BlockSpec((L, HEAD_DIM), lambda b, h, i: (0, 0))] * 2
        args += [cos, sin]
    if qk_norm:
        in_specs += [pl.BlockSpec((1, HEAD_DIM), lambda b, h, i: (0, 0))] * 2
        args += [q_gain.reshape(1, HEAD_DIM), k_gain.reshape(1, HEAD_DIM)]
    if use_sink:
        in_specs += [pl.BlockSpec(memory_space=pltpu.SMEM)]
        args += [sink]
    out_shape = [jax.ShapeDtypeStruct((n_seq * L, n_heads * HEAD_DIM), BF16)]
    out_specs = [pl.BlockSpec((tq, q_w), lambda b, h, i: (b * nq + i, h))]
    if emit_k:
        out_shape += [jax.ShapeDtypeStruct((n_seq * L * n_kv, HEAD_DIM), F32)]
        out_specs += [pl.BlockSpec((L * n_kv, HEAD_DIM), lambda b, h, i: (b, 0))]
    kern = functools.partial(_attn_kernel, tq=tq, seq_len=L, group=group, n_kv=n_kv, kvs=kvs, n_ctx=n_ctx,
                             use_sink=use_sink, window=window, qk_norm=qk_norm, rope=rope, emit_k=emit_k)
    res = pl.pallas_call(
        kern,
        out_shape=tuple(out_shape),
        grid=(n_seq, n_kv // kvs, nq),
        in_specs=in_specs,
        out_specs=tuple(out_specs),
        scratch_shapes=scratch,
        compiler_params=pltpu.CompilerParams(dimension_semantics=("arbitrary", "arbitrary", "arbitrary"),
                                             vmem_limit_bytes=VMEM_LIMIT_BYTES),
        name="attention",
    )(*args)
    return res if emit_k else res[0]


def _out_mlp_kernel(*refs, n_x, n_parts, final, n_prompt_tiles, tf):
    it = iter(refs)
    x_refs = [next(it) for _ in range(n_x)]
    a_refs = [[next(it), next(it)] for _ in range(n_parts)]
    wo_refs = [next(it) for _ in range(n_parts)]
    mod_ref, g2_ref, w1_ref, w2_ref = next(it), next(it), next(it), next(it)
    fg_ref = next(it) if final else None
    o_refs = [next(it) for _ in range(2 if final else 1)]
    i = pl.program_id(0)
    d_ff = w1_ref.shape[1]

    mix = None
    for pair, wo_ref in zip(a_refs, wo_refs):
        term = _dot(_token_tile(pair, n_prompt_tiles), wo_ref[...])
        mix = term if mix is None else mix + term
    x1 = _token_tile(x_refs, n_prompt_tiles) + mod_ref[2:3, :] * mix
    h2 = (_rms_rows(x1, g2_ref[...]) * (1.0 + mod_ref[4:5, :]) + mod_ref[3:4, :]).astype(BF16)
    acc = None
    for c in range(d_ff // tf):
        hid = jnp.maximum(_dot(h2, w1_ref[:, c * tf:(c + 1) * tf]), 0.0)
        term = _dot((hid * hid).astype(BF16), w2_ref[c * tf:(c + 1) * tf, :])
        acc = term if acc is None else acc + term
    y = x1 + mod_ref[5:6, :] * acc

    if final:
        y = _rms_rows(y, fg_ref[...])

        @pl.when(i < n_prompt_tiles)
        def _():
            o_refs[0][...] = y

        @pl.when(i >= n_prompt_tiles)
        def _():
            o_refs[1][...] = y
    else:
        o_refs[0][...] = y


def _out_mlp(xs, parts, w_out, wo_layer, mod, layer, g2, w1, w2, n_prompt, sample_len, final_g=None, tm=512,
             tf=1024):
    n_tok = sum(a.shape[0] for a in xs)
    d = xs[0].shape[1]
    d_ff = w1.shape[2]
    final = final_g is not None
    npt = n_prompt // tm
    row = functools.partial(_cond_row, n_prompt_tiles=npt, tiles_per_sample=sample_len // tm)
    in_specs = _token_specs(xs, tm, npt)
    for pair in parts:
        in_specs += _token_specs(pair, tm, npt)
    resident = lambda shape: pl.BlockSpec((None, *shape), lambda i: (layer, 0, 0), pipeline_mode=pl.Buffered(1))
    width = w_out.shape[1] // len(parts)
    in_specs += [pl.BlockSpec((None, width, d), lambda i, p=p: (wo_layer, p, 0), pipeline_mode=pl.Buffered(1))
                 for p in range(len(parts))]
    in_specs += [
        pl.BlockSpec((None, None, 6, d), lambda i: (layer, row(i), 0, 0)),
        pl.BlockSpec((None, 1, d), lambda i: (layer, 0, 0)),
        resident((d, d_ff)),
        resident((d_ff, d)),
    ]
    args = [*xs, *[a for pair in parts for a in pair], *([w_out] * len(parts)), mod, g2, w1, w2]
    if final:
        in_specs += [pl.BlockSpec((1, d), lambda i: (0, 0))]
        args += [final_g.reshape(1, d)]
        out_shape = (jax.ShapeDtypeStruct((n_prompt, d), F32), jax.ShapeDtypeStruct((n_tok - n_prompt, d), F32))
        out_specs = (pl.BlockSpec((tm, d), lambda i: (jnp.minimum(i, npt - 1), 0)),
                     pl.BlockSpec((tm, d), lambda i: (jnp.maximum(i - npt, 0), 0)))
    else:
        out_shape = jax.ShapeDtypeStruct((n_tok, d), F32)
        out_specs = pl.BlockSpec((tm, d), lambda i: (i, 0))
    kern = functools.partial(_out_mlp_kernel, n_x=len(xs), n_parts=len(parts), final=final, n_prompt_tiles=npt,
                             tf=tf)
    return pl.pallas_call(
        kern,
        out_shape=out_shape,
        grid=(n_tok // tm,),
        in_specs=in_specs,
        out_specs=out_specs,
        compiler_params=pltpu.CompilerParams(dimension_semantics=("arbitrary",),
                                             vmem_limit_bytes=VMEM_LIMIT_BYTES),
        name="out_mlp",
    )(*args)


def _rope_tables(seq_len):
    rows = seq_len // GRID_W
    inv = ROPE_THETA ** (-np.arange(ROT_FREQS, dtype=np.float64) / ROT_FREQS)
    ang_r = np.repeat(np.arange(rows, dtype=np.float64), GRID_W)[:, None] * inv
    ang_c = np.tile(np.arange(GRID_W, dtype=np.float64), rows)[:, None] * inv
    cos = np.concatenate([np.cos(ang_r), np.cos(ang_r), np.cos(ang_c), np.cos(ang_c)], axis=-1)
    sin = np.concatenate([-np.sin(ang_r), np.sin(ang_r), -np.sin(ang_c), np.sin(ang_c)], axis=-1)
    return jnp.asarray(cos, dtype=F32), jnp.asarray(sin, dtype=F32)


def kernel(x_prompt, x_sample, state_a, cache_b_kv, cache_c_kv, c, c_ctx, ada_w, ada_b, norm1_g, norm2_g,
           final_g, mlp_w1, mlp_w2, ev_w_in, a_conv, a_log, a_dt_bias, a_norm_g, b_sink, ev_w_out,
           od_w_in, c_qnorm_g, c_knorm_g, od_w_out):
    batch, seq, d = x_prompt.shape
    dec_batch, dec_seq, _ = x_sample.shape
    depth = ada_w.shape[0]
    n_prompt, n_sample = batch * seq, dec_batch * dec_seq
    past = cache_b_kv.shape[3]

    xs = [x_prompt.reshape(n_prompt, d), x_sample.reshape(n_sample, d)]
    cond = jnp.concatenate([c_ctx[None, :], c, jnp.zeros((8 - 1 - dec_batch, d), F32)], axis=0)
    mod = _adaln(cond, ada_w, ada_b).reshape(depth, 8, 6, d)

    cos, sin = _rope_tables(dec_seq)
    s_blk0 = n_prompt // dec_seq
    ctx_b = cache_b_kv.reshape(*cache_b_kv.shape[:4], B_KV_HEADS * HEAD_DIM)
    ctx_c = cache_c_kv.reshape(*cache_c_kv.shape[:4], C_KV_HEADS * HEAD_DIM)
    ev_w_in_bf16 = _ev_w_in(ev_w_in.astype(BF16))
    od_w_in_bf16 = od_w_in.astype(BF16)
    ev_w_out_bf16 = ev_w_out.astype(BF16)
    od_w_out_bf16 = od_w_out.astype(BF16)
    w1 = mlp_w1.astype(BF16)
    w2 = mlp_w2.astype(BF16)
    g1 = norm1_g.reshape(depth, 1, d)
    g2 = norm2_g.reshape(depth, 1, d)
    lane_pad = lambda v: jnp.zeros((1, LANES), F32).at[0, 2 * A_HEADS:4 * A_HEADS].set(v.reshape(-1))

    new_a, new_b, new_c = [], [], []
    for l in range(depth):
        i = l // 2
        fin = final_g if l == depth - 1 else None
        if l % 2 == 0:
            kv_cols = B_KV_HEADS * HEAD_DIM
            proj, (cache_k, cache_v), (beta_all, gc_all) = _in_proj(
                xs, mod, l, g1, ev_w_in_bf16, i, n_prompt, dec_seq,
                cache_cols=(EV_KVB_OFF, EV_KVB_OFF + kv_cols), n_kv=B_KV_HEADS,
                gate_rows=(lane_pad(a_log[i]), lane_pad(a_dt_bias[i])))
            oa_p, st_p = _gdn(proj, beta_all, gc_all, a_conv[i], a_norm_g[i], 0, batch, seq, heads=A_HEADS,
                              steps=seq // CHUNK)
            oa_s, _ = _gdn(proj, beta_all, gc_all, a_conv[i], a_norm_g[i], s_blk0, dec_batch, dec_seq, heads=1,
                           steps=8, s0=state_a, s0_layer=i)
            ob_p = _attention(proj, q_off=EV_QB_OFF, k_off=EV_KVB_OFF, v_off=EV_KVB_OFF + kv_cols,
                              n_heads=B_HEADS, n_kv=B_KV_HEADS, row_block0=0, n_seq=batch, seq_len=seq,
                              tq=seq, kvs=B_KV_HEADS, sink=b_sink[i])
            ob_s = _attention(proj, q_off=EV_QB_OFF, k_off=EV_KVB_OFF, v_off=EV_KVB_OFF + kv_cols,
                              n_heads=B_HEADS, n_kv=B_KV_HEADS, row_block0=s_blk0, n_seq=dec_batch,
                              seq_len=dec_seq, tq=512,
                              ctx=ctx_b, ctx_layer=i, cos=cos, sin=sin, sink=b_sink[i], window=WINDOW)
            new_a.append(st_p)
            new_b.append(jnp.stack([cache_k.reshape(batch, seq, B_KV_HEADS, HEAD_DIM),
                                    cache_v.reshape(batch, seq, B_KV_HEADS, HEAD_DIM)], axis=1))
            parts = [[oa_p, oa_s], [ob_p, ob_s]]
            w_out = ev_w_out_bf16
        else:
            kv_cols = C_KV_HEADS * HEAD_DIM
            proj, (cache_v,), _ = _in_proj(xs, mod, l, g1, od_w_in_bf16, i, n_prompt, dec_seq,
                                           cache_cols=(OD_V_OFF,), n_kv=C_KV_HEADS)
            oc_p, kn_p = _attention(proj, q_off=0, k_off=OD_K_OFF, v_off=OD_V_OFF, n_heads=C_HEADS,
                                    n_kv=C_KV_HEADS, row_block0=0, n_seq=batch, seq_len=seq, tq=seq, kvs=C_KV_HEADS,
                                    q_gain=c_qnorm_g[i], k_gain=c_knorm_g[i], emit_k=True)
            oc_s = _attention(proj, q_off=0, k_off=OD_K_OFF, v_off=OD_V_OFF, n_heads=C_HEADS,
                              n_kv=C_KV_HEADS, row_block0=s_blk0, n_seq=dec_batch, seq_len=dec_seq, tq=256,
                              ctx=ctx_c, ctx_layer=i, cos=cos, sin=sin, q_gain=c_qnorm_g[i],
                              k_gain=c_knorm_g[i])
            new_c.append(jnp.stack([kn_p.reshape(batch, seq, C_KV_HEADS, HEAD_DIM),
                                    cache_v.reshape(batch, seq, C_KV_HEADS, HEAD_DIM)], axis=1))
            parts = [[oc_p, oc_s]]
            w_out = od_w_out_bf16
        out = _out_mlp(xs, parts, w_out, i, mod, l, g2, w1, w2, n_prompt, dec_seq, final_g=fin)
        xs = list(out) if fin is not None else [out]

    y_prompt = xs[0].reshape(batch, seq, d)
    y_sample = xs[1].reshape(dec_batch, dec_seq, d)
    return (y_prompt, y_sample, jnp.stack(new_a, axis=1), jnp.stack(new_b, axis=1), jnp.stack(new_c, axis=1))
```

```python
import functools

import numpy as np
import jax
import jax.numpy as jnp
from jax import lax
from jax.experimental import pallas as pl
from jax.experimental.pallas import tpu as pltpu

F32 = jnp.float32
BF16 = jnp.bfloat16

HEAD_DIM = 128
GRID_W = 64
A_HEADS = 4
CHUNK = 64
B_HEADS = 4
B_KV_HEADS = 2
WINDOW = 128
C_HEADS = 8
C_KV_HEADS = 2
ROT_FREQS = HEAD_DIM // 4
ROPE_THETA = 10000.0
EPS = 1e-6
NEG = -1e30
LOG2_E = 1.4426950408889634
A_WIDTH = A_HEADS * HEAD_DIM
B_WIDTH = B_HEADS * HEAD_DIM
C_WIDTH = C_HEADS * HEAD_DIM

SUBLANES = 8
LANES = 128
VMEM_LIMIT_BYTES = 56 * 1024 * 1024

EV_QB_OFF = 4 * A_WIDTH
EV_KVB_OFF = EV_QB_OFF + B_WIDTH
EV_BG_OFF = EV_KVB_OFF + 2 * B_KV_HEADS * HEAD_DIM
EV_COLS = EV_BG_OFF + LANES
OD_K_OFF = C_WIDTH
OD_V_OFF = C_WIDTH + C_KV_HEADS * HEAD_DIM


def _sigmoid(x):
    return 1.0 / (1.0 + jnp.exp(-x))


def _silu(x):
    return x * _sigmoid(x)


def _softplus(x):
    return jnp.maximum(x, 0.0) + jnp.log1p(jnp.exp(-jnp.abs(x)))


def _rms_rows(x, g):
    return x * lax.rsqrt(jnp.mean(x * x, axis=-1, keepdims=True) + EPS) * g


def _dot(a, b):
    return jnp.dot(a, b, preferred_element_type=F32)


def _dot_nt(a, b):
    return lax.dot_general(a, b, (((1,), (1,)), ((), ())), preferred_element_type=F32)


def _cond_row(i, n_prompt_tiles, tiles_per_sample):
    return jnp.where(i < n_prompt_tiles, 0, 1 + (i - n_prompt_tiles) // tiles_per_sample)


def _split_bf16(x):
    hi = x.astype(BF16)
    lo = (x - hi.astype(F32)).astype(BF16)
    return hi, lo


def _adaln_kernel(cond_ref, w_ref, b_ref, o_ref):
    rows = cond_ref.shape[0]
    s_hi, s_lo = _split_bf16(_silu(cond_ref[...]))
    w_hi, w_lo = _split_bf16(w_ref[0])
    r = _dot(jnp.concatenate([s_hi, s_lo], axis=0), w_hi)
    o_ref[0] = r[:rows] + r[rows:] + _dot(s_hi, w_lo) + b_ref[0]


def _adaln(cond, ada_w, ada_b, tn=2048):
    depth, d, n = ada_w.shape
    rows = cond.shape[0]
    return pl.pallas_call(
        _adaln_kernel,
        out_shape=jax.ShapeDtypeStruct((depth, rows, n), F32),
        grid=(depth, n // tn),
        in_specs=[
            pl.BlockSpec((rows, d), lambda l, j: (0, 0)),
            pl.BlockSpec((1, d, tn), lambda l, j: (l, 0, j)),
            pl.BlockSpec((1, 1, tn), lambda l, j: (l, 0, j)),
        ],
        out_specs=pl.BlockSpec((1, rows, tn), lambda l, j: (l, 0, j)),
        compiler_params=pltpu.CompilerParams(dimension_semantics=("arbitrary", "arbitrary"),
                                             vmem_limit_bytes=VMEM_LIMIT_BYTES),
        name="adaln",
    )(cond, ada_w, ada_b.reshape(depth, 1, n))


def _ev_w_in_kernel(w_ref, o_ref):
    bg_off = 4 * A_WIDTH
    bg_end = bg_off + 4 * A_HEADS
    rows = w_ref.shape[0]
    o_ref[:, :bg_off] = w_ref[:, :bg_off]
    o_ref[:, bg_off:EV_BG_OFF] = w_ref[:, bg_end:]
    o_ref[:, EV_BG_OFF:] = jnp.concatenate([w_ref[:, bg_off:bg_end],
                                            jnp.zeros((rows, LANES - 4 * A_HEADS), BF16)], axis=1)


def _ev_w_in(w, tr=256):
    layers, d, n = w.shape
    return pl.pallas_call(
        _ev_w_in_kernel,
        out_shape=jax.ShapeDtypeStruct((layers, d, EV_COLS), BF16),
        grid=(layers, d // tr),
        in_specs=[pl.BlockSpec((None, tr, n), lambda l, r: (l, r, 0))],
        out_specs=pl.BlockSpec((None, tr, EV_COLS), lambda l, r: (l, r, 0)),
        compiler_params=pltpu.CompilerParams(dimension_semantics=("arbitrary", "arbitrary")),
        name="ev_w_in",
    )(w)


def _token_specs(arrays, tm, n_prompt_tiles):
    if len(arrays) == 1:
        return [pl.BlockSpec((tm, arrays[0].shape[1]), lambda i, *_: (i, 0))]
    return [pl.BlockSpec((tm, arrays[0].shape[1]), lambda i, *_: (jnp.minimum(i, n_prompt_tiles - 1), 0)),
            pl.BlockSpec((tm, arrays[1].shape[1]), lambda i, *_: (jnp.maximum(i - n_prompt_tiles, 0), 0))]


def _token_tile(refs, n_prompt_tiles):
    if len(refs) == 1:
        return refs[0][...]
    return jnp.where(pl.program_id(0) < n_prompt_tiles, refs[0][...], refs[1][...])


def _gate_columns(bg, a_row, dt_row):
    rows = bg.shape[0]
    g = -jnp.exp(a_row) * _softplus(bg + dt_row)
    chunk_row = lax.broadcasted_iota(jnp.int32, (rows, LANES), 0) % CHUNK
    lane = lax.broadcasted_iota(jnp.int32, (rows, LANES), 1)
    pre, suf = g, g
    shift = 1
    while shift < CHUNK:
        pre = pre + jnp.where(chunk_row >= shift, pltpu.roll(pre, shift, 0), 0.0)
        suf = suf + jnp.where(chunk_row < CHUNK - shift, pltpu.roll(suf, rows - shift, 0), 0.0)
        shift *= 2
    return _sigmoid(bg), jnp.where(lane >= 3 * A_HEADS, suf, pre)


def _inproj_kernel(*refs, n_x, n_prompt_tiles, cache_cols, n_kv, gates):
    it = iter(refs)
    x_refs = [next(it) for _ in range(n_x)]
    mod_ref, g_ref, w_ref = next(it), next(it), next(it)
    a_ref, dt_ref = (next(it), next(it)) if gates else (None, None)
    o_ref = next(it)
    cache_refs = [next(it) for _ in cache_cols]
    x = _token_tile(x_refs, n_prompt_tiles)
    h = _rms_rows(x, g_ref[...]) * (1.0 + mod_ref[1:2, :]) + mod_ref[0:1, :]
    o = _dot(h.astype(BF16), w_ref[...])
    o_ref[...] = o
    if gates:
        beta_ref, gc_ref = next(it), next(it)
        beta_ref[...], gc_ref[...] = _gate_columns(o[:, EV_BG_OFF:EV_BG_OFF + LANES], a_ref[...], dt_ref[...])

    @pl.when(pl.program_id(0) < n_prompt_tiles)
    def _():
        tm = o.shape[0]
        for c_ref, col in zip(cache_refs, cache_cols):
            for hd in range(n_kv):
                c_ref[pl.ds(hd, tm, stride=n_kv), :] = o[:, col + hd * HEAD_DIM:col + (hd + 1) * HEAD_DIM]


def _in_proj(xs, mod, layer, g, w, w_layer, n_prompt, sample_len, cache_cols, n_kv, gate_rows=None, tm=512):
    n_tok = sum(a.shape[0] for a in xs)
    d = xs[0].shape[1]
    n = w.shape[2]
    npt = n_prompt // tm
    assert tm % CHUNK == 0
    gates = gate_rows is not None
    row = functools.partial(_cond_row, n_prompt_tiles=npt, tiles_per_sample=sample_len // tm)
    cache_spec = pl.BlockSpec((tm * n_kv, HEAD_DIM), lambda i: (jnp.minimum(i, npt - 1), 0))
    lane_row = pl.BlockSpec((1, LANES), lambda i: (0, 0))
    gate_spec = pl.BlockSpec((tm, LANES), lambda i: (i, 0))
    res = pl.pallas_call(
        functools.partial(_inproj_kernel, n_x=len(xs), n_prompt_tiles=npt, cache_cols=cache_cols, n_kv=n_kv,
                          gates=gates),
        out_shape=(jax.ShapeDtypeStruct((n_tok, n), F32),
                   *[jax.ShapeDtypeStruct((n_prompt * n_kv, HEAD_DIM), F32) for _ in cache_cols],
                   *[jax.ShapeDtypeStruct((n_tok, LANES), F32) for _ in range(2 if gates else 0)]),
        grid=(n_tok // tm,),
        in_specs=_token_specs(xs, tm, npt) + [
            pl.BlockSpec((None, None, 6, d), lambda i: (layer, row(i), 0, 0)),
            pl.BlockSpec((None, 1, d), lambda i: (layer, 0, 0)),
            pl.BlockSpec((None, d, n), lambda i: (w_layer, 0, 0)),
        ] + ([lane_row, lane_row] if gates else []),
        out_specs=(pl.BlockSpec((tm, n), lambda i: (i, 0)), *[cache_spec for _ in cache_cols],
                   *([gate_spec, gate_spec] if gates else [])),
        compiler_params=pltpu.CompilerParams(dimension_semantics=("arbitrary",),
                                             vmem_limit_bytes=VMEM_LIMIT_BYTES),
        name="in_proj",
    )(*xs, mod, g, w, *(gate_rows if gates else ()))
    n_c = len(cache_cols)
    return res[0], res[1:1 + n_c], res[1 + n_c:]


def _gdn_kernel(q_ref, k_ref, v_ref, gate_ref, betaall_ref, gcall_ref, cwq_ref, cwk_ref, cwv_ref, ng_ref, *rest,
                seq_len, heads, steps, has_state):
    s0_ref = rest[0] if has_state else None
    (o_ref, sout_ref, qn_ref, kn_ref, vn_ref, beta_ref, gc_ref, gcrow_ref, s_ref, oacc_ref,
     mq_a, c_a, oo_a, eg_a, mq_b, c_b, oo_b, eg_b) = rest[1:] if has_state else rest
    L = seq_len
    nc = L // CHUNK
    nblk = nc // steps
    h0 = pl.program_id(1) * heads
    row_idx = lax.broadcasted_iota(jnp.int32, (L, LANES), 0)
    lane_idx = lax.broadcasted_iota(jnp.int32, (L, LANES), 1)

    def conv_silu(x, cw):
        prev = jnp.where(row_idx == 0, 0.0, pltpu.roll(x, 1, 0))
        nxt = jnp.where(row_idx == L - 1, 0.0, pltpu.roll(x, L - 1, 0))
        return _silu(prev * cw[0:1, :] + x * cw[1:2, :] + nxt * cw[2:3, :])

    def l2norm(x):
        return x * lax.rsqrt(jnp.sum(x * x, axis=-1, keepdims=True) + EPS)

    beta_all = betaall_ref[...]
    gc_all = gcall_ref[...]

    def pick(x, col):
        col_vals = jnp.sum(jnp.where(lane_idx == col, x, 0.0), axis=1, keepdims=True)
        return jnp.broadcast_to(col_vals, (L, LANES))

    for r in range(L // LANES):
        rows = jnp.transpose(gc_all[r * LANES:(r + 1) * LANES, :])[2 * A_HEADS:4 * A_HEADS, :]
        gcrow_ref[0, r] = rows
        gcrow_ref[1, r] = pltpu.roll(rows, CHUNK, 1)

    for hh in range(heads):
        cols = slice(hh * HEAD_DIM, (hh + 1) * HEAD_DIM)
        qn_ref[:, cols] = l2norm(conv_silu(q_ref[:, cols], cwq_ref[:, cols])) * (HEAD_DIM ** -0.5)
        kn_ref[:, cols] = l2norm(conv_silu(k_ref[:, cols], cwk_ref[:, cols]))
        vn_ref[:, cols] = conv_silu(v_ref[:, cols], cwv_ref[:, cols])
        for d in range(2):
            s_ref[hh, d] = s0_ref[d, hh] if has_state else jnp.zeros((HEAD_DIM, HEAD_DIM), F32)
        head = h0 + hh
        for d in range(2):
            beta_ref[hh, d] = pick(beta_all, d * A_HEADS + head)
            gc_ref[hh, d] = pick(gc_all, (2 + d) * A_HEADS + head)

    r64 = lax.broadcasted_iota(jnp.int32, (CHUNK, LANES), 0)
    l64 = lax.broadcasted_iota(jnp.int32, (CHUNK, LANES), 1)
    is_f = l64 < CHUNK
    is_f_row = lax.broadcasted_iota(jnp.int32, (1, LANES), 1) < CHUNK
    cpos = jnp.where(is_f, l64, l64 - CHUNK)
    ahead = jnp.where(is_f, r64 - cpos, cpos - r64)
    incl = ahead >= 0
    strict = ahead > 0
    eye = jnp.where(r64 == cpos, 1.0, 0.0)
    is_f2 = lax.broadcasted_iota(jnp.int32, (2 * CHUNK, LANES), 1) < CHUNK
    zeros_c = jnp.zeros((CHUNK, HEAD_DIM), F32)

    def aligned(off, m):
        return off if isinstance(off, int) else pl.multiple_of(off, m)

    def half(n):
        return n // 2 if isinstance(n, int) else lax.shift_right_logical(n, 1)

    def block_diag(xp):
        z = jnp.zeros_like(xp)
        return jnp.concatenate([jnp.where(is_f, xp, z), jnp.where(is_f, z, xp)], axis=0)

    def split_dot(a_hi, a_lo, b_hi, b_lo):
        lhs = jnp.concatenate([a_hi, a_lo, a_hi], axis=1)
        rhs = jnp.concatenate([b_hi, b_hi, b_lo], axis=0)
        return _dot(lhs, rhs)

    def prep_stages(blk, slot):
        mq_ref, c_ref, oo_ref, eg_ref = slot
        probs = [(si, hh) for si in range(steps) for hh in range(heads)]
        st = {p: {} for p in probs}

        def load(p):
            si, hh = p
            j = blk * steps + si
            n_b = nc - 1 - j
            cols = slice(hh * HEAD_DIM, (hh + 1) * HEAD_DIM)
            off_f = aligned(j * CHUNK, CHUNK)
            off_b = aligned(n_b * CHUNK, CHUNK)
            q_f, k_f, v_f = (r[pl.ds(off_f, CHUNK), cols] for r in (qn_ref, kn_ref, vn_ref))
            q_b, k_b, v_b = (r[pl.ds(off_b, CHUNK), cols] for r in (qn_ref, kn_ref, vn_ref))
            beta_f = beta_ref[hh, 0, pl.ds(off_f, CHUNK), :]
            beta_b = beta_ref[hh, 1, pl.ds(off_b, CHUNK), :]
            gc_f = gc_ref[hh, 0, pl.ds(off_f, CHUNK), :]
            gc_b = gc_ref[hh, 1, pl.ds(off_b, CHUNK), :]
            row_f = gcrow_ref[si % 2, half(j), pl.ds(h0 + hh, 1), :]
            row_b = gcrow_ref[si % 2, half(n_b), pl.ds(A_HEADS + h0 + hh, 1), :]
            gcrow = jnp.where(is_f_row, row_f, row_b)
            diff = jnp.where(is_f, gc_f, gc_b) - gcrow
            decay = jnp.where(incl, jnp.exp(jnp.where(incl, diff, 0.0)), 0.0)
            kb_f = k_f * beta_f
            kb_b = k_b * beta_b
            e_f = jnp.exp(gc_f)
            e_b = jnp.exp(gc_b)
            glast_f = gc_f[CHUNK - 1:CHUNK, :]
            glast_b = gc_b[0:1, :]
            kdt = jnp.transpose(jnp.concatenate([k_f * jnp.exp(glast_f - gc_f), k_b * jnp.exp(glast_b - gc_b)],
                                                axis=0))
            zk = jnp.zeros_like(kdt)
            eg_ref[si, hh, 0] = jnp.broadcast_to(jnp.exp(glast_f), (SUBLANES, LANES))
            eg_ref[si, hh, 1] = jnp.broadcast_to(jnp.exp(glast_b), (SUBLANES, LANES))
            st[p].update(
                decay=decay,
                lhs=jnp.concatenate([jnp.concatenate([kb_f, kb_b], axis=1),
                                     jnp.concatenate([q_f, q_b], axis=1)], axis=0).astype(BF16),
                rhs=jnp.concatenate([jnp.concatenate([k_f, zeros_c], axis=1),
                                     jnp.concatenate([zeros_c, k_b], axis=1)], axis=0).astype(BF16),
                rhs_uw=jnp.concatenate([jnp.concatenate([v_f * beta_f, kb_f * e_f], axis=1),
                                        jnp.concatenate([v_b * beta_b, kb_b * e_b], axis=1)], axis=0).astype(BF16),
                kd2=jnp.concatenate([jnp.where(is_f2, kdt, zk), jnp.where(is_f2, zk, kdt)], axis=0).astype(BF16),
                qe=(q_f * e_f, q_b * e_b))

        def gram(p):
            s = st[p]
            aq = _dot_nt(s.pop("lhs"), s.pop("rhs"))
            decay = s.pop("decay")
            x = -jnp.where(strict, aq[:CHUNK] * decay, 0.0)
            s["qk"] = jnp.where(incl, aq[CHUNK:] * decay, 0.0)
            s["t"] = eye + x
            s["p"] = x

        def square(p):
            s = st[p]
            p_hi, p_lo = _split_bf16(s["p"])
            s["p"] = split_dot(p_hi, p_lo, block_diag(p_hi), block_diag(p_lo))

        def double(p):
            s = st[p]
            p_hi, p_lo = _split_bf16(s["p"])
            t_hi, t_lo = _split_bf16(s["t"])
            r = split_dot(p_hi, p_lo, jnp.concatenate([block_diag(t_hi), block_diag(p_hi)], axis=1),
                          jnp.concatenate([block_diag(t_lo), block_diag(p_lo)], axis=1))
            s["t"] = s["t"] + r[:, :LANES]
            s["p"] = r[:, LANES:]

        def last(p):
            s = st[p]
            p_hi, p_lo = _split_bf16(s.pop("p"))
            t_hi, t_lo = _split_bf16(s["t"])
            s["t"] = s["t"] + split_dot(p_hi, p_lo, block_diag(t_hi), block_diag(t_lo))

        def solve(p):
            s = st[p]
            s["uw"] = _dot(block_diag(s.pop("t")).astype(BF16), s.pop("rhs_uw"))

        def fold(p):
            si, hh = p
            s = st[p]
            lhs3 = jnp.concatenate([s.pop("kd2"), block_diag(s.pop("qk")).astype(BF16)], axis=0)
            big = _dot(lhs3, s.pop("uw").astype(BF16))
            qe = s.pop("qe")
            for d in range(2):
                rows_s = slice(d * 2 * CHUNK, (d + 1) * 2 * CHUNK)
                rows_o = slice(4 * CHUNK + d * CHUNK, 4 * CHUNK + (d + 1) * CHUNK)
                c_ref[si, hh, d] = big[rows_s, :HEAD_DIM]
                oo_ref[si, hh, d] = big[rows_o, :HEAD_DIM]
                mq_ref[si, hh, d] = jnp.concatenate([big[rows_s, HEAD_DIM:], qe[d] - big[rows_o, HEAD_DIM:]],
                                                    axis=0).astype(BF16)

        n_double = 0
        span = 2
        while 2 * span < CHUNK:
            n_double += 1
            span *= 2
        stage_fns = [load, gram, square] + [double] * n_double + [last, solve, fold]
        return [functools.partial(lambda fn: [fn(p) for p in probs], fn) for fn in stage_fns]

    def scan_step(blk, slot, si):
        mq_ref, c_ref, oo_ref, eg_ref = slot
        j = blk * steps + si
        for hh in range(heads):
            cols = slice(hh * HEAD_DIM, (hh + 1) * HEAD_DIM)
            for d, n in ((0, j), (1, nc - 1 - j)):
                s = s_ref[hh, d]
                r = _dot(mq_ref[si, hh, d], s.astype(BF16))
                s_ref[hh, d] = s * eg_ref[si, hh, d][0:1, :] - r[:2 * CHUNK] + c_ref[si, hh, d]
                oacc_ref[d, pl.ds(aligned(n * CHUNK, CHUNK), CHUNK), cols] = r[2 * CHUNK:] + oo_ref[si, hh, d]

    def run(prep, scan):
        stages = prep_stages(*prep) if prep else []
        n_scan = steps if scan else 0
        for k in range(max(len(stages), n_scan)):
            if k < len(stages):
                stages[k]()
            if k < n_scan:
                scan_step(scan[0], scan[1], k)

    slot_a = (mq_a, c_a, oo_a, eg_a)
    slot_b = (mq_b, c_b, oo_b, eg_b)
    run((0, slot_a), None)
    if nblk == 1:
        run(None, (0, slot_a))
    else:
        def body(i, carry):
            blk = 2 * i
            run((blk + 1, slot_b), (blk, slot_a))
            run((blk + 2, slot_a), (blk + 1, slot_b))
            return carry

        lax.fori_loop(0, nblk // 2 - 1, body, 0)
        run((nblk - 1, slot_b), (nblk - 2, slot_a))
        run(None, (nblk - 1, slot_b))

    for hh in range(heads):
        cols = slice(hh * HEAD_DIM, (hh + 1) * HEAD_DIM)
        sout_ref[0, hh] = s_ref[hh, 0]
        sout_ref[1, hh] = s_ref[hh, 1]
        o = oacc_ref[0, :, cols] + oacc_ref[1, :, cols]
        o = _rms_rows(o, ng_ref[...]) * _silu(gate_ref[:, cols])
        o_ref[:, cols] = o.astype(o_ref.dtype)


def _gdn(proj, beta_all, gc_all, conv_w, norm_g, row_block0, n_seq, seq_len, heads, steps, s0=None, s0_layer=0):
    L = seq_len
    nblk = L // CHUNK // steps
    assert L % (CHUNK * steps) == 0 and steps % 2 == 0 and 2 * CHUNK == LANES and A_HEADS % heads == 0
    assert nblk == 1 or nblk % 2 == 0
    width = heads * HEAD_DIM
    groups = A_HEADS // heads
    blk = lambda base: pl.BlockSpec((L, width), lambda b, h: (row_block0 + b, base * groups + h))
    wblk = lambda base: pl.BlockSpec((3, width), lambda b, h: (0, base * groups + h))
    state_spec = pl.BlockSpec((None, 2, heads, HEAD_DIM, HEAD_DIM), lambda b, h: (b, 0, h, 0, 0))
    slot = [pltpu.VMEM((steps, heads, 2, 3 * CHUNK, HEAD_DIM), BF16),
            pltpu.VMEM((steps, heads, 2, 2 * CHUNK, HEAD_DIM), F32),
            pltpu.VMEM((steps, heads, 2, CHUNK, HEAD_DIM), F32),
            pltpu.VMEM((steps, heads, 2, SUBLANES, LANES), F32)]
    kern = functools.partial(_gdn_kernel, seq_len=L, heads=heads, steps=steps, has_state=s0 is not None)
    state_in, state_arg = [], []
    if s0 is not None:
        state_in = [pl.BlockSpec((None, None, 2, heads, HEAD_DIM, HEAD_DIM), lambda b, h: (b, s0_layer, 0, h, 0, 0))]
        state_arg = [s0]
    out, s_out = pl.pallas_call(
        kern,
        out_shape=(jax.ShapeDtypeStruct((n_seq * L, A_WIDTH), BF16),
                   jax.ShapeDtypeStruct((n_seq, 2, A_HEADS, HEAD_DIM, HEAD_DIM), F32)),
        grid=(n_seq, groups),
        in_specs=[
            blk(0), blk(1), blk(2), blk(3),
            pl.BlockSpec((L, LANES), lambda b, h: (row_block0 + b, 0)),
            pl.BlockSpec((L, LANES), lambda b, h: (row_block0 + b, 0)),
            wblk(0), wblk(1), wblk(2),
            pl.BlockSpec((1, HEAD_DIM), lambda b, h: (0, 0)),
        ] + state_in,
        out_specs=(pl.BlockSpec((L, width), lambda b, h: (b, h)), state_spec),
        scratch_shapes=[
            pltpu.VMEM((L, width), F32), pltpu.VMEM((L, width), F32), pltpu.VMEM((L, width), F32),
            pltpu.VMEM((heads, 2, L, LANES), F32), pltpu.VMEM((heads, 2, L, LANES), F32),
            pltpu.VMEM((2, L // LANES, 2 * A_HEADS, LANES), F32),
            pltpu.VMEM((heads, 2, HEAD_DIM, HEAD_DIM), F32),
            pltpu.VMEM((2, L, width), F32),
        ] + slot + slot,
        compiler_params=pltpu.CompilerParams(dimension_semantics=("arbitrary", "arbitrary"),
                                             vmem_limit_bytes=VMEM_LIMIT_BYTES),
        name="gdn",
    )(proj, proj, proj, proj, beta_all, gc_all, conv_w, conv_w, conv_w, norm_g.reshape(1, HEAD_DIM), *state_arg)
    return out, s_out


def _rope_rows(x, cos, sin_signed):
    lane = lax.broadcasted_iota(jnp.int32, x.shape, 1)
    swapped = jnp.where((lane % (2 * ROT_FREQS)) < ROT_FREQS,
                        pltpu.roll(x, HEAD_DIM - ROT_FREQS, 1), pltpu.roll(x, ROT_FREQS, 1))
    return x * cos + swapped * sin_signed


def _attn_kernel(*refs, tq, seq_len, group, n_kv, kvs, n_ctx, use_sink, window, qk_norm, rope, emit_k):
    it = iter(refs)
    q_ref, k_ref, v_ref = next(it), next(it), next(it)
    ck_ref = cv_ref = cos_ref = sin_ref = qg_ref = kg_ref = sink_ref = kn_out_ref = None
    if n_ctx:
        ck_ref, cv_ref = next(it), next(it)
    if rope:
        cos_ref, sin_ref = next(it), next(it)
    if qk_norm:
        qg_ref, kg_ref = next(it), next(it)
    if use_sink:
        sink_ref = next(it)
    o_ref = next(it)
    if emit_k:
        kn_out_ref = next(it)
    kbf_ref, vbf_ref = next(it), next(it)
    ckbf_ref = cvbf_ref = None
    if n_ctx:
        ckbf_ref, cvbf_ref = next(it), next(it)

    L = seq_len
    kvg = pl.program_id(1)
    qi = pl.program_id(2)

    def ones_column(rows):
        lane = lax.broadcasted_iota(jnp.int32, (rows, HEAD_DIM), 1)
        return jnp.where(lane == 0, 1.0, 0.0).astype(BF16)

    @pl.when(qi == 0)
    def _():
        for kk in range(kvs):
            cols = slice(kk * HEAD_DIM, (kk + 1) * HEAD_DIM)
            k = k_ref[:, cols]
            if qk_norm:
                k = _rms_rows(k, kg_ref[...])
            if emit_k:
                for hd in range(n_kv):
                    @pl.when(kvg * kvs + kk == hd)
                    def _():
                        kn_out_ref[pl.ds(hd, L, stride=n_kv), :] = k
            if rope:
                k = _rope_rows(k, cos_ref[...], sin_ref[...])
            kbf_ref[kk] = k.astype(BF16)
            vbf_ref[kk, :, :HEAD_DIM] = v_ref[:, cols].astype(BF16)
            vbf_ref[kk, :, HEAD_DIM:] = ones_column(L)
            if n_ctx:
                ckbf_ref[kk] = ck_ref[:, cols].astype(BF16)
                cvbf_ref[kk, :, :HEAD_DIM] = cv_ref[:, cols].astype(BF16)
                cvbf_ref[kk, :, HEAD_DIM:] = ones_column(n_ctx)

    q0 = pl.multiple_of(qi * tq, tq)
    if window:
        wk = tq + 2 * window
        ws = pl.multiple_of(jnp.clip(q0 - window, 0, L - wk), LANES)
        qpos = q0 + lax.broadcasted_iota(jnp.int32, (tq, wk), 0)
        kpos = ws + lax.broadcasted_iota(jnp.int32, (tq, wk), 1)
        valid = jnp.abs(qpos - kpos) <= window

    def scores(g):
        kk = g // group
        q = q_ref[:, g * HEAD_DIM:(g + 1) * HEAD_DIM]
        if qk_norm:
            q = _rms_rows(q, qg_ref[...])
        if rope:
            q = _rope_rows(q, cos_ref[pl.ds(q0, tq), :], sin_ref[pl.ds(q0, tq), :])
        q = (q * (HEAD_DIM ** -0.5 * LOG2_E)).astype(BF16)
        if window:
            s_loc = jnp.where(valid, _dot_nt(q, kbf_ref[kk, pl.ds(ws, wk), :]), NEG)
            pieces = [(s_loc, vbf_ref[kk, pl.ds(ws, wk), :])]
        else:
            pieces = [(_dot_nt(q, kbf_ref[kk]), vbf_ref[kk])]
        if n_ctx:
            pieces.append((_dot_nt(q, ckbf_ref[kk]), cvbf_ref[kk]))
        return pieces

    def finish(g, pieces):
        m = None
        for s, _ in pieces:
            row_max = jnp.max(s, axis=-1, keepdims=True)
            m = row_max if m is None else jnp.maximum(m, row_max)
        if use_sink:
            sink = sink_ref[kvg * (kvs * group) + g] * LOG2_E
            m = jnp.maximum(m, sink)
        acc = None
        for s, v in pieces:
            term = _dot(jnp.exp2(s - m).astype(BF16), v)
            acc = term if acc is None else acc + term
        denom = acc[:, HEAD_DIM:HEAD_DIM + 1]
        if use_sink:
            denom = denom + jnp.exp2(sink - m)
        o_ref[:, g * HEAD_DIM:(g + 1) * HEAD_DIM] = (acc[:, :HEAD_DIM] / denom).astype(o_ref.dtype)

    ahead = scores(0)
    for g in range(kvs * group):
        cur = ahead
        if g + 1 < kvs * group:
            ahead = scores(g + 1)
        finish(g, cur)


def _attention(proj, *, q_off, k_off, v_off, n_heads, n_kv, row_block0, n_seq, seq_len, tq,
               kvs=1, ctx=None, ctx_layer=0, cos=None, sin=None, q_gain=None, k_gain=None, sink=None,
               window=0, emit_k=False):
    L = seq_len
    group = n_heads // n_kv
    n_ctx = 0 if ctx is None else ctx.shape[3]
    rope = cos is not None
    qk_norm = q_gain is not None
    use_sink = sink is not None
    nq = L // tq
    q_w, kv_w = kvs * group * HEAD_DIM, kvs * HEAD_DIM
    assert n_kv % kvs == 0 and q_off % q_w == 0 and k_off % kv_w == 0 and v_off % kv_w == 0
    qb, kb, vb = q_off // q_w, k_off // kv_w, v_off // kv_w
    in_specs = [
        pl.BlockSpec((tq, q_w), lambda b, h, i: ((row_block0 + b) * nq + i, qb + h)),
        pl.BlockSpec((L, kv_w), lambda b, h, i: (row_block0 + b, kb + h)),
        pl.BlockSpec((L, kv_w), lambda b, h, i: (row_block0 + b, vb + h)),
    ]
    args = [proj, proj, proj]
    scratch = [pltpu.VMEM((kvs, L, HEAD_DIM), BF16), pltpu.VMEM((kvs, L, 2 * HEAD_DIM), BF16)]
    if n_ctx:
        in_specs += [pl.BlockSpec((None, None, None, n_ctx, kv_w), lambda b, h, i: (b, ctx_layer, 0, 0, h)),
                     pl.BlockSpec((None, None, None, n_ctx, kv_w), lambda b, h, i: (b, ctx_layer, 1, 0, h))]
        args += [ctx, ctx]
        scratch += [pltpu.VMEM((kvs, n_ctx, HEAD_DIM), BF16), pltpu.VMEM((kvs, n_ctx, 2 * HEAD_DIM), BF16)]
    if rope:
        in_specs += [pl.BlockSpec((L, HEAD_DIM), lambda b, h, i: (0, 0))] * 2
        args += [cos, sin]
    if qk_norm:
        in_specs += [pl.BlockSpec((1, HEAD_DIM), lambda b, h, i: (0, 0))] * 2
        args += [q_gain.reshape(1, HEAD_DIM), k_gain.reshape(1, HEAD_DIM)]
    if use_sink:
        in_specs += [pl.BlockSpec(memory_space=pltpu.SMEM)]
        args += [sink]
    out_shape = [jax.ShapeDtypeStruct((n_seq * L, n_heads * HEAD_DIM), BF16)]
    out_specs = [pl.BlockSpec((tq, q_w), lambda b, h, i: (b * nq + i, h))]
    if emit_k:
        out_shape += [jax.ShapeDtypeStruct((n_seq * L * n_kv, HEAD_DIM), F32)]
        out_specs += [pl.BlockSpec((L * n_kv, HEAD_DIM), lambda b, h, i: (b, 0))]
    kern = functools.partial(_attn_kernel, tq=tq, seq_len=L, group=group, n_kv=n_kv, kvs=kvs, n_ctx=n_ctx,
                             use_sink=use_sink, window=window, qk_norm=qk_norm, rope=rope, emit_k=emit_k)
    res = pl.pallas_call(
        kern,
        out_shape=tuple(out_shape),
        grid=(n_seq, n_kv // kvs, nq),
        in_specs=in_specs,
        out_specs=tuple(out_specs),
        scratch_shapes=scratch,
        compiler_params=pltpu.CompilerParams(dimension_semantics=("arbitrary", "arbitrary", "arbitrary"),
                                             vmem_limit_bytes=VMEM_LIMIT_BYTES),
        name="attention",
    )(*args)
    return res if emit_k else res[0]


def _out_mlp_kernel(*refs, n_x, n_parts, final, n_prompt_tiles, tf):
    it = iter(refs)
    x_refs = [next(it) for _ in range(n_x)]
    a_refs = [[next(it), next(it)] for _ in range(n_parts)]
    wo_refs = [next(it) for _ in range(n_parts)]
    mod_ref, g2_ref, w1_ref, w2_ref = next(it), next(it), next(it), next(it)
    fg_ref = next(it) if final else None
    o_refs = [next(it) for _ in range(2 if final else 1)]
    i = pl.program_id(0)
    d_ff = w1_ref.shape[1]

    mix = None
    for pair, wo_ref in zip(a_refs, wo_refs):
        term = _dot(_token_tile(pair, n_prompt_tiles), wo_ref[...])
        mix = term if mix is None else mix + term
    x1 = _token_tile(x_refs, n_prompt_tiles) + mod_ref[2:3, :] * mix
    h2 = (_rms_rows(x1, g2_ref[...]) * (1.0 + mod_ref[4:5, :]) + mod_ref[3:4, :]).astype(BF16)
    acc = None
    for c in range(d_ff // tf):
        hid = jnp.maximum(_dot(h2, w1_ref[:, c * tf:(c + 1) * tf]), 0.0)
        term = _dot((hid * hid).astype(BF16), w2_ref[c * tf:(c + 1) * tf, :])
        acc = term if acc is None else acc + term
    y = x1 + mod_ref[5:6, :] * acc

    if final:
        y = _rms_rows(y, fg_ref[...])

        @pl.when(i < n_prompt_tiles)
        def _():
            o_refs[0][...] = y

        @pl.when(i >= n_prompt_tiles)
        def _():
            o_refs[1][...] = y
    else:
        o_refs[0][...] = y


def _out_mlp(xs, parts, w_out, wo_layer, mod, layer, g2, w1, w2, n_prompt, sample_len, final_g=None, tm=512,
             tf=1024):
    n_tok = sum(a.shape[0] for a in xs)
    d = xs[0].shape[1]
    d_ff = w1.shape[2]
    final = final_g is not None
    npt = n_prompt // tm
    row = functools.partial(_cond_row, n_prompt_tiles=npt, tiles_per_sample=sample_len // tm)
    in_specs = _token_specs(xs, tm, npt)
    for pair in parts:
        in_specs += _token_specs(pair, tm, npt)
    resident = lambda shape: pl.BlockSpec((None, *shape), lambda i: (layer, 0, 0), pipeline_mode=pl.Buffered(1))
    width = w_out.shape[1] // len(parts)
    in_specs += [pl.BlockSpec((None, width, d), lambda i, p=p: (wo_layer, p, 0), pipeline_mode=pl.Buffered(1))
                 for p in range(len(parts))]
    in_specs += [
        pl.BlockSpec((None, None, 6, d), lambda i: (layer, row(i), 0, 0)),
        pl.BlockSpec((None, 1, d), lambda i: (layer, 0, 0)),
        resident((d, d_ff)),
        resident((d_ff, d)),
    ]
    args = [*xs, *[a for pair in parts for a in pair], *([w_out] * len(parts)), mod, g2, w1, w2]
    if final:
        in_specs += [pl.BlockSpec((1, d), lambda i: (0, 0))]
        args += [final_g.reshape(1, d)]
        out_shape = (jax.ShapeDtypeStruct((n_prompt, d), F32), jax.ShapeDtypeStruct((n_tok - n_prompt, d), F32))
        out_specs = (pl.BlockSpec((tm, d), lambda i: (jnp.minimum(i, npt - 1), 0)),
                     pl.BlockSpec((tm, d), lambda i: (jnp.maximum(i - npt, 0), 0)))
    else:
        out_shape = jax.ShapeDtypeStruct((n_tok, d), F32)
        out_specs = pl.BlockSpec((tm, d), lambda i: (i, 0))
    kern = functools.partial(_out_mlp_kernel, n_x=len(xs), n_parts=len(parts), final=final, n_prompt_tiles=npt,
                             tf=tf)
    return pl.pallas_call(
        kern,
        out_shape=out_shape,
        grid=(n_tok // tm,),
        in_specs=in_specs,
        out_specs=out_specs,
        compiler_params=pltpu.CompilerParams(dimension_semantics=("arbitrary",),
                                             vmem_limit_bytes=VMEM_LIMIT_BYTES),
        name="out_mlp",
    )(*args)


def _rope_tables(seq_len):
    rows = seq_len // GRID_W
    inv = ROPE_THETA ** (-np.arange(ROT_FREQS, dtype=np.float64) / ROT_FREQS)
    ang_r = np.repeat(np.arange(rows, dtype=np.float64), GRID_W)[:, None] * inv
    ang_c = np.tile(np.arange(GRID_W, dtype=np.float64), rows)[:, None] * inv
    cos = np.concatenate([np.cos(ang_r), np.cos(ang_r), np.cos(ang_c), np.cos(ang_c)], axis=-1)
    sin = np.concatenate([-np.sin(ang_r), np.sin(ang_r), -np.sin(ang_c), np.sin(ang_c)], axis=-1)
    return jnp.asarray(cos, dtype=F32), jnp.asarray(sin, dtype=F32)


def kernel(x_prompt, x_sample, state_a, cache_b_kv, cache_c_kv, c, c_ctx, ada_w, ada_b, norm1_g, norm2_g,
           final_g, mlp_w1, mlp_w2, ev_w_in, a_conv, a_log, a_dt_bias, a_norm_g, b_sink, ev_w_out,
           od_w_in, c_qnorm_g, c_knorm_g, od_w_out):
    batch, seq, d = x_prompt.shape
    dec_batch, dec_seq, _ = x_sample.shape
    depth = ada_w.shape[0]
    n_prompt, n_sample = batch * seq, dec_batch * dec_seq
    past = cache_b_kv.shape[3]

    xs = [x_prompt.reshape(n_prompt, d), x_sample.reshape(n_sample, d)]
    cond_rows = -(-(1 + dec_batch) // SUBLANES) * SUBLANES
    cond = jnp.concatenate([c_ctx[None, :], c, jnp.zeros((cond_rows - 1 - dec_batch, d), F32)], axis=0)
    mod = _adaln(cond, ada_w, ada_b).reshape(depth, cond_rows, 6, d)

    cos, sin = _rope_tables(dec_seq)
    s_blk0 = n_prompt // dec_seq
    ctx_b = cache_b_kv.reshape(*cache_b_kv.shape[:4], B_KV_HEADS * HEAD_DIM)
    ctx_c = cache_c_kv.reshape(*cache_c_kv.shape[:4], C_KV_HEADS * HEAD_DIM)
    ev_w_in_bf16 = _ev_w_in(ev_w_in.astype(BF16))
    od_w_in_bf16 = od_w_in.astype(BF16)
    ev_w_out_bf16 = ev_w_out.astype(BF16)
    od_w_out_bf16 = od_w_out.astype(BF16)
    w1 = mlp_w1.astype(BF16)
    w2 = mlp_w2.astype(BF16)
    g1 = norm1_g.reshape(depth, 1, d)
    g2 = norm2_g.reshape(depth, 1, d)
    lane_pad = lambda v: jnp.zeros((1, LANES), F32).at[0, 2 * A_HEADS:4 * A_HEADS].set(v.reshape(-1))

    new_a, new_b, new_c = [], [], []
    for l in range(depth):
        i = l // 2
        fin = final_g if l == depth - 1 else None
        if l % 2 == 0:
            kv_cols = B_KV_HEADS * HEAD_DIM
            proj, (cache_k, cache_v), (beta_all, gc_all) = _in_proj(
                xs, mod, l, g1, ev_w_in_bf16, i, n_prompt, dec_seq,
                cache_cols=(EV_KVB_OFF, EV_KVB_OFF + kv_cols), n_kv=B_KV_HEADS,
                gate_rows=(lane_pad(a_log[i]), lane_pad(a_dt_bias[i])))
            oa_p, st_p = _gdn(proj, beta_all, gc_all, a_conv[i], a_norm_g[i], 0, batch, seq, heads=A_HEADS,
                              steps=seq // CHUNK)
            oa_s, _ = _gdn(proj, beta_all, gc_all, a_conv[i], a_norm_g[i], s_blk0, dec_batch, dec_seq, heads=1,
                           steps=8, s0=state_a, s0_layer=i)
            ob_p = _attention(proj, q_off=EV_QB_OFF, k_off=EV_KVB_OFF, v_off=EV_KVB_OFF + kv_cols,
                              n_heads=B_HEADS, n_kv=B_KV_HEADS, row_block0=0, n_seq=batch, seq_len=seq,
                              tq=seq, kvs=B_KV_HEADS, sink=b_sink[i])
            ob_s = _attention(proj, q_off=EV_QB_OFF, k_off=EV_KVB_OFF, v_off=EV_KVB_OFF + kv_cols,
                              n_heads=B_HEADS, n_kv=B_KV_HEADS, row_block0=s_blk0, n_seq=dec_batch,
                              seq_len=dec_seq, tq=512,
                              ctx=ctx_b, ctx_layer=i, cos=cos, sin=sin, sink=b_sink[i], window=WINDOW)
            new_a.append(st_p)
            new_b.append(jnp.stack([cache_k.reshape(batch, seq, B_KV_HEADS, HEAD_DIM),
                                    cache_v.reshape(batch, seq, B_KV_HEADS, HEAD_DIM)], axis=1))
            parts = [[oa_p, oa_s], [ob_p, ob_s]]
            w_out = ev_w_out_bf16
        else:
            kv_cols = C_KV_HEADS * HEAD_DIM
            proj, (cache_v,), _ = _in_proj(xs, mod, l, g1, od_w_in_bf16, i, n_prompt, dec_seq,
                                           cache_cols=(OD_V_OFF,), n_kv=C_KV_HEADS)
            oc_p, kn_p = _attention(proj, q_off=0, k_off=OD_K_OFF, v_off=OD_V_OFF, n_heads=C_HEADS,
                                    n_kv=C_KV_HEADS, row_block0=0, n_seq=batch, seq_len=seq, tq=seq, kvs=C_KV_HEADS,
                                    q_gain=c_qnorm_g[i], k_gain=c_knorm_g[i], emit_k=True)
            oc_s = _attention(proj, q_off=0, k_off=OD_K_OFF, v_off=OD_V_OFF, n_heads=C_HEADS,
                              n_kv=C_KV_HEADS, row_block0=s_blk0, n_seq=dec_batch, seq_len=dec_seq, tq=256,
                              ctx=ctx_c, ctx_layer=i, cos=cos, sin=sin, q_gain=c_qnorm_g[i],
                              k_gain=c_knorm_g[i])
            new_c.append(jnp.stack([kn_p.reshape(batch, seq, C_KV_HEADS, HEAD_DIM),
                                    cache_v.reshape(batch, seq, C_KV_HEADS, HEAD_DIM)], axis=1))
            parts = [[oc_p, oc_s]]
            w_out = od_w_out_bf16
        out = _out_mlp(xs, parts, w_out, i, mod, l, g2, w1, w2, n_prompt, dec_seq, final_g=fin)
        xs = list(out) if fin is not None else [out]

    y_prompt = xs[0].reshape(batch, seq, d)
    y_sample = xs[1].reshape(dec_batch, dec_seq, d)
    return (y_prompt, y_sample, jnp.stack(new_a, axis=1), jnp.stack(new_b, axis=1), jnp.stack(new_c, axis=1))
```

```python
import functools

import numpy as np
import jax
import jax.numpy as jnp
from jax import lax
from jax.experimental import pallas as pl
from jax.experimental.pallas import tpu as pltpu

F32 = jnp.float32
BF16 = jnp.bfloat16

HEAD_DIM = 128
GRID_W = 64
A_HEADS = 4
CHUNK = 64
B_HEADS = 4
B_KV_HEADS = 2
WINDOW = 128
C_HEADS = 8
C_KV_HEADS = 2
ROT_FREQS = HEAD_DIM // 4
ROPE_THETA = 10000.0
EPS = 1e-6
NEG = -1e30
LOG2_E = 1.4426950408889634
A_WIDTH = A_HEADS * HEAD_DIM
B_WIDTH = B_HEADS * HEAD_DIM
C_WIDTH = C_HEADS * HEAD_DIM

SUBLANES = 8
LANES = 128
VMEM_LIMIT_BYTES = 56 * 1024 * 1024

EV_QB_OFF = 4 * A_WIDTH
EV_KVB_OFF = EV_QB_OFF + B_WIDTH
EV_BG_OFF = EV_KVB_OFF + 2 * B_KV_HEADS * HEAD_DIM
EV_COLS = EV_BG_OFF + LANES
OD_K_OFF = C_WIDTH
OD_V_OFF = C_WIDTH + C_KV_HEADS * HEAD_DIM


def _sigmoid(x):
    return 1.0 / (1.0 + jnp.exp(-x))


def _silu(x):
    return x * _sigmoid(x)


def _softplus(x):
    return jnp.maximum(x, 0.0) + jnp.log1p(jnp.exp(-jnp.abs(x)))


def _rms_rows(x, g):
    return x * lax.rsqrt(jnp.mean(x * x, axis=-1, keepdims=True) + EPS) * g


def _dot(a, b):
    return jnp.dot(a, b, preferred_element_type=F32)


def _dot_nt(a, b):
    return lax.dot_general(a, b, (((1,), (1,)), ((), ())), preferred_element_type=F32)


def _cond_row(i, n_prompt_tiles, tiles_per_sample):
    return jnp.where(i < n_prompt_tiles, 0, 1 + (i - n_prompt_tiles) // tiles_per_sample)


def _split_bf16(x):
    hi = x.astype(BF16)
    lo = (x - hi.astype(F32)).astype(BF16)
    return hi, lo


def _adaln_kernel(cond_ref, w_ref, b_ref, o_ref):
    rows = cond_ref.shape[0]
    s_hi, s_lo = _split_bf16(_silu(cond_ref[...]))
    w_hi, w_lo = _split_bf16(w_ref[0])
    r = _dot(jnp.concatenate([s_hi, s_lo], axis=0), w_hi)
    o_ref[0] = r[:rows] + r[rows:] + _dot(s_hi, w_lo) + b_ref[0]


def _adaln(cond, ada_w, ada_b, tn=2048):
    depth, d, n = ada_w.shape
    rows = cond.shape[0]
    return pl.pallas_call(
        _adaln_kernel,
        out_shape=jax.ShapeDtypeStruct((depth, rows, n), F32),
        grid=(depth, n // tn),
        in_specs=[
            pl.BlockSpec((rows, d), lambda l, j: (0, 0)),
            pl.BlockSpec((1, d, tn), lambda l, j: (l, 0, j)),
            pl.BlockSpec((1, 1, tn), lambda l, j: (l, 0, j)),
        ],
        out_specs=pl.BlockSpec((1, rows, tn), lambda l, j: (l, 0, j)),
        compiler_params=pltpu.CompilerParams(dimension_semantics=("arbitrary", "arbitrary"),
                                             vmem_limit_bytes=VMEM_LIMIT_BYTES),
        name="adaln",
    )(cond, ada_w, ada_b.reshape(depth, 1, n))


def _ev_w_in_kernel(w_ref, o_ref):
    bg_off = 4 * A_WIDTH
    bg_end = bg_off + 4 * A_HEADS
    d = w_ref.shape[1]
    o_ref[:bg_off, :] = w_ref[:bg_off, :].astype(BF16)
    o_ref[bg_off:EV_BG_OFF, :] = w_ref[bg_end:, :].astype(BF16)
    o_ref[EV_BG_OFF:EV_BG_OFF + 4 * A_HEADS, :] = w_ref[bg_off:bg_end, :].astype(BF16)
    o_ref[EV_BG_OFF + 4 * A_HEADS:, :] = jnp.zeros((LANES - 4 * A_HEADS, d), BF16)


def _ev_w_in(w_t):
    layers, n, d = w_t.shape
    return pl.pallas_call(
        _ev_w_in_kernel,
        out_shape=jax.ShapeDtypeStruct((layers, EV_COLS, d), BF16),
        grid=(layers,),
        in_specs=[pl.BlockSpec((None, n, d), lambda l: (l, 0, 0))],
        out_specs=pl.BlockSpec((None, EV_COLS, d), lambda l: (l, 0, 0)),
        compiler_params=pltpu.CompilerParams(dimension_semantics=("arbitrary",),
                                             vmem_limit_bytes=VMEM_LIMIT_BYTES),
        name="ev_w_in",
    )(w_t)


def _token_specs(arrays, tm, n_prompt_tiles):
    if len(arrays) == 1:
        return [pl.BlockSpec((tm, arrays[0].shape[1]), lambda i, *_: (i, 0))]
    return [pl.BlockSpec((tm, arrays[0].shape[1]), lambda i, *_: (jnp.minimum(i, n_prompt_tiles - 1), 0)),
            pl.BlockSpec((tm, arrays[1].shape[1]), lambda i, *_: (jnp.maximum(i - n_prompt_tiles, 0), 0))]


def _token_tile(refs, n_prompt_tiles):
    if len(refs) == 1:
        return refs[0][...]
    return jnp.where(pl.program_id(0) < n_prompt_tiles, refs[0][...], refs[1][...])


def _gate_columns(bg, a_row, dt_row):
    rows = bg.shape[0]
    g = -jnp.exp(a_row) * _softplus(bg + dt_row)
    chunk_row = lax.broadcasted_iota(jnp.int32, (rows, LANES), 0) % CHUNK
    lane = lax.broadcasted_iota(jnp.int32, (rows, LANES), 1)
    pre, suf = g, g
    shift = 1
    while shift < CHUNK:
        pre = pre + jnp.where(chunk_row >= shift, pltpu.roll(pre, shift, 0), 0.0)
        suf = suf + jnp.where(chunk_row < CHUNK - shift, pltpu.roll(suf, rows - shift, 0), 0.0)
        shift *= 2
    return _sigmoid(bg), jnp.where(lane >= 3 * A_HEADS, suf, pre)


def _inproj_kernel(*refs, n_x, n_prompt_tiles, cache_cols, n_kv, gates, w_transposed):
    it = iter(refs)
    x_refs = [next(it) for _ in range(n_x)]
    mod_ref, g_ref, w_ref = next(it), next(it), next(it)
    a_ref, dt_ref = (next(it), next(it)) if gates else (None, None)
    o_ref = next(it)
    cache_refs = [next(it) for _ in cache_cols]
    x = _token_tile(x_refs, n_prompt_tiles)
    h = _rms_rows(x, g_ref[...]) * (1.0 + mod_ref[1:2, :]) + mod_ref[0:1, :]
    o = (_dot_nt if w_transposed else _dot)(h.astype(BF16), w_ref[...])
    o_ref[...] = o
    if gates:
        beta_ref, gc_ref = next(it), next(it)
        beta_ref[...], gc_ref[...] = _gate_columns(o[:, EV_BG_OFF:EV_BG_OFF + LANES], a_ref[...], dt_ref[...])

    @pl.when(pl.program_id(0) < n_prompt_tiles)
    def _():
        tm = o.shape[0]
        for c_ref, col in zip(cache_refs, cache_cols):
            for hd in range(n_kv):
                c_ref[pl.ds(hd, tm, stride=n_kv), :] = o[:, col + hd * HEAD_DIM:col + (hd + 1) * HEAD_DIM]


def _in_proj(xs, mod, layer, g, w, w_layer, n_prompt, sample_len, cache_cols, n_kv, gate_rows=None,
             w_transposed=False, tm=512):
    n_tok = sum(a.shape[0] for a in xs)
    d = xs[0].shape[1]
    n = w.shape[1] if w_transposed else w.shape[2]
    npt = n_prompt // tm
    assert tm % CHUNK == 0
    gates = gate_rows is not None
    row = functools.partial(_cond_row, n_prompt_tiles=npt, tiles_per_sample=sample_len // tm)
    cache_spec = pl.BlockSpec((tm * n_kv, HEAD_DIM), lambda i: (jnp.minimum(i, npt - 1), 0))
    lane_row = pl.BlockSpec((1, LANES), lambda i: (0, 0))
    gate_spec = pl.BlockSpec((tm, LANES), lambda i: (i, 0))
    res = pl.pallas_call(
        functools.partial(_inproj_kernel, n_x=len(xs), n_prompt_tiles=npt, cache_cols=cache_cols, n_kv=n_kv,
                          gates=gates, w_transposed=w_transposed),
        out_shape=(jax.ShapeDtypeStruct((n_tok, n), F32),
                   *[jax.ShapeDtypeStruct((n_prompt * n_kv, HEAD_DIM), F32) for _ in cache_cols],
                   *[jax.ShapeDtypeStruct((n_tok, LANES), F32) for _ in range(2 if gates else 0)]),
        grid=(n_tok // tm,),
        in_specs=_token_specs(xs, tm, npt) + [
            pl.BlockSpec((None, None, 6, d), lambda i: (layer, row(i), 0, 0)),
            pl.BlockSpec((None, 1, d), lambda i: (layer, 0, 0)),
            pl.BlockSpec((None, *w.shape[1:]), lambda i: (w_layer, 0, 0)),
        ] + ([lane_row, lane_row] if gates else []),
        out_specs=(pl.BlockSpec((tm, n), lambda i: (i, 0)), *[cache_spec for _ in cache_cols],
                   *([gate_spec, gate_spec] if gates else [])),
        compiler_params=pltpu.CompilerParams(dimension_semantics=("arbitrary",),
                                             vmem_limit_bytes=VMEM_LIMIT_BYTES),
        name="in_proj",
    )(*xs, mod, g, w, *(gate_rows if gates else ()))
    n_c = len(cache_cols)
    return res[0], res[1:1 + n_c], res[1 + n_c:]


def _gdn_kernel(q_ref, k_ref, v_ref, gate_ref, betaall_ref, gcall_ref, cwq_ref, cwk_ref, cwv_ref, ng_ref, *rest,
                seq_len, heads, steps, has_state):
    s0_ref = rest[0] if has_state else None
    (o_ref, sout_ref, qn_ref, kn_ref, vn_ref, beta_ref, gc_ref, gcrow_ref, s_ref, oacc_ref,
     mq_a, c_a, oo_a, eg_a, mq_b, c_b, oo_b, eg_b) = rest[1:] if has_state else rest
    L = seq_len
    nc = L // CHUNK
    nblk = nc // steps
    h0 = pl.program_id(1) * heads
    row_idx = lax.broadcasted_iota(jnp.int32, (L, LANES), 0)
    lane_idx = lax.broadcasted_iota(jnp.int32, (L, LANES), 1)

    def conv_silu(x, cw):
        prev = jnp.where(row_idx == 0, 0.0, pltpu.roll(x, 1, 0))
        nxt = jnp.where(row_idx == L - 1, 0.0, pltpu.roll(x, L - 1, 0))
        return _silu(prev * cw[0:1, :] + x * cw[1:2, :] + nxt * cw[2:3, :])

    def l2norm(x):
        return x * lax.rsqrt(jnp.sum(x * x, axis=-1, keepdims=True) + EPS)

    beta_all = betaall_ref[...]
    gc_all = gcall_ref[...]

    def pick(x, col):
        col_vals = jnp.sum(jnp.where(lane_idx == col, x, 0.0), axis=1, keepdims=True)
        return jnp.broadcast_to(col_vals, (L, LANES))

    for r in range(L // LANES):
        rows = jnp.transpose(gc_all[r * LANES:(r + 1) * LANES, :])[2 * A_HEADS:4 * A_HEADS, :]
        gcrow_ref[0, r] = rows
        gcrow_ref[1, r] = pltpu.roll(rows, CHUNK, 1)

    for hh in range(heads):
        cols = slice(hh * HEAD_DIM, (hh + 1) * HEAD_DIM)
        qn_ref[:, cols] = l2norm(conv_silu(q_ref[:, cols], cwq_ref[:, cols])) * (HEAD_DIM ** -0.5)
        kn_ref[:, cols] = l2norm(conv_silu(k_ref[:, cols], cwk_ref[:, cols]))
        vn_ref[:, cols] = conv_silu(v_ref[:, cols], cwv_ref[:, cols])
        for d in range(2):
            s_ref[hh, d] = s0_ref[d, hh] if has_state else jnp.zeros((HEAD_DIM, HEAD_DIM), F32)
        head = h0 + hh
        for d in range(2):
            beta_ref[hh, d] = pick(beta_all, d * A_HEADS + head)
            gc_ref[hh, d] = pick(gc_all, (2 + d) * A_HEADS + head)

    r64 = lax.broadcasted_iota(jnp.int32, (CHUNK, LANES), 0)
    l64 = lax.broadcasted_iota(jnp.int32, (CHUNK, LANES), 1)
    is_f = l64 < CHUNK
    is_f_row = lax.broadcasted_iota(jnp.int32, (1, LANES), 1) < CHUNK
    cpos = jnp.where(is_f, l64, l64 - CHUNK)
    ahead = jnp.where(is_f, r64 - cpos, cpos - r64)
    incl = ahead >= 0
    strict = ahead > 0
    eye = jnp.where(r64 == cpos, 1.0, 0.0)
    is_f2 = lax.broadcasted_iota(jnp.int32, (2 * CHUNK, LANES), 1) < CHUNK
    zeros_c = jnp.zeros((CHUNK, HEAD_DIM), F32)

    def aligned(off, m):
        return off if isinstance(off, int) else pl.multiple_of(off, m)

    def half(n):
        return n // 2 if isinstance(n, int) else lax.shift_right_logical(n, 1)

    def block_diag(xp):
        z = jnp.zeros_like(xp)
        return jnp.concatenate([jnp.where(is_f, xp, z), jnp.where(is_f, z, xp)], axis=0)

    def split_dot(a_hi, a_lo, b_hi, b_lo):
        lhs = jnp.concatenate([a_hi, a_lo, a_hi], axis=1)
        rhs = jnp.concatenate([b_hi, b_hi, b_lo], axis=0)
        return _dot(lhs, rhs)

    def prep_stages(blk, slot):
        mq_ref, c_ref, oo_ref, eg_ref = slot
        probs = [(si, hh) for si in range(steps) for hh in range(heads)]
        st = {p: {} for p in probs}

        def load(p):
            si, hh = p
            j = blk * steps + si
            n_b = nc - 1 - j
            cols = slice(hh * HEAD_DIM, (hh + 1) * HEAD_DIM)
            off_f = aligned(j * CHUNK, CHUNK)
            off_b = aligned(n_b * CHUNK, CHUNK)
            q_f, k_f, v_f = (r[pl.ds(off_f, CHUNK), cols] for r in (qn_ref, kn_ref, vn_ref))
            q_b, k_b, v_b = (r[pl.ds(off_b, CHUNK), cols] for r in (qn_ref, kn_ref, vn_ref))
            beta_f = beta_ref[hh, 0, pl.ds(off_f, CHUNK), :]
            beta_b = beta_ref[hh, 1, pl.ds(off_b, CHUNK), :]
            gc_f = gc_ref[hh, 0, pl.ds(off_f, CHUNK), :]
            gc_b = gc_ref[hh, 1, pl.ds(off_b, CHUNK), :]
            row_f = gcrow_ref[si % 2, half(j), pl.ds(h0 + hh, 1), :]
            row_b = gcrow_ref[si % 2, half(n_b), pl.ds(A_HEADS + h0 + hh, 1), :]
            gcrow = jnp.where(is_f_row, row_f, row_b)
            diff = jnp.where(is_f, gc_f, gc_b) - gcrow
            decay = jnp.where(incl, jnp.exp(jnp.where(incl, diff, 0.0)), 0.0)
            kb_f = k_f * beta_f
            kb_b = k_b * beta_b
            e_f = jnp.exp(gc_f)
            e_b = jnp.exp(gc_b)
            glast_f = gc_f[CHUNK - 1:CHUNK, :]
            glast_b = gc_b[0:1, :]
            kdt = jnp.transpose(jnp.concatenate([k_f * jnp.exp(glast_f - gc_f), k_b * jnp.exp(glast_b - gc_b)],
                                                axis=0))
            zk = jnp.zeros_like(kdt)
            eg_ref[si, hh, 0] = jnp.broadcast_to(jnp.exp(glast_f), (SUBLANES, LANES))
            eg_ref[si, hh, 1] = jnp.broadcast_to(jnp.exp(glast_b), (SUBLANES, LANES))
            st[p].update(
                decay=decay,
                lhs=jnp.concatenate([jnp.concatenate([kb_f, kb_b], axis=1),
                                     jnp.concatenate([q_f, q_b], axis=1)], axis=0).astype(BF16),
                rhs=jnp.concatenate([jnp.concatenate([k_f, zeros_c], axis=1),
                                     jnp.concatenate([zeros_c, k_b], axis=1)], axis=0).astype(BF16),
                rhs_uw=jnp.concatenate([jnp.concatenate([v_f * beta_f, kb_f * e_f], axis=1),
                                        jnp.concatenate([v_b * beta_b, kb_b * e_b], axis=1)], axis=0).astype(BF16),
                kd2=jnp.concatenate([jnp.where(is_f2, kdt, zk), jnp.where(is_f2, zk, kdt)], axis=0).astype(BF16),
                qe=(q_f * e_f, q_b * e_b))

        def gram(p):
            s = st[p]
            aq = _dot_nt(s.pop("lhs"), s.pop("rhs"))
            decay = s.pop("decay")
            x = -jnp.where(strict, aq[:CHUNK] * decay, 0.0)
            s["qk"] = jnp.where(incl, aq[CHUNK:] * decay, 0.0)
            s["t"] = eye + x
            s["p"] = x

        def square(p):
            s = st[p]
            p_hi, p_lo = _split_bf16(s["p"])
            s["p"] = split_dot(p_hi, p_lo, block_diag(p_hi), block_diag(p_lo))

        def double(p):
            s = st[p]
            p_hi, p_lo = _split_bf16(s["p"])
            t_hi, t_lo = _split_bf16(s["t"])
            r = split_dot(p_hi, p_lo, jnp.concatenate([block_diag(t_hi), block_diag(p_hi)], axis=1),
                          jnp.concatenate([block_diag(t_lo), block_diag(p_lo)], axis=1))
            s["t"] = s["t"] + r[:, :LANES]
            s["p"] = r[:, LANES:]

        def last(p):
            s = st[p]
            p_hi, p_lo = _split_bf16(s.pop("p"))
            t_hi, t_lo = _split_bf16(s["t"])
            s["t"] = s["t"] + split_dot(p_hi, p_lo, block_diag(t_hi), block_diag(t_lo))

        def solve(p):
            s = st[p]
            s["uw"] = _dot(block_diag(s.pop("t")).astype(BF16), s.pop("rhs_uw"))

        def fold(p):
            si, hh = p
            s = st[p]
            lhs3 = jnp.concatenate([s.pop("kd2"), block_diag(s.pop("qk")).astype(BF16)], axis=0)
            big = _dot(lhs3, s.pop("uw").astype(BF16))
            qe = s.pop("qe")
            for d in range(2):
                rows_s = slice(d * 2 * CHUNK, (d + 1) * 2 * CHUNK)
                rows_o = slice(4 * CHUNK + d * CHUNK, 4 * CHUNK + (d + 1) * CHUNK)
                c_ref[si, hh, d] = big[rows_s, :HEAD_DIM]
                oo_ref[si, hh, d] = big[rows_o, :HEAD_DIM]
                mq_ref[si, hh, d] = jnp.concatenate([big[rows_s, HEAD_DIM:], qe[d] - big[rows_o, HEAD_DIM:]],
                                                    axis=0).astype(BF16)

        n_double = 0
        span = 2
        while 2 * span < CHUNK:
            n_double += 1
            span *= 2
        stage_fns = [load, gram, square] + [double] * n_double + [last, solve, fold]
        return [functools.partial(lambda fn: [fn(p) for p in probs], fn) for fn in stage_fns]

    def scan_step(blk, slot, si):
        mq_ref, c_ref, oo_ref, eg_ref = slot
        j = blk * steps + si
        for hh in range(heads):
            cols = slice(hh * HEAD_DIM, (hh + 1) * HEAD_DIM)
            for d, n in ((0, j), (1, nc - 1 - j)):
                s = s_ref[hh, d]
                r = _dot(mq_ref[si, hh, d], s.astype(BF16))
                s_ref[hh, d] = s * eg_ref[si, hh, d][0:1, :] - r[:2 * CHUNK] + c_ref[si, hh, d]
                oacc_ref[d, pl.ds(aligned(n * CHUNK, CHUNK), CHUNK), cols] = r[2 * CHUNK:] + oo_ref[si, hh, d]

    def run(prep, scan):
        stages = prep_stages(*prep) if prep else []
        n_scan = steps if scan else 0
        for k in range(max(len(stages), n_scan)):
            if k < len(stages):
                stages[k]()
            if k < n_scan:
                scan_step(scan[0], scan[1], k)

    slot_a = (mq_a, c_a, oo_a, eg_a)
    slot_b = (mq_b, c_b, oo_b, eg_b)
    run((0, slot_a), None)
    if nblk == 1:
        run(None, (0, slot_a))
    else:
        def body(i, carry):
            blk = 2 * i
            run((blk + 1, slot_b), (blk, slot_a))
            run((blk + 2, slot_a), (blk + 1, slot_b))
            return carry

        lax.fori_loop(0, nblk // 2 - 1, body, 0)
        run((nblk - 1, slot_b), (nblk - 2, slot_a))
        run(None, (nblk - 1, slot_b))

    for hh in range(heads):
        cols = slice(hh * HEAD_DIM, (hh + 1) * HEAD_DIM)
        sout_ref[0, hh] = s_ref[hh, 0]
        sout_ref[1, hh] = s_ref[hh, 1]
        o = oacc_ref[0, :, cols] + oacc_ref[1, :, cols]
        o = _rms_rows(o, ng_ref[...]) * _silu(gate_ref[:, cols])
        o_ref[:, cols] = o.astype(o_ref.dtype)


def _gdn(proj, beta_all, gc_all, conv_w, norm_g, row_block0, n_seq, seq_len, heads, steps, s0=None, s0_layer=0):
    L = seq_len
    nblk = L // CHUNK // steps
    assert L % (CHUNK * steps) == 0 and steps % 2 == 0 and 2 * CHUNK == LANES and A_HEADS % heads == 0
    assert nblk == 1 or nblk % 2 == 0
    width = heads * HEAD_DIM
    groups = A_HEADS // heads
    blk = lambda base: pl.BlockSpec((L, width), lambda b, h: (row_block0 + b, base * groups + h))
    wblk = lambda base: pl.BlockSpec((3, width), lambda b, h: (0, base * groups + h))
    state_spec = pl.BlockSpec((None, 2, heads, HEAD_DIM, HEAD_DIM), lambda b, h: (b, 0, h, 0, 0))
    slot = [pltpu.VMEM((steps, heads, 2, 3 * CHUNK, HEAD_DIM), BF16),
            pltpu.VMEM((steps, heads, 2, 2 * CHUNK, HEAD_DIM), F32),
            pltpu.VMEM((steps, heads, 2, CHUNK, HEAD_DIM), F32),
            pltpu.VMEM((steps, heads, 2, SUBLANES, LANES), F32)]
    kern = functools.partial(_gdn_kernel, seq_len=L, heads=heads, steps=steps, has_state=s0 is not None)
    state_in, state_arg = [], []
    if s0 is not None:
        state_in = [pl.BlockSpec((None, None, 2, heads, HEAD_DIM, HEAD_DIM), lambda b, h: (b, s0_layer, 0, h, 0, 0))]
        state_arg = [s0]
    out, s_out = pl.pallas_call(
        kern,
        out_shape=(jax.ShapeDtypeStruct((n_seq * L, A_WIDTH), BF16),
                   jax.ShapeDtypeStruct((n_seq, 2, A_HEADS, HEAD_DIM, HEAD_DIM), F32)),
        grid=(n_seq, groups),
        in_specs=[
            blk(0), blk(1), blk(2), blk(3),
            pl.BlockSpec((L, LANES), lambda b, h: (row_block0 + b, 0)),
            pl.BlockSpec((L, LANES), lambda b, h: (row_block0 + b, 0)),
            wblk(0), wblk(1), wblk(2),
            pl.BlockSpec((1, HEAD_DIM), lambda b, h: (0, 0)),
        ] + state_in,
        out_specs=(pl.BlockSpec((L, width), lambda b, h: (b, h)), state_spec),
        scratch_shapes=[
            pltpu.VMEM((L, width), F32), pltpu.VMEM((L, width), F32), pltpu.VMEM((L, width), F32),
            pltpu.VMEM((heads, 2, L, LANES), F32), pltpu.VMEM((heads, 2, L, LANES), F32),
            pltpu.VMEM((2, L // LANES, 2 * A_HEADS, LANES), F32),
            pltpu.VMEM((heads, 2, HEAD_DIM, HEAD_DIM), F32),
            pltpu.VMEM((2, L, width), F32),
        ] + slot + slot,
        compiler_params=pltpu.CompilerParams(dimension_semantics=("arbitrary", "arbitrary"),
                                             vmem_limit_bytes=VMEM_LIMIT_BYTES),
        name="gdn",
    )(proj, proj, proj, proj, beta_all, gc_all, conv_w, conv_w, conv_w, norm_g.reshape(1, HEAD_DIM), *state_arg)
    return out, s_out


def _rope_rows(x, cos, sin_signed):
    lane = lax.broadcasted_iota(jnp.int32, x.shape, 1)
    swapped = jnp.where((lane % (2 * ROT_FREQS)) < ROT_FREQS,
                        pltpu.roll(x, HEAD_DIM - ROT_FREQS, 1), pltpu.roll(x, ROT_FREQS, 1))
    return x * cos + swapped * sin_signed


def _attn_kernel(*refs, tq, seq_len, group, n_kv, kvs, n_ctx, use_sink, window, qk_norm, rope, emit_k):
    it = iter(refs)
    q_ref, k_ref, v_ref = next(it), next(it), next(it)
    ck_ref = cv_ref = cos_ref = sin_ref = qg_ref = kg_ref = sink_ref = kn_out_ref = None
    if n_ctx:
        ck_ref, cv_ref = next(it), next(it)
    if rope:
        cos_ref, sin_ref = next(it), next(it)
    if qk_norm:
        qg_ref, kg_ref = next(it), next(it)
    if use_sink:
        sink_ref = next(it)
    o_ref = next(it)
    if emit_k:
        kn_out_ref = next(it)
    kbf_ref, vbf_ref = next(it), next(it)
    ckbf_ref = cvbf_ref = None
    if n_ctx:
        ckbf_ref, cvbf_ref = next(it), next(it)

    L = seq_len
    kvg = pl.program_id(1)
    qi = pl.program_id(2)

    def ones_column(rows):
        lane = lax.broadcasted_iota(jnp.int32, (rows, HEAD_DIM), 1)
        return jnp.where(lane == 0, 1.0, 0.0).astype(BF16)

    @pl.when(qi == 0)
    def _():
        for kk in range(kvs):
            cols = slice(kk * HEAD_DIM, (kk + 1) * HEAD_DIM)
            k = k_ref[:, cols]
            if qk_norm:
                k = _rms_rows(k, kg_ref[...])
            if emit_k:
                for hd in range(n_kv):
                    @pl.when(kvg * kvs + kk == hd)
                    def _():
                        kn_out_ref[pl.ds(hd, L, stride=n_kv), :] = k
            if rope:
                k = _rope_rows(k, cos_ref[...], sin_ref[...])
            kbf_ref[kk] = k.astype(BF16)
            vbf_ref[kk, :, :HEAD_DIM] = v_ref[:, cols].astype(BF16)
            vbf_ref[kk, :, HEAD_DIM:] = ones_column(L)
            if n_ctx:
                ckbf_ref[kk] = ck_ref[:, cols].astype(BF16)
                cvbf_ref[kk, :, :HEAD_DIM] = cv_ref[:, cols].astype(BF16)
                cvbf_ref[kk, :, HEAD_DIM:] = ones_column(n_ctx)

    q0 = pl.multiple_of(qi * tq, tq)
    if window:
        wk = tq + 2 * window
        ws = pl.multiple_of(jnp.clip(q0 - window, 0, L - wk), LANES)
        qpos = q0 + lax.broadcasted_iota(jnp.int32, (tq, wk), 0)
        kpos = ws + lax.broadcasted_iota(jnp.int32, (tq, wk), 1)
        valid = jnp.abs(qpos - kpos) <= window

    def scores(g):
        kk = g // group
        q = q_ref[:, g * HEAD_DIM:(g + 1) * HEAD_DIM]
        if qk_norm:
            q = _rms_rows(q, qg_ref[...])
        if rope:
            q = _rope_rows(q, cos_ref[pl.ds(q0, tq), :], sin_ref[pl.ds(q0, tq), :])
        q = (q * (HEAD_DIM ** -0.5 * LOG2_E)).astype(BF16)
        if window:
            s_loc = jnp.where(valid, _dot_nt(q, kbf_ref[kk, pl.ds(ws, wk), :]), NEG)
            pieces = [(s_loc, vbf_ref[kk, pl.ds(ws, wk), :])]
        else:
            pieces = [(_dot_nt(q, kbf_ref[kk]), vbf_ref[kk])]
        if n_ctx:
            pieces.append((_dot_nt(q, ckbf_ref[kk]), cvbf_ref[kk]))
        return pieces

    def finish(g, pieces):
        m = None
        for s, _ in pieces:
            row_max = jnp.max(s, axis=-1, keepdims=True)
            m = row_max if m is None else jnp.maximum(m, row_max)
        if use_sink:
            sink = sink_ref[kvg * (kvs * group) + g] * LOG2_E
            m = jnp.maximum(m, sink)
        acc = None
        for s, v in pieces:
            term = _dot(jnp.exp2(s - m).astype(BF16), v)
            acc = term if acc is None else acc + term
        denom = acc[:, HEAD_DIM:HEAD_DIM + 1]
        if use_sink:
            denom = denom + jnp.exp2(sink - m)
        o_ref[:, g * HEAD_DIM:(g + 1) * HEAD_DIM] = (acc[:, :HEAD_DIM] / denom).astype(o_ref.dtype)

    ahead = scores(0)
    for g in range(kvs * group):
        cur = ahead
        if g + 1 < kvs * group:
            ahead = scores(g + 1)
        finish(g, cur)


def _attention(proj, *, q_off, k_off, v_off, n_heads, n_kv, row_block0, n_seq, seq_len, tq,
               kvs=1, ctx=None, ctx_layer=0, cos=None, sin=None, q_gain=None, k_gain=None, sink=None,
               window=0, emit_k=False):
    L = seq_len
    group = n_heads // n_kv
    n_ctx = 0 if ctx is None else ctx.shape[3]
    rope = cos is not None
    qk_norm = q_gain is not None
    use_sink = sink is not None
    nq = L // tq
    q_w, kv_w = kvs * group * HEAD_DIM, kvs * HEAD_DIM
    assert n_kv % kvs == 0 and q_off % q_w == 0 and k_off % kv_w == 0 and v_off % kv_w == 0
    qb, kb, vb = q_off // q_w, k_off // kv_w, v_off // kv_w
    in_specs = [
        pl.BlockSpec((tq, q_w), lambda b, h, i: ((row_block0 + b) * nq + i, qb + h)),
        pl.BlockSpec((L, kv_w), lambda b, h, i: (row_block0 + b, kb + h)),
        pl.BlockSpec((L, kv_w), lambda b, h, i: (row_block0 + b, vb + h)),
    ]
    args = [proj, proj, proj]
    scratch = [pltpu.VMEM((kvs, L, HEAD_DIM), BF16), pltpu.VMEM((kvs, L, 2 * HEAD_DIM), BF16)]
    if n_ctx:
        in_specs += [pl.BlockSpec((None, None, None, n_ctx, kv_w), lambda b, h, i: (b, ctx_layer, 0, 0, h)),
                     pl.BlockSpec((None, None, None, n_ctx, kv_w), lambda b, h, i: (b, ctx_layer, 1, 0, h))]
        args += [ctx, ctx]
        scratch += [pltpu.VMEM((kvs, n_ctx, HEAD_DIM), BF16), pltpu.VMEM((kvs, n_ctx, 2 * HEAD_DIM), BF16)]
    if rope:
        in_specs += [pl.BlockSpec((L, HEAD_DIM), lambda b, h, i: (0, 0))] * 2
        args += [cos, sin]
    if qk_norm:
        in_specs += [pl.BlockSpec((1, HEAD_DIM), lambda b, h, i: (0, 0))] * 2
        args += [q_gain.reshape(1, HEAD_DIM), k_gain.reshape(1, HEAD_DIM)]
    if use_sink:
        in_specs += [pl.BlockSpec(memory_space=pltpu.SMEM)]
        args += [sink]
    out_shape = [jax.ShapeDtypeStruct((n_seq * L, n_heads * HEAD_DIM), BF16)]
    out_specs = [pl.BlockSpec((tq, q_w), lambda b, h, i: (b * nq + i, h))]
    if emit_k:
        out_shape += [jax.ShapeDtypeStruct((n_seq * L * n_kv, HEAD_DIM), F32)]
        out_specs += [pl.BlockSpec((L * n_kv, HEAD_DIM), lambda b, h, i: (b, 0))]
    kern = functools.partial(_attn_kernel, tq=tq, seq_len=L, group=group, n_kv=n_kv, kvs=kvs, n_ctx=n_ctx,
                             use_sink=use_sink, window=window, qk_norm=qk_norm, rope=rope, emit_k=emit_k)
    res = pl.pallas_call(
        kern,
        out_shape=tuple(out_shape),
        grid=(n_seq, n_kv // kvs, nq),
        in_specs=in_specs,
        out_specs=tuple(out_specs),
        scratch_shapes=scratch,
        compiler_params=pltpu.CompilerParams(dimension_semantics=("arbitrary", "arbitrary", "arbitrary"),
                                             vmem_limit_bytes=VMEM_LIMIT_BYTES),
        name="attention",
    )(*args)
    return res if emit_k else res[0]


def _out_mlp_kernel(*refs, n_x, n_parts, final, n_prompt_tiles, tf):
    it = iter(refs)
    x_refs = [next(it) for _ in range(n_x)]
    a_refs = [[next(it), next(it)] for _ in range(n_parts)]
    wo_refs = [next(it) for _ in range(n_parts)]
    mod_ref, g2_ref, w1_ref, w2_ref = next(it), next(it), next(it), next(it)
    fg_ref = next(it) if final else None
    o_refs = [next(it) for _ in range(2 if final else 1)]
    i = pl.program_id(0)
    d_ff = w1_ref.shape[1]

    mix = None
    for pair, wo_ref in zip(a_refs, wo_refs):
        term = _dot(_token_tile(pair, n_prompt_tiles), wo_ref[...])
        mix = term if mix is None else mix + term
    x1 = _token_tile(x_refs, n_prompt_tiles) + mod_ref[2:3, :] * mix
    h2 = (_rms_rows(x1, g2_ref[...]) * (1.0 + mod_ref[4:5, :]) + mod_ref[3:4, :]).astype(BF16)
    acc = None
    for c in range(d_ff // tf):
        hid = jnp.maximum(_dot(h2, w1_ref[:, c * tf:(c + 1) * tf]), 0.0)
        term = _dot((hid * hid).astype(BF16), w2_ref[c * tf:(c + 1) * tf, :])
        acc = term if acc is None else acc + term
    y = x1 + mod_ref[5:6, :] * acc

    if final:
        y = _rms_rows(y, fg_ref[...])

        @pl.when(i < n_prompt_tiles)
        def _():
            o_refs[0][...] = y

        @pl.when(i >= n_prompt_tiles)
        def _():
            o_refs[1][...] = y
    else:
        o_refs[0][...] = y


def _out_mlp(xs, parts, w_out, wo_layer, mod, layer, g2, w1, w2, n_prompt, sample_len, final_g=None, tm=512,
             tf=1024):
    n_tok = sum(a.shape[0] for a in xs)
    d = xs[0].shape[1]
    d_ff = w1.shape[2]
    final = final_g is not None
    npt = n_prompt // tm
    row = functools.partial(_cond_row, n_prompt_tiles=npt, tiles_per_sample=sample_len // tm)
    in_specs = _token_specs(xs, tm, npt)
    for pair in parts:
        in_specs += _token_specs(pair, tm, npt)
    resident = lambda shape: pl.BlockSpec((None, *shape), lambda i: (layer, 0, 0), pipeline_mode=pl.Buffered(1))
    width = w_out.shape[1] // len(parts)
    in_specs += [pl.BlockSpec((None, width, d), lambda i, p=p: (wo_layer, p, 0), pipeline_mode=pl.Buffered(1))
                 for p in range(len(parts))]
    in_specs += [
        pl.BlockSpec((None, None, 6, d), lambda i: (layer, row(i), 0, 0)),
        pl.BlockSpec((None, 1, d), lambda i: (layer, 0, 0)),
        resident((d, d_ff)),
        resident((d_ff, d)),
    ]
    args = [*xs, *[a for pair in parts for a in pair], *([w_out] * len(parts)), mod, g2, w1, w2]
    if final:
        in_specs += [pl.BlockSpec((1, d), lambda i: (0, 0))]
        args += [final_g.reshape(1, d)]
        out_shape = (jax.ShapeDtypeStruct((n_prompt, d), F32), jax.ShapeDtypeStruct((n_tok - n_prompt, d), F32))
        out_specs = (pl.BlockSpec((tm, d), lambda i: (jnp.minimum(i, npt - 1), 0)),
                     pl.BlockSpec((tm, d), lambda i: (jnp.maximum(i - npt, 0), 0)))
    else:
        out_shape = jax.ShapeDtypeStruct((n_tok, d), F32)
        out_specs = pl.BlockSpec((tm, d), lambda i: (i, 0))
    kern = functools.partial(_out_mlp_kernel, n_x=len(xs), n_parts=len(parts), final=final, n_prompt_tiles=npt,
                             tf=tf)
    return pl.pallas_call(
        kern,
        out_shape=out_shape,
        grid=(n_tok // tm,),
        in_specs=in_specs,
        out_specs=out_specs,
        compiler_params=pltpu.CompilerParams(dimension_semantics=("arbitrary",),
                                             vmem_limit_bytes=VMEM_LIMIT_BYTES),
        name="out_mlp",
    )(*args)


def _rope_tables(seq_len):
    rows = seq_len // GRID_W
    inv = ROPE_THETA ** (-np.arange(ROT_FREQS, dtype=np.float64) / ROT_FREQS)
    ang_r = np.repeat(np.arange(rows, dtype=np.float64), GRID_W)[:, None] * inv
    ang_c = np.tile(np.arange(GRID_W, dtype=np.float64), rows)[:, None] * inv
    cos = np.concatenate([np.cos(ang_r), np.cos(ang_r), np.cos(ang_c), np.cos(ang_c)], axis=-1)
    sin = np.concatenate([-np.sin(ang_r), np.sin(ang_r), -np.sin(ang_c), np.sin(ang_c)], axis=-1)
    return jnp.asarray(cos, dtype=F32), jnp.asarray(sin, dtype=F32)


def kernel(x_prompt, x_sample, state_a, cache_b_kv, cache_c_kv, c, c_ctx, ada_w, ada_b, norm1_g, norm2_g,
           final_g, mlp_w1, mlp_w2, ev_w_in, a_conv, a_log, a_dt_bias, a_norm_g, b_sink, ev_w_out,
           od_w_in, c_qnorm_g, c_knorm_g, od_w_out):
    batch, seq, d = x_prompt.shape
    dec_batch, dec_seq, _ = x_sample.shape
    depth = ada_w.shape[0]
    n_prompt, n_sample = batch * seq, dec_batch * dec_seq
    past = cache_b_kv.shape[3]

    xs = [x_prompt.reshape(n_prompt, d), x_sample.reshape(n_sample, d)]
    cond_rows = -(-(1 + dec_batch) // SUBLANES) * SUBLANES
    cond = jnp.concatenate([c_ctx[None, :], c, jnp.zeros((cond_rows - 1 - dec_batch, d), F32)], axis=0)
    mod = _adaln(cond, ada_w, ada_b).reshape(depth, cond_rows, 6, d)

    cos, sin = _rope_tables(dec_seq)
    s_blk0 = n_prompt // dec_seq
    ctx_b = cache_b_kv.reshape(*cache_b_kv.shape[:4], B_KV_HEADS * HEAD_DIM)
    ctx_c = cache_c_kv.reshape(*cache_c_kv.shape[:4], C_KV_HEADS * HEAD_DIM)
    ev_w_in_bf16 = _ev_w_in(jnp.swapaxes(ev_w_in, 1, 2))
    od_w_in_bf16 = od_w_in.astype(BF16)
    ev_w_out_bf16 = ev_w_out.astype(BF16)
    od_w_out_bf16 = od_w_out.astype(BF16)
    w1 = mlp_w1.astype(BF16)
    w2 = mlp_w2.astype(BF16)
    g1 = norm1_g.reshape(depth, 1, d)
    g2 = norm2_g.reshape(depth, 1, d)
    lane_pad = lambda v: jnp.zeros((1, LANES), F32).at[0, 2 * A_HEADS:4 * A_HEADS].set(v.reshape(-1))

    new_a, new_b, new_c = [], [], []
    for l in range(depth):
        i = l // 2
        fin = final_g if l == depth - 1 else None
        if l % 2 == 0:
            kv_cols = B_KV_HEADS * HEAD_DIM
            proj, (cache_k, cache_v), (beta_all, gc_all) = _in_proj(
                xs, mod, l, g1, ev_w_in_bf16, i, n_prompt, dec_seq,
                cache_cols=(EV_KVB_OFF, EV_KVB_OFF + kv_cols), n_kv=B_KV_HEADS,
                gate_rows=(lane_pad(a_log[i]), lane_pad(a_dt_bias[i])), w_transposed=True)
            oa_p, st_p = _gdn(proj, beta_all, gc_all, a_conv[i], a_norm_g[i], 0, batch, seq, heads=A_HEADS,
                              steps=seq // CHUNK)
            oa_s, _ = _gdn(proj, beta_all, gc_all, a_conv[i], a_norm_g[i], s_blk0, dec_batch, dec_seq, heads=1,
                           steps=8, s0=state_a, s0_layer=i)
            ob_p = _attention(proj, q_off=EV_QB_OFF, k_off=EV_KVB_OFF, v_off=EV_KVB_OFF + kv_cols,
                              n_heads=B_HEADS, n_kv=B_KV_HEADS, row_block0=0, n_seq=batch, seq_len=seq,
                              tq=seq, kvs=B_KV_HEADS, sink=b_sink[i])
            ob_s = _attention(proj, q_off=EV_QB_OFF, k_off=EV_KVB_OFF, v_off=EV_KVB_OFF + kv_cols,
                              n_heads=B_HEADS, n_kv=B_KV_HEADS, row_block0=s_blk0, n_seq=dec_batch,
                              seq_len=dec_seq, tq=512,
                              ctx=ctx_b, ctx_layer=i, cos=cos, sin=sin, sink=b_sink[i], window=WINDOW)
            new_a.append(st_p)
            new_b.append(jnp.stack([cache_k.reshape(batch, seq, B_KV_HEADS, HEAD_DIM),
                                    cache_v.reshape(batch, seq, B_KV_HEADS, HEAD_DIM)], axis=1))
            parts = [[oa_p, oa_s], [ob_p, ob_s]]
            w_out = ev_w_out_bf16
        else:
            kv_cols = C_KV_HEADS * HEAD_DIM
            proj, (cache_v,), _ = _in_proj(xs, mod, l, g1, od_w_in_bf16, i, n_prompt, dec_seq,
                                           cache_cols=(OD_V_OFF,), n_kv=C_KV_HEADS)
            oc_p, kn_p = _attention(proj, q_off=0, k_off=OD_K_OFF, v_off=OD_V_OFF, n_heads=C_HEADS,
                                    n_kv=C_KV_HEADS, row_block0=0, n_seq=batch, seq_len=seq, tq=seq, kvs=C_KV_HEADS,
                                    q_gain=c_qnorm_g[i], k_gain=c_knorm_g[i], emit_k=True)
            oc_s = _attention(proj, q_off=0, k_off=OD_K_OFF, v_off=OD_V_OFF, n_heads=C_HEADS,
                              n_kv=C_KV_HEADS, row_block0=s_blk0, n_seq=dec_batch, seq_len=dec_seq, tq=256,
                              ctx=ctx_c, ctx_layer=i, cos=cos, sin=sin, q_gain=c_qnorm_g[i],
                              k_gain=c_knorm_g[i])
            new_c.append(jnp.stack([kn_p.reshape(batch, seq, C_KV_HEADS, HEAD_DIM),
                                    cache_v.reshape(batch, seq, C_KV_HEADS, HEAD_DIM)], axis=1))
            parts = [[oc_p, oc_s]]
            w_out = od_w_out_bf16
        out = _out_mlp(xs, parts, w_out, i, mod, l, g2, w1, w2, n_prompt, dec_seq, final_g=fin)
        xs = list(out) if fin is not None else [out]

    y_prompt = xs[0].reshape(batch, seq, d)
    y_sample = xs[1].reshape(dec_batch, dec_seq, d)
    return (y_prompt, y_sample, jnp.stack(new_a, axis=1), jnp.stack(new_b, axis=1), jnp.stack(new_c, axis=1))
```

```python
import functools

import numpy as np
import jax
import jax.numpy as jnp
from jax import lax
from jax.experimental import pallas as pl
from jax.experimental.pallas import tpu as pltpu

F32 = jnp.float32
BF16 = jnp.bfloat16

HEAD_DIM = 128
GRID_W = 64
A_HEADS = 4
CHUNK = 64
B_HEADS = 4
B_KV_HEADS = 2
WINDOW = 128
C_HEADS = 8
C_KV_HEADS = 2
ROT_FREQS = HEAD_DIM // 4
ROPE_THETA = 10000.0
EPS = 1e-6
NEG = -1e30
LOG2_E = 1.4426950408889634
A_WIDTH = A_HEADS * HEAD_DIM
B_WIDTH = B_HEADS * HEAD_DIM
C_WIDTH = C_HEADS * HEAD_DIM

SUBLANES = 8
LANES = 128
VMEM_LIMIT_BYTES = 56 * 1024 * 1024

EV_QB_OFF = 4 * A_WIDTH
EV_KVB_OFF = EV_QB_OFF + B_WIDTH
EV_BG_OFF = EV_KVB_OFF + 2 * B_KV_HEADS * HEAD_DIM
EV_COLS = EV_BG_OFF + LANES
OD_K_OFF = C_WIDTH
OD_V_OFF = C_WIDTH + C_KV_HEADS * HEAD_DIM


def _sigmoid(x):
    return 1.0 / (1.0 + jnp.exp(-x))


def _silu(x):
    return x * _sigmoid(x)


def _softplus(x):
    return jnp.maximum(x, 0.0) + jnp.log1p(jnp.exp(-jnp.abs(x)))


def _rms_rows(x, g):
    return x * lax.rsqrt(jnp.mean(x * x, axis=-1, keepdims=True) + EPS) * g


def _dot(a, b):
    return jnp.dot(a, b, preferred_element_type=F32)


def _dot_nt(a, b):
    return lax.dot_general(a, b, (((1,), (1,)), ((), ())), preferred_element_type=F32)


def _cond_row(i, n_prompt_tiles, tiles_per_sample):
    return jnp.where(i < n_prompt_tiles, 0, 1 + (i - n_prompt_tiles) // tiles_per_sample)


def _split_bf16(x):
    hi = x.astype(BF16)
    lo = (x - hi.astype(F32)).astype(BF16)
    return hi, lo


def _adaln_kernel(cond_ref, w_ref, b_ref, o_ref):
    rows = cond_ref.shape[0]
    s_hi, s_lo = _split_bf16(_silu(cond_ref[...]))
    w_hi, w_lo = _split_bf16(w_ref[0])
    r = _dot(jnp.concatenate([s_hi, s_lo], axis=0), w_hi)
    o_ref[0] = r[:rows] + r[rows:] + _dot(s_hi, w_lo) + b_ref[0]


def _adaln(cond, ada_w, ada_b, tn=2048):
    depth, d, n = ada_w.shape
    rows = cond.shape[0]
    return pl.pallas_call(
        _adaln_kernel,
        out_shape=jax.ShapeDtypeStruct((depth, rows, n), F32),
        grid=(depth, n // tn),
        in_specs=[
            pl.BlockSpec((rows, d), lambda l, j: (0, 0)),
            pl.BlockSpec((1, d, tn), lambda l, j: (l, 0, j)),
            pl.BlockSpec((1, 1, tn), lambda l, j: (l, 0, j)),
        ],
        out_specs=pl.BlockSpec((1, rows, tn), lambda l, j: (l, 0, j)),
        compiler_params=pltpu.CompilerParams(dimension_semantics=("arbitrary", "arbitrary"),
                                             vmem_limit_bytes=VMEM_LIMIT_BYTES),
        name="adaln",
    )(cond, ada_w, ada_b.reshape(depth, 1, n))


def _ev_w_in_kernel(w_ref, o_ref):
    bg_off = 4 * A_WIDTH
    bg_end = bg_off + 4 * A_HEADS
    d = w_ref.shape[1]
    o_ref[:bg_off, :] = w_ref[:bg_off, :].astype(BF16)
    o_ref[bg_off:EV_BG_OFF, :] = w_ref[bg_end:, :].astype(BF16)
    o_ref[EV_BG_OFF:EV_BG_OFF + 4 * A_HEADS, :] = w_ref[bg_off:bg_end, :].astype(BF16)
    o_ref[EV_BG_OFF + 4 * A_HEADS:, :] = jnp.zeros((LANES - 4 * A_HEADS, d), BF16)


def _ev_w_in(w_t):
    layers, n, d = w_t.shape
    return pl.pallas_call(
        _ev_w_in_kernel,
        out_shape=jax.ShapeDtypeStruct((layers, EV_COLS, d), BF16),
        grid=(layers,),
        in_specs=[pl.BlockSpec((None, n, d), lambda l: (l, 0, 0))],
        out_specs=pl.BlockSpec((None, EV_COLS, d), lambda l: (l, 0, 0)),
        compiler_params=pltpu.CompilerParams(dimension_semantics=("arbitrary",),
                                             vmem_limit_bytes=VMEM_LIMIT_BYTES),
        name="ev_w_in",
    )(w_t)


def _token_specs(arrays, tm, n_prompt_tiles):
    if len(arrays) == 1:
        return [pl.BlockSpec((tm, arrays[0].shape[1]), lambda i, *_: (i, 0))]
    return [pl.BlockSpec((tm, arrays[0].shape[1]), lambda i, *_: (jnp.minimum(i, n_prompt_tiles - 1), 0)),
            pl.BlockSpec((tm, arrays[1].shape[1]), lambda i, *_: (jnp.maximum(i - n_prompt_tiles, 0), 0))]


def _token_tile(refs, n_prompt_tiles):
    if len(refs) == 1:
        return refs[0][...]
    return jnp.where(pl.program_id(0) < n_prompt_tiles, refs[0][...], refs[1][...])


def _gate_columns(bg, a_row, dt_row):
    rows = bg.shape[0]
    g = -jnp.exp(a_row) * _softplus(bg + dt_row)
    chunk_row = lax.broadcasted_iota(jnp.int32, (rows, LANES), 0) % CHUNK
    lane = lax.broadcasted_iota(jnp.int32, (rows, LANES), 1)
    pre, suf = g, g
    shift = 1
    while shift < CHUNK:
        pre = pre + jnp.where(chunk_row >= shift, pltpu.roll(pre, shift, 0), 0.0)
        suf = suf + jnp.where(chunk_row < CHUNK - shift, pltpu.roll(suf, rows - shift, 0), 0.0)
        shift *= 2
    return _sigmoid(bg), jnp.where(lane >= 3 * A_HEADS, suf, pre)


def _inproj_kernel(*refs, n_x, n_prompt_tiles, cache_cols, n_kv, gates, w_transposed):
    it = iter(refs)
    x_refs = [next(it) for _ in range(n_x)]
    mod_ref, g_ref, w_ref = next(it), next(it), next(it)
    a_ref, dt_ref = (next(it), next(it)) if gates else (None, None)
    o_ref = next(it)
    cache_refs = [next(it) for _ in cache_cols]
    x = _token_tile(x_refs, n_prompt_tiles)
    h = _rms_rows(x, g_ref[...]) * (1.0 + mod_ref[1:2, :]) + mod_ref[0:1, :]
    o = (_dot_nt if w_transposed else _dot)(h.astype(BF16), w_ref[...])
    o_ref[...] = o
    if gates:
        beta_ref, gc_ref = next(it), next(it)
        beta_ref[...], gc_ref[...] = _gate_columns(o[:, EV_BG_OFF:EV_BG_OFF + LANES], a_ref[...], dt_ref[...])

    @pl.when(pl.program_id(0) < n_prompt_tiles)
    def _():
        tm = o.shape[0]
        for c_ref, col in zip(cache_refs, cache_cols):
            for hd in range(n_kv):
                c_ref[pl.ds(hd, tm, stride=n_kv), :] = o[:, col + hd * HEAD_DIM:col + (hd + 1) * HEAD_DIM]


def _in_proj(xs, mod, layer, g, w, w_layer, n_prompt, sample_len, cache_cols, n_kv, gate_rows=None,
             w_transposed=False, tm=512):
    n_tok = sum(a.shape[0] for a in xs)
    d = xs[0].shape[1]
    n = w.shape[1] if w_transposed else w.shape[2]
    npt = n_prompt // tm
    assert tm % CHUNK == 0
    gates = gate_rows is not None
    row = functools.partial(_cond_row, n_prompt_tiles=npt, tiles_per_sample=sample_len // tm)
    cache_spec = pl.BlockSpec((tm * n_kv, HEAD_DIM), lambda i: (jnp.minimum(i, npt - 1), 0))
    lane_row = pl.BlockSpec((1, LANES), lambda i: (0, 0))
    gate_spec = pl.BlockSpec((tm, LANES), lambda i: (i, 0))
    res = pl.pallas_call(
        functools.partial(_inproj_kernel, n_x=len(xs), n_prompt_tiles=npt, cache_cols=cache_cols, n_kv=n_kv,
                          gates=gates, w_transposed=w_transposed),
        out_shape=(jax.ShapeDtypeStruct((n_tok, n), F32),
                   *[jax.ShapeDtypeStruct((n_prompt * n_kv, HEAD_DIM), F32) for _ in cache_cols],
                   *[jax.ShapeDtypeStruct((n_tok, LANES), F32) for _ in range(2 if gates else 0)]),
        grid=(n_tok // tm,),
        in_specs=_token_specs(xs, tm, npt) + [
            pl.BlockSpec((None, None, 6, d), lambda i: (layer, row(i), 0, 0)),
            pl.BlockSpec((None, 1, d), lambda i: (layer, 0, 0)),
            pl.BlockSpec((None, *w.shape[1:]), lambda i: (w_layer, 0, 0)),
        ] + ([lane_row, lane_row] if gates else []),
        out_specs=(pl.BlockSpec((tm, n), lambda i: (i, 0)), *[cache_spec for _ in cache_cols],
                   *([gate_spec, gate_spec] if gates else [])),
        compiler_params=pltpu.CompilerParams(dimension_semantics=("arbitrary",),
                                             vmem_limit_bytes=VMEM_LIMIT_BYTES),
        name="in_proj",
    )(*xs, mod, g, w, *(gate_rows if gates else ()))
    n_c = len(cache_cols)
    return res[0], res[1:1 + n_c], res[1 + n_c:]


def _gdn_kernel(q_ref, k_ref, v_ref, gate_ref, betaall_ref, gcall_ref, cwq_ref, cwk_ref, cwv_ref, ng_ref, *rest,
                seq_len, heads, steps, has_state):
    s0_ref = rest[0] if has_state else None
    (o_ref, sout_ref, qn_ref, kn_ref, vn_ref, beta_ref, gc_ref, gcrow_ref, s_ref, oacc_ref,
     mq_a, c_a, oo_a, eg_a, mq_b, c_b, oo_b, eg_b) = rest[1:] if has_state else rest
    L = seq_len
    nc = L // CHUNK
    nblk = nc // steps
    h0 = pl.program_id(1) * heads
    row_idx = lax.broadcasted_iota(jnp.int32, (L, LANES), 0)
    lane_idx = lax.broadcasted_iota(jnp.int32, (L, LANES), 1)

    def conv_silu(x, cw):
        prev = jnp.where(row_idx == 0, 0.0, pltpu.roll(x, 1, 0))
        nxt = jnp.where(row_idx == L - 1, 0.0, pltpu.roll(x, L - 1, 0))
        return _silu(prev * cw[0:1, :] + x * cw[1:2, :] + nxt * cw[2:3, :])

    def l2norm(x):
        return x * lax.rsqrt(jnp.sum(x * x, axis=-1, keepdims=True) + EPS)

    beta_all = betaall_ref[...]
    gc_all = gcall_ref[...]

    def pick(x, col):
        col_vals = jnp.sum(jnp.where(lane_idx == col, x, 0.0), axis=1, keepdims=True)
        return jnp.broadcast_to(col_vals, (L, LANES))

    for r in range(L // LANES):
        rows = jnp.transpose(gc_all[r * LANES:(r + 1) * LANES, :])[2 * A_HEADS:4 * A_HEADS, :]
        gcrow_ref[0, r] = rows
        gcrow_ref[1, r] = pltpu.roll(rows, CHUNK, 1)

    for hh in range(heads):
        cols = slice(hh * HEAD_DIM, (hh + 1) * HEAD_DIM)
        qn_ref[:, cols] = l2norm(conv_silu(q_ref[:, cols], cwq_ref[:, cols])) * (HEAD_DIM ** -0.5)
        kn_ref[:, cols] = l2norm(conv_silu(k_ref[:, cols], cwk_ref[:, cols]))
        vn_ref[:, cols] = conv_silu(v_ref[:, cols], cwv_ref[:, cols])
        for d in range(2):
            s_ref[hh, d] = s0_ref[d, hh] if has_state else jnp.zeros((HEAD_DIM, HEAD_DIM), F32)
        head = h0 + hh
        for d in range(2):
            beta_ref[hh, d] = pick(beta_all, d * A_HEADS + head)
            gc_ref[hh, d] = pick(gc_all, (2 + d) * A_HEADS + head)

    r64 = lax.broadcasted_iota(jnp.int32, (CHUNK, LANES), 0)
    l64 = lax.broadcasted_iota(jnp.int32, (CHUNK, LANES), 1)
    is_f = l64 < CHUNK
    is_f_row = lax.broadcasted_iota(jnp.int32, (1, LANES), 1) < CHUNK
    cpos = jnp.where(is_f, l64, l64 - CHUNK)
    ahead = jnp.where(is_f, r64 - cpos, cpos - r64)
    incl = ahead >= 0
    strict = ahead > 0
    eye = jnp.where(r64 == cpos, 1.0, 0.0)
    is_f2 = lax.broadcasted_iota(jnp.int32, (2 * CHUNK, LANES), 1) < CHUNK
    zeros_c = jnp.zeros((CHUNK, HEAD_DIM), F32)

    def aligned(off, m):
        return off if isinstance(off, int) else pl.multiple_of(off, m)

    def half(n):
        return n // 2 if isinstance(n, int) else lax.shift_right_logical(n, 1)

    def block_diag(xp):
        z = jnp.zeros_like(xp)
        return jnp.concatenate([jnp.where(is_f, xp, z), jnp.where(is_f, z, xp)], axis=0)

    def split_dot(a_hi, a_lo, b_hi, b_lo):
        lhs = jnp.concatenate([a_hi, a_lo, a_hi], axis=1)
        rhs = jnp.concatenate([b_hi, b_hi, b_lo], axis=0)
        return _dot(lhs, rhs)

    def prep_stages(blk, slot):
        mq_ref, c_ref, oo_ref, eg_ref = slot
        probs = [(si, hh) for si in range(steps) for hh in range(heads)]
        st = {p: {} for p in probs}

        def load(p):
            si, hh = p
            j = blk * steps + si
            n_b = nc - 1 - j
            cols = slice(hh * HEAD_DIM, (hh + 1) * HEAD_DIM)
            off_f = aligned(j * CHUNK, CHUNK)
            off_b = aligned(n_b * CHUNK, CHUNK)
            q_f, k_f, v_f = (r[pl.ds(off_f, CHUNK), cols] for r in (qn_ref, kn_ref, vn_ref))
            q_b, k_b, v_b = (r[pl.ds(off_b, CHUNK), cols] for r in (qn_ref, kn_ref, vn_ref))
            beta_f = beta_ref[hh, 0, pl.ds(off_f, CHUNK), :]
            beta_b = beta_ref[hh, 1, pl.ds(off_b, CHUNK), :]
            gc_f = gc_ref[hh, 0, pl.ds(off_f, CHUNK), :]
            gc_b = gc_ref[hh, 1, pl.ds(off_b, CHUNK), :]
            row_f = gcrow_ref[si % 2, half(j), pl.ds(h0 + hh, 1), :]
            row_b = gcrow_ref[si % 2, half(n_b), pl.ds(A_HEADS + h0 + hh, 1), :]
            gcrow = jnp.where(is_f_row, row_f, row_b)
            diff = jnp.where(is_f, gc_f, gc_b) - gcrow
            decay = jnp.where(incl, jnp.exp(jnp.where(incl, diff, 0.0)), 0.0)
            kb_f = k_f * beta_f
            kb_b = k_b * beta_b
            e_f = jnp.exp(gc_f)
            e_b = jnp.exp(gc_b)
            glast_f = gc_f[CHUNK - 1:CHUNK, :]
            glast_b = gc_b[0:1, :]
            kdt = jnp.transpose(jnp.concatenate([k_f * jnp.exp(glast_f - gc_f), k_b * jnp.exp(glast_b - gc_b)],
                                                axis=0))
            zk = jnp.zeros_like(kdt)
            eg_ref[si, hh, 0] = jnp.broadcast_to(jnp.exp(glast_f), (SUBLANES, LANES))
            eg_ref[si, hh, 1] = jnp.broadcast_to(jnp.exp(glast_b), (SUBLANES, LANES))
            st[p].update(
                decay=decay,
                lhs=jnp.concatenate([jnp.concatenate([kb_f, kb_b], axis=1),
                                     jnp.concatenate([q_f, q_b], axis=1)], axis=0).astype(BF16),
                rhs=jnp.concatenate([jnp.concatenate([k_f, zeros_c], axis=1),
                                     jnp.concatenate([zeros_c, k_b], axis=1)], axis=0).astype(BF16),
                rhs_uw=jnp.concatenate([jnp.concatenate([v_f * beta_f, kb_f * e_f], axis=1),
                                        jnp.concatenate([v_b * beta_b, kb_b * e_b], axis=1)], axis=0).astype(BF16),
                kd2=jnp.concatenate([jnp.where(is_f2, kdt, zk), jnp.where(is_f2, zk, kdt)], axis=0).astype(BF16),
                qe=(q_f * e_f, q_b * e_b))

        def gram(p):
            s = st[p]
            aq = _dot_nt(s.pop("lhs"), s.pop("rhs"))
            decay = s.pop("decay")
            x = -jnp.where(strict, aq[:CHUNK] * decay, 0.0)
            s["qk"] = jnp.where(incl, aq[CHUNK:] * decay, 0.0)
            s["t"] = eye + x
            s["p"] = x

        def square(p):
            s = st[p]
            p_hi, p_lo = _split_bf16(s["p"])
            s["p"] = split_dot(p_hi, p_lo, block_diag(p_hi), block_diag(p_lo))

        def double(p):
            s = st[p]
            p_hi, p_lo = _split_bf16(s["p"])
            t_hi, t_lo = _split_bf16(s["t"])
            r = split_dot(p_hi, p_lo, jnp.concatenate([block_diag(t_hi), block_diag(p_hi)], axis=1),
                          jnp.concatenate([block_diag(t_lo), block_diag(p_lo)], axis=1))
            s["t"] = s["t"] + r[:, :LANES]
            s["p"] = r[:, LANES:]

        def last(p):
            s = st[p]
            p_hi, p_lo = _split_bf16(s.pop("p"))
            t_hi, t_lo = _split_bf16(s["t"])
            s["t"] = s["t"] + split_dot(p_hi, p_lo, block_diag(t_hi), block_diag(t_lo))

        def solve(p):
            s = st[p]
            s["uw"] = _dot(block_diag(s.pop("t")).astype(BF16), s.pop("rhs_uw"))

        def fold(p):
            si, hh = p
            s = st[p]
            lhs3 = jnp.concatenate([s.pop("kd2"), block_diag(s.pop("qk")).astype(BF16)], axis=0)
            big = _dot(lhs3, s.pop("uw").astype(BF16))
            qe = s.pop("qe")
            for d in range(2):
                rows_s = slice(d * 2 * CHUNK, (d + 1) * 2 * CHUNK)
                rows_o = slice(4 * CHUNK + d * CHUNK, 4 * CHUNK + (d + 1) * CHUNK)
                c_ref[si, hh, d] = big[rows_s, :HEAD_DIM]
                oo_ref[si, hh, d] = big[rows_o, :HEAD_DIM]
                mq_ref[si, hh, d] = jnp.concatenate([big[rows_s, HEAD_DIM:], qe[d] - big[rows_o, HEAD_DIM:]],
                                                    axis=0).astype(BF16)

        n_double = 0
        span = 2
        while 2 * span < CHUNK:
            n_double += 1
            span *= 2
        stage_fns = [load, gram, square] + [double] * n_double + [last, solve, fold]
        return [functools.partial(lambda fn: [fn(p) for p in probs], fn) for fn in stage_fns]

    def scan_step(blk, slot, si):
        mq_ref, c_ref, oo_ref, eg_ref = slot
        j = blk * steps + si
        for hh in range(heads):
            cols = slice(hh * HEAD_DIM, (hh + 1) * HEAD_DIM)
            for d, n in ((0, j), (1, nc - 1 - j)):
                s = s_ref[hh, d]
                r = _dot(mq_ref[si, hh, d], s.astype(BF16))
                s_ref[hh, d] = s * eg_ref[si, hh, d][0:1, :] - r[:2 * CHUNK] + c_ref[si, hh, d]
                oacc_ref[d, pl.ds(aligned(n * CHUNK, CHUNK), CHUNK), cols] = r[2 * CHUNK:] + oo_ref[si, hh, d]

    def run(prep, scan):
        stages = prep_stages(*prep) if prep else []
        n_scan = steps if scan else 0
        for k in range(max(len(stages), n_scan)):
            if k < len(stages):
                stages[k]()
            if k < n_scan:
                scan_step(scan[0], scan[1], k)

    slot_a = (mq_a, c_a, oo_a, eg_a)
    slot_b = (mq_b, c_b, oo_b, eg_b)
    run((0, slot_a), None)
    if nblk == 1:
        run(None, (0, slot_a))
    else:
        def body(i, carry):
            blk = 2 * i
            run((blk + 1, slot_b), (blk, slot_a))
            run((blk + 2, slot_a), (blk + 1, slot_b))
            return carry

        lax.fori_loop(0, nblk // 2 - 1, body, 0)
        run((nblk - 1, slot_b), (nblk - 2, slot_a))
        run(None, (nblk - 1, slot_b))

    for hh in range(heads):
        cols = slice(hh * HEAD_DIM, (hh + 1) * HEAD_DIM)
        sout_ref[0, hh] = s_ref[hh, 0]
        sout_ref[1, hh] = s_ref[hh, 1]
        o = oacc_ref[0, :, cols] + oacc_ref[1, :, cols]
        o = _rms_rows(o, ng_ref[...]) * _silu(gate_ref[:, cols])
        o_ref[:, cols] = o.astype(o_ref.dtype)


def _gdn(proj, beta_all, gc_all, conv_w, norm_g, row_block0, n_seq, seq_len, heads, steps, s0=None, s0_layer=0):
    L = seq_len
    nblk = L // CHUNK // steps
    assert L % (CHUNK * steps) == 0 and steps % 2 == 0 and 2 * CHUNK == LANES and A_HEADS % heads == 0
    assert nblk == 1 or nblk % 2 == 0
    width = heads * HEAD_DIM
    groups = A_HEADS // heads
    blk = lambda base: pl.BlockSpec((L, width), lambda b, h: (row_block0 + b, base * groups + h))
    wblk = lambda base: pl.BlockSpec((3, width), lambda b, h: (0, base * groups + h))
    state_spec = pl.BlockSpec((None, 2, heads, HEAD_DIM, HEAD_DIM), lambda b, h: (b, 0, h, 0, 0))
    slot = [pltpu.VMEM((steps, heads, 2, 3 * CHUNK, HEAD_DIM), BF16),
            pltpu.VMEM((steps, heads, 2, 2 * CHUNK, HEAD_DIM), F32),
            pltpu.VMEM((steps, heads, 2, CHUNK, HEAD_DIM), F32),
            pltpu.VMEM((steps, heads, 2, SUBLANES, LANES), F32)]
    kern = functools.partial(_gdn_kernel, seq_len=L, heads=heads, steps=steps, has_state=s0 is not None)
    state_in, state_arg = [], []
    if s0 is not None:
        state_in = [pl.BlockSpec((None, None, 2, heads, HEAD_DIM, HEAD_DIM), lambda b, h: (b, s0_layer, 0, h, 0, 0))]
        state_arg = [s0]
    out, s_out = pl.pallas_call(
        kern,
        out_shape=(jax.ShapeDtypeStruct((n_seq * L, A_WIDTH), BF16),
                   jax.ShapeDtypeStruct((n_seq, 2, A_HEADS, HEAD_DIM, HEAD_DIM), F32)),
        grid=(n_seq, groups),
        in_specs=[
            blk(0), blk(1), blk(2), blk(3),
            pl.BlockSpec((L, LANES), lambda b, h: (row_block0 + b, 0)),
            pl.BlockSpec((L, LANES), lambda b, h: (row_block0 + b, 0)),
            wblk(0), wblk(1), wblk(2),
            pl.BlockSpec((1, HEAD_DIM), lambda b, h: (0, 0)),
        ] + state_in,
        out_specs=(pl.BlockSpec((L, width), lambda b, h: (b, h)), state_spec),
        scratch_shapes=[
            pltpu.VMEM((L, width), F32), pltpu.VMEM((L, width), F32), pltpu.VMEM((L, width), F32),
            pltpu.VMEM((heads, 2, L, LANES), F32), pltpu.VMEM((heads, 2, L, LANES), F32),
            pltpu.VMEM((2, L // LANES, 2 * A_HEADS, LANES), F32),
            pltpu.VMEM((heads, 2, HEAD_DIM, HEAD_DIM), F32),
            pltpu.VMEM((2, L, width), F32),
        ] + slot + slot,
        compiler_params=pltpu.CompilerParams(dimension_semantics=("arbitrary", "arbitrary"),
                                             vmem_limit_bytes=VMEM_LIMIT_BYTES),
        name="gdn",
    )(proj, proj, proj, proj, beta_all, gc_all, conv_w, conv_w, conv_w, norm_g.reshape(1, HEAD_DIM), *state_arg)
    return out, s_out


def _rope_rows(x, cos, sin_signed):
    lane = lax.broadcasted_iota(jnp.int32, x.shape, 1)
    swapped = jnp.where((lane % (2 * ROT_FREQS)) < ROT_FREQS,
                        pltpu.roll(x, HEAD_DIM - ROT_FREQS, 1), pltpu.roll(x, ROT_FREQS, 1))
    return x * cos + swapped * sin_signed


def _attn_kernel(*refs, tq, seq_len, group, n_kv, kvs, n_ctx, use_sink, window, qk_norm, rope, emit_k):
    it = iter(refs)
    q_ref, k_ref, v_ref = next(it), next(it), next(it)
    ck_ref = cv_ref = cos_ref = sin_ref = qg_ref = kg_ref = sink_ref = kn_out_ref = None
    if n_ctx:
        ck_ref, cv_ref = next(it), next(it)
    if rope:
        cos_ref, sin_ref = next(it), next(it)
    if qk_norm:
        qg_ref, kg_ref = next(it), next(it)
    if use_sink:
        sink_ref = next(it)
    o_ref = next(it)
    if emit_k:
        kn_out_ref = next(it)
    kbf_ref, vbf_ref = next(it), next(it)
    ckbf_ref = cvbf_ref = None
    if n_ctx:
        ckbf_ref, cvbf_ref = next(it), next(it)

    L = seq_len
    kvg = pl.program_id(1)
    qi = pl.program_id(2)

    def ones_column(rows):
        lane = lax.broadcasted_iota(jnp.int32, (rows, HEAD_DIM), 1)
        return jnp.where(lane == 0, 1.0, 0.0).astype(BF16)

    @pl.when(qi == 0)
    def _():
        for kk in range(kvs):
            cols = slice(kk * HEAD_DIM, (kk + 1) * HEAD_DIM)
            k = k_ref[:, cols]
            if qk_norm:
                k = _rms_rows(k, kg_ref[...])
            if emit_k:
                for hd in range(n_kv):
                    @pl.when(kvg * kvs + kk == hd)
                    def _():
                        kn_out_ref[pl.ds(hd, L, stride=n_kv), :] = k
            if rope:
                k = _rope_rows(k, cos_ref[...], sin_ref[...])
            kbf_ref[kk] = k.astype(BF16)
            vbf_ref[kk, :, :HEAD_DIM] = v_ref[:, cols].astype(BF16)
            vbf_ref[kk, :, HEAD_DIM:] = ones_column(L)
            if n_ctx:
                ckbf_ref[kk] = ck_ref[:, cols].astype(BF16)
                cvbf_ref[kk, :, :HEAD_DIM] = cv_ref[:, cols].astype(BF16)
                cvbf_ref[kk, :, HEAD_DIM:] = ones_column(n_ctx)

    q0 = pl.multiple_of(qi * tq, tq)
    if window:
        wk = tq + 2 * window
        ws = pl.multiple_of(jnp.clip(q0 - window, 0, L - wk), LANES)
        qpos = q0 + lax.broadcasted_iota(jnp.int32, (tq, wk), 0)
        kpos = ws + lax.broadcasted_iota(jnp.int32, (tq, wk), 1)
        valid = jnp.abs(qpos - kpos) <= window

    def scores(g):
        kk = g // group
        q = q_ref[:, g * HEAD_DIM:(g + 1) * HEAD_DIM]
        if qk_norm:
            q = _rms_rows(q, qg_ref[...])
        if rope:
            q = _rope_rows(q, cos_ref[pl.ds(q0, tq), :], sin_ref[pl.ds(q0, tq), :])
        q = (q * (HEAD_DIM ** -0.5 * LOG2_E)).astype(BF16)
        if window:
            s_loc = jnp.where(valid, _dot_nt(q, kbf_ref[kk, pl.ds(ws, wk), :]), NEG)
            pieces = [(s_loc, vbf_ref[kk, pl.ds(ws, wk), :])]
        else:
            pieces = [(_dot_nt(q, kbf_ref[kk]), vbf_ref[kk])]
        if n_ctx:
            pieces.append((_dot_nt(q, ckbf_ref[kk]), cvbf_ref[kk]))
        return pieces

    def finish(g, pieces):
        m = None
        for s, _ in pieces:
            row_max = jnp.max(s, axis=-1, keepdims=True)
            m = row_max if m is None else jnp.maximum(m, row_max)
        if use_sink:
            sink = sink_ref[kvg * (kvs * group) + g] * LOG2_E
            m = jnp.maximum(m, sink)
        acc = None
        for s, v in pieces:
            term = _dot(jnp.exp2(s - m).astype(BF16), v)
            acc = term if acc is None else acc + term
        denom = acc[:, HEAD_DIM:HEAD_DIM + 1]
        if use_sink:
            denom = denom + jnp.exp2(sink - m)
        o_ref[:, g * HEAD_DIM:(g + 1) * HEAD_DIM] = (acc[:, :HEAD_DIM] / denom).astype(o_ref.dtype)

    ahead = scores(0)
    for g in range(kvs * group):
        cur = ahead
        if g + 1 < kvs * group:
            ahead = scores(g + 1)
        finish(g, cur)


def _attention(proj, *, q_off, k_off, v_off, n_heads, n_kv, row_block0, n_seq, seq_len, tq,
               kvs=1, ctx=None, ctx_layer=0, cos=None, sin=None, q_gain=None, k_gain=None, sink=None,
               window=0, emit_k=False):
    L = seq_len
    group = n_heads // n_kv
    n_ctx = 0 if ctx is None else ctx.shape[3]
    rope = cos is not None
    qk_norm = q_gain is not None
    use_sink = sink is not None
    nq = L // tq
    q_w, kv_w = kvs * group * HEAD_DIM, kvs * HEAD_DIM
    assert n_kv % kvs == 0 and q_off % q_w == 0 and k_off % kv_w == 0 and v_off % kv_w == 0
    qb, kb, vb = q_off // q_w, k_off // kv_w, v_off // kv_w
    in_specs = [
        pl.BlockSpec((tq, q_w), lambda b, h, i: ((row_block0 + b) * nq + i, qb + h)),
        pl.BlockSpec((L, kv_w), lambda b, h, i: (row_block0 + b, kb + h)),
        pl.BlockSpec((L, kv_w), lambda b, h, i: (row_block0 + b, vb + h)),
    ]
    args = [proj, proj, proj]
    scratch = [pltpu.VMEM((kvs, L, HEAD_DIM), BF16), pltpu.VMEM((kvs, L, 2 * HEAD_DIM), BF16)]
    if n_ctx:
        in_specs += [pl.BlockSpec((None, None, None, n_ctx, kv_w), lambda b, h, i: (b, ctx_layer, 0, 0, h)),
                     pl.BlockSpec((None, None, None, n_ctx, kv_w), lambda b, h, i: (b, ctx_layer, 1, 0, h))]
        args += [ctx, ctx]
        scratch += [pltpu.VMEM((kvs, n_ctx, HEAD_DIM), BF16), pltpu.VMEM((kvs, n_ctx, 2 * HEAD_DIM), BF16)]
    if rope:
        in_specs += [pl.BlockSpec((L, HEAD_DIM), lambda b, h, i: (0, 0))] * 2
        args += [cos, sin]
    if qk_norm:
        in_specs += [pl.BlockSpec((1, HEAD_DIM), lambda b, h, i: (0, 0))] * 2
        args += [q_gain.reshape(1, HEAD_DIM), k_gain.reshape(1, HEAD_DIM)]
    if use_sink:
        in_specs += [pl.BlockSpec(memory_space=pltpu.SMEM)]
        args += [sink]
    out_shape = [jax.ShapeDtypeStruct((n_seq * L, n_heads * HEAD_DIM), BF16)]
    out_specs = [pl.BlockSpec((tq, q_w), lambda b, h, i: (b * nq + i, h))]
    if emit_k:
        out_shape += [jax.ShapeDtypeStruct((n_seq * L * n_kv, HEAD_DIM), F32)]
        out_specs += [pl.BlockSpec((L * n_kv, HEAD_DIM), lambda b, h, i: (b, 0))]
    kern = functools.partial(_attn_kernel, tq=tq, seq_len=L, group=group, n_kv=n_kv, kvs=kvs, n_ctx=n_ctx,
                             use_sink=use_sink, window=window, qk_norm=qk_norm, rope=rope, emit_k=emit_k)
    res = pl.pallas_call(
        kern,
        out_shape=tuple(out_shape),
        grid=(n_seq, n_kv // kvs, nq),
        in_specs=in_specs,
        out_specs=tuple(out_specs),
        scratch_shapes=scratch,
        compiler_params=pltpu.CompilerParams(dimension_semantics=("arbitrary", "arbitrary", "arbitrary"),
                                             vmem_limit_bytes=VMEM_LIMIT_BYTES),
        name="attention",
    )(*args)
    return res if emit_k else res[0]


def _out_mlp_kernel(*refs, n_x, n_parts, final, n_prompt_tiles, tf):
    it = iter(refs)
    x_refs = [next(it) for _ in range(n_x)]
    a_refs = [[next(it), next(it)] for _ in range(n_parts)]
    wo_refs = [next(it) for _ in range(n_parts)]
    mod_ref, g2_ref, w1_ref, w2_ref = next(it), next(it), next(it), next(it)
    fg_ref = next(it) if final else None
    o_refs = [next(it) for _ in range(2 if final else 1)]
    i = pl.program_id(0)
    d_ff = w1_ref.shape[1]

    mix = None
    for pair, wo_ref in zip(a_refs, wo_refs):
        term = _dot(_token_tile(pair, n_prompt_tiles), wo_ref[...])
        mix = term if mix is None else mix + term
    x1 = _token_tile(x_refs, n_prompt_tiles) + mod_ref[2:3, :] * mix
    h2 = (_rms_rows(x1, g2_ref[...]) * (1.0 + mod_ref[4:5, :]) + mod_ref[3:4, :]).astype(BF16)
    acc = None
    for c in range(d_ff // tf):
        hid = jnp.maximum(_dot(h2, w1_ref[:, c * tf:(c + 1) * tf]), 0.0)
        term = _dot((hid * hid).astype(BF16), w2_ref[c * tf:(c + 1) * tf, :])
        acc = term if acc is None else acc + term
    y = x1 + mod_ref[5:6, :] * acc

    if final:
        y = _rms_rows(y, fg_ref[...])

        @pl.when(i < n_prompt_tiles)
        def _():
            o_refs[0][...] = y

        @pl.when(i >= n_prompt_tiles)
        def _():
            o_refs[1][...] = y
    else:
        o_refs[0][...] = y


def _out_mlp(xs, parts, w_out, wo_layer, mod, layer, g2, w1, w2, n_prompt, sample_len, final_g=None, tm=512,
             tf=1024):
    n_tok = sum(a.shape[0] for a in xs)
    d = xs[0].shape[1]
    d_ff = w1.shape[2]
    final = final_g is not None
    npt = n_prompt // tm
    row = functools.partial(_cond_row, n_prompt_tiles=npt, tiles_per_sample=sample_len // tm)
    in_specs = _token_specs(xs, tm, npt)
    for pair in parts:
        in_specs += _token_specs(pair, tm, npt)
    resident = lambda shape: pl.BlockSpec((None, *shape), lambda i: (layer, 0, 0), pipeline_mode=pl.Buffered(1))
    width = w_out.shape[1] // len(parts)
    in_specs += [pl.BlockSpec((None, width, d), lambda i, p=p: (wo_layer, p, 0), pipeline_mode=pl.Buffered(1))
                 for p in range(len(parts))]
    in_specs += [
        pl.BlockSpec((None, None, 6, d), lambda i: (layer, row(i), 0, 0)),
        pl.BlockSpec((None, 1, d), lambda i: (layer, 0, 0)),
        resident((d, d_ff)),
        resident((d_ff, d)),
    ]
    args = [*xs, *[a for pair in parts for a in pair], *([w_out] * len(parts)), mod, g2, w1, w2]
    if final:
        in_specs += [pl.BlockSpec((1, d), lambda i: (0, 0))]
        args += [final_g.reshape(1, d)]
        out_shape = (jax.ShapeDtypeStruct((n_prompt, d), F32), jax.ShapeDtypeStruct((n_tok - n_prompt, d), F32))
        out_specs = (pl.BlockSpec((tm, d), lambda i: (jnp.minimum(i, npt - 1), 0)),
                     pl.BlockSpec((tm, d), lambda i: (jnp.maximum(i - npt, 0), 0)))
    else:
        out_shape = jax.ShapeDtypeStruct((n_tok, d), F32)
        out_specs = pl.BlockSpec((tm, d), lambda i: (i, 0))
    kern = functools.partial(_out_mlp_kernel, n_x=len(xs), n_parts=len(parts), final=final, n_prompt_tiles=npt,
                             tf=tf)
    return pl.pallas_call(
        kern,
        out_shape=out_shape,
        grid=(n_tok // tm,),
        in_specs=in_specs,
        out_specs=out_specs,
        compiler_params=pltpu.CompilerParams(dimension_semantics=("arbitrary",),
                                             vmem_limit_bytes=VMEM_LIMIT_BYTES),
        name="out_mlp",
    )(*args)


def _rope_tables(seq_len):
    rows = seq_len // GRID_W
    inv = ROPE_THETA ** (-np.arange(ROT_FREQS, dtype=np.float64) / ROT_FREQS)
    ang_r = np.repeat(np.arange(rows, dtype=np.float64), GRID_W)[:, None] * inv
    ang_c = np.tile(np.arange(GRID_W, dtype=np.float64), rows)[:, None] * inv
    cos = np.concatenate([np.cos(ang_r), np.cos(ang_r), np.cos(ang_c), np.cos(ang_c)], axis=-1)
    sin = np.concatenate([-np.sin(ang_r), np.sin(ang_r), -np.sin(ang_c), np.sin(ang_c)], axis=-1)
    return jnp.asarray(cos, dtype=F32), jnp.asarray(sin, dtype=F32)


def kernel(x_prompt, x_sample, state_a, cache_b_kv, cache_c_kv, c, c_ctx, ada_w, ada_b, norm1_g, norm2_g,
           final_g, mlp_w1, mlp_w2, ev_w_in, a_conv, a_log, a_dt_bias, a_norm_g, b_sink, ev_w_out,
           od_w_in, c_qnorm_g, c_knorm_g, od_w_out):
    batch, seq, d = x_prompt.shape
    dec_batch, dec_seq, _ = x_sample.shape
    depth = ada_w.shape[0]
    n_prompt, n_sample = batch * seq, dec_batch * dec_seq
    past = cache_b_kv.shape[3]

    xs = [x_prompt.reshape(n_prompt, d), x_sample.reshape(n_sample, d)]
    cond_rows = -(-(1 + dec_batch) // SUBLANES) * SUBLANES
    cond = jnp.concatenate([c_ctx[None, :], c, jnp.zeros((cond_rows - 1 - dec_batch, d), F32)], axis=0)
    mod = _adaln(cond, ada_w, ada_b).reshape(depth, cond_rows, 6, d)

    cos, sin = _rope_tables(dec_seq)
    s_blk0 = n_prompt // dec_seq
    ctx_b = cache_b_kv.reshape(*cache_b_kv.shape[:4], B_KV_HEADS * HEAD_DIM)
    ctx_c = cache_c_kv.reshape(*cache_c_kv.shape[:4], C_KV_HEADS * HEAD_DIM)
    ev_w_in_bf16 = _ev_w_in(jnp.swapaxes(ev_w_in, 1, 2))
    od_w_in_bf16 = od_w_in.astype(BF16)
    ev_w_out_bf16 = ev_w_out.astype(BF16)
    od_w_out_bf16 = od_w_out.astype(BF16)
    w1 = mlp_w1.astype(BF16)
    w2 = mlp_w2.astype(BF16)
    g1 = norm1_g.reshape(depth, 1, d)
    g2 = norm2_g.reshape(depth, 1, d)
    lane_pad = lambda v: jnp.zeros((1, LANES), F32).at[0, 2 * A_HEADS:4 * A_HEADS].set(v.reshape(-1))

    new_a, new_b, new_c = [], [], []
    for l in range(depth):
        i = l // 2
        fin = final_g if l == depth - 1 else None
        if l % 2 == 0:
            kv_cols = B_KV_HEADS * HEAD_DIM
            proj, (cache_k, cache_v), (beta_all, gc_all) = _in_proj(
                xs, mod, l, g1, ev_w_in_bf16, i, n_prompt, dec_seq,
                cache_cols=(EV_KVB_OFF, EV_KVB_OFF + kv_cols), n_kv=B_KV_HEADS,
                gate_rows=(lane_pad(a_log[i]), lane_pad(a_dt_bias[i])), w_transposed=True)
            oa_p, st_p = _gdn(proj, beta_all, gc_all, a_conv[i], a_norm_g[i], 0, batch, seq, heads=A_HEADS,
                              steps=seq // CHUNK)
            oa_s, _ = _gdn(proj, beta_all, gc_all, a_conv[i], a_norm_g[i], s_blk0, dec_batch, dec_seq, heads=1,
                           steps=8, s0=state_a, s0_layer=i)
            ob_p = _attention(proj, q_off=EV_QB_OFF, k_off=EV_KVB_OFF, v_off=EV_KVB_OFF + kv_cols,
                              n_heads=B_HEADS, n_kv=B_KV_HEADS, row_block0=0, n_seq=batch, seq_len=seq,
                              tq=seq, kvs=B_KV_HEADS, sink=b_sink[i])
            ob_s = _attention(proj, q_off=EV_QB_OFF, k_off=EV_KVB_OFF, v_off=EV_KVB_OFF + kv_cols,
                              n_heads=B_HEADS, n_kv=B_KV_HEADS, row_block0=s_blk0, n_seq=dec_batch,
                              seq_len=dec_seq, tq=512,
                              ctx=ctx_b, ctx_layer=i, cos=cos, sin=sin, sink=b_sink[i], window=WINDOW)
            new_a.append(st_p)
            new_b.append(jnp.stack([cache_k.reshape(batch, seq, B_KV_HEADS, HEAD_DIM),
                                    cache_v.reshape(batch, seq, B_KV_HEADS, HEAD_DIM)], axis=1))
            parts = [[oa_p, oa_s], [ob_p, ob_s]]
            w_out = ev_w_out_bf16
        else:
            kv_cols = C_KV_HEADS * HEAD_DIM
            proj, (cache_v,), _ = _in_proj(xs, mod, l, g1, od_w_in_bf16, i, n_prompt, dec_seq,
                                           cache_cols=(OD_V_OFF,), n_kv=C_KV_HEADS)
            oc_p, kn_p = _attention(proj, q_off=0, k_off=OD_K_OFF, v_off=OD_V_OFF, n_heads=C_HEADS,
                                    n_kv=C_KV_HEADS, row_block0=0, n_seq=batch, seq_len=seq, tq=seq, kvs=C_KV_HEADS,
                                    q_gain=c_qnorm_g[i], k_gain=c_knorm_g[i], emit_k=True)
            oc_s = _attention(proj, q_off=0, k_off=OD_K_OFF, v_off=OD_V_OFF, n_heads=C_HEADS,
                              n_kv=C_KV_HEADS, row_block0=s_blk0, n_seq=dec_batch, seq_len=dec_seq, tq=256, kvs=2,
                              ctx=ctx_c, ctx_layer=i, cos=cos, sin=sin, q_gain=c_qnorm_g[i],
                              k_gain=c_knorm_g[i])
            new_c.append(jnp.stack([kn_p.reshape(batch, seq, C_KV_HEADS, HEAD_DIM),
                                    cache_v.reshape(batch, seq, C_KV_HEADS, HEAD_DIM)], axis=1))
            parts = [[oc_p, oc_s]]
            w_out = od_w_out_bf16
        out = _out_mlp(xs, parts, w_out, i, mod, l, g2, w1, w2, n_prompt, dec_seq, final_g=fin)
        xs = list(out) if fin is not None else [out]

    y_prompt = xs[0].reshape(batch, seq, d)
    y_sample = xs[1].reshape(dec_batch, dec_seq, d)
    return (y_prompt, y_sample, jnp.stack(new_a, axis=1), jnp.stack(new_b, axis=1), jnp.stack(new_c, axis=1))
```

```python
import functools

import numpy as np
import jax
import jax.numpy as jnp
from jax import lax
from jax.experimental import pallas as pl
from jax.experimental.pallas import tpu as pltpu

F32 = jnp.float32
BF16 = jnp.bfloat16

HEAD_DIM = 128
GRID_W = 64
A_HEADS = 4
CHUNK = 64
B_HEADS = 4
B_KV_HEADS = 2
WINDOW = 128
C_HEADS = 8
C_KV_HEADS = 2
ROT_FREQS = HEAD_DIM // 4
ROPE_THETA = 10000.0
EPS = 1e-6
NEG = -1e30
LOG2_E = 1.4426950408889634
A_WIDTH = A_HEADS * HEAD_DIM
B_WIDTH = B_HEADS * HEAD_DIM
C_WIDTH = C_HEADS * HEAD_DIM

SUBLANES = 8
LANES = 128
VMEM_LIMIT_BYTES = 56 * 1024 * 1024

EV_QB_OFF = 4 * A_WIDTH
EV_KVB_OFF = EV_QB_OFF + B_WIDTH
EV_BG_OFF = EV_KVB_OFF + 2 * B_KV_HEADS * HEAD_DIM
EV_COLS = EV_BG_OFF + LANES
OD_K_OFF = C_WIDTH
OD_V_OFF = C_WIDTH + C_KV_HEADS * HEAD_DIM


def _sigmoid(x):
    return 1.0 / (1.0 + jnp.exp(-x))


def _silu(x):
    return x * _sigmoid(x)


def _softplus(x):
    return jnp.maximum(x, 0.0) + jnp.log1p(jnp.exp(-jnp.abs(x)))


def _rms_rows(x, g):
    return x * lax.rsqrt(jnp.mean(x * x, axis=-1, keepdims=True) + EPS) * g


def _dot(a, b):
    return jnp.dot(a, b, preferred_element_type=F32)


def _dot_nt(a, b):
    return lax.dot_general(a, b, (((1,), (1,)), ((), ())), preferred_element_type=F32)


def _cond_row(i, n_prompt_tiles, tiles_per_sample):
    return jnp.where(i < n_prompt_tiles, 0, 1 + (i - n_prompt_tiles) // tiles_per_sample)


def _split_bf16(x):
    hi = x.astype(BF16)
    lo = (x - hi.astype(F32)).astype(BF16)
    return hi, lo


def _adaln_kernel(cond_ref, w_ref, b_ref, o_ref):
    rows = cond_ref.shape[0]
    s_hi, s_lo = _split_bf16(_silu(cond_ref[...]))
    w_hi, w_lo = _split_bf16(w_ref[0])
    r = _dot(jnp.concatenate([s_hi, s_lo], axis=0), w_hi)
    o_ref[0] = r[:rows] + r[rows:] + _dot(s_hi, w_lo) + b_ref[0]


def _adaln(cond, ada_w, ada_b, tn=2048):
    depth, d, n = ada_w.shape
    rows = cond.shape[0]
    return pl.pallas_call(
        _adaln_kernel,
        out_shape=jax.ShapeDtypeStruct((depth, rows, n), F32),
        grid=(depth, n // tn),
        in_specs=[
            pl.BlockSpec((rows, d), lambda l, j: (0, 0)),
            pl.BlockSpec((1, d, tn), lambda l, j: (l, 0, j)),
            pl.BlockSpec((1, 1, tn), lambda l, j: (l, 0, j)),
        ],
        out_specs=pl.BlockSpec((1, rows, tn), lambda l, j: (l, 0, j)),
        compiler_params=pltpu.CompilerParams(dimension_semantics=("arbitrary", "arbitrary"),
                                             vmem_limit_bytes=VMEM_LIMIT_BYTES),
        name="adaln",
    )(cond, ada_w, ada_b.reshape(depth, 1, n))


def _ev_w_in_kernel(w_ref, o_ref):
    bg_off = 4 * A_WIDTH
    bg_end = bg_off + 4 * A_HEADS
    d = w_ref.shape[1]
    o_ref[:bg_off, :] = w_ref[:bg_off, :].astype(BF16)
    o_ref[bg_off:EV_BG_OFF, :] = w_ref[bg_end:, :].astype(BF16)
    o_ref[EV_BG_OFF:EV_BG_OFF + 4 * A_HEADS, :] = w_ref[bg_off:bg_end, :].astype(BF16)
    o_ref[EV_BG_OFF + 4 * A_HEADS:, :] = jnp.zeros((LANES - 4 * A_HEADS, d), BF16)


def _ev_w_in(w_t):
    layers, n, d = w_t.shape
    return pl.pallas_call(
        _ev_w_in_kernel,
        out_shape=jax.ShapeDtypeStruct((layers, EV_COLS, d), BF16),
        grid=(layers,),
        in_specs=[pl.BlockSpec((None, n, d), lambda l: (l, 0, 0))],
        out_specs=pl.BlockSpec((None, EV_COLS, d), lambda l: (l, 0, 0)),
        compiler_params=pltpu.CompilerParams(dimension_semantics=("arbitrary",),
                                             vmem_limit_bytes=VMEM_LIMIT_BYTES),
        name="ev_w_in",
    )(w_t)


def _token_specs(arrays, tm, n_prompt_tiles, buffers=None):
    if len(arrays) == 1:
        mode = {} if buffers is None else {"pipeline_mode": pl.Buffered(buffers)}
        return [pl.BlockSpec((tm, arrays[0].shape[1]), lambda i, *_: (i, 0), **mode)]
    return [pl.BlockSpec((tm, arrays[0].shape[1]), lambda i, *_: (jnp.minimum(i, n_prompt_tiles - 1), 0)),
            pl.BlockSpec((tm, arrays[1].shape[1]), lambda i, *_: (jnp.maximum(i - n_prompt_tiles, 0), 0))]


def _token_tile(refs, n_prompt_tiles):
    if len(refs) == 1:
        return refs[0][...]
    return jnp.where(pl.program_id(0) < n_prompt_tiles, refs[0][...], refs[1][...])


def _gate_columns(bg, a_row, dt_row):
    rows = bg.shape[0]
    g = -jnp.exp(a_row) * _softplus(bg + dt_row)
    chunk_row = lax.broadcasted_iota(jnp.int32, (rows, LANES), 0) % CHUNK
    lane = lax.broadcasted_iota(jnp.int32, (rows, LANES), 1)
    pre, suf = g, g
    shift = 1
    while shift < CHUNK:
        pre = pre + jnp.where(chunk_row >= shift, pltpu.roll(pre, shift, 0), 0.0)
        suf = suf + jnp.where(chunk_row < CHUNK - shift, pltpu.roll(suf, rows - shift, 0), 0.0)
        shift *= 2
    return _sigmoid(bg), jnp.where(lane >= 3 * A_HEADS, suf, pre)


def _inproj_kernel(*refs, n_x, n_prompt_tiles, cache_cols, n_kv, gates, w_transposed):
    it = iter(refs)
    x_refs = [next(it) for _ in range(n_x)]
    mod_ref, g_ref, w_ref = next(it), next(it), next(it)
    a_ref, dt_ref = (next(it), next(it)) if gates else (None, None)
    o_ref = next(it)
    cache_refs = [next(it) for _ in cache_cols]
    x = _token_tile(x_refs, n_prompt_tiles)
    h = _rms_rows(x, g_ref[...]) * (1.0 + mod_ref[1:2, :]) + mod_ref[0:1, :]
    o = (_dot_nt if w_transposed else _dot)(h.astype(BF16), w_ref[...])
    o_ref[...] = o
    if gates:
        beta_ref, gc_ref = next(it), next(it)
        beta_ref[...], gc_ref[...] = _gate_columns(o[:, EV_BG_OFF:EV_BG_OFF + LANES], a_ref[...], dt_ref[...])

    @pl.when(pl.program_id(0) < n_prompt_tiles)
    def _():
        tm = o.shape[0]
        for c_ref, col in zip(cache_refs, cache_cols):
            for hd in range(n_kv):
                c_ref[pl.ds(hd, tm, stride=n_kv), :] = o[:, col + hd * HEAD_DIM:col + (hd + 1) * HEAD_DIM]


def _in_proj(xs, mod, layer, g, w, w_layer, n_prompt, sample_len, cache_cols, n_kv, gate_rows=None,
             w_transposed=False, tm=512):
    n_tok = sum(a.shape[0] for a in xs)
    d = xs[0].shape[1]
    n = w.shape[1] if w_transposed else w.shape[2]
    npt = n_prompt // tm
    assert tm % CHUNK == 0
    gates = gate_rows is not None
    row = functools.partial(_cond_row, n_prompt_tiles=npt, tiles_per_sample=sample_len // tm)
    cache_spec = pl.BlockSpec((tm * n_kv, HEAD_DIM), lambda i: (jnp.minimum(i, npt - 1), 0))
    lane_row = pl.BlockSpec((1, LANES), lambda i: (0, 0))
    gate_spec = pl.BlockSpec((tm, LANES), lambda i: (i, 0))
    res = pl.pallas_call(
        functools.partial(_inproj_kernel, n_x=len(xs), n_prompt_tiles=npt, cache_cols=cache_cols, n_kv=n_kv,
                          gates=gates, w_transposed=w_transposed),
        out_shape=(jax.ShapeDtypeStruct((n_tok, n), F32),
                   *[jax.ShapeDtypeStruct((n_prompt * n_kv, HEAD_DIM), F32) for _ in cache_cols],
                   *[jax.ShapeDtypeStruct((n_tok, LANES), F32) for _ in range(2 if gates else 0)]),
        grid=(n_tok // tm,),
        in_specs=_token_specs(xs, tm, npt) + [
            pl.BlockSpec((None, None, 6, d), lambda i: (layer, row(i), 0, 0)),
            pl.BlockSpec((None, 1, d), lambda i: (layer, 0, 0)),
            pl.BlockSpec((None, *w.shape[1:]), lambda i: (w_layer, 0, 0), pipeline_mode=pl.Buffered(1)),
        ] + ([lane_row, lane_row] if gates else []),
        out_specs=(pl.BlockSpec((tm, n), lambda i: (i, 0)), *[cache_spec for _ in cache_cols],
                   *([gate_spec, gate_spec] if gates else [])),
        compiler_params=pltpu.CompilerParams(dimension_semantics=("arbitrary",),
                                             vmem_limit_bytes=VMEM_LIMIT_BYTES),
        name="in_proj",
    )(*xs, mod, g, w, *(gate_rows if gates else ()))
    n_c = len(cache_cols)
    return res[0], res[1:1 + n_c], res[1 + n_c:]


def _gdn_kernel(q_ref, k_ref, v_ref, gate_ref, betaall_ref, gcall_ref, cwq_ref, cwk_ref, cwv_ref, ng_ref, *rest,
                seq_len, heads, steps, has_state):
    s0_ref = rest[0] if has_state else None
    (o_ref, sout_ref, qn_ref, kn_ref, vn_ref, beta_ref, gc_ref, gcrow_ref, s_ref, oacc_ref,
     mq_a, c_a, oo_a, eg_a, mq_b, c_b, oo_b, eg_b) = rest[1:] if has_state else rest
    L = seq_len
    nc = L // CHUNK
    nblk = nc // steps
    h0 = pl.program_id(1) * heads
    row_idx = lax.broadcasted_iota(jnp.int32, (L, LANES), 0)
    lane_idx = lax.broadcasted_iota(jnp.int32, (L, LANES), 1)

    def conv_silu(x, cw):
        prev = jnp.where(row_idx == 0, 0.0, pltpu.roll(x, 1, 0))
        nxt = jnp.where(row_idx == L - 1, 0.0, pltpu.roll(x, L - 1, 0))
        return _silu(prev * cw[0:1, :] + x * cw[1:2, :] + nxt * cw[2:3, :])

    def l2norm(x):
        return x * lax.rsqrt(jnp.sum(x * x, axis=-1, keepdims=True) + EPS)

    beta_all = betaall_ref[...]
    gc_all = gcall_ref[...]

    def pick(x, col):
        col_vals = jnp.sum(jnp.where(lane_idx == col, x, 0.0), axis=1, keepdims=True)
        return jnp.broadcast_to(col_vals, (L, LANES))

    for r in range(L // LANES):
        rows = jnp.transpose(gc_all[r * LANES:(r + 1) * LANES, :])[2 * A_HEADS:4 * A_HEADS, :]
        gcrow_ref[0, r] = rows
        gcrow_ref[1, r] = pltpu.roll(rows, CHUNK, 1)

    for hh in range(heads):
        cols = slice(hh * HEAD_DIM, (hh + 1) * HEAD_DIM)
        qn_ref[:, cols] = l2norm(conv_silu(q_ref[:, cols], cwq_ref[:, cols])) * (HEAD_DIM ** -0.5)
        kn_ref[:, cols] = l2norm(conv_silu(k_ref[:, cols], cwk_ref[:, cols]))
        vn_ref[:, cols] = conv_silu(v_ref[:, cols], cwv_ref[:, cols])
        for d in range(2):
            s_ref[hh, d] = s0_ref[d, hh] if has_state else jnp.zeros((HEAD_DIM, HEAD_DIM), F32)
        head = h0 + hh
        for d in range(2):
            beta_ref[hh, d] = pick(beta_all, d * A_HEADS + head)
            gc_ref[hh, d] = pick(gc_all, (2 + d) * A_HEADS + head)

    r64 = lax.broadcasted_iota(jnp.int32, (CHUNK, LANES), 0)
    l64 = lax.broadcasted_iota(jnp.int32, (CHUNK, LANES), 1)
    is_f = l64 < CHUNK
    is_f_row = lax.broadcasted_iota(jnp.int32, (1, LANES), 1) < CHUNK
    cpos = jnp.where(is_f, l64, l64 - CHUNK)
    ahead = jnp.where(is_f, r64 - cpos, cpos - r64)
    incl = ahead >= 0
    strict = ahead > 0
    eye = jnp.where(r64 == cpos, 1.0, 0.0)
    is_f2 = lax.broadcasted_iota(jnp.int32, (2 * CHUNK, LANES), 1) < CHUNK
    zeros_c = jnp.zeros((CHUNK, HEAD_DIM), F32)

    def aligned(off, m):
        return off if isinstance(off, int) else pl.multiple_of(off, m)

    def half(n):
        return n // 2 if isinstance(n, int) else lax.shift_right_logical(n, 1)

    def block_diag(xp):
        z = jnp.zeros_like(xp)
        return jnp.concatenate([jnp.where(is_f, xp, z), jnp.where(is_f, z, xp)], axis=0)

    def split_dot(a_hi, a_lo, b_hi, b_lo):
        lhs = jnp.concatenate([a_hi, a_lo, a_hi], axis=1)
        rhs = jnp.concatenate([b_hi, b_hi, b_lo], axis=0)
        return _dot(lhs, rhs)

    def prep_stages(blk, slot):
        mq_ref, c_ref, oo_ref, eg_ref = slot
        probs = [(si, hh) for si in range(steps) for hh in range(heads)]
        st = {p: {} for p in probs}

        def load(p):
            si, hh = p
            j = blk * steps + si
            n_b = nc - 1 - j
            cols = slice(hh * HEAD_DIM, (hh + 1) * HEAD_DIM)
            off_f = aligned(j * CHUNK, CHUNK)
            off_b = aligned(n_b * CHUNK, CHUNK)
            q_f, k_f, v_f = (r[pl.ds(off_f, CHUNK), cols] for r in (qn_ref, kn_ref, vn_ref))
            q_b, k_b, v_b = (r[pl.ds(off_b, CHUNK), cols] for r in (qn_ref, kn_ref, vn_ref))
            beta_f = beta_ref[hh, 0, pl.ds(off_f, CHUNK), :]
            beta_b = beta_ref[hh, 1, pl.ds(off_b, CHUNK), :]
            gc_f = gc_ref[hh, 0, pl.ds(off_f, CHUNK), :]
            gc_b = gc_ref[hh, 1, pl.ds(off_b, CHUNK), :]
            row_f = gcrow_ref[si % 2, half(j), pl.ds(h0 + hh, 1), :]
            row_b = gcrow_ref[si % 2, half(n_b), pl.ds(A_HEADS + h0 + hh, 1), :]
            gcrow = jnp.where(is_f_row, row_f, row_b)
            diff = jnp.where(is_f, gc_f, gc_b) - gcrow
            decay = jnp.where(incl, jnp.exp(jnp.where(incl, diff, 0.0)), 0.0)
            kb_f = k_f * beta_f
            kb_b = k_b * beta_b
            e_f = jnp.exp(gc_f)
            e_b = jnp.exp(gc_b)
            glast_f = gc_f[CHUNK - 1:CHUNK, :]
            glast_b = gc_b[0:1, :]
            kdt = jnp.transpose(jnp.concatenate([k_f * jnp.exp(glast_f - gc_f), k_b * jnp.exp(glast_b - gc_b)],
                                                axis=0))
            zk = jnp.zeros_like(kdt)
            eg_ref[si, hh, 0] = jnp.broadcast_to(jnp.exp(glast_f), (SUBLANES, LANES))
            eg_ref[si, hh, 1] = jnp.broadcast_to(jnp.exp(glast_b), (SUBLANES, LANES))
            st[p].update(
                decay=decay,
                lhs=jnp.concatenate([jnp.concatenate([kb_f, kb_b], axis=1),
                                     jnp.concatenate([q_f, q_b], axis=1)], axis=0).astype(BF16),
                rhs=jnp.concatenate([jnp.concatenate([k_f, zeros_c], axis=1),
                                     jnp.concatenate([zeros_c, k_b], axis=1)], axis=0).astype(BF16),
                rhs_uw=jnp.concatenate([jnp.concatenate([v_f * beta_f, kb_f * e_f], axis=1),
                                        jnp.concatenate([v_b * beta_b, kb_b * e_b], axis=1)], axis=0).astype(BF16),
                kd2=jnp.concatenate([jnp.where(is_f2, kdt, zk), jnp.where(is_f2, zk, kdt)], axis=0).astype(BF16),
                qe=(q_f * e_f, q_b * e_b))

        def gram(p):
            s = st[p]
            aq = _dot_nt(s.pop("lhs"), s.pop("rhs"))
            decay = s.pop("decay")
            x = -jnp.where(strict, aq[:CHUNK] * decay, 0.0)
            s["qk"] = jnp.where(incl, aq[CHUNK:] * decay, 0.0)
            s["t"] = eye + x
            s["p"] = x

        def square(p):
            s = st[p]
            p_hi, p_lo = _split_bf16(s["p"])
            s["p"] = split_dot(p_hi, p_lo, block_diag(p_hi), block_diag(p_lo))

        def double(p):
            s = st[p]
            p_hi, p_lo = _split_bf16(s["p"])
            t_hi, t_lo = _split_bf16(s["t"])
            r = split_dot(p_hi, p_lo, jnp.concatenate([block_diag(t_hi), block_diag(p_hi)], axis=1),
                          jnp.concatenate([block_diag(t_lo), block_diag(p_lo)], axis=1))
            s["t"] = s["t"] + r[:, :LANES]
            s["p"] = r[:, LANES:]

        def last(p):
            s = st[p]
            p_hi, p_lo = _split_bf16(s.pop("p"))
            t_hi, t_lo = _split_bf16(s["t"])
            s["t"] = s["t"] + split_dot(p_hi, p_lo, block_diag(t_hi), block_diag(t_lo))

        def solve(p):
            s = st[p]
            s["uw"] = _dot(block_diag(s.pop("t")).astype(BF16), s.pop("rhs_uw"))

        def fold(p):
            si, hh = p
            s = st[p]
            lhs3 = jnp.concatenate([s.pop("kd2"), block_diag(s.pop("qk")).astype(BF16)], axis=0)
            big = _dot(lhs3, s.pop("uw").astype(BF16))
            qe = s.pop("qe")
            for d in range(2):
                rows_s = slice(d * 2 * CHUNK, (d + 1) * 2 * CHUNK)
                rows_o = slice(4 * CHUNK + d * CHUNK, 4 * CHUNK + (d + 1) * CHUNK)
                c_ref[si, hh, d] = big[rows_s, :HEAD_DIM]
                oo_ref[si, hh, d] = big[rows_o, :HEAD_DIM]
                mq_ref[si, hh, d] = jnp.concatenate([big[rows_s, HEAD_DIM:], qe[d] - big[rows_o, HEAD_DIM:]],
                                                    axis=0).astype(BF16)

        n_double = 0
        span = 2
        while 2 * span < CHUNK:
            n_double += 1
            span *= 2
        stage_fns = [load, gram, square] + [double] * n_double + [last, solve, fold]
        return [functools.partial(lambda fn: [fn(p) for p in probs], fn) for fn in stage_fns]

    def scan_step(blk, slot, si):
        mq_ref, c_ref, oo_ref, eg_ref = slot
        j = blk * steps + si
        for hh in range(heads):
            cols = slice(hh * HEAD_DIM, (hh + 1) * HEAD_DIM)
            for d, n in ((0, j), (1, nc - 1 - j)):
                s = s_ref[hh, d]
                r = _dot(mq_ref[si, hh, d], s.astype(BF16))
                s_ref[hh, d] = s * eg_ref[si, hh, d][0:1, :] - r[:2 * CHUNK] + c_ref[si, hh, d]
                oacc_ref[d, pl.ds(aligned(n * CHUNK, CHUNK), CHUNK), cols] = r[2 * CHUNK:] + oo_ref[si, hh, d]

    def run(prep, scan):
        stages = prep_stages(*prep) if prep else []
        n_scan = steps if scan else 0
        for k in range(max(len(stages), n_scan)):
            if k < len(stages):
                stages[k]()
            if k < n_scan:
                scan_step(scan[0], scan[1], k)

    slot_a = (mq_a, c_a, oo_a, eg_a)
    slot_b = (mq_b, c_b, oo_b, eg_b)
    run((0, slot_a), None)
    if nblk == 1:
        run(None, (0, slot_a))
    else:
        def body(i, carry):
            blk = 2 * i
            run((blk + 1, slot_b), (blk, slot_a))
            run((blk + 2, slot_a), (blk + 1, slot_b))
            return carry

        lax.fori_loop(0, nblk // 2 - 1, body, 0)
        run((nblk - 1, slot_b), (nblk - 2, slot_a))
        run(None, (nblk - 1, slot_b))

    for hh in range(heads):
        cols = slice(hh * HEAD_DIM, (hh + 1) * HEAD_DIM)
        sout_ref[0, hh] = s_ref[hh, 0]
        sout_ref[1, hh] = s_ref[hh, 1]
        o = oacc_ref[0, :, cols] + oacc_ref[1, :, cols]
        o = _rms_rows(o, ng_ref[...]) * _silu(gate_ref[:, cols])
        o_ref[:, cols] = o.astype(o_ref.dtype)


def _gdn(proj, beta_all, gc_all, conv_w, norm_g, row_block0, n_seq, seq_len, heads, steps, s0=None, s0_layer=0):
    L = seq_len
    nblk = L // CHUNK // steps
    assert L % (CHUNK * steps) == 0 and steps % 2 == 0 and 2 * CHUNK == LANES and A_HEADS % heads == 0
    assert nblk == 1 or nblk % 2 == 0
    width = heads * HEAD_DIM
    groups = A_HEADS // heads
    blk = lambda base: pl.BlockSpec((L, width), lambda b, h: (row_block0 + b, base * groups + h))
    wblk = lambda base: pl.BlockSpec((3, width), lambda b, h: (0, base * groups + h))
    state_spec = pl.BlockSpec((None, 2, heads, HEAD_DIM, HEAD_DIM), lambda b, h: (b, 0, h, 0, 0))
    slot = [pltpu.VMEM((steps, heads, 2, 3 * CHUNK, HEAD_DIM), BF16),
            pltpu.VMEM((steps, heads, 2, 2 * CHUNK, HEAD_DIM), F32),
            pltpu.VMEM((steps, heads, 2, CHUNK, HEAD_DIM), F32),
            pltpu.VMEM((steps, heads, 2, SUBLANES, LANES), F32)]
    kern = functools.partial(_gdn_kernel, seq_len=L, heads=heads, steps=steps, has_state=s0 is not None)
    state_in, state_arg = [], []
    if s0 is not None:
        state_in = [pl.BlockSpec((None, None, 2, heads, HEAD_DIM, HEAD_DIM), lambda b, h: (b, s0_layer, 0, h, 0, 0))]
        state_arg = [s0]
    out, s_out = pl.pallas_call(
        kern,
        out_shape=(jax.ShapeDtypeStruct((n_seq * L, A_WIDTH), BF16),
                   jax.ShapeDtypeStruct((n_seq, 2, A_HEADS, HEAD_DIM, HEAD_DIM), F32)),
        grid=(n_seq, groups),
        in_specs=[
            blk(0), blk(1), blk(2), blk(3),
            pl.BlockSpec((L, LANES), lambda b, h: (row_block0 + b, 0)),
            pl.BlockSpec((L, LANES), lambda b, h: (row_block0 + b, 0)),
            wblk(0), wblk(1), wblk(2),
            pl.BlockSpec((1, HEAD_DIM), lambda b, h: (0, 0)),
        ] + state_in,
        out_specs=(pl.BlockSpec((L, width), lambda b, h: (b, h)), state_spec),
        scratch_shapes=[
            pltpu.VMEM((L, width), F32), pltpu.VMEM((L, width), F32), pltpu.VMEM((L, width), F32),
            pltpu.VMEM((heads, 2, L, LANES), F32), pltpu.VMEM((heads, 2, L, LANES), F32),
            pltpu.VMEM((2, L // LANES, 2 * A_HEADS, LANES), F32),
            pltpu.VMEM((heads, 2, HEAD_DIM, HEAD_DIM), F32),
            pltpu.VMEM((2, L, width), F32),
        ] + slot + slot,
        compiler_params=pltpu.CompilerParams(dimension_semantics=("arbitrary", "arbitrary"),
                                             vmem_limit_bytes=VMEM_LIMIT_BYTES),
        name="gdn",
    )(proj, proj, proj, proj, beta_all, gc_all, conv_w, conv_w, conv_w, norm_g.reshape(1, HEAD_DIM), *state_arg)
    return out, s_out


def _rope_rows(x, cos, sin_signed):
    lane = lax.broadcasted_iota(jnp.int32, x.shape, 1)
    swapped = jnp.where((lane % (2 * ROT_FREQS)) < ROT_FREQS,
                        pltpu.roll(x, HEAD_DIM - ROT_FREQS, 1), pltpu.roll(x, ROT_FREQS, 1))
    return x * cos + swapped * sin_signed


def _attn_kernel(*refs, tq, seq_len, group, n_kv, kvs, n_ctx, use_sink, window, qk_norm, rope, emit_k):
    it = iter(refs)
    q_ref, k_ref, v_ref = next(it), next(it), next(it)
    ck_ref = cv_ref = cos_ref = sin_ref = qg_ref = kg_ref = sink_ref = kn_out_ref = None
    if n_ctx:
        ck_ref, cv_ref = next(it), next(it)
    if rope:
        cos_ref, sin_ref = next(it), next(it)
    if qk_norm:
        qg_ref, kg_ref = next(it), next(it)
    if use_sink:
        sink_ref = next(it)
    o_ref = next(it)
    if emit_k:
        kn_out_ref = next(it)
    kbf_ref, vbf_ref = next(it), next(it)
    ckbf_ref = cvbf_ref = None
    if n_ctx:
        ckbf_ref, cvbf_ref = next(it), next(it)

    L = seq_len
    kvg = pl.program_id(1)
    qi = pl.program_id(2)

    def ones_column(rows):
        lane = lax.broadcasted_iota(jnp.int32, (rows, HEAD_DIM), 1)
        return jnp.where(lane == 0, 1.0, 0.0).astype(BF16)

    @pl.when(qi == 0)
    def _():
        for kk in range(kvs):
            cols = slice(kk * HEAD_DIM, (kk + 1) * HEAD_DIM)
            k = k_ref[:, cols]
            if qk_norm:
                k = _rms_rows(k, kg_ref[...])
            if emit_k:
                for hd in range(n_kv):
                    @pl.when(kvg * kvs + kk == hd)
                    def _():
                        kn_out_ref[pl.ds(hd, L, stride=n_kv), :] = k
            if rope:
                k = _rope_rows(k, cos_ref[...], sin_ref[...])
            kbf_ref[kk] = k.astype(BF16)
            vbf_ref[kk, :, :HEAD_DIM] = v_ref[:, cols].astype(BF16)
            vbf_ref[kk, :, HEAD_DIM:] = ones_column(L)
            if n_ctx:
                ckbf_ref[kk] = ck_ref[:, cols].astype(BF16)
                cvbf_ref[kk, :, :HEAD_DIM] = cv_ref[:, cols].astype(BF16)
                cvbf_ref[kk, :, HEAD_DIM:] = ones_column(n_ctx)

    q0 = pl.multiple_of(qi * tq, tq)
    if window:
        wk = tq + 2 * window
        ws = pl.multiple_of(jnp.clip(q0 - window, 0, L - wk), LANES)
        qpos = q0 + lax.broadcasted_iota(jnp.int32, (tq, wk), 0)
        kpos = ws + lax.broadcasted_iota(jnp.int32, (tq, wk), 1)
        valid = jnp.abs(qpos - kpos) <= window

    def scores(g):
        kk = g // group
        q = q_ref[:, g * HEAD_DIM:(g + 1) * HEAD_DIM]
        if qk_norm:
            q = _rms_rows(q, qg_ref[...])
        if rope:
            q = _rope_rows(q, cos_ref[pl.ds(q0, tq), :], sin_ref[pl.ds(q0, tq), :])
        q = (q * (HEAD_DIM ** -0.5 * LOG2_E)).astype(BF16)
        if window:
            s_loc = jnp.where(valid, _dot_nt(q, kbf_ref[kk, pl.ds(ws, wk), :]), NEG)
            pieces = [(s_loc, vbf_ref[kk, pl.ds(ws, wk), :])]
        else:
            pieces = [(_dot_nt(q, kbf_ref[kk]), vbf_ref[kk])]
        if n_ctx:
            pieces.append((_dot_nt(q, ckbf_ref[kk]), cvbf_ref[kk]))
        return pieces

    def finish(g, pieces):
        m = None
        for s, _ in pieces:
            row_max = jnp.max(s, axis=-1, keepdims=True)
            m = row_max if m is None else jnp.maximum(m, row_max)
        if use_sink:
            sink = sink_ref[kvg * (kvs * group) + g] * LOG2_E
            m = jnp.maximum(m, sink)
        acc = None
        for s, v in pieces:
            term = _dot(jnp.exp2(s - m).astype(BF16), v)
            acc = term if acc is None else acc + term
        denom = acc[:, HEAD_DIM:HEAD_DIM + 1]
        if use_sink:
            denom = denom + jnp.exp2(sink - m)
        o_ref[:, g * HEAD_DIM:(g + 1) * HEAD_DIM] = (acc[:, :HEAD_DIM] / denom).astype(o_ref.dtype)

    ahead = scores(0)
    for g in range(kvs * group):
        cur = ahead
        if g + 1 < kvs * group:
            ahead = scores(g + 1)
        finish(g, cur)


def _attention(proj, *, q_off, k_off, v_off, n_heads, n_kv, row_block0, n_seq, seq_len, tq,
               kvs=1, ctx=None, ctx_layer=0, cos=None, sin=None, q_gain=None, k_gain=None, sink=None,
               window=0, emit_k=False):
    L = seq_len
    group = n_heads // n_kv
    n_ctx = 0 if ctx is None else ctx.shape[3]
    rope = cos is not None
    qk_norm = q_gain is not None
    use_sink = sink is not None
    nq = L // tq
    q_w, kv_w = kvs * group * HEAD_DIM, kvs * HEAD_DIM
    assert n_kv % kvs == 0 and q_off % q_w == 0 and k_off % kv_w == 0 and v_off % kv_w == 0
    qb, kb, vb = q_off // q_w, k_off // kv_w, v_off // kv_w
    in_specs = [
        pl.BlockSpec((tq, q_w), lambda b, h, i: ((row_block0 + b) * nq + i, qb + h)),
        pl.BlockSpec((L, kv_w), lambda b, h, i: (row_block0 + b, kb + h)),
        pl.BlockSpec((L, kv_w), lambda b, h, i: (row_block0 + b, vb + h)),
    ]
    args = [proj, proj, proj]
    scratch = [pltpu.VMEM((kvs, L, HEAD_DIM), BF16), pltpu.VMEM((kvs, L, 2 * HEAD_DIM), BF16)]
    if n_ctx:
        in_specs += [pl.BlockSpec((None, None, None, n_ctx, kv_w), lambda b, h, i: (b, ctx_layer, 0, 0, h)),
                     pl.BlockSpec((None, None, None, n_ctx, kv_w), lambda b, h, i: (b, ctx_layer, 1, 0, h))]
        args += [ctx, ctx]
        scratch += [pltpu.VMEM((kvs, n_ctx, HEAD_DIM), BF16), pltpu.VMEM((kvs, n_ctx, 2 * HEAD_DIM), BF16)]
    if rope:
        in_specs += [pl.BlockSpec((L, HEAD_DIM), lambda b, h, i: (0, 0))] * 2
        args += [cos, sin]
    if qk_norm:
        in_specs += [pl.BlockSpec((1, HEAD_DIM), lambda b, h, i: (0, 0))] * 2
        args += [q_gain.reshape(1, HEAD_DIM), k_gain.reshape(1, HEAD_DIM)]
    if use_sink:
        in_specs += [pl.BlockSpec(memory_space=pltpu.SMEM)]
        args += [sink]
    out_shape = [jax.ShapeDtypeStruct((n_seq * L, n_heads * HEAD_DIM), BF16)]
    out_specs = [pl.BlockSpec((tq, q_w), lambda b, h, i: (b * nq + i, h))]
    if emit_k:
        out_shape += [jax.ShapeDtypeStruct((n_seq * L * n_kv, HEAD_DIM), F32)]
        out_specs += [pl.BlockSpec((L * n_kv, HEAD_DIM), lambda b, h, i: (b, 0))]
    kern = functools.partial(_attn_kernel, tq=tq, seq_len=L, group=group, n_kv=n_kv, kvs=kvs, n_ctx=n_ctx,
                             use_sink=use_sink, window=window, qk_norm=qk_norm, rope=rope, emit_k=emit_k)
    res = pl.pallas_call(
        kern,
        out_shape=tuple(out_shape),
        grid=(n_seq, n_kv // kvs, nq),
        in_specs=in_specs,
        out_specs=tuple(out_specs),
        scratch_shapes=scratch,
        compiler_params=pltpu.CompilerParams(dimension_semantics=("arbitrary", "arbitrary", "arbitrary"),
                                             vmem_limit_bytes=VMEM_LIMIT_BYTES),
        name="attention",
    )(*args)
    return res if emit_k else res[0]


def _out_mlp_kernel(*refs, n_x, n_parts, final, n_prompt_tiles, tf):
    it = iter(refs)
    x_refs = [next(it) for _ in range(n_x)]
    a_refs = [[next(it), next(it)] for _ in range(n_parts)]
    wo_refs = [next(it) for _ in range(n_parts)]
    mod_ref, g2_ref, w1_ref, w2_ref = next(it), next(it), next(it), next(it)
    fg_ref = next(it) if final else None
    o_refs = [next(it) for _ in range(2 if final else 1)]
    i = pl.program_id(0)
    d_ff = w1_ref.shape[1]

    mix = None
    for pair, wo_ref in zip(a_refs, wo_refs):
        term = _dot(_token_tile(pair, n_prompt_tiles), wo_ref[...])
        mix = term if mix is None else mix + term
    x1 = _token_tile(x_refs, n_prompt_tiles) + mod_ref[2:3, :] * mix
    h2 = (_rms_rows(x1, g2_ref[...]) * (1.0 + mod_ref[4:5, :]) + mod_ref[3:4, :]).astype(BF16)
    acc = None
    for c in range(d_ff // tf):
        hid = jnp.maximum(_dot(h2, w1_ref[:, c * tf:(c + 1) * tf]), 0.0)
        term = _dot((hid * hid).astype(BF16), w2_ref[c * tf:(c + 1) * tf, :])
        acc = term if acc is None else acc + term
    y = x1 + mod_ref[5:6, :] * acc

    if final:
        y = _rms_rows(y, fg_ref[...])

        @pl.when(i < n_prompt_tiles)
        def _():
            o_refs[0][...] = y

        @pl.when(i >= n_prompt_tiles)
        def _():
            o_refs[1][...] = y
    else:
        o_refs[0][...] = y


def _out_mlp(xs, parts, w_out, wo_layer, mod, layer, g2, w1, w2, n_prompt, sample_len, final_g=None, tm=512,
             tf=1024):
    n_tok = sum(a.shape[0] for a in xs)
    d = xs[0].shape[1]
    d_ff = w1.shape[2]
    final = final_g is not None
    npt = n_prompt // tm
    row = functools.partial(_cond_row, n_prompt_tiles=npt, tiles_per_sample=sample_len // tm)
    in_specs = _token_specs(xs, tm, npt)
    for pair in parts:
        in_specs += _token_specs(pair, tm, npt)
    resident = lambda shape: pl.BlockSpec((None, *shape), lambda i: (layer, 0, 0), pipeline_mode=pl.Buffered(1))
    width = w_out.shape[1] // len(parts)
    in_specs += [pl.BlockSpec((None, width, d), lambda i, p=p: (wo_layer, p, 0), pipeline_mode=pl.Buffered(1))
                 for p in range(len(parts))]
    in_specs += [
        pl.BlockSpec((None, None, 6, d), lambda i: (layer, row(i), 0, 0)),
        pl.BlockSpec((None, 1, d), lambda i: (layer, 0, 0)),
        resident((d, d_ff)),
        resident((d_ff, d)),
    ]
    args = [*xs, *[a for pair in parts for a in pair], *([w_out] * len(parts)), mod, g2, w1, w2]
    if final:
        in_specs += [pl.BlockSpec((1, d), lambda i: (0, 0))]
        args += [final_g.reshape(1, d)]
        out_shape = (jax.ShapeDtypeStruct((n_prompt, d), F32), jax.ShapeDtypeStruct((n_tok - n_prompt, d), F32))
        out_specs = (pl.BlockSpec((tm, d), lambda i: (jnp.minimum(i, npt - 1), 0)),
                     pl.BlockSpec((tm, d), lambda i: (jnp.maximum(i - npt, 0), 0)))
    else:
        out_shape = jax.ShapeDtypeStruct((n_tok, d), F32)
        out_specs = pl.BlockSpec((tm, d), lambda i: (i, 0))
    kern = functools.partial(_out_mlp_kernel, n_x=len(xs), n_parts=len(parts), final=final, n_prompt_tiles=npt,
                             tf=tf)
    return pl.pallas_call(
        kern,
        out_shape=out_shape,
        grid=(n_tok // tm,),
        in_specs=in_specs,
        out_specs=out_specs,
        compiler_params=pltpu.CompilerParams(dimension_semantics=("arbitrary",),
                                             vmem_limit_bytes=VMEM_LIMIT_BYTES),
        name="out_mlp",
    )(*args)


def _rope_tables(seq_len):
    rows = seq_len // GRID_W
    inv = ROPE_THETA ** (-np.arange(ROT_FREQS, dtype=np.float64) / ROT_FREQS)
    ang_r = np.repeat(np.arange(rows, dtype=np.float64), GRID_W)[:, None] * inv
    ang_c = np.tile(np.arange(GRID_W, dtype=np.float64), rows)[:, None] * inv
    cos = np.concatenate([np.cos(ang_r), np.cos(ang_r), np.cos(ang_c), np.cos(ang_c)], axis=-1)
    sin = np.concatenate([-np.sin(ang_r), np.sin(ang_r), -np.sin(ang_c), np.sin(ang_c)], axis=-1)
    return jnp.asarray(cos, dtype=F32), jnp.asarray(sin, dtype=F32)


def kernel(x_prompt, x_sample, state_a, cache_b_kv, cache_c_kv, c, c_ctx, ada_w, ada_b, norm1_g, norm2_g,
           final_g, mlp_w1, mlp_w2, ev_w_in, a_conv, a_log, a_dt_bias, a_norm_g, b_sink, ev_w_out,
           od_w_in, c_qnorm_g, c_knorm_g, od_w_out):
    batch, seq, d = x_prompt.shape
    dec_batch, dec_seq, _ = x_sample.shape
    depth = ada_w.shape[0]
    n_prompt, n_sample = batch * seq, dec_batch * dec_seq
    past = cache_b_kv.shape[3]

    xs = [x_prompt.reshape(n_prompt, d), x_sample.reshape(n_sample, d)]
    cond_rows = -(-(1 + dec_batch) // SUBLANES) * SUBLANES
    cond = jnp.concatenate([c_ctx[None, :], c, jnp.zeros((cond_rows - 1 - dec_batch, d), F32)], axis=0)
    mod = _adaln(cond, ada_w, ada_b).reshape(depth, cond_rows, 6, d)

    cos, sin = _rope_tables(dec_seq)
    s_blk0 = n_prompt // dec_seq
    ctx_b = cache_b_kv.reshape(*cache_b_kv.shape[:4], B_KV_HEADS * HEAD_DIM)
    ctx_c = cache_c_kv.reshape(*cache_c_kv.shape[:4], C_KV_HEADS * HEAD_DIM)
    ev_w_in_bf16 = _ev_w_in(jnp.swapaxes(ev_w_in, 1, 2))
    od_w_in_bf16 = od_w_in.astype(BF16)
    ev_w_out_bf16 = ev_w_out.astype(BF16)
    od_w_out_bf16 = od_w_out.astype(BF16)
    w1 = mlp_w1.astype(BF16)
    w2 = mlp_w2.astype(BF16)
    g1 = norm1_g.reshape(depth, 1, d)
    g2 = norm2_g.reshape(depth, 1, d)
    lane_pad = lambda v: jnp.zeros((1, LANES), F32).at[0, 2 * A_HEADS:4 * A_HEADS].set(v.reshape(-1))

    new_a, new_b, new_c = [], [], []
    for l in range(depth):
        i = l // 2
        fin = final_g if l == depth - 1 else None
        if l % 2 == 0:
            kv_cols = B_KV_HEADS * HEAD_DIM
            proj, (cache_k, cache_v), (beta_all, gc_all) = _in_proj(
                xs, mod, l, g1, ev_w_in_bf16, i, n_prompt, dec_seq,
                cache_cols=(EV_KVB_OFF, EV_KVB_OFF + kv_cols), n_kv=B_KV_HEADS,
                gate_rows=(lane_pad(a_log[i]), lane_pad(a_dt_bias[i])), w_transposed=True)
            oa_p, st_p = _gdn(proj, beta_all, gc_all, a_conv[i], a_norm_g[i], 0, batch, seq, heads=A_HEADS,
                              steps=seq // CHUNK)
            oa_s, _ = _gdn(proj, beta_all, gc_all, a_conv[i], a_norm_g[i], s_blk0, dec_batch, dec_seq, heads=1,
                           steps=8, s0=state_a, s0_layer=i)
            ob_p = _attention(proj, q_off=EV_QB_OFF, k_off=EV_KVB_OFF, v_off=EV_KVB_OFF + kv_cols,
                              n_heads=B_HEADS, n_kv=B_KV_HEADS, row_block0=0, n_seq=batch, seq_len=seq,
                              tq=seq, kvs=B_KV_HEADS, sink=b_sink[i])
            ob_s = _attention(proj, q_off=EV_QB_OFF, k_off=EV_KVB_OFF, v_off=EV_KVB_OFF + kv_cols,
                              n_heads=B_HEADS, n_kv=B_KV_HEADS, row_block0=s_blk0, n_seq=dec_batch,
                              seq_len=dec_seq, tq=512,
                              ctx=ctx_b, ctx_layer=i, cos=cos, sin=sin, sink=b_sink[i], window=WINDOW)
            new_a.append(st_p)
            new_b.append(jnp.stack([cache_k.reshape(batch, seq, B_KV_HEADS, HEAD_DIM),
                                    cache_v.reshape(batch, seq, B_KV_HEADS, HEAD_DIM)], axis=1))
            parts = [[oa_p, oa_s], [ob_p, ob_s]]
            w_out = ev_w_out_bf16
        else:
            kv_cols = C_KV_HEADS * HEAD_DIM
            proj, (cache_v,), _ = _in_proj(xs, mod, l, g1, od_w_in_bf16, i, n_prompt, dec_seq,
                                           cache_cols=(OD_V_OFF,), n_kv=C_KV_HEADS, tm=1024)
            oc_p, kn_p = _attention(proj, q_off=0, k_off=OD_K_OFF, v_off=OD_V_OFF, n_heads=C_HEADS,
                                    n_kv=C_KV_HEADS, row_block0=0, n_seq=batch, seq_len=seq, tq=seq, kvs=C_KV_HEADS,
                                    q_gain=c_qnorm_g[i], k_gain=c_knorm_g[i], emit_k=True)
            oc_s = _attention(proj, q_off=0, k_off=OD_K_OFF, v_off=OD_V_OFF, n_heads=C_HEADS,
                              n_kv=C_KV_HEADS, row_block0=s_blk0, n_seq=dec_batch, seq_len=dec_seq, tq=256, kvs=2,
                              ctx=ctx_c, ctx_layer=i, cos=cos, sin=sin, q_gain=c_qnorm_g[i],
                              k_gain=c_knorm_g[i])
            new_c.append(jnp.stack([kn_p.reshape(batch, seq, C_KV_HEADS, HEAD_DIM),
                                    cache_v.reshape(batch, seq, C_KV_HEADS, HEAD_DIM)], axis=1))
            parts = [[oc_p, oc_s]]
            w_out = od_w_out_bf16
        out = _out_mlp(xs, parts, w_out, i, mod, l, g2, w1, w2, n_prompt, dec_seq, final_g=fin)
        xs = list(out) if fin is not None else [out]

    y_prompt = xs[0].reshape(batch, seq, d)
    y_sample = xs[1].reshape(dec_batch, dec_seq, d)
    return (y_prompt, y_sample, jnp.stack(new_a, axis=1), jnp.stack(new_b, axis=1), jnp.stack(new_c, axis=1))
```
